```python
import jax, jax.numpy as jnp
from jax import lax
import numpy as np

D_MODEL = 1024
BATCH = 8
SEQ = 2048
DEPTH = 1
DEC_BATCH = 128
DEC_SEQ = 4
PAST_LEN = 16384
PAGE_SIZE = 128

PLE_DIM = 256
POOL_WIDTH = D_MODEL // 2
POOL_WINDOWS = (2, 4, 8, 16)
POOL_GROUPS = len(POOL_WINDOWS)
POOL_GROUP_DIM = POOL_WIDTH // POOL_GROUPS
POOL_STATE = max(POOL_WINDOWS) - 1
RWKV_WIDTH = D_MODEL // 2
RWKV_HEAD_DIM = 64
RWKV_HEADS = RWKV_WIDTH // RWKV_HEAD_DIM
DECAY_LORA = 64
AAA_LORA = 64
GATE_LORA = 128
SHIFT_WIDTH = 3 * RWKV_WIDTH + DECAY_LORA + AAA_LORA + GATE_LORA
SHIFT_SPLITS = (RWKV_WIDTH, 2 * RWKV_WIDTH, 3 * RWKV_WIDTH,
                3 * RWKV_WIDTH + DECAY_LORA, 3 * RWKV_WIDTH + DECAY_LORA + AAA_LORA)
IN_WIDTH = POOL_WIDTH + SHIFT_WIDTH + 2 * D_MODEL
N_GROUPS = 4
EXPERTS_PER_GROUP = 8
N_EXPERTS = N_GROUPS * EXPERTS_PER_GROUP
TOP_K_IN_GROUP = 2
D_EXPERT = 256
RMS_EPS = 1e-6
GN_EPS = 64e-5

kernel_name = "pool_rwkv7_hier_moe_ple_decode_step"


def rmsnorm(x, g):
    xf = x.astype(jnp.float32)
    y = xf * lax.rsqrt(jnp.mean(xf * xf, axis=-1, keepdims=True) + RMS_EPS)
    return (y * g.astype(jnp.float32)).astype(x.dtype)


def pool_mixer(u, u_past, pos0, mix, scale):
    B, T, _ = u.shape
    f32 = jnp.float32
    u_ext = jnp.concatenate([u_past.astype(u.dtype), u], axis=1)
    uf = u_ext.astype(f32)
    cs = jnp.concatenate([jnp.zeros_like(uf[:, :1]), jnp.cumsum(uf, axis=1)], axis=1)
    pos = pos0 + jnp.arange(T)
    end = cs[:, POOL_STATE + 1:]
    cur = uf[:, POOL_STATE:]
    outs = []
    for g, w in enumerate(POOL_WINDOWS):
        sl = slice(g * POOL_GROUP_DIM, (g + 1) * POOL_GROUP_DIM)
        start = cs[:, POOL_STATE + 1 - w:POOL_STATE + 1 - w + T, sl]
        cnt = jnp.minimum(pos + 1, w).astype(f32)[None, :, None]
        outs.append((end[..., sl] - start) / cnt - cur[..., sl])
    pooled = jnp.stack(outs, axis=2)
    y = jnp.einsum('btgc,gcd->btgd', pooled, mix.astype(f32)).reshape(B, T, POOL_WIDTH)
    y = y * scale.astype(f32)
    return y.astype(u.dtype), u_ext[:, -POOL_STATE:]


def _wkv_step(s, inp):
    r, w, k, v, a, b = inp
    sa = jnp.einsum('bhvk,bhk->bhv', s, a)
    s = s * w[:, :, None, :] + sa[..., None] * b[:, :, None, :] + v[..., None] * k[:, :, None, :]
    return s, jnp.einsum('bhvk,bhk->bhv', s, r)


def rwkv7_mixer(z, shift_past, wkv_past, mu, w0, w2, a0, a2, g2, k_k, k_a, r_k, ln_w, ln_b):
    B, T, _ = z.shape
    f32 = jnp.float32
    z_ext = jnp.concatenate([shift_past[:, None, :].astype(z.dtype), z], axis=1)
    zc = z.astype(f32)
    zm = zc + (z_ext[:, :-1].astype(f32) - zc) * mu
    r, k, v, wd, ad, gd = jnp.split(zm, SHIFT_SPLITS, axis=-1)
    w_log = -jax.nn.softplus(-(w0 + jnp.tanh(wd) @ w2)) - 0.5
    decay = jnp.exp(-jnp.exp(w_log))
    a = jax.nn.sigmoid(a0 + ad @ a2)
    g = jax.nn.sigmoid(gd) @ g2
    hd = lambda t: t.reshape(B, T, RWKV_HEADS, RWKV_HEAD_DIM)
    kk = hd(k * k_k)
    kk = kk / jnp.maximum(jnp.sqrt(jnp.sum(kk * kk, axis=-1, keepdims=True)), 1e-12)
    k = k * (1.0 + (a - 1.0) * k_a)
    r_h, w_h, k_h, v_h, a_h = hd(r), hd(decay), hd(k), hd(v), hd(a)
    seq = tuple(jnp.swapaxes(t, 0, 1) for t in (r_h, w_h, k_h, v_h, -kk, kk * a_h))
    s_last, o = lax.scan(_wkv_step, wkv_past.astype(f32), seq)
    o = jnp.swapaxes(o, 0, 1)
    mean = jnp.mean(o, axis=-1, keepdims=True)
    var = jnp.mean(jnp.square(o - mean), axis=-1, keepdims=True)
    o = ((o - mean) * lax.rsqrt(var + GN_EPS)).reshape(B, T, RWKV_WIDTH) * ln_w + ln_b
    bonus = jnp.sum(r_h * k_h * r_k, axis=-1, keepdims=True) * v_h
    o = (o + bonus.reshape(B, T, RWKV_WIDTH)) * g
    return o.astype(z.dtype), z_ext[:, -1], s_last


def hier_moe(h, w_rg, b_rg, w_re, b_re, e_gate, e_up, e_down):
    B, T, D = h.shape
    f32 = jnp.float32
    hf = h.reshape(-1, D)
    glog = (hf @ w_rg).astype(f32) + b_rg
    gprob = jax.nn.softmax(glog, axis=-1)
    _, gsel = lax.top_k(glog, 1)
    pg = jnp.take_along_axis(gprob, gsel, axis=-1)
    elog = ((hf @ w_re).astype(f32) + b_re).reshape(-1, N_GROUPS, EXPERTS_PER_GROUP)
    elog = jnp.take_along_axis(elog, gsel[:, :, None], axis=1)[:, 0]
    tv, ti = lax.top_k(elog, TOP_K_IN_GROUP)
    pe = jax.nn.softmax(tv, axis=-1) * pg
    eidx = gsel * EXPERTS_PER_GROUP + ti
    comb = jnp.sum(jax.nn.one_hot(eidx, N_EXPERTS, dtype=f32) * pe[..., None], axis=1)
    y = jnp.zeros(hf.shape, f32)
    for e in range(N_EXPERTS):
        hid = jax.nn.silu(hf @ e_gate[e]) * (hf @ e_up[e])
        y = y + comb[:, e:e + 1] * (hid @ e_down[e]).astype(f32)
    return y.reshape(B, T, D).astype(h.dtype)


def decoder_layer(x, p, pool_past, shift_past, wkv_past, pos0, lp):
    dt = x.dtype
    f32 = jnp.float32
    h = rmsnorm(x, lp['norm_mix'])
    proj = h @ lp['w_in']
    o1 = POOL_WIDTH
    o2 = o1 + SHIFT_WIDTH
    o3 = o2 + D_MODEL
    u, z, gate_a, gate_b = proj[..., :o1], proj[..., o1:o2], proj[..., o2:o3], proj[..., o3:]
    y_a, pool_new = pool_mixer(u, pool_past, pos0, lp['pool_mix'], lp['pool_scale'])
    y_b, shift_new, wkv_new = rwkv7_mixer(
        z, shift_past, wkv_past, lp['shift_mu'], lp['decay_w0'], lp['decay_w2'], lp['iclr_a0'],
        lp['iclr_a2'], lp['gate_g2'], lp['k_k'], lp['k_a'], lp['r_k'], lp['ln_x_w'], lp['ln_x_b'])
    merged = (jax.nn.sigmoid(gate_a.astype(f32)) * (y_a @ lp['w_branch_a']).astype(f32)
              + jax.nn.sigmoid(gate_b.astype(f32)) * (y_b @ lp['w_branch_b']).astype(f32))
    x = x + (merged.astype(dt) @ lp['w_out']).astype(dt)
    h = rmsnorm(x, lp['norm_ffn'])
    x = x + hier_moe(h, lp['w_route_group'], lp['b_route_group'], lp['w_route_expert'],
                     lp['b_route_expert'], lp['expert_gate'], lp['expert_up'], lp['expert_down'])
    h = rmsnorm(x, lp['norm_ple'])
    ple = jax.nn.sigmoid((h @ lp['w_ple_gate']).astype(f32)) * (p.astype(dt) @ lp['w_ple_proj']).astype(f32)
    x = x + ple.astype(dt)
    return x, pool_new.astype(dt), shift_new.astype(dt), wkv_new.astype(dt)


def setup_inputs(seed: int = 0) -> dict:
    key = jax.random.key(seed)
    ks = list(jax.random.split(key, 48))
    f32 = jnp.float32
    L = DEPTH

    def nrm(shape, scale):
        return scale * jax.random.normal(ks.pop(), shape, f32)

    def uni(shape, lo, hi):
        return jax.random.uniform(ks.pop(), shape, f32, lo, hi)

    return {
        'x_prompt': nrm((BATCH, SEQ, D_MODEL), 1.0),
        'x_sample': nrm((DEC_BATCH, DEC_SEQ, D_MODEL), 1.0),
        'state_pool': nrm((L, DEC_BATCH, POOL_STATE, POOL_WIDTH), 1.0),
        'state_shift': nrm((L, DEC_BATCH, SHIFT_WIDTH), 1.0),
        'state_wkv': nrm((L, DEC_BATCH, RWKV_HEADS, RWKV_HEAD_DIM, RWKV_HEAD_DIM), 0.5),
        'p_prompt': nrm((L, BATCH, SEQ, PLE_DIM), 1.0),
        'p_sample': nrm((L, DEC_BATCH, DEC_SEQ, PLE_DIM), 1.0),
        'norm_mix': 1.0 + nrm((L, D_MODEL), 0.05),
        'w_in': nrm((L, D_MODEL, IN_WIDTH), D_MODEL ** -0.5),
        'pool_mix': nrm((L, POOL_GROUPS, POOL_GROUP_DIM, POOL_GROUP_DIM), POOL_GROUP_DIM ** -0.5),
        'pool_scale': 1.0 + nrm((L, POOL_WIDTH), 0.1),
        'w_branch_a': nrm((L, POOL_WIDTH, D_MODEL), POOL_WIDTH ** -0.5),
        'shift_mu': uni((L, SHIFT_WIDTH), 0.0, 1.0),
        'decay_w0': uni((L, RWKV_WIDTH), -5.0, -0.5),
        'decay_w2': nrm((L, DECAY_LORA, RWKV_WIDTH), 0.1 * DECAY_LORA ** -0.5),
        'iclr_a0': nrm((L, RWKV_WIDTH), 0.5),
        'iclr_a2': nrm((L, AAA_LORA, RWKV_WIDTH), 0.5 * AAA_LORA ** -0.5),
        'gate_g2': nrm((L, GATE_LORA, RWKV_WIDTH), GATE_LORA ** -0.5),
        'k_k': 0.85 + nrm((L, RWKV_WIDTH), 0.05),
        'k_a': 1.0 + nrm((L, RWKV_WIDTH), 0.05),
        'r_k': nrm((L, RWKV_HEADS, RWKV_HEAD_DIM), 0.1),
        'ln_x_w': 1.0 + nrm((L, RWKV_WIDTH), 0.05),
        'ln_x_b': nrm((L, RWKV_WIDTH), 0.01),
        'w_branch_b': nrm((L, RWKV_WIDTH, D_MODEL), RWKV_WIDTH ** -0.5),
        'w_out': nrm((L, D_MODEL, D_MODEL), D_MODEL ** -0.5),
        'norm_ffn': 1.0 + nrm((L, D_MODEL), 0.05),
        'w_route_group': nrm((L, D_MODEL, N_GROUPS), D_MODEL ** -0.5),
        'b_route_group': nrm((L, N_GROUPS), 0.01),
        'w_route_expert': nrm((L, D_MODEL, N_EXPERTS), D_MODEL ** -0.5),
        'b_route_expert': nrm((L, N_EXPERTS), 0.01),
        'expert_gate': nrm((L, N_EXPERTS, D_MODEL, D_EXPERT), D_MODEL ** -0.5),
        'expert_up': nrm((L, N_EXPERTS, D_MODEL, D_EXPERT), D_MODEL ** -0.5),
        'expert_down': nrm((L, N_EXPERTS, D_EXPERT, D_MODEL), D_EXPERT ** -0.5),
        'norm_ple': 1.0 + nrm((L, D_MODEL), 0.05),
        'w_ple_gate': nrm((L, D_MODEL, D_MODEL), D_MODEL ** -0.5),
        'w_ple_proj': nrm((L, PLE_DIM, D_MODEL), PLE_DIM ** -0.5),
        'norm_final': 1.0 + nrm((D_MODEL,), 0.05),
    }


def reference(x_prompt, x_sample, state_pool, state_shift, state_wkv, p_prompt, p_sample,
              norm_mix, w_in, pool_mix, pool_scale, w_branch_a, shift_mu, decay_w0, decay_w2,
              iclr_a0, iclr_a2, gate_g2, k_k, k_a, r_k, ln_x_w, ln_x_b, w_branch_b, w_out,
              norm_ffn, w_route_group, b_route_group, w_route_expert, b_route_expert,
              expert_gate, expert_up, expert_down, norm_ple, w_ple_gate, w_ple_proj, norm_final):
    dt = x_prompt.dtype
    xp, xs = x_prompt, x_sample
    pool_p, shift_p, wkv_p, pool_s, shift_s, wkv_s = [], [], [], [], [], []
    for l in range(DEPTH):
        lp = dict(norm_mix=norm_mix[l], w_in=w_in[l], pool_mix=pool_mix[l], pool_scale=pool_scale[l],
                  w_branch_a=w_branch_a[l], shift_mu=shift_mu[l], decay_w0=decay_w0[l],
                  decay_w2=decay_w2[l], iclr_a0=iclr_a0[l], iclr_a2=iclr_a2[l], gate_g2=gate_g2[l],
                  k_k=k_k[l], k_a=k_a[l], r_k=r_k[l], ln_x_w=ln_x_w[l], ln_x_b=ln_x_b[l],
                  w_branch_b=w_branch_b[l], w_out=w_out[l], norm_ffn=norm_ffn[l],
                  w_route_group=w_route_group[l], b_route_group=b_route_group[l],
                  w_route_expert=w_route_expert[l], b_route_expert=b_route_expert[l],
                  expert_gate=expert_gate[l], expert_up=expert_up[l], expert_down=expert_down[l],
                  norm_ple=norm_ple[l], w_ple_gate=w_ple_gate[l], w_ple_proj=w_ple_proj[l])
        b = xp.shape[0]
        zero_pool = jnp.zeros((b, POOL_STATE, POOL_WIDTH), dt)
        zero_shift = jnp.zeros((b, SHIFT_WIDTH), dt)
        zero_wkv = jnp.zeros((b, RWKV_HEADS, RWKV_HEAD_DIM, RWKV_HEAD_DIM), dt)
        xp, pp, sp, wp = decoder_layer(xp, p_prompt[l], zero_pool, zero_shift, zero_wkv, 0, lp)
        xs, ps, ss, ws = decoder_layer(xs, p_sample[l], state_pool[l], state_shift[l], state_wkv[l],
                                       PAST_LEN, lp)
        pool_p.append(pp); shift_p.append(sp); wkv_p.append(wp)
        pool_s.append(ps); shift_s.append(ss); wkv_s.append(ws)
    y_prompt = rmsnorm(xp, norm_final)
    y_sample = rmsnorm(xs, norm_final)
    return (y_prompt, y_sample, jnp.stack(pool_p), jnp.stack(shift_p), jnp.stack(wkv_p),
            jnp.stack(pool_s), jnp.stack(shift_s), jnp.stack(wkv_s))
```

```python
import functools

import jax
import jax.numpy as jnp
from jax import lax
from jax.experimental import pallas as pl
from jax.experimental.pallas import tpu as pltpu

F32 = jnp.float32
BF16 = jnp.bfloat16

D_MODEL = 1024
PLE_DIM = 256
POOL_WIDTH = 512
POOL_WINDOWS = (2, 4, 8, 16)
POOL_GROUP_DIM = 128
POOL_STATE = 15
RWKV_WIDTH = 512
HEAD_DIM = 64
HEADS = 8
LORA_PAIR = 128
GATE_LORA = 128
SHIFT_WIDTH = 3 * RWKV_WIDTH + LORA_PAIR + GATE_LORA
IN_WIDTH = POOL_WIDTH + SHIFT_WIDTH + 2 * D_MODEL
N_GROUPS = 4
EXPERTS_PER_GROUP = 8
N_EXPERTS = 32
D_EXPERT = 256
RMS_EPS = 1e-6
GN_EPS = 64e-5
PAST_LEN = 16384

LANES = 128
SEQ_PER_BLOCK = 64
KEY_HALF = HEAD_DIM // 2
ROUTER_LANES = 128
NEG_BIG = -1e30
VMEM_LIMIT = 56 * 1024 * 1024


def _dot(a, b):
    return jnp.dot(a, b, preferred_element_type=F32)


def _seg_sum(x, ones_bd):
    hi = x.astype(BF16)
    lo = (x - hi.astype(F32)).astype(BF16)
    return _dot(hi, ones_bd) + _dot(lo, ones_bd)


def _rmsnorm(x, g):
    return x * lax.rsqrt(jnp.mean(x * x, axis=-1, keepdims=True) + RMS_EPS) * g


def _softplus(y):
    return jnp.maximum(y, 0.0) + jnp.log1p(jnp.exp(-jnp.abs(y)))


def _mix_prep_kernel(x_ref, zc0_ref, uc0_ref, nmix_ref, win_ref, mu_ref, w0_ref, w2_ref, a0_ref,
                     a2_ref, g2_ref, kk_ref, ka_ref, rk_ref, ones_ref, mix_ref, pscale_ref, wa_ref,
                     u_ref, zl_ref, r_ref, w_ref, k_ref, v_ref, a_ref, b_ref, g_ref, bon_ref,
                     pa_ref, sgb_ref, zext, uext, *, tm, s, pos0):
    t = pl.program_id(1)
    zp = max(8, s)
    up = 16 * s

    @pl.when(t == 0)
    def _():
        zext[0:zp] = zc0_ref[...]
        uext[0:up] = uc0_ref[...]

    x = x_ref[...]
    h = _rmsnorm(x, nmix_ref[...]).astype(BF16)
    o1 = POOL_WIDTH
    o2 = o1 + SHIFT_WIDTH
    o3 = o2 + D_MODEL
    u = _dot(h, win_ref[:, 0:o1])
    z = _dot(h, win_ref[:, o1:o2])
    ga = _dot(h, win_ref[:, o2:o3])
    gb = _dot(h, win_ref[:, o3:IN_WIDTH])
    u_ref[...] = u
    sgb_ref[...] = jax.nn.sigmoid(gb)
    zext[zp:zp + tm] = z
    uext[up:up + tm] = u
    zl_ref[0] = zext[pl.ds(zp + tm - s, s)]

    zprev = zext[pl.ds(zp - s, tm)]
    zm = z + (zprev - z) * mu_ref[...]
    r = zm[:, 0:RWKV_WIDTH]
    k = zm[:, RWKV_WIDTH:2 * RWKV_WIDTH]
    v = zm[:, 2 * RWKV_WIDTH:3 * RWKV_WIDTH]
    lora_in = zm[:, 3 * RWKV_WIDTH:3 * RWKV_WIDTH + LORA_PAIR]
    gd = zm[:, 3 * RWKV_WIDTH + LORA_PAIR:SHIFT_WIDTH]
    dw = _dot(jnp.tanh(lora_in).astype(BF16), w2_ref[...])
    da = _dot(lora_in.astype(BF16), a2_ref[...])
    w_log = -_softplus(-(w0_ref[...] + dw)) - 0.5
    decay = jnp.exp(-jnp.exp(w_log))
    a = jax.nn.sigmoid(a0_ref[...] + da)
    g = _dot(jax.nn.sigmoid(gd).astype(BF16), g2_ref[...])
    ones_bd = ones_ref[...]
    kk = k * kk_ref[...]
    kk = kk / jnp.maximum(jnp.sqrt(_seg_sum(kk * kk, ones_bd)), 1e-12)
    k2 = k * (1.0 + (a - 1.0) * ka_ref[...])
    r_ref[...] = r
    w_ref[...] = decay
    k_ref[...] = k2
    v_ref[...] = v
    a_ref[...] = -kk
    b_ref[...] = kk * a
    g_ref[...] = g
    bon_ref[...] = _seg_sum(r * k2 * rk_ref[...], ones_bd) * v

    rows = lax.broadcasted_iota(jnp.int32, (tm, POOL_GROUP_DIM), 0)
    if s > 1:
        rows = rows // s
    pos = pos0 + t * (tm // s) + rows
    ys = []
    for gi, wnd in enumerate(POOL_WINDOWS):
        lanes = slice(gi * POOL_GROUP_DIM, (gi + 1) * POOL_GROUP_DIM)
        cur = uext[pl.ds(up, tm), lanes]
        acc = cur
        for j in range(1, wnd):
            acc = acc + uext[pl.ds(up - j * s, tm), lanes]
        cnt = jnp.minimum(pos + 1, wnd).astype(F32)
        pooled = acc / cnt - cur
        ys.append(_dot(pooled.astype(BF16), mix_ref[gi]))
    y = jnp.concatenate(ys, axis=-1) * pscale_ref[...]
    pa_ref[...] = jax.nn.sigmoid(ga) * _dot(y.astype(BF16), wa_ref[...])

    zext[0:zp] = zext[tm:tm + zp]
    uext[0:up] = uext[tm:tm + up]


def _mix_prep(x2d, zc0, uc0, wts, *, nb, tiles, tm, s, pos0):
    n = x2d.shape[0]
    zp = max(8, s)
    up = 16 * s
    row = lambda b, t: (b * tiles + t, 0)
    const2 = lambda b, t: (0, 0)
    const3 = lambda b, t: (0, 0, 0)
    full = lambda arr: pl.BlockSpec(arr.shape, const2 if arr.ndim == 2 else const3)
    in_specs = [
        pl.BlockSpec((tm, D_MODEL), row),
        pl.BlockSpec((zp, SHIFT_WIDTH), lambda b, t: (b, 0)),
        pl.BlockSpec((up, POOL_WIDTH), lambda b, t: (b, 0)),
    ] + [full(w) for w in wts]
    tok = lambda width: pl.BlockSpec((tm, width), row)
    out_specs = [tok(POOL_WIDTH), pl.BlockSpec((1, s, SHIFT_WIDTH), lambda b, t: (b, 0, 0))] \
        + [tok(RWKV_WIDTH)] * 8 + [tok(D_MODEL)] * 2
    out_shape = [jax.ShapeDtypeStruct((n, POOL_WIDTH), F32),
                 jax.ShapeDtypeStruct((nb, s, SHIFT_WIDTH), F32)] \
        + [jax.ShapeDtypeStruct((n, RWKV_WIDTH), F32)] * 8 \
        + [jax.ShapeDtypeStruct((n, D_MODEL), F32)] * 2
    return pl.pallas_call(
        functools.partial(_mix_prep_kernel, tm=tm, s=s, pos0=pos0),
        grid=(nb, tiles),
        in_specs=in_specs,
        out_specs=out_specs,
        out_shape=out_shape,
        scratch_shapes=[pltpu.VMEM((zp + tm, SHIFT_WIDTH), F32),
                        pltpu.VMEM((up + tm, POOL_WIDTH), F32)],
        compiler_params=pltpu.CompilerParams(
            dimension_semantics=("arbitrary", "arbitrary"), vmem_limit_bytes=VMEM_LIMIT),
        name="mix_prep",
    )(x2d, zc0, uc0, *wts)


def _wkv_scan_kernel(a_ref, w_ref, b_ref, k_ref, r_ref, v_ref, s0_ref, o_ref, sout_ref, s_ref, *, tt):
    t = pl.program_id(1)

    @pl.when(t == 0)
    def _():
        s_ref[...] = s0_ref[0]

    def step(i, carry):
        acc = jnp.zeros((HEAD_DIM, LANES), F32)
        for kp in range(KEY_HALF):
            acc = acc + s_ref[kp] * a_ref[0, i, pl.ds(kp, 1), :]
        sa = acc + pltpu.roll(acc, SEQ_PER_BLOCK, axis=1)
        vv = v_ref[0, i]
        acc_o = jnp.zeros((HEAD_DIM, LANES), F32)
        for kp in range(KEY_HALF):
            sn = (s_ref[kp] * w_ref[0, i, pl.ds(kp, 1), :]
                  + sa * b_ref[0, i, pl.ds(kp, 1), :]
                  + vv * k_ref[0, i, pl.ds(kp, 1), :])
            s_ref[kp] = sn
            acc_o = acc_o + sn * r_ref[0, i, pl.ds(kp, 1), :]
        o_ref[0, i] = acc_o + pltpu.roll(acc_o, SEQ_PER_BLOCK, axis=1)
        return carry

    lax.fori_loop(0, tt, step, 0)

    @pl.when(t == pl.num_programs(1) - 1)
    def _():
        sout_ref[0] = s_ref[...]


def _wkv_scan(a, w, b, k, r, v, s0, *, tt):
    g, t_len = a.shape[0], a.shape[1]
    kspec = pl.BlockSpec((1, tt, KEY_HALF, LANES), lambda gi, ti: (gi, ti, 0, 0))
    vspec = pl.BlockSpec((1, tt, HEAD_DIM, LANES), lambda gi, ti: (gi, ti, 0, 0))
    sspec = pl.BlockSpec((1, KEY_HALF, HEAD_DIM, LANES), lambda gi, ti: (gi, 0, 0, 0))
    return pl.pallas_call(
        functools.partial(_wkv_scan_kernel, tt=tt),
        grid=(g, t_len // tt),
        in_specs=[kspec] * 5 + [vspec, sspec],
        out_specs=[vspec, sspec],
        out_shape=[jax.ShapeDtypeStruct((g, t_len, HEAD_DIM, LANES), F32),
                   jax.ShapeDtypeStruct((g, KEY_HALF, HEAD_DIM, LANES), F32)],
        scratch_shapes=[pltpu.VMEM((KEY_HALF, HEAD_DIM, LANES), F32)],
        compiler_params=pltpu.CompilerParams(
            dimension_semantics=("arbitrary", "arbitrary"), vmem_limit_bytes=VMEM_LIMIT),
        name="wkv_scan",
    )(a, w, b, k, r, v, s0)


def _post_kernel(o_ref, bon_ref, g_ref, pa_ref, sgb_ref, x_ref, lnw_ref, lnb_ref, ones_ref, wb_ref,
                 wout_ref, nffn_ref, wrh_ref, wrl_ref, br_ref, x1_ref, h2_ref, comb_ref):
    ones_bd = ones_ref[...]
    o = o_ref[...]
    mean = _seg_sum(o, ones_bd) * (1.0 / HEAD_DIM)
    d = o - mean
    var = _seg_sum(d * d, ones_bd) * (1.0 / HEAD_DIM)
    on = d * lax.rsqrt(var + GN_EPS) * lnw_ref[...] + lnb_ref[...]
    yb = (on + bon_ref[...]) * g_ref[...]
    merged = pa_ref[...] + sgb_ref[...] * _dot(yb.astype(BF16), wb_ref[...])
    x1 = x_ref[...] + _dot(merged.astype(BF16), wout_ref[...])
    x1_ref[...] = x1
    h2 = _rmsnorm(x1, nffn_ref[...])
    h2_ref[...] = h2.astype(BF16)

    h_hi = h2.astype(BF16)
    h_lo = (h2 - h_hi.astype(F32)).astype(BF16)
    logits = (_dot(h_hi, wrh_ref[...]) + _dot(h_lo, wrh_ref[...]) + _dot(h_hi, wrl_ref[...])
              + br_ref[...])
    ln = lax.broadcasted_iota(jnp.int32, logits.shape, 1)
    is_group = (ln >= N_EXPERTS) & (ln < N_EXPERTS + N_GROUPS)
    gl = jnp.where(is_group, logits, NEG_BIG)
    gmax = jnp.max(gl, axis=-1, keepdims=True)
    gsel = jnp.min(jnp.where(gl == gmax, ln, ROUTER_LANES), axis=-1, keepdims=True) - N_EXPERTS
    den = jnp.sum(jnp.where(is_group, jnp.exp(gl - gmax), 0.0), axis=-1, keepdims=True)
    pg = 1.0 / den
    in_group = (ln < N_EXPERTS) & ((ln // EXPERTS_PER_GROUP) == gsel)
    el = jnp.where(in_group, logits, NEG_BIG)
    m1 = jnp.max(el, axis=-1, keepdims=True)
    i1 = jnp.min(jnp.where(el == m1, ln, ROUTER_LANES), axis=-1, keepdims=True)
    el2 = jnp.where(ln == i1, NEG_BIG, el)
    m2 = jnp.max(el2, axis=-1, keepdims=True)
    i2 = jnp.min(jnp.where(el2 == m2, ln, ROUTER_LANES), axis=-1, keepdims=True)
    e2 = jnp.exp(m2 - m1)
    p1 = 1.0 / (1.0 + e2)
    p2 = e2 / (1.0 + e2)
    comb_ref[...] = jnp.where(ln == i1, p1 * pg, 0.0) + jnp.where(ln == i2, p2 * pg, 0.0)


def _post(o, bon, g, pa, sgb, x2d, wts, *, tm):
    n = x2d.shape[0]
    row = lambda i: (i, 0)
    tok = lambda width: pl.BlockSpec((tm, width), row)
    full = lambda arr: pl.BlockSpec(arr.shape, lambda i: (0, 0))
    return pl.pallas_call(
        _post_kernel,
        grid=(n // tm,),
        in_specs=[tok(RWKV_WIDTH)] * 3 + [tok(D_MODEL)] * 3 + [full(w) for w in wts],
        out_specs=[tok(D_MODEL), tok(D_MODEL), tok(ROUTER_LANES)],
        out_shape=[jax.ShapeDtypeStruct((n, D_MODEL), F32),
                   jax.ShapeDtypeStruct((n, D_MODEL), BF16),
                   jax.ShapeDtypeStruct((n, ROUTER_LANES), F32)],
        compiler_params=pltpu.CompilerParams(
            dimension_semantics=("parallel",), vmem_limit_bytes=VMEM_LIMIT),
        name="post",
    )(o, bon, g, pa, sgb, x2d, *wts)


def _moe_tail_kernel(h2_ref, comb_ref, x1_ref, p_ref, wg_ref, wu_ref, wd_ref, nple_ref, wpg_ref,
                     wpp_ref, nfin_ref, y_ref, acc_ref):
    e = pl.program_id(1)

    @pl.when(e == 0)
    def _():
        acc_ref[...] = jnp.zeros_like(acc_ref)

    x = h2_ref[...]
    hid = jax.nn.silu(_dot(x, wg_ref[0])) * _dot(x, wu_ref[0])
    ye = _dot(hid.astype(BF16), wd_ref[0])
    comb = comb_ref[...]
    ln = lax.broadcasted_iota(jnp.int32, comb.shape, 1)
    ce = jnp.sum(jnp.where(ln == e, comb, 0.0), axis=-1, keepdims=True)
    acc_ref[...] += ce * ye

    @pl.when(e == pl.num_programs(1) - 1)
    def _():
        x2 = x1_ref[...] + acc_ref[...]
        h3 = _rmsnorm(x2, nple_ref[...]).astype(BF16)
        ple = jax.nn.sigmoid(_dot(h3, wpg_ref[...])) * _dot(p_ref[...].astype(BF16), wpp_ref[...])
        y_ref[...] = _rmsnorm(x2 + ple, nfin_ref[...])


def _moe_tail(h2, comb, x1, p2d, wg, wu, wd, nple, wpg, wpp, nfin, *, tm):
    n = h2.shape[0]
    row = lambda i, e: (i, 0)
    tok = lambda width: pl.BlockSpec((tm, width), row)
    full = lambda arr: pl.BlockSpec(arr.shape, lambda i, e: (0, 0))
    return pl.pallas_call(
        _moe_tail_kernel,
        grid=(n // tm, N_EXPERTS),
        in_specs=[tok(D_MODEL), tok(ROUTER_LANES), tok(D_MODEL), tok(PLE_DIM),
                  pl.BlockSpec((1, D_MODEL, D_EXPERT), lambda i, e: (e, 0, 0)),
                  pl.BlockSpec((1, D_MODEL, D_EXPERT), lambda i, e: (e, 0, 0)),
                  pl.BlockSpec((1, D_EXPERT, D_MODEL), lambda i, e: (e, 0, 0)),
                  full(nple), full(wpg), full(wpp), full(nfin)],
        out_specs=tok(D_MODEL),
        out_shape=jax.ShapeDtypeStruct((n, D_MODEL), F32),
        scratch_shapes=[pltpu.VMEM((tm, D_MODEL), F32)],
        compiler_params=pltpu.CompilerParams(
            dimension_semantics=("parallel", "arbitrary"), vmem_limit_bytes=VMEM_LIMIT),
        name="moe_tail",
    )(h2, comb, x1, p2d, wg, wu, wd, nple, wpg, wpp, nfin)


def _to_scan_k(x, *, time_major):
    if time_major:
        t_len, b = x.shape[0], x.shape[1]
        x = x.reshape(t_len, b // 8, 8, HEADS, 2, KEY_HALF)
        x = x.transpose(1, 0, 5, 4, 2, 3)
    else:
        b, t_len = x.shape[0], x.shape[1]
        x = x.reshape(b // 8, 8, t_len, HEADS, 2, KEY_HALF)
        x = x.transpose(0, 2, 5, 4, 1, 3)
    return x.reshape(x.shape[0], t_len, KEY_HALF, LANES)


def _to_scan_v(x, *, time_major):
    if time_major:
        t_len, b = x.shape[0], x.shape[1]
        x = x.reshape(t_len, b // 8, 8, HEADS, HEAD_DIM).transpose(1, 0, 4, 2, 3)
    else:
        b, t_len = x.shape[0], x.shape[1]
        x = x.reshape(b // 8, 8, t_len, HEADS, HEAD_DIM).transpose(0, 2, 4, 1, 3)
    x = x.reshape(x.shape[0], t_len, HEAD_DIM, SEQ_PER_BLOCK)
    return jnp.concatenate([x, x], axis=-1)


def _from_scan_o(o, *, time_major):
    g, t_len = o.shape[0], o.shape[1]
    o = o[..., :SEQ_PER_BLOCK].reshape(g, t_len, HEAD_DIM, 8, HEADS)
    if time_major:
        return o.transpose(1, 0, 3, 4, 2).reshape(t_len * g * 8, RWKV_WIDTH)
    return o.transpose(0, 3, 1, 4, 2).reshape(g * 8 * t_len, RWKV_WIDTH)


def _state_to_scan(s):
    b = s.shape[0]
    s = s.reshape(b // 8, 8, HEADS, HEAD_DIM, 2, KEY_HALF).transpose(0, 5, 3, 4, 1, 2)
    return s.reshape(b // 8, KEY_HALF, HEAD_DIM, LANES)


def _state_from_scan(s):
    g = s.shape[0]
    s = s.reshape(g, KEY_HALF, HEAD_DIM, 2, 8, HEADS).transpose(0, 4, 5, 2, 3, 1)
    return s.reshape(g * 8, HEADS, HEAD_DIM, HEAD_DIM)


def _layer(x2d, p2d, zc0, uc0, s0_scan, wt, *, nb, tiles, tm, s, pos0, time_major, seq_shape, tt,
           tm_post, tm_moe):
    outs = _mix_prep(x2d, zc0, uc0, wt["prep"], nb=nb, tiles=tiles, tm=tm, s=s, pos0=pos0)
    u, zlast, r, w, k, v, a, b, g, bon, pa, sgb = outs
    shp = seq_shape + (RWKV_WIDTH,)
    ks = [_to_scan_k(t.reshape(shp), time_major=time_major) for t in (a, w, b, k, r)]
    vs = _to_scan_v(v.reshape(shp), time_major=time_major)
    o_scan, s_out = _wkv_scan(*ks, vs, s0_scan, tt=tt)
    o = _from_scan_o(o_scan, time_major=time_major)
    x1, h2, comb = _post(o, bon, g, pa, sgb, x2d, wt["post"], tm=tm_post)
    y = _moe_tail(h2, comb, x1, p2d, *wt["moe"], tm=tm_moe)
    return y, u, zlast, _state_from_scan(s_out)


def kernel(x_prompt, x_sample, state_pool, state_shift, state_wkv, p_prompt, p_sample, norm_mix, w_in, pool_mix, pool_scale, w_branch_a, shift_mu, decay_w0, decay_w2, iclr_a0, iclr_a2, gate_g2, k_k, k_a, r_k, ln_x_w, ln_x_b, w_branch_b, w_out, norm_ffn, w_route_group, b_route_group, w_route_expert, b_route_expert, expert_gate, expert_up, expert_down, norm_ple, w_ple_gate, w_ple_proj, norm_final):
    l = 0
    bsz, seq, _ = x_prompt.shape
    dbsz, dseq, _ = x_sample.shape
    row = lambda vec: vec.reshape(1, -1).astype(F32)

    lane = jnp.arange(RWKV_WIDTH) // HEAD_DIM
    ones_bd = (lane[:, None] == lane[None, :]).astype(BF16)
    zeros_lora = jnp.zeros((LORA_PAIR // 2, RWKV_WIDTH), F32)
    w2_pad = jnp.concatenate([decay_w2[l], zeros_lora], axis=0).astype(BF16)
    a2_pad = jnp.concatenate([zeros_lora, iclr_a2[l]], axis=0).astype(BF16)
    w_router = jnp.concatenate(
        [w_route_expert[l], w_route_group[l],
         jnp.zeros((D_MODEL, ROUTER_LANES - N_EXPERTS - N_GROUPS), F32)], axis=1)
    wr_hi = w_router.astype(BF16)
    wr_lo = (w_router - wr_hi.astype(F32)).astype(BF16)
    b_router = jnp.concatenate(
        [b_route_expert[l], b_route_group[l],
         jnp.zeros((ROUTER_LANES - N_EXPERTS - N_GROUPS,), F32)]).reshape(1, -1)
    wt = {
        "prep": [row(norm_mix[l]), w_in[l].astype(BF16), row(shift_mu[l]), row(decay_w0[l]), w2_pad,
                 row(iclr_a0[l]), a2_pad, gate_g2[l].astype(BF16), row(k_k[l]), row(k_a[l]),
                 row(r_k[l]), ones_bd, pool_mix[l].astype(BF16), row(pool_scale[l]),
                 w_branch_a[l].astype(BF16)],
        "post": [row(ln_x_w[l]), row(ln_x_b[l]), ones_bd, w_branch_b[l].astype(BF16),
                 w_out[l].astype(BF16), row(norm_ffn[l]), wr_hi, wr_lo, b_router],
        "moe": [expert_gate[l].astype(BF16), expert_up[l].astype(BF16), expert_down[l].astype(BF16),
                row(norm_ple[l]), w_ple_gate[l].astype(BF16), w_ple_proj[l].astype(BF16),
                row(norm_final)],
    }

    tm_p = 256
    y_p, u_p, zl_p, wkv_p = _layer(
        x_prompt.reshape(bsz * seq, D_MODEL), p_prompt[l].reshape(bsz * seq, PLE_DIM),
        jnp.zeros((bsz * 8, SHIFT_WIDTH), F32), jnp.zeros((bsz * 16, POOL_WIDTH), F32),
        jnp.zeros((bsz // 8, KEY_HALF, HEAD_DIM, LANES), F32), wt,
        nb=bsz, tiles=seq // tm_p, tm=tm_p, s=1, pos0=0, time_major=False,
        seq_shape=(bsz, seq), tt=64, tm_post=512, tm_moe=1024)

    n_s = dbsz * dseq
    x_s = x_sample.transpose(1, 0, 2).reshape(n_s, D_MODEL)
    p_s = p_sample[l].transpose(1, 0, 2).reshape(n_s, PLE_DIM)
    uc0_s = jnp.concatenate(
        [jnp.zeros((dbsz, POOL_WIDTH), F32),
         state_pool[l].transpose(1, 0, 2).reshape(POOL_STATE * dbsz, POOL_WIDTH)], axis=0)
    y_s, u_s, zl_s, wkv_s = _layer(
        x_s, p_s, state_shift[l], uc0_s, _state_to_scan(state_wkv[l]), wt,
        nb=1, tiles=dseq, tm=dbsz, s=dbsz, pos0=PAST_LEN, time_major=True,
        seq_shape=(dseq, dbsz), tt=dseq, tm_post=n_s, tm_moe=n_s)

    y_prompt = y_p.reshape(bsz, seq, D_MODEL)
    y_sample = y_s.reshape(dseq, dbsz, D_MODEL).transpose(1, 0, 2)
    pool_prompt = u_p.reshape(bsz, seq, POOL_WIDTH)[:, seq - POOL_STATE:]
    u_s_bt = u_s.reshape(dseq, dbsz, POOL_WIDTH).transpose(1, 0, 2)
    pool_sample = jnp.concatenate([state_pool[l][:, dseq:], u_s_bt], axis=1)
    return (y_prompt, y_sample, pool_prompt[None], zl_p.reshape(1, bsz, SHIFT_WIDTH), wkv_p[None],
            pool_sample[None], zl_s.reshape(1, dbsz, SHIFT_WIDTH), wkv_s[None])
```

```python
import functools

import jax
import jax.numpy as jnp
import numpy as np
from jax import lax
from jax.experimental import pallas as pl
from jax.experimental.pallas import tpu as pltpu

F32 = jnp.float32
BF16 = jnp.bfloat16

D_MODEL = 1024
PLE_DIM = 256
POOL_WIDTH = 512
POOL_WINDOWS = (2, 4, 8, 16)
POOL_GROUP_DIM = 128
POOL_STATE = 15
RWKV_WIDTH = 512
HEAD_DIM = 64
HEADS = 8
LORA_PAIR = 128
GATE_LORA = 128
SHIFT_WIDTH = 3 * RWKV_WIDTH + LORA_PAIR + GATE_LORA
IN_WIDTH = POOL_WIDTH + SHIFT_WIDTH + 2 * D_MODEL
N_GROUPS = 4
EXPERTS_PER_GROUP = 8
N_EXPERTS = 32
D_EXPERT = 256
RMS_EPS = 1e-6
GN_EPS = 64e-5
PAST_LEN = 16384

LANES = 128
SUBLANES = 8
SEQ_PER_BLOCK = 64
KEY_HALF = HEAD_DIM // 2
ROUTER_LANES = 128
NEG_BIG = -1e30
VMEM_LIMIT = 56 * 1024 * 1024

_J = np.arange(RWKV_WIDTH)
CHANNEL_PERM = (_J % 8) * HEAD_DIM + ((_J // 8) % 2) * KEY_HALF + _J // 16


def _dot(a, b):
    return jnp.dot(a, b, preferred_element_type=F32)


def _split_bf16(x):
    hi = x.astype(BF16)
    return hi, (x - hi.astype(F32)).astype(BF16)


def _seg_sum_rows(ones_bd, x):
    hi, lo = _split_bf16(x)
    return _dot(ones_bd, hi) + _dot(ones_bd, lo)


def _rmsnorm(x, g):
    return x * lax.rsqrt(jnp.mean(x * x, axis=-1, keepdims=True) + RMS_EPS) * g


def _softplus(y):
    return jnp.maximum(y, 0.0) + jnp.log1p(jnp.exp(-jnp.abs(y)))


def _mix_prep_kernel(x_ref, zc0_ref, uc0_ref, nmix_ref, wu_ref, wzt_ref, wgab_ref, mu_ref, w0_ref,
                     w2t_ref, a0_ref, a2t_ref, g2t_ref, kk_ref, ka_ref, rk_ref, ones_ref, mix_ref,
                     pscale_ref, wa_ref,
                     u_ref, zl_ref, a_ref, w_ref, b_ref, k_ref, r_ref, v_ref, g_ref, bon_ref,
                     pa_ref, sgb_ref, zc, uext, *, tm, s, pos0):
    t = pl.program_id(1)
    up = 16 * s

    @pl.when(t == 0)
    def _():
        zc[...] = zc0_ref[...]
        uext[0:up] = uc0_ref[...]

    x = x_ref[...]
    h = _rmsnorm(x, nmix_ref[...]).astype(BF16)
    u = _dot(h, wu_ref[...])
    gab = _dot(h, wgab_ref[...])
    u_ref[...] = u
    sgb_ref[...] = jax.nn.sigmoid(gab[:, D_MODEL:])
    uext[up:up + tm] = u

    z_t = lax.dot_general(wzt_ref[...], h, (((1,), (1,)), ((), ())), preferred_element_type=F32)
    if s == 1:
        lane = lax.broadcasted_iota(jnp.int32, z_t.shape, 1)
        zprev = jnp.where(lane == 0, zc[:, LANES - 1:LANES], pltpu.roll(z_t, 1, axis=1))
    else:
        zprev = zc[...]
    zc[...] = z_t[:, tm - LANES:tm]
    zl_ref[0] = z_t[:, tm - LANES:tm]
    zm = z_t + (zprev - z_t) * mu_ref[...]
    r = zm[0:RWKV_WIDTH]
    k = zm[RWKV_WIDTH:2 * RWKV_WIDTH]
    v = zm[2 * RWKV_WIDTH:3 * RWKV_WIDTH]
    lora_in = zm[3 * RWKV_WIDTH:3 * RWKV_WIDTH + LORA_PAIR]
    gd = zm[3 * RWKV_WIDTH + LORA_PAIR:SHIFT_WIDTH]
    dw = _dot(w2t_ref[...], jnp.tanh(lora_in).astype(BF16))
    da = _dot(a2t_ref[...], lora_in.astype(BF16))
    w_log = -_softplus(-(w0_ref[...] + dw)) - 0.5
    a = jax.nn.sigmoid(a0_ref[...] + da)
    ones_bd = ones_ref[...]
    kk = k * kk_ref[...]
    kk = kk / jnp.maximum(jnp.sqrt(_seg_sum_rows(ones_bd, kk * kk)), 1e-12)
    k2 = k * (1.0 + (a - 1.0) * ka_ref[...])
    a_ref[0] = -kk
    w_ref[0] = jnp.exp(-jnp.exp(w_log))
    b_ref[0] = kk * a
    k_ref[0] = k2
    r_ref[0] = r
    v_ref[0] = v
    g_ref[0] = _dot(g2t_ref[...], jax.nn.sigmoid(gd).astype(BF16))
    bon_ref[0] = _seg_sum_rows(ones_bd, r * k2 * rk_ref[...]) * v

    rows = lax.broadcasted_iota(jnp.int32, (tm, POOL_GROUP_DIM), 0)
    if s > 1:
        rows = rows // s
    pos = pos0 + t * (tm // s) + rows
    ys = []
    for gi, wnd in enumerate(POOL_WINDOWS):
        lanes = slice(gi * POOL_GROUP_DIM, (gi + 1) * POOL_GROUP_DIM)
        cur = uext[pl.ds(up, tm), lanes]
        acc = cur
        for j in range(1, wnd):
            acc = acc + uext[pl.ds(up - j * s, tm), lanes]
        cnt = jnp.minimum(pos + 1, wnd).astype(F32)
        pooled = acc / cnt - cur
        ys.append(_dot(pooled.astype(BF16), mix_ref[gi]))
    y = jnp.concatenate(ys, axis=-1) * pscale_ref[...]
    pa_ref[...] = jax.nn.sigmoid(gab[:, :D_MODEL]) * _dot(y.astype(BF16), wa_ref[...])

    uext[0:up] = uext[tm:tm + up]


def _mix_prep(x2d, zc0, uc0, wts, *, nb, tiles, tm, s, pos0, cm_index):
    n = x2d.shape[0]
    up = 16 * s
    row = lambda b, t: (b * tiles + t, 0)
    full = lambda arr: pl.BlockSpec(arr.shape, lambda b, t: (0,) * arr.ndim)
    in_specs = [
        pl.BlockSpec((tm, D_MODEL), row),
        pl.BlockSpec((SHIFT_WIDTH, LANES), lambda b, t: (b, 0)),
        pl.BlockSpec((up, POOL_WIDTH), lambda b, t: (b, 0)),
    ] + [full(w) for w in wts]
    tok = lambda width: pl.BlockSpec((tm, width), row)
    cm_shape = (nb, RWKV_WIDTH, tiles * tm) if s == 1 else (tiles, RWKV_WIDTH, tm)
    cm = pl.BlockSpec((1, RWKV_WIDTH, tm), cm_index)
    out_specs = [tok(POOL_WIDTH), pl.BlockSpec((1, SHIFT_WIDTH, LANES), lambda b, t: (b, 0, 0))] \
        + [cm] * 8 + [tok(D_MODEL)] * 2
    out_shape = [jax.ShapeDtypeStruct((n, POOL_WIDTH), F32),
                 jax.ShapeDtypeStruct((nb, SHIFT_WIDTH, LANES), F32)] \
        + [jax.ShapeDtypeStruct(cm_shape, F32)] * 8 \
        + [jax.ShapeDtypeStruct((n, D_MODEL), F32)] * 2
    return pl.pallas_call(
        functools.partial(_mix_prep_kernel, tm=tm, s=s, pos0=pos0),
        grid=(nb, tiles),
        in_specs=in_specs,
        out_specs=out_specs,
        out_shape=out_shape,
        scratch_shapes=[pltpu.VMEM((SHIFT_WIDTH, LANES), F32),
                        pltpu.VMEM((up + tm, POOL_WIDTH), F32)],
        compiler_params=pltpu.CompilerParams(
            dimension_semantics=("arbitrary", "arbitrary"), vmem_limit_bytes=VMEM_LIMIT),
        name="mix_prep",
    )(x2d, zc0, uc0, *wts)


def _wkv_step(s_ref, row, vv, *, n_key_rows, fold_halves):
    acc = jnp.zeros(vv.shape, F32)
    for kp in range(n_key_rows):
        acc = acc + s_ref[kp] * row(0, kp)
    sa = acc + pltpu.roll(acc, SEQ_PER_BLOCK, axis=1) if fold_halves else acc
    acc_o = jnp.zeros(vv.shape, F32)
    for kp in range(n_key_rows):
        sn = s_ref[kp] * row(1, kp) + sa * row(2, kp) + vv * row(3, kp)
        s_ref[kp] = sn
        acc_o = acc_o + sn * row(4, kp)
    return acc_o + pltpu.roll(acc_o, SEQ_PER_BLOCK, axis=1) if fold_halves else acc_o


def _swap_major_sublane(x):
    n = x.shape[1]
    y = pltpu.einshape("ahil->hial", x.reshape(SUBLANES, n // SUBLANES, SUBLANES, LANES))
    return y.reshape(n, SUBLANES, LANES)


def _wkv_scan_prompt_kernel(a_ref, w_ref, b_ref, k_ref, r_ref, v_ref, o_ref, sout_ref,
                            s_ref, kv_ref, vv_ref, ov_ref, *, tt, ts):
    t = pl.program_id(0)

    @pl.when(t == 0)
    def _():
        s_ref[...] = jnp.zeros_like(s_ref)

    for half in range(tt // ts):
        cols = pl.ds(half * ts, ts)

        def gather_t(ref, base, both_halves):
            pieces = [ref[bb, pl.ds(base + (kh * SUBLANES if both_halves else 0), SUBLANES), cols]
                      for kh in range(2) for bb in range(8)]
            return jnp.concatenate(pieces, axis=0).T

        for q, ref in enumerate((a_ref, w_ref, b_ref, k_ref, r_ref)):
            def kgroup(gi, c, ref=ref, q=q):
                slabs = [gather_t(ref, pl.multiple_of((gi * SUBLANES + kp) * 16, 16), True)
                         for kp in range(SUBLANES)]
                kv_ref[q, :, pl.ds(pl.multiple_of(gi * SUBLANES, SUBLANES), SUBLANES), :] = \
                    _swap_major_sublane(jnp.stack(slabs))
                return c
            lax.fori_loop(0, KEY_HALF // SUBLANES, kgroup, 0)

        def vgroup(gi, c):
            slabs = [gather_t(v_ref, pl.multiple_of((gi * SUBLANES + vi) * SUBLANES, SUBLANES), False)
                     for vi in range(SUBLANES)]
            vv_ref[:, pl.ds(pl.multiple_of(gi * SUBLANES, SUBLANES), SUBLANES), :] = \
                _swap_major_sublane(jnp.stack(slabs))
            return c
        lax.fori_loop(0, HEAD_DIM // SUBLANES, vgroup, 0)

        def step(i, c):
            row = lambda q, kp: kv_ref[q, i, pl.ds(kp, 1), :]
            ov_ref[i] = _wkv_step(s_ref, row, vv_ref[i], n_key_rows=KEY_HALF, fold_halves=True)
            return c
        lax.fori_loop(0, ts, step, 0)

        def ogroup(gi, c):
            g0 = pl.multiple_of(gi * SUBLANES, SUBLANES)
            x = ov_ref[:, pl.ds(g0, SUBLANES), :]
            x = pltpu.einshape("hial->ahil", x.reshape(ts // SUBLANES, SUBLANES, SUBLANES, LANES))
            x = x.reshape(SUBLANES, ts, LANES)
            for vi in range(SUBLANES):
                xt = x[vi].T
                for bb in range(8):
                    o_ref[bb, pl.ds(pl.multiple_of((gi * SUBLANES + vi) * SUBLANES, SUBLANES), SUBLANES), cols] = \
                        xt[bb * SUBLANES:(bb + 1) * SUBLANES]
            return c
        lax.fori_loop(0, HEAD_DIM // SUBLANES, ogroup, 0)

    @pl.when(t == pl.num_programs(0) - 1)
    def _():
        sout_ref[...] = s_ref[...]


def _wkv_scan_prompt(a, w, b, k, r, v, *, tt, ts):
    t_len = a.shape[2]
    spec = pl.BlockSpec((8, RWKV_WIDTH, tt), lambda ti: (0, 0, ti))
    sspec = pl.BlockSpec((KEY_HALF, HEAD_DIM, LANES), lambda ti: (0, 0, 0))
    return pl.pallas_call(
        functools.partial(_wkv_scan_prompt_kernel, tt=tt, ts=ts),
        grid=(t_len // tt,),
        in_specs=[spec] * 6,
        out_specs=[spec, sspec],
        out_shape=[jax.ShapeDtypeStruct(a.shape, F32),
                   jax.ShapeDtypeStruct((KEY_HALF, HEAD_DIM, LANES), F32)],
        scratch_shapes=[pltpu.VMEM((KEY_HALF, HEAD_DIM, LANES), F32),
                        pltpu.VMEM((5, ts, KEY_HALF, LANES), F32),
                        pltpu.VMEM((ts, HEAD_DIM, LANES), F32),
                        pltpu.VMEM((ts, HEAD_DIM, LANES), F32)],
        compiler_params=pltpu.CompilerParams(
            dimension_semantics=("arbitrary",), vmem_limit_bytes=VMEM_LIMIT),
        name="wkv_scan_prompt",
    )(a, w, b, k, r, v)


def _wkv_scan_sample_kernel(a_ref, w_ref, b_ref, k_ref, r_ref, v_ref, s0_ref, o_ref, sout_ref, s_ref,
                            *, t_len):
    h = pl.program_id(0)
    s_ref[...] = s0_ref[0]
    refs = (a_ref, w_ref, b_ref, k_ref, r_ref)
    for i in range(t_len):
        row = lambda q, kp, i=i: refs[q][i, pl.ds(kp * HEADS + h, 1), :]
        vv = v_ref[i, pl.ds(h, HEAD_DIM, stride=HEADS), :]
        o_ref[i, pl.ds(h, HEAD_DIM, stride=HEADS), :] = _wkv_step(
            s_ref, row, vv, n_key_rows=HEAD_DIM, fold_halves=False)
    sout_ref[0] = s_ref[...]


def _wkv_scan_sample(a, w, b, k, r, v, s0):
    t_len = a.shape[0]
    spec = pl.BlockSpec(a.shape, lambda h: (0, 0, 0))
    sspec = pl.BlockSpec((1, HEAD_DIM, HEAD_DIM, LANES), lambda h: (h, 0, 0, 0))
    return pl.pallas_call(
        functools.partial(_wkv_scan_sample_kernel, t_len=t_len),
        grid=(HEADS,),
        in_specs=[spec] * 6 + [sspec],
        out_specs=[spec, sspec],
        out_shape=[jax.ShapeDtypeStruct(a.shape, F32), jax.ShapeDtypeStruct(s0.shape, F32)],
        scratch_shapes=[pltpu.VMEM((HEAD_DIM, HEAD_DIM, LANES), F32)],
        compiler_params=pltpu.CompilerParams(
            dimension_semantics=("arbitrary",), vmem_limit_bytes=VMEM_LIMIT),
        name="wkv_scan_sample",
    )(a, w, b, k, r, v, s0)


def _post_kernel(o_ref, bon_ref, g_ref, pa_ref, sgb_ref, x_ref, lnw_ref, lnb_ref, ones_ref, wb_ref,
                 wout_ref, nffn_ref, wrh_ref, wrl_ref, br_ref, x1_ref, h2_ref, comb_ref):
    ones_bd = ones_ref[...]
    o = o_ref[0]
    mean = _seg_sum_rows(ones_bd, o) * (1.0 / HEAD_DIM)
    d = o - mean
    var = _seg_sum_rows(ones_bd, d * d) * (1.0 / HEAD_DIM)
    on = d * lax.rsqrt(var + GN_EPS) * lnw_ref[...] + lnb_ref[...]
    yb = ((on + bon_ref[0]) * g_ref[0]).astype(BF16)
    mb = lax.dot_general(yb, wb_ref[...], (((0,), (0,)), ((), ())), preferred_element_type=F32)
    merged = pa_ref[...] + sgb_ref[...] * mb
    x1 = x_ref[...] + _dot(merged.astype(BF16), wout_ref[...])
    x1_ref[...] = x1
    h2 = _rmsnorm(x1, nffn_ref[...])
    h2_ref[...] = h2.astype(BF16)

    h_hi, h_lo = _split_bf16(h2)
    logits = (_dot(h_hi, wrh_ref[...]) + _dot(h_lo, wrh_ref[...]) + _dot(h_hi, wrl_ref[...])
              + br_ref[...])
    ln = lax.broadcasted_iota(jnp.int32, logits.shape, 1)
    is_group = (ln >= N_EXPERTS) & (ln < N_EXPERTS + N_GROUPS)
    gl = jnp.where(is_group, logits, NEG_BIG)
    gmax = jnp.max(gl, axis=-1, keepdims=True)
    gsel = jnp.min(jnp.where(gl == gmax, ln, ROUTER_LANES), axis=-1, keepdims=True) - N_EXPERTS
    den = jnp.sum(jnp.where(is_group, jnp.exp(gl - gmax), 0.0), axis=-1, keepdims=True)
    pg = 1.0 / den
    in_group = (ln < N_EXPERTS) & ((ln // EXPERTS_PER_GROUP) == gsel)
    el = jnp.where(in_group, logits, NEG_BIG)
    m1 = jnp.max(el, axis=-1, keepdims=True)
    i1 = jnp.min(jnp.where(el == m1, ln, ROUTER_LANES), axis=-1, keepdims=True)
    el2 = jnp.where(ln == i1, NEG_BIG, el)
    m2 = jnp.max(el2, axis=-1, keepdims=True)
    i2 = jnp.min(jnp.where(el2 == m2, ln, ROUTER_LANES), axis=-1, keepdims=True)
    e2 = jnp.exp(m2 - m1)
    p1 = 1.0 / (1.0 + e2)
    p2 = e2 / (1.0 + e2)
    comb_ref[...] = jnp.where(ln == i1, p1 * pg, 0.0) + jnp.where(ln == i2, p2 * pg, 0.0)


def _post(o, bon, g, pa, sgb, x2d, wts, *, tm, cm_index):
    n = x2d.shape[0]
    row = lambda i: (i, 0)
    tok = lambda width: pl.BlockSpec((tm, width), row)
    cm = pl.BlockSpec((1, RWKV_WIDTH, tm), cm_index)
    full = lambda arr: pl.BlockSpec(arr.shape, lambda i: (0, 0))
    return pl.pallas_call(
        _post_kernel,
        grid=(n // tm,),
        in_specs=[cm] * 3 + [tok(D_MODEL)] * 3 + [full(w) for w in wts],
        out_specs=[tok(D_MODEL), tok(D_MODEL), tok(ROUTER_LANES)],
        out_shape=[jax.ShapeDtypeStruct((n, D_MODEL), F32),
                   jax.ShapeDtypeStruct((n, D_MODEL), BF16),
                   jax.ShapeDtypeStruct((n, ROUTER_LANES), F32)],
        compiler_params=pltpu.CompilerParams(
            dimension_semantics=("parallel",), vmem_limit_bytes=VMEM_LIMIT),
        name="post",
    )(o, bon, g, pa, sgb, x2d, *wts)


def _moe_tail_kernel(h2_ref, comb_ref, x1_ref, p_ref, wg_ref, wu_ref, wd_ref, nple_ref, wpg_ref,
                     wpp_ref, nfin_ref, y_ref, acc_ref):
    e = pl.program_id(1)

    @pl.when(e == 0)
    def _():
        acc_ref[...] = jnp.zeros_like(acc_ref)

    x = h2_ref[...]
    hid = jax.nn.silu(_dot(x, wg_ref[0])) * _dot(x, wu_ref[0])
    ye = _dot(hid.astype(BF16), wd_ref[0])
    comb = comb_ref[...]
    ln = lax.broadcasted_iota(jnp.int32, comb.shape, 1)
    ce = jnp.sum(jnp.where(ln == e, comb, 0.0), axis=-1, keepdims=True)
    acc_ref[...] += ce * ye

    @pl.when(e == pl.num_programs(1) - 1)
    def _():
        x2 = x1_ref[...] + acc_ref[...]
        h3 = _rmsnorm(x2, nple_ref[...]).astype(BF16)
        ple = jax.nn.sigmoid(_dot(h3, wpg_ref[...])) * _dot(p_ref[...].astype(BF16), wpp_ref[...])
        y_ref[...] = _rmsnorm(x2 + ple, nfin_ref[...])


def _moe_tail(h2, comb, x1, p2d, wg, wu, wd, nple, wpg, wpp, nfin, *, tm):
    n = h2.shape[0]
    row = lambda i, e: (i, 0)
    tok = lambda width: pl.BlockSpec((tm, width), row)
    full = lambda arr: pl.BlockSpec(arr.shape, lambda i, e: (0, 0))
    return pl.pallas_call(
        _moe_tail_kernel,
        grid=(n // tm, N_EXPERTS),
        in_specs=[tok(D_MODEL), tok(ROUTER_LANES), tok(D_MODEL), tok(PLE_DIM),
                  pl.BlockSpec((1, D_MODEL, D_EXPERT), lambda i, e: (e, 0, 0)),
                  pl.BlockSpec((1, D_MODEL, D_EXPERT), lambda i, e: (e, 0, 0)),
                  pl.BlockSpec((1, D_EXPERT, D_MODEL), lambda i, e: (e, 0, 0)),
                  full(nple), full(wpg), full(wpp), full(nfin)],
        out_specs=tok(D_MODEL),
        out_shape=jax.ShapeDtypeStruct((n, D_MODEL), F32),
        scratch_shapes=[pltpu.VMEM((tm, D_MODEL), F32)],
        compiler_params=pltpu.CompilerParams(
            dimension_semantics=("parallel", "arbitrary"), vmem_limit_bytes=VMEM_LIMIT),
        name="moe_tail",
    )(h2, comb, x1, p2d, wg, wu, wd, nple, wpg, wpp, nfin)


def kernel(x_prompt, x_sample, state_pool, state_shift, state_wkv, p_prompt, p_sample, norm_mix, w_in, pool_mix, pool_scale, w_branch_a, shift_mu, decay_w0, decay_w2, iclr_a0, iclr_a2, gate_g2, k_k, k_a, r_k, ln_x_w, ln_x_b, w_branch_b, w_out, norm_ffn, w_route_group, b_route_group, w_route_expert, b_route_expert, expert_gate, expert_up, expert_down, norm_ple, w_ple_gate, w_ple_proj, norm_final):
    l = 0
    bsz, seq, _ = x_prompt.shape
    dbsz, dseq, _ = x_sample.shape
    row = lambda vec: vec.reshape(1, -1).astype(F32)
    col = lambda vec: vec.reshape(-1, 1).astype(F32)
    perm = CHANNEL_PERM

    head_of = perm // HEAD_DIM
    ones_bd = (head_of[:, None] == head_of[None, :]).astype(np.float32)
    ones_bd = jnp.asarray(ones_bd, BF16)
    o1 = POOL_WIDTH
    o2 = o1 + SHIFT_WIDTH
    zperm = np.concatenate([perm, RWKV_WIDTH + perm, 2 * RWKV_WIDTH + perm,
                            np.arange(3 * RWKV_WIDTH, SHIFT_WIDTH)])
    w_zt = w_in[l][:, o1:o2].T[zperm].astype(BF16)
    zeros_lora = jnp.zeros((RWKV_WIDTH, LORA_PAIR // 2), F32)
    w2t_pad = jnp.concatenate([decay_w2[l].T[perm], zeros_lora], axis=1).astype(BF16)
    a2t_pad = jnp.concatenate([zeros_lora, iclr_a2[l].T[perm]], axis=1).astype(BF16)
    w_router = jnp.concatenate(
        [w_route_expert[l], w_route_group[l],
         jnp.zeros((D_MODEL, ROUTER_LANES - N_EXPERTS - N_GROUPS), F32)], axis=1)
    wr_hi = w_router.astype(BF16)
    wr_lo = (w_router - wr_hi.astype(F32)).astype(BF16)
    b_router = jnp.concatenate(
        [b_route_expert[l], b_route_group[l],
         jnp.zeros((ROUTER_LANES - N_EXPERTS - N_GROUPS,), F32)]).reshape(1, -1)
    prep_w = [row(norm_mix[l]), w_in[l][:, :o1].astype(BF16), w_zt, w_in[l][:, o2:].astype(BF16),
              col(shift_mu[l][zperm]), col(decay_w0[l][perm]), w2t_pad, col(iclr_a0[l][perm]), a2t_pad,
              gate_g2[l].T[perm].astype(BF16), col(k_k[l][perm]), col(k_a[l][perm]),
              col(r_k[l].reshape(-1)[perm]), ones_bd, pool_mix[l].astype(BF16), row(pool_scale[l]),
              w_branch_a[l].astype(BF16)]
    post_w = [col(ln_x_w[l][perm]), col(ln_x_b[l][perm]), ones_bd, w_branch_b[l][perm].astype(BF16),
              w_out[l].astype(BF16), row(norm_ffn[l]), wr_hi, wr_lo, b_router]
    moe_w = [expert_gate[l].astype(BF16), expert_up[l].astype(BF16), expert_down[l].astype(BF16),
             row(norm_ple[l]), w_ple_gate[l].astype(BF16), w_ple_proj[l].astype(BF16), row(norm_final)]

    tm_p = 256
    tiles_p = seq // tm_p
    x_p = x_prompt.reshape(bsz * seq, D_MODEL)
    outs = _mix_prep(x_p, jnp.zeros((bsz * SHIFT_WIDTH, LANES), F32),
                     jnp.zeros((bsz * 16, POOL_WIDTH), F32), prep_w,
                     nb=bsz, tiles=tiles_p, tm=tm_p, s=1, pos0=0, cm_index=lambda b, t: (b, 0, t))
    u_p, zl_p, a, w, b, k, r, v, g, bon, pa, sgb = outs
    o_p, s_p = _wkv_scan_prompt(a, w, b, k, r, v, tt=128, ts=64)
    x1, h2, comb = _post(o_p, bon, g, pa, sgb, x_p, post_w, tm=tm_p,
                         cm_index=lambda i: (i // tiles_p, 0, i % tiles_p))
    y_p = _moe_tail(h2, comb, x1, p_prompt[l].reshape(bsz * seq, PLE_DIM), *moe_w, tm=1024)
    wkv_p = s_p.reshape(KEY_HALF, KEY_HALF, 2, 2, bsz, HEADS).transpose(4, 5, 2, 1, 3, 0)
    wkv_p = wkv_p.reshape(bsz, HEADS, HEAD_DIM, HEAD_DIM)

    n_s = dbsz * dseq
    x_s = x_sample.transpose(1, 0, 2).reshape(n_s, D_MODEL)
    p_s = p_sample[l].transpose(1, 0, 2).reshape(n_s, PLE_DIM)
    uc0_s = jnp.concatenate(
        [jnp.zeros((dbsz, POOL_WIDTH), F32),
         state_pool[l].transpose(1, 0, 2).reshape(POOL_STATE * dbsz, POOL_WIDTH)], axis=0)
    zc0_s = state_shift[l].T[zperm]
    outs = _mix_prep(x_s, zc0_s, uc0_s, prep_w, nb=1, tiles=dseq, tm=dbsz, s=dbsz, pos0=PAST_LEN,
                     cm_index=lambda b, t: (t, 0, 0))
    u_s, zl_s, a, w, b, k, r, v, g, bon, pa, sgb = outs
    s0_s = state_wkv[l].reshape(dbsz, HEADS, 2, KEY_HALF, 2, KEY_HALF).transpose(1, 5, 4, 3, 2, 0)
    s0_s = s0_s.reshape(HEADS, HEAD_DIM, HEAD_DIM, dbsz)
    o_s, s_s = _wkv_scan_sample(a, w, b, k, r, v, s0_s)
    x1, h2, comb = _post(o_s, bon, g, pa, sgb, x_s, post_w, tm=dbsz, cm_index=lambda i: (i, 0, 0))
    y_s = _moe_tail(h2, comb, x1, p_s, *moe_w, tm=n_s)
    wkv_s = s_s.reshape(HEADS, KEY_HALF, 2, KEY_HALF, 2, dbsz).transpose(5, 0, 4, 3, 2, 1)
    wkv_s = wkv_s.reshape(dbsz, HEADS, HEAD_DIM, HEAD_DIM)

    inv_z = np.argsort(zperm)
    y_prompt = y_p.reshape(bsz, seq, D_MODEL)
    y_sample = y_s.reshape(dseq, dbsz, D_MODEL).transpose(1, 0, 2)
    pool_prompt = u_p.reshape(bsz, seq, POOL_WIDTH)[:, seq - POOL_STATE:]
    u_s_bt = u_s.reshape(dseq, dbsz, POOL_WIDTH).transpose(1, 0, 2)
    pool_sample = jnp.concatenate([state_pool[l][:, dseq:], u_s_bt], axis=1)
    shift_prompt = zl_p[:, :, LANES - 1][:, inv_z]
    shift_sample = zl_s[0].T[:, inv_z]
    return (y_prompt, y_sample, pool_prompt[None], shift_prompt[None], wkv_p[None],
            pool_sample[None], shift_sample[None], wkv_s[None])
```

```python
import functools

import jax
import jax.numpy as jnp
import numpy as np
from jax import lax
from jax.experimental import pallas as pl
from jax.experimental.pallas import tpu as pltpu

F32 = jnp.float32
BF16 = jnp.bfloat16

D_MODEL = 1024
PLE_DIM = 256
POOL_WIDTH = 512
POOL_WINDOWS = (2, 4, 8, 16)
POOL_GROUP_DIM = 128
POOL_STATE = 15
RWKV_WIDTH = 512
HEAD_DIM = 64
HEADS = 8
LORA_PAIR = 128
GATE_LORA = 128
SHIFT_WIDTH = 3 * RWKV_WIDTH + LORA_PAIR + GATE_LORA
IN_WIDTH = POOL_WIDTH + SHIFT_WIDTH + 2 * D_MODEL
N_GROUPS = 4
EXPERTS_PER_GROUP = 8
N_EXPERTS = 32
D_EXPERT = 256
RMS_EPS = 1e-6
GN_EPS = 64e-5
PAST_LEN = 16384

LANES = 128
SUBLANES = 8
SEQ_PER_BLOCK = 64
KEY_HALF = HEAD_DIM // 2
ROUTER_LANES = 128
EXPERT_TILE = 128
NEG_BIG = -1e30
VMEM_LIMIT = 56 * 1024 * 1024

_J = np.arange(RWKV_WIDTH)
CHANNEL_PERM = (_J % 8) * HEAD_DIM + ((_J // 8) % 2) * KEY_HALF + _J // 16


def _dot(a, b):
    return jnp.dot(a, b, preferred_element_type=F32)


def _split_bf16(x):
    hi = x.astype(BF16)
    return hi, (x - hi.astype(F32)).astype(BF16)


def _seg_sum_rows(ones_bd, x):
    hi, lo = _split_bf16(x)
    return _dot(ones_bd, hi) + _dot(ones_bd, lo)


def _rmsnorm(x, g):
    return x * lax.rsqrt(jnp.mean(x * x, axis=-1, keepdims=True) + RMS_EPS) * g


def _softplus(y):
    return jnp.maximum(y, 0.0) + jnp.log1p(jnp.exp(-jnp.abs(y)))


def _mix_prep_kernel(x_ref, zc0_ref, uc0_ref, nmix_ref, wu_ref, wzt_ref, wgab_ref, mu_ref, w0_ref,
                     w2t_ref, a0_ref, a2t_ref, g2t_ref, kk_ref, ka_ref, rk_ref, ones_ref, mix_ref,
                     pscale_ref, wa_ref,
                     u_ref, zl_ref, a_ref, w_ref, b_ref, k_ref, r_ref, v_ref, g_ref, bon_ref,
                     pa_ref, sgb_ref, zc, uext, *, tm, s, pos0):
    t = pl.program_id(1)
    up = 16 * s

    @pl.when(t == 0)
    def _():
        zc[...] = zc0_ref[...]
        uext[0:up] = uc0_ref[...]

    x = x_ref[...]
    h = _rmsnorm(x, nmix_ref[...]).astype(BF16)
    u = _dot(h, wu_ref[...])
    gab = _dot(h, wgab_ref[...])
    u_ref[...] = u
    sgb_ref[...] = jax.nn.sigmoid(gab[:, D_MODEL:])
    uext[up:up + tm] = u

    z_t = lax.dot_general(wzt_ref[...], h, (((1,), (1,)), ((), ())), preferred_element_type=F32)
    if s == 1:
        lane = lax.broadcasted_iota(jnp.int32, z_t.shape, 1)
        zprev = jnp.where(lane == 0, zc[:, LANES - 1:LANES], pltpu.roll(z_t, 1, axis=1))
    else:
        zprev = zc[...]
    zc[...] = z_t[:, tm - LANES:tm]
    zl_ref[0] = z_t[:, tm - LANES:tm]
    zm = z_t + (zprev - z_t) * mu_ref[...]
    r = zm[0:RWKV_WIDTH]
    k = zm[RWKV_WIDTH:2 * RWKV_WIDTH]
    v = zm[2 * RWKV_WIDTH:3 * RWKV_WIDTH]
    lora_in = zm[3 * RWKV_WIDTH:3 * RWKV_WIDTH + LORA_PAIR]
    gd = zm[3 * RWKV_WIDTH + LORA_PAIR:SHIFT_WIDTH]
    dw = _dot(w2t_ref[...], jnp.tanh(lora_in).astype(BF16))
    da = _dot(a2t_ref[...], lora_in.astype(BF16))
    w_log = -_softplus(-(w0_ref[...] + dw)) - 0.5
    a = jax.nn.sigmoid(a0_ref[...] + da)
    ones_bd = ones_ref[...]
    kk = k * kk_ref[...]
    kk = kk / jnp.maximum(jnp.sqrt(_seg_sum_rows(ones_bd, kk * kk)), 1e-12)
    k2 = k * (1.0 + (a - 1.0) * ka_ref[...])
    a_ref[0] = -kk
    w_ref[0] = jnp.exp(-jnp.exp(w_log))
    b_ref[0] = kk * a
    k_ref[0] = k2
    r_ref[0] = r
    v_ref[0] = v
    g_ref[0] = _dot(g2t_ref[...], jax.nn.sigmoid(gd).astype(BF16))
    bon_ref[0] = _seg_sum_rows(ones_bd, r * k2 * rk_ref[...]) * v

    rows = lax.broadcasted_iota(jnp.int32, (tm, POOL_GROUP_DIM), 0)
    if s > 1:
        rows = rows // s
    pos = pos0 + t * (tm // s) + rows
    ys = []
    for gi, wnd in enumerate(POOL_WINDOWS):
        lanes = slice(gi * POOL_GROUP_DIM, (gi + 1) * POOL_GROUP_DIM)
        cur = uext[pl.ds(up, tm), lanes]
        acc = cur
        for j in range(1, wnd):
            acc = acc + uext[pl.ds(up - j * s, tm), lanes]
        cnt = jnp.minimum(pos + 1, wnd).astype(F32)
        pooled = acc / cnt - cur
        ys.append(_dot(pooled.astype(BF16), mix_ref[gi]))
    y = jnp.concatenate(ys, axis=-1) * pscale_ref[...]
    pa_ref[...] = jax.nn.sigmoid(gab[:, :D_MODEL]) * _dot(y.astype(BF16), wa_ref[...])

    uext[0:up] = uext[tm:tm + up]


def _mix_prep(x2d, zc0, uc0, wts, *, nb, tiles, tm, s, pos0, cm_index):
    n = x2d.shape[0]
    up = 16 * s
    row = lambda b, t: (b * tiles + t, 0)
    full = lambda arr: pl.BlockSpec(arr.shape, lambda b, t: (0,) * arr.ndim)
    in_specs = [
        pl.BlockSpec((tm, D_MODEL), row),
        pl.BlockSpec((SHIFT_WIDTH, LANES), lambda b, t: (b, 0)),
        pl.BlockSpec((up, POOL_WIDTH), lambda b, t: (b, 0)),
    ] + [full(w) for w in wts]
    tok = lambda width: pl.BlockSpec((tm, width), row)
    cm_shape = (nb, RWKV_WIDTH, tiles * tm) if s == 1 else (tiles, RWKV_WIDTH, tm)
    cm = pl.BlockSpec((1, RWKV_WIDTH, tm), cm_index)
    out_specs = [tok(POOL_WIDTH), pl.BlockSpec((1, SHIFT_WIDTH, LANES), lambda b, t: (b, 0, 0))] \
        + [cm] * 8 + [tok(D_MODEL)] * 2
    out_shape = [jax.ShapeDtypeStruct((n, POOL_WIDTH), F32),
                 jax.ShapeDtypeStruct((nb, SHIFT_WIDTH, LANES), F32)] \
        + [jax.ShapeDtypeStruct(cm_shape, F32)] * 8 \
        + [jax.ShapeDtypeStruct((n, D_MODEL), F32)] * 2
    return pl.pallas_call(
        functools.partial(_mix_prep_kernel, tm=tm, s=s, pos0=pos0),
        grid=(nb, tiles),
        in_specs=in_specs,
        out_specs=out_specs,
        out_shape=out_shape,
        scratch_shapes=[pltpu.VMEM((SHIFT_WIDTH, LANES), F32),
                        pltpu.VMEM((up + tm, POOL_WIDTH), F32)],
        compiler_params=pltpu.CompilerParams(
            dimension_semantics=("arbitrary", "arbitrary"), vmem_limit_bytes=VMEM_LIMIT),
        name="mix_prep",
    )(x2d, zc0, uc0, *wts)


def _wkv_step(s_ref, row, vv, *, n_key_rows, fold_halves):
    acc = jnp.zeros(vv.shape, F32)
    for kp in range(n_key_rows):
        acc = acc + s_ref[kp] * row(0, kp)
    sa = acc + pltpu.roll(acc, SEQ_PER_BLOCK, axis=1) if fold_halves else acc
    acc_o = jnp.zeros(vv.shape, F32)
    for kp in range(n_key_rows):
        sn = s_ref[kp] * row(1, kp) + sa * row(2, kp) + vv * row(3, kp)
        s_ref[kp] = sn
        acc_o = acc_o + sn * row(4, kp)
    return acc_o + pltpu.roll(acc_o, SEQ_PER_BLOCK, axis=1) if fold_halves else acc_o


def _swap_major_sublane(x):
    return jnp.swapaxes(x, 0, 1)


def _wkv_scan_prompt_kernel(a_ref, w_ref, b_ref, k_ref, r_ref, v_ref, o_ref, sout_ref,
                            s_ref, kv_ref, vv_ref, ov_ref, *, tt, ts):
    t = pl.program_id(0)

    @pl.when(t == 0)
    def _():
        s_ref[...] = jnp.zeros_like(s_ref)

    for half in range(tt // ts):
        cols = pl.ds(half * ts, ts)

        def gather_t(ref, base, both_halves):
            pieces = [ref[bb, pl.ds(base + (kh * SUBLANES if both_halves else 0), SUBLANES), cols]
                      for kh in range(2) for bb in range(8)]
            return jnp.concatenate(pieces, axis=0).T

        for q, ref in enumerate((a_ref, w_ref, b_ref, k_ref, r_ref)):
            def kgroup(gi, c, ref=ref, q=q):
                slabs = [gather_t(ref, pl.multiple_of((gi * SUBLANES + kp) * 16, 16), True)
                         for kp in range(SUBLANES)]
                kv_ref[q, :, pl.ds(pl.multiple_of(gi * SUBLANES, SUBLANES), SUBLANES), :] = \
                    _swap_major_sublane(jnp.stack(slabs))
                return c
            lax.fori_loop(0, KEY_HALF // SUBLANES, kgroup, 0)

        def vgroup(gi, c):
            slabs = [gather_t(v_ref, pl.multiple_of((gi * SUBLANES + vi) * SUBLANES, SUBLANES), False)
                     for vi in range(SUBLANES)]
            vv_ref[:, pl.ds(pl.multiple_of(gi * SUBLANES, SUBLANES), SUBLANES), :] = \
                _swap_major_sublane(jnp.stack(slabs))
            return c
        lax.fori_loop(0, HEAD_DIM // SUBLANES, vgroup, 0)

        def step(i, c):
            row = lambda q, kp: kv_ref[q, i, pl.ds(kp, 1), :]
            ov_ref[i] = _wkv_step(s_ref, row, vv_ref[i], n_key_rows=KEY_HALF, fold_halves=True)
            return c
        lax.fori_loop(0, ts, step, 0)

        def ogroup(gi, c):
            g0 = pl.multiple_of(gi * SUBLANES, SUBLANES)
            x = _swap_major_sublane(ov_ref[:, pl.ds(g0, SUBLANES), :])
            for vi in range(SUBLANES):
                xt = x[vi].T
                for bb in range(8):
                    o_ref[bb, pl.ds(pl.multiple_of((gi * SUBLANES + vi) * SUBLANES, SUBLANES), SUBLANES), cols] = \
                        xt[bb * SUBLANES:(bb + 1) * SUBLANES]
            return c
        lax.fori_loop(0, HEAD_DIM // SUBLANES, ogroup, 0)

    @pl.when(t == pl.num_programs(0) - 1)
    def _():
        sout_ref[...] = s_ref[...]


def _wkv_scan_prompt(a, w, b, k, r, v, *, tt, ts):
    t_len = a.shape[2]
    spec = pl.BlockSpec((8, RWKV_WIDTH, tt), lambda ti: (0, 0, ti))
    sspec = pl.BlockSpec((KEY_HALF, HEAD_DIM, LANES), lambda ti: (0, 0, 0))
    return pl.pallas_call(
        functools.partial(_wkv_scan_prompt_kernel, tt=tt, ts=ts),
        grid=(t_len // tt,),
        in_specs=[spec] * 6,
        out_specs=[spec, sspec],
        out_shape=[jax.ShapeDtypeStruct(a.shape, F32),
                   jax.ShapeDtypeStruct((KEY_HALF, HEAD_DIM, LANES), F32)],
        scratch_shapes=[pltpu.VMEM((KEY_HALF, HEAD_DIM, LANES), F32),
                        pltpu.VMEM((5, ts, KEY_HALF, LANES), F32),
                        pltpu.VMEM((ts, HEAD_DIM, LANES), F32),
                        pltpu.VMEM((ts, HEAD_DIM, LANES), F32)],
        compiler_params=pltpu.CompilerParams(
            dimension_semantics=("arbitrary",), vmem_limit_bytes=VMEM_LIMIT),
        name="wkv_scan_prompt",
    )(a, w, b, k, r, v)


def _wkv_scan_sample_kernel(a_ref, w_ref, b_ref, k_ref, r_ref, v_ref, s0_ref, o_ref, sout_ref, s_ref,
                            *, t_len):
    h = pl.program_id(0)
    s_ref[...] = s0_ref[0]
    refs = (a_ref, w_ref, b_ref, k_ref, r_ref)
    for i in range(t_len):
        row = lambda q, kp, i=i: refs[q][i, pl.ds(kp * HEADS + h, 1), :]
        vv = v_ref[i, pl.ds(h, HEAD_DIM, stride=HEADS), :]
        o_ref[i, pl.ds(h, HEAD_DIM, stride=HEADS), :] = _wkv_step(
            s_ref, row, vv, n_key_rows=HEAD_DIM, fold_halves=False)
    sout_ref[0] = s_ref[...]


def _wkv_scan_sample(a, w, b, k, r, v, s0):
    t_len = a.shape[0]
    spec = pl.BlockSpec(a.shape, lambda h: (0, 0, 0))
    sspec = pl.BlockSpec((1, HEAD_DIM, HEAD_DIM, LANES), lambda h: (h, 0, 0, 0))
    return pl.pallas_call(
        functools.partial(_wkv_scan_sample_kernel, t_len=t_len),
        grid=(HEADS,),
        in_specs=[spec] * 6 + [sspec],
        out_specs=[spec, sspec],
        out_shape=[jax.ShapeDtypeStruct(a.shape, F32), jax.ShapeDtypeStruct(s0.shape, F32)],
        scratch_shapes=[pltpu.VMEM((HEAD_DIM, HEAD_DIM, LANES), F32)],
        compiler_params=pltpu.CompilerParams(
            dimension_semantics=("arbitrary",), vmem_limit_bytes=VMEM_LIMIT),
        name="wkv_scan_sample",
    )(a, w, b, k, r, v, s0)


def _post_kernel(o_ref, bon_ref, g_ref, pa_ref, sgb_ref, x_ref, lnw_ref, lnb_ref, ones_ref, wb_ref,
                 wout_ref, nffn_ref, wrh_ref, wrl_ref, br_ref, x1_ref, h2_ref, rt_ref, cnt_ref, carry):
    @pl.when(pl.program_id(0) == 0)
    def _():
        carry[...] = jnp.zeros_like(carry)

    ones_bd = ones_ref[...]
    o = o_ref[0]
    mean = _seg_sum_rows(ones_bd, o) * (1.0 / HEAD_DIM)
    d = o - mean
    var = _seg_sum_rows(ones_bd, d * d) * (1.0 / HEAD_DIM)
    on = d * lax.rsqrt(var + GN_EPS) * lnw_ref[...] + lnb_ref[...]
    yb = ((on + bon_ref[0]) * g_ref[0]).astype(BF16)
    mb = lax.dot_general(yb, wb_ref[...], (((0,), (0,)), ((), ())), preferred_element_type=F32)
    merged = pa_ref[...] + sgb_ref[...] * mb
    x1 = x_ref[...] + _dot(merged.astype(BF16), wout_ref[...])
    x1_ref[...] = x1
    h2 = _rmsnorm(x1, nffn_ref[...])
    h2_ref[...] = h2

    h_hi, h_lo = _split_bf16(h2)
    logits = (_dot(h_hi, wrh_ref[...]) + _dot(h_lo, wrh_ref[...]) + _dot(h_hi, wrl_ref[...])
              + br_ref[...])
    ln = lax.broadcasted_iota(jnp.int32, logits.shape, 1)
    is_group = (ln >= N_EXPERTS) & (ln < N_EXPERTS + N_GROUPS)
    gl = jnp.where(is_group, logits, NEG_BIG)
    gmax = jnp.max(gl, axis=-1, keepdims=True)
    gsel = jnp.min(jnp.where(gl == gmax, ln, ROUTER_LANES), axis=-1, keepdims=True) - N_EXPERTS
    den = jnp.sum(jnp.where(is_group, jnp.exp(gl - gmax), 0.0), axis=-1, keepdims=True)
    pg = 1.0 / den
    in_group = (ln < N_EXPERTS) & ((ln // EXPERTS_PER_GROUP) == gsel)
    el = jnp.where(in_group, logits, NEG_BIG)
    m1 = jnp.max(el, axis=-1, keepdims=True)
    i1 = jnp.min(jnp.where(el == m1, ln, ROUTER_LANES), axis=-1, keepdims=True)
    el2 = jnp.where(ln == i1, NEG_BIG, el)
    m2 = jnp.max(el2, axis=-1, keepdims=True)
    i2 = jnp.min(jnp.where(el2 == m2, ln, ROUTER_LANES), axis=-1, keepdims=True)
    e2 = jnp.exp(m2 - m1)
    p1 = 1.0 / (1.0 + e2)
    p2 = e2 / (1.0 + e2)

    tm = logits.shape[0]
    sel = (ln == i1) | (ln == i2)
    tri = (lax.broadcasted_iota(jnp.int32, (tm, tm), 1)
           < lax.broadcasted_iota(jnp.int32, (tm, tm), 0)).astype(BF16)
    before = carry[0:1, :] + _dot(tri, sel.astype(BF16))
    r1 = jnp.sum(jnp.where(ln == i1, before, 0.0), axis=-1, keepdims=True)
    r2 = jnp.sum(jnp.where(ln == i2, before, 0.0), axis=-1, keepdims=True)
    total = carry[0:1, :] + jnp.sum(sel.astype(F32), axis=0, keepdims=True)
    carry[...] = jnp.broadcast_to(total, carry.shape)
    cnt_ref[...] = jnp.broadcast_to(total, cnt_ref.shape)
    fields = (i1.astype(F32), i2.astype(F32), r1, r2, p1 * pg, p2 * pg)
    rt = jnp.zeros(logits.shape, F32)
    for lane_idx, val in enumerate(fields):
        rt = jnp.where(ln == lane_idx, val, rt)
    rt_ref[...] = rt


def _post(o, bon, g, pa, sgb, x2d, wts, *, tm, cm_index):
    n = x2d.shape[0]
    row = lambda i: (i, 0)
    tok = lambda width: pl.BlockSpec((tm, width), row)
    cm = pl.BlockSpec((1, RWKV_WIDTH, tm), cm_index)
    full = lambda arr: pl.BlockSpec(arr.shape, lambda i: (0, 0))
    return pl.pallas_call(
        _post_kernel,
        grid=(n // tm,),
        in_specs=[cm] * 3 + [tok(D_MODEL)] * 3 + [full(w) for w in wts],
        out_specs=[tok(D_MODEL), tok(D_MODEL), tok(ROUTER_LANES),
                   pl.BlockSpec((SUBLANES, ROUTER_LANES), lambda i: (0, 0))],
        out_shape=[jax.ShapeDtypeStruct((n, D_MODEL), F32),
                   jax.ShapeDtypeStruct((n, D_MODEL), F32),
                   jax.ShapeDtypeStruct((n, ROUTER_LANES), F32),
                   jax.ShapeDtypeStruct((SUBLANES, ROUTER_LANES), F32)],
        scratch_shapes=[pltpu.VMEM((SUBLANES, ROUTER_LANES), F32)],
        compiler_params=pltpu.CompilerParams(
            dimension_semantics=("arbitrary",), vmem_limit_bytes=VMEM_LIMIT),
        name="post",
    )(o, bon, g, pa, sgb, x2d, *wts)


def _route_tables(rt, cnt, n_tiles_max):
    counts = cnt[0, :N_EXPERTS].astype(jnp.int32)
    tiles_e = (counts + EXPERT_TILE - 1) // EXPERT_TILE
    tile_end = jnp.cumsum(tiles_e)
    row0 = (tile_end - tiles_e) * EXPERT_TILE
    tile_ids = jnp.arange(n_tiles_max, dtype=jnp.int32)
    tile_expert = jnp.minimum(jnp.sum(tile_ids[:, None] >= tile_end[None, :], axis=-1),
                              N_EXPERTS - 1).astype(jnp.int32)
    e = rt[:, 0:2].astype(jnp.int32)
    r = rt[:, 2:4].astype(jnp.int32)
    pos = r + jnp.sum(jnp.where(e[..., None] == jnp.arange(N_EXPERTS), row0, 0), axis=-1)
    return pos.reshape(-1), tile_expert, tile_end[N_EXPERTS - 1:].astype(jnp.int32)


def _dispatch_kernel(pos_ref, h2_ref, xs0_ref, xs_ref, sem, *, td):
    del xs0_ref

    def issue(r, c):
        for j in range(2):
            pltpu.make_async_copy(h2_ref.at[pl.ds(r, 1)],
                                  xs_ref.at[pl.ds(pos_ref[2 * r + j], 1)], sem).start()
        return c
    lax.fori_loop(0, td, issue, 0, unroll=8)
    for j in range(2):
        pltpu.make_async_copy(h2_ref, xs_ref.at[pl.ds(0, td)], sem).wait()


def _dispatch(h2, pos, n_rows, *, td):
    n = h2.shape[0]
    return pl.pallas_call(
        functools.partial(_dispatch_kernel, td=td),
        grid=(n // td,),
        in_specs=[pl.BlockSpec((2 * td,), lambda i: (i,), memory_space=pltpu.SMEM),
                  pl.BlockSpec((td, D_MODEL), lambda i: (i, 0)),
                  pl.BlockSpec(memory_space=pl.ANY)],
        out_specs=pl.BlockSpec(memory_space=pl.ANY),
        out_shape=jax.ShapeDtypeStruct((n_rows, D_MODEL), F32),
        scratch_shapes=[pltpu.SemaphoreType.DMA(())],
        input_output_aliases={2: 0},
        compiler_params=pltpu.CompilerParams(
            dimension_semantics=("arbitrary",), vmem_limit_bytes=VMEM_LIMIT),
        name="moe_dispatch",
    )(pos, h2, jnp.zeros((n_rows, D_MODEL), F32))


def _experts_kernel(te_ref, nu_ref, xs_ref, wg_ref, wu_ref, wd_ref, ys_ref, wgu, wdn):
    i = pl.program_id(0)

    @pl.when(i < nu_ref[0])
    def _():
        @pl.when((i == 0) | (te_ref[i] != te_ref[jnp.maximum(i - 1, 0)]))
        def _():
            wgu[:, 0:D_EXPERT] = wg_ref[0].astype(BF16)
            wgu[:, D_EXPERT:2 * D_EXPERT] = wu_ref[0].astype(BF16)
            wdn[...] = wd_ref[0].astype(BF16)

        gu = _dot(xs_ref[...].astype(BF16), wgu[...])
        hid = jax.nn.silu(gu[:, 0:D_EXPERT]) * gu[:, D_EXPERT:2 * D_EXPERT]
        ys_ref[...] = _dot(hid.astype(BF16), wdn[...])

    @pl.when(i >= nu_ref[0])
    def _():
        ys_ref[...] = jnp.zeros_like(ys_ref)


def _experts(xs, tile_expert, n_used, wg, wu, wd):
    n_tiles = xs.shape[0] // EXPERT_TILE
    tile = lambda i, te, nu: (jnp.minimum(i, nu[0] - 1), 0)
    wsel = lambda i, te, nu: (te[jnp.minimum(i, nu[0] - 1)], 0, 0)
    return pl.pallas_call(
        _experts_kernel,
        grid_spec=pltpu.PrefetchScalarGridSpec(
            num_scalar_prefetch=2,
            grid=(n_tiles,),
            in_specs=[pl.BlockSpec((EXPERT_TILE, D_MODEL), tile),
                      pl.BlockSpec((1, D_MODEL, D_EXPERT), wsel),
                      pl.BlockSpec((1, D_MODEL, D_EXPERT), wsel),
                      pl.BlockSpec((1, D_EXPERT, D_MODEL), wsel)],
            out_specs=pl.BlockSpec((EXPERT_TILE, D_MODEL), lambda i, te, nu: (i, 0)),
            scratch_shapes=[pltpu.VMEM((D_MODEL, 2 * D_EXPERT), BF16),
                            pltpu.VMEM((D_EXPERT, D_MODEL), BF16)]),
        out_shape=jax.ShapeDtypeStruct(xs.shape, F32),
        compiler_params=pltpu.CompilerParams(
            dimension_semantics=("arbitrary",), vmem_limit_bytes=VMEM_LIMIT),
        name="moe_experts",
    )(tile_expert, n_used, xs, wg, wu, wd)


def _tail_kernel(pos_ref, x1_ref, rt_ref, p_ref, ys_ref, nple_ref, wpg_ref, wpp_ref, nfin_ref,
                 y_ref, ybuf, sem, *, tm):
    def issue(r, c):
        for j in range(2):
            pltpu.make_async_copy(ys_ref.at[pl.ds(pos_ref[2 * r + j], 1)],
                                  ybuf.at[j, pl.ds(r, 1)], sem).start()
        return c
    lax.fori_loop(0, tm, issue, 0, unroll=8)
    for j in range(2):
        pltpu.make_async_copy(ys_ref.at[pl.ds(0, tm)], ybuf.at[j], sem).wait()
    rt = rt_ref[...]
    x2 = x1_ref[...] + (rt[:, 4:5] * ybuf[0] + rt[:, 5:6] * ybuf[1])
    h3 = _rmsnorm(x2, nple_ref[...]).astype(BF16)
    ple = jax.nn.sigmoid(_dot(h3, wpg_ref[...])) * _dot(p_ref[...].astype(BF16), wpp_ref[...])
    y_ref[...] = _rmsnorm(x2 + ple, nfin_ref[...])


def _tail(pos, x1, rt, p2d, ys, nple, wpg, wpp, nfin, *, tm):
    n = x1.shape[0]
    tok = lambda width: pl.BlockSpec((tm, width), lambda i: (i, 0))
    full = lambda arr: pl.BlockSpec(arr.shape, lambda i: (0, 0))
    return pl.pallas_call(
        functools.partial(_tail_kernel, tm=tm),
        grid=(n // tm,),
        in_specs=[pl.BlockSpec((2 * tm,), lambda i: (i,), memory_space=pltpu.SMEM),
                  tok(D_MODEL), tok(ROUTER_LANES), tok(PLE_DIM),
                  pl.BlockSpec(memory_space=pl.ANY),
                  full(nple), full(wpg), full(wpp), full(nfin)],
        out_specs=tok(D_MODEL),
        out_shape=jax.ShapeDtypeStruct((n, D_MODEL), F32),
        scratch_shapes=[pltpu.VMEM((2, tm, D_MODEL), F32), pltpu.SemaphoreType.DMA(())],
        compiler_params=pltpu.CompilerParams(
            dimension_semantics=("arbitrary",), vmem_limit_bytes=VMEM_LIMIT),
        name="moe_tail",
    )(pos, x1, rt, p2d, ys, nple, wpg, wpp, nfin)


def _moe_tail(h2, rt, cnt, x1, p2d, wg, wu, wd, nple, wpg, wpp, nfin, *, td, tm):
    n = h2.shape[0]
    n_tiles_max = (2 * n) // EXPERT_TILE + N_EXPERTS
    pos, tile_expert, n_used = _route_tables(rt, cnt, n_tiles_max)
    xs = _dispatch(h2, pos, n_tiles_max * EXPERT_TILE, td=td)
    ys = _experts(xs, tile_expert, n_used, wg, wu, wd)
    return _tail(pos, x1, rt, p2d, ys, nple, wpg, wpp, nfin, tm=tm)


def kernel(x_prompt, x_sample, state_pool, state_shift, state_wkv, p_prompt, p_sample, norm_mix, w_in, pool_mix, pool_scale, w_branch_a, shift_mu, decay_w0, decay_w2, iclr_a0, iclr_a2, gate_g2, k_k, k_a, r_k, ln_x_w, ln_x_b, w_branch_b, w_out, norm_ffn, w_route_group, b_route_group, w_route_expert, b_route_expert, expert_gate, expert_up, expert_down, norm_ple, w_ple_gate, w_ple_proj, norm_final):
    l = 0
    bsz, seq, _ = x_prompt.shape
    dbsz, dseq, _ = x_sample.shape
    row = lambda vec: vec.reshape(1, -1).astype(F32)
    col = lambda vec: vec.reshape(-1, 1).astype(F32)
    perm = CHANNEL_PERM

    head_of = perm // HEAD_DIM
    ones_bd = (head_of[:, None] == head_of[None, :]).astype(np.float32)
    ones_bd = jnp.asarray(ones_bd, BF16)
    o1 = POOL_WIDTH
    o2 = o1 + SHIFT_WIDTH
    zperm = np.concatenate([perm, RWKV_WIDTH + perm, 2 * RWKV_WIDTH + perm,
                            np.arange(3 * RWKV_WIDTH, SHIFT_WIDTH)])
    w_zt = w_in[l][:, o1:o2].T[zperm].astype(BF16)
    zeros_lora = jnp.zeros((RWKV_WIDTH, LORA_PAIR // 2), F32)
    w2t_pad = jnp.concatenate([decay_w2[l].T[perm], zeros_lora], axis=1).astype(BF16)
    a2t_pad = jnp.concatenate([zeros_lora, iclr_a2[l].T[perm]], axis=1).astype(BF16)
    w_router = jnp.concatenate(
        [w_route_expert[l], w_route_group[l],
         jnp.zeros((D_MODEL, ROUTER_LANES - N_EXPERTS - N_GROUPS), F32)], axis=1)
    wr_hi = w_router.astype(BF16)
    wr_lo = (w_router - wr_hi.astype(F32)).astype(BF16)
    b_router = jnp.concatenate(
        [b_route_expert[l], b_route_group[l],
         jnp.zeros((ROUTER_LANES - N_EXPERTS - N_GROUPS,), F32)]).reshape(1, -1)
    prep_w = [row(norm_mix[l]), w_in[l][:, :o1].astype(BF16), w_zt, w_in[l][:, o2:].astype(BF16),
              col(shift_mu[l][zperm]), col(decay_w0[l][perm]), w2t_pad, col(iclr_a0[l][perm]), a2t_pad,
              gate_g2[l].T[perm].astype(BF16), col(k_k[l][perm]), col(k_a[l][perm]),
              col(r_k[l].reshape(-1)[perm]), ones_bd, pool_mix[l].astype(BF16), row(pool_scale[l]),
              w_branch_a[l].astype(BF16)]
    post_w = [col(ln_x_w[l][perm]), col(ln_x_b[l][perm]), ones_bd, w_branch_b[l][perm].astype(BF16),
              w_out[l].astype(BF16), row(norm_ffn[l]), wr_hi, wr_lo, b_router]
    moe_w = [expert_gate[l], expert_up[l], expert_down[l],
             row(norm_ple[l]), w_ple_gate[l].astype(BF16), w_ple_proj[l].astype(BF16), row(norm_final)]

    tm_p = 256
    tiles_p = seq // tm_p
    x_p = x_prompt.reshape(bsz * seq, D_MODEL)
    outs = _mix_prep(x_p, jnp.zeros((bsz * SHIFT_WIDTH, LANES), F32),
                     jnp.zeros((bsz * 16, POOL_WIDTH), F32), prep_w,
                     nb=bsz, tiles=tiles_p, tm=tm_p, s=1, pos0=0, cm_index=lambda b, t: (b, 0, t))
    u_p, zl_p, a, w, b, k, r, v, g, bon, pa, sgb = outs
    o_p, s_p = _wkv_scan_prompt(a, w, b, k, r, v, tt=128, ts=64)
    x1, h2, rt, cnt = _post(o_p, bon, g, pa, sgb, x_p, post_w, tm=tm_p,
                            cm_index=lambda i: (i // tiles_p, 0, i % tiles_p))
    y_p = _moe_tail(h2, rt, cnt, x1, p_prompt[l].reshape(bsz * seq, PLE_DIM), *moe_w, td=512, tm=256)
    wkv_p = s_p.reshape(KEY_HALF, KEY_HALF, 2, 2, bsz, HEADS).transpose(4, 5, 2, 1, 3, 0)
    wkv_p = wkv_p.reshape(bsz, HEADS, HEAD_DIM, HEAD_DIM)

    n_s = dbsz * dseq
    x_s = x_sample.transpose(1, 0, 2).reshape(n_s, D_MODEL)
    p_s = p_sample[l].transpose(1, 0, 2).reshape(n_s, PLE_DIM)
    uc0_s = jnp.concatenate(
        [jnp.zeros((dbsz, POOL_WIDTH), F32),
         state_pool[l].transpose(1, 0, 2).reshape(POOL_STATE * dbsz, POOL_WIDTH)], axis=0)
    zc0_s = state_shift[l].T[zperm]
    outs = _mix_prep(x_s, zc0_s, uc0_s, prep_w, nb=1, tiles=dseq, tm=dbsz, s=dbsz, pos0=PAST_LEN,
                     cm_index=lambda b, t: (t, 0, 0))
    u_s, zl_s, a, w, b, k, r, v, g, bon, pa, sgb = outs
    s0_s = state_wkv[l].reshape(dbsz, HEADS, 2, KEY_HALF, 2, KEY_HALF).transpose(1, 5, 4, 3, 2, 0)
    s0_s = s0_s.reshape(HEADS, HEAD_DIM, HEAD_DIM, dbsz)
    o_s, s_s = _wkv_scan_sample(a, w, b, k, r, v, s0_s)
    x1, h2, rt, cnt = _post(o_s, bon, g, pa, sgb, x_s, post_w, tm=dbsz, cm_index=lambda i: (i, 0, 0))
    y_s = _moe_tail(h2, rt, cnt, x1, p_s, *moe_w, td=n_s, tm=256)
    wkv_s = s_s.reshape(HEADS, KEY_HALF, 2, KEY_HALF, 2, dbsz).transpose(5, 0, 4, 3, 2, 1)
    wkv_s = wkv_s.reshape(dbsz, HEADS, HEAD_DIM, HEAD_DIM)

    inv_z = np.argsort(zperm)
    y_prompt = y_p.reshape(bsz, seq, D_MODEL)
    y_sample = y_s.reshape(dseq, dbsz, D_MODEL).transpose(1, 0, 2)
    pool_prompt = u_p.reshape(bsz, seq, POOL_WIDTH)[:, seq - POOL_STATE:]
    u_s_bt = u_s.reshape(dseq, dbsz, POOL_WIDTH).transpose(1, 0, 2)
    pool_sample = jnp.concatenate([state_pool[l][:, dseq:], u_s_bt], axis=1)
    shift_prompt = zl_p[:, :, LANES - 1][:, inv_z]
    shift_sample = zl_s[0].T[:, inv_z]
    return (y_prompt, y_sample, pool_prompt[None], shift_prompt[None], wkv_p[None],
            pool_sample[None], shift_sample[None], wkv_s[None])
```

```python
import functools

import jax
import jax.numpy as jnp
import numpy as np
from jax import lax
from jax.experimental import pallas as pl
from jax.experimental.pallas import tpu as pltpu

F32 = jnp.float32
BF16 = jnp.bfloat16

D_MODEL = 1024
PLE_DIM = 256
POOL_WIDTH = 512
POOL_WINDOWS = (2, 4, 8, 16)
POOL_GROUP_DIM = 128
POOL_STATE = 15
RWKV_WIDTH = 512
HEAD_DIM = 64
HEADS = 8
LORA_PAIR = 128
GATE_LORA = 128
SHIFT_WIDTH = 3 * RWKV_WIDTH + LORA_PAIR + GATE_LORA
IN_WIDTH = POOL_WIDTH + SHIFT_WIDTH + 2 * D_MODEL
N_GROUPS = 4
EXPERTS_PER_GROUP = 8
N_EXPERTS = 32
D_EXPERT = 256
RMS_EPS = 1e-6
GN_EPS = 64e-5
PAST_LEN = 16384

LANES = 128
SUBLANES = 8
KEY_HALF = HEAD_DIM // 2
SCAN_OPERANDS = 6
KEY_GROUP = 16
ROUTER_LANES = 128
EXPERT_TILE = 256
NEG_BIG = -1e30
VMEM_LIMIT = 56 * 1024 * 1024

_J = np.arange(RWKV_WIDTH)
CHANNEL_PERM = (_J % 8) * HEAD_DIM + ((_J // 8) % 2) * KEY_HALF + _J // 16


def _dot(a, b):
    return jnp.dot(a, b, preferred_element_type=F32)


def _split_bf16(x):
    hi = x.astype(BF16)
    return hi, (x - hi.astype(F32)).astype(BF16)


def _seg_sum_rows(ones_bd, x):
    hi, lo = _split_bf16(x)
    return _dot(ones_bd, hi) + _dot(ones_bd, lo)


def _rmsnorm(x, g):
    return x * lax.rsqrt(jnp.mean(x * x, axis=-1, keepdims=True) + RMS_EPS) * g


def _softplus(y):
    return jnp.maximum(y, 0.0) + jnp.log1p(jnp.exp(-jnp.abs(y)))


def _mix_prep_kernel(x_ref, zc0_ref, uc0_ref, nmix_ref, wu_ref, wzt_ref, wgab_ref, mu_ref, w0_ref,
                     w2t_ref, a0_ref, a2t_ref, g2t_ref, kk_ref, ka_ref, rk_ref, ones_ref, mix_ref,
                     pscale_ref, wa_ref,
                     u_ref, zl_ref, q_ref, g_ref, bon_ref,
                     pa_ref, sgb_ref, zc, uext, *, tm, s, pos0):
    t = pl.program_id(1)
    up = 16 * s

    @pl.when(t == 0)
    def _():
        zc[...] = zc0_ref[...]
        uext[0:up] = uc0_ref[...]

    x = x_ref[...]
    h = _rmsnorm(x, nmix_ref[...]).astype(BF16)
    u = _dot(h, wu_ref[...])
    gab = _dot(h, wgab_ref[...])
    u_ref[...] = u
    sgb_ref[...] = jax.nn.sigmoid(gab[:, D_MODEL:])
    uext[up:up + tm] = u

    z_t = lax.dot_general(wzt_ref[...], h, (((1,), (1,)), ((), ())), preferred_element_type=F32)
    if s == 1:
        lane = lax.broadcasted_iota(jnp.int32, z_t.shape, 1)
        zprev = jnp.where(lane == 0, zc[:, LANES - 1:LANES], pltpu.roll(z_t, 1, axis=1))
    else:
        zprev = zc[...]
    zc[...] = z_t[:, tm - LANES:tm]
    zl_ref[0] = z_t[:, tm - LANES:tm]
    zm = z_t + (zprev - z_t) * mu_ref[...]
    r = zm[0:RWKV_WIDTH]
    k = zm[RWKV_WIDTH:2 * RWKV_WIDTH]
    v = zm[2 * RWKV_WIDTH:3 * RWKV_WIDTH]
    lora_in = zm[3 * RWKV_WIDTH:3 * RWKV_WIDTH + LORA_PAIR]
    gd = zm[3 * RWKV_WIDTH + LORA_PAIR:SHIFT_WIDTH]
    dw = _dot(w2t_ref[...], jnp.tanh(lora_in).astype(BF16))
    da = _dot(a2t_ref[...], lora_in.astype(BF16))
    w_log = -_softplus(-(w0_ref[...] + dw)) - 0.5
    a = jax.nn.sigmoid(a0_ref[...] + da)
    ones_bd = ones_ref[...]
    kk = k * kk_ref[...]
    kk = kk / jnp.maximum(jnp.sqrt(_seg_sum_rows(ones_bd, kk * kk)), 1e-12)
    k2 = k * (1.0 + (a - 1.0) * ka_ref[...])
    q_ref[0, 0] = -kk
    q_ref[1, 0] = jnp.exp(-jnp.exp(w_log))
    q_ref[2, 0] = kk * a
    q_ref[3, 0] = k2
    q_ref[4, 0] = r
    q_ref[5, 0] = v
    g_ref[0] = _dot(g2t_ref[...], jax.nn.sigmoid(gd).astype(BF16))
    bon_ref[0] = _seg_sum_rows(ones_bd, r * k2 * rk_ref[...]) * v

    rows = lax.broadcasted_iota(jnp.int32, (tm, POOL_GROUP_DIM), 0)
    if s > 1:
        rows = rows // s
    pos = pos0 + t * (tm // s) + rows
    ys = []
    for gi, wnd in enumerate(POOL_WINDOWS):
        lanes = slice(gi * POOL_GROUP_DIM, (gi + 1) * POOL_GROUP_DIM)
        cur = uext[pl.ds(up, tm), lanes]
        acc = cur
        for j in range(1, wnd):
            acc = acc + uext[pl.ds(up - j * s, tm), lanes]
        cnt = jnp.minimum(pos + 1, wnd).astype(F32)
        pooled = acc / cnt - cur
        ys.append(_dot(pooled.astype(BF16), mix_ref[gi]))
    y = jnp.concatenate(ys, axis=-1) * pscale_ref[...]
    pa_ref[...] = jax.nn.sigmoid(gab[:, :D_MODEL]) * _dot(y.astype(BF16), wa_ref[...])

    uext[0:up] = uext[tm:tm + up]


def _mix_prep(x2d, zc0, uc0, wts, *, nb, tiles, tm, s, pos0, cm_index):
    n = x2d.shape[0]
    up = 16 * s
    row = lambda b, t: (b * tiles + t, 0)
    full = lambda arr: pl.BlockSpec(arr.shape, lambda b, t: (0,) * arr.ndim)
    in_specs = [
        pl.BlockSpec((tm, D_MODEL), row),
        pl.BlockSpec((SHIFT_WIDTH, LANES), lambda b, t: (b, 0)),
        pl.BlockSpec((up, POOL_WIDTH), lambda b, t: (b, 0)),
    ] + [full(w) for w in wts]
    tok = lambda width: pl.BlockSpec((tm, width), row)
    cm_shape = (nb, RWKV_WIDTH, tiles * tm) if s == 1 else (tiles, RWKV_WIDTH, tm)
    cm = pl.BlockSpec((1, RWKV_WIDTH, tm), cm_index)
    cm_stack = pl.BlockSpec((SCAN_OPERANDS, 1, RWKV_WIDTH, tm), lambda b, t: (0,) + cm_index(b, t))
    out_specs = [tok(POOL_WIDTH), pl.BlockSpec((1, SHIFT_WIDTH, LANES), lambda b, t: (b, 0, 0)),
                 cm_stack, cm, cm] + [tok(D_MODEL)] * 2
    out_shape = [jax.ShapeDtypeStruct((n, POOL_WIDTH), F32),
                 jax.ShapeDtypeStruct((nb, SHIFT_WIDTH, LANES), F32),
                 jax.ShapeDtypeStruct((SCAN_OPERANDS,) + cm_shape, F32)] \
        + [jax.ShapeDtypeStruct(cm_shape, F32)] * 2 \
        + [jax.ShapeDtypeStruct((n, D_MODEL), F32)] * 2
    return pl.pallas_call(
        functools.partial(_mix_prep_kernel, tm=tm, s=s, pos0=pos0),
        grid=(nb, tiles),
        in_specs=in_specs,
        out_specs=out_specs,
        out_shape=out_shape,
        scratch_shapes=[pltpu.VMEM((SHIFT_WIDTH, LANES), F32),
                        pltpu.VMEM((up + tm, POOL_WIDTH), F32)],
        compiler_params=pltpu.CompilerParams(
            dimension_semantics=("arbitrary", "arbitrary"), vmem_limit_bytes=VMEM_LIMIT),
        name="mix_prep",
    )(x2d, zc0, uc0, *wts)


def _wkv_step(s_ref, row, vv):
    groups = HEAD_DIM // KEY_GROUP

    def sa_pass(g, acc):
        for kk in range(KEY_GROUP):
            kp = g * KEY_GROUP + kk
            acc = acc + s_ref[kp] * row(0, kp)
        return acc
    sa = lax.fori_loop(0, groups, sa_pass, jnp.zeros(vv.shape, F32))

    def update_pass(g, acc):
        for kk in range(KEY_GROUP):
            kp = g * KEY_GROUP + kk
            sn = s_ref[kp] * row(1, kp) + sa * row(2, kp) + vv * row(3, kp)
            s_ref[kp] = sn
            acc = acc + sn * row(4, kp)
        return acc
    return lax.fori_loop(0, groups, update_pass, jnp.zeros(vv.shape, F32))


def _swap_major_sublane(x):
    return jnp.swapaxes(x, 0, 1)


def _wkv_scan_prompt_kernel(q_ref, o_ref, sout_ref, s_ref, kv_ref, vv_ref, ov_ref, *, tt):
    t = pl.program_id(0)
    q = pl.program_id(1)

    @pl.when((t == 0) & (q == 0))
    def _():
        s_ref[...] = jnp.zeros_like(s_ref)

    def gather_t(base_lo, base_hi):
        pieces = [q_ref[0, bb, pl.ds(base, SUBLANES), :] for base in (base_lo, base_hi) for bb in range(8)]
        return jnp.concatenate(pieces, axis=0).T

    @pl.when(q < SCAN_OPERANDS - 1)
    def _():
        def kgroup(gi, c):
            slabs = []
            for kk in range(SUBLANES):
                base = pl.multiple_of((gi * SUBLANES + kk) * SUBLANES, SUBLANES)
                slabs.append(gather_t(base, base))
            kv_ref[q, :, pl.ds(pl.multiple_of(gi * SUBLANES, SUBLANES), SUBLANES), :] = \
                _swap_major_sublane(jnp.stack(slabs))
            return c
        lax.fori_loop(0, HEAD_DIM // SUBLANES, kgroup, 0)

    @pl.when(q == SCAN_OPERANDS - 1)
    def _():
        def vgroup(gi, c):
            slabs = []
            for vi in range(SUBLANES):
                base = pl.multiple_of((gi * SUBLANES + vi) * 2 * SUBLANES, 2 * SUBLANES)
                slabs.append(gather_t(base, base + SUBLANES))
            vv_ref[:, pl.ds(pl.multiple_of(gi * SUBLANES, SUBLANES), SUBLANES), :] = \
                _swap_major_sublane(jnp.stack(slabs))
            return c
        lax.fori_loop(0, KEY_HALF // SUBLANES, vgroup, 0)

        def step(i, c):
            row = lambda qi, kp: kv_ref[qi, i, pl.ds(kp, 1), :]
            ov_ref[i] = _wkv_step(s_ref, row, vv_ref[i])
            return c
        lax.fori_loop(0, tt, step, 0)

        def ogroup(gi, c):
            g0 = pl.multiple_of(gi * SUBLANES, SUBLANES)
            x = _swap_major_sublane(ov_ref[:, pl.ds(g0, SUBLANES), :])
            for vi in range(SUBLANES):
                xt = x[vi].T
                base = pl.multiple_of((gi * SUBLANES + vi) * 2 * SUBLANES, 2 * SUBLANES)
                for vh in range(2):
                    for bb in range(8):
                        r0 = (vh * 8 + bb) * SUBLANES
                        o_ref[bb, pl.ds(base + vh * SUBLANES, SUBLANES), :] = xt[r0:r0 + SUBLANES]
            return c
        lax.fori_loop(0, KEY_HALF // SUBLANES, ogroup, 0)

        @pl.when(t == pl.num_programs(0) - 1)
        def _():
            sout_ref[...] = s_ref[...]


def _wkv_scan_prompt(ops, *, tt):
    _, nb, _, t_len = ops.shape
    sspec = pl.BlockSpec((HEAD_DIM, KEY_HALF, LANES), lambda ti, qi: (0, 0, 0))
    return pl.pallas_call(
        functools.partial(_wkv_scan_prompt_kernel, tt=tt),
        grid=(t_len // tt, SCAN_OPERANDS),
        in_specs=[pl.BlockSpec((1, nb, RWKV_WIDTH, tt), lambda ti, qi: (qi, 0, 0, ti))],
        out_specs=[pl.BlockSpec((nb, RWKV_WIDTH, tt), lambda ti, qi: (0, 0, ti)), sspec],
        out_shape=[jax.ShapeDtypeStruct((nb, RWKV_WIDTH, t_len), F32),
                   jax.ShapeDtypeStruct((HEAD_DIM, KEY_HALF, LANES), F32)],
        scratch_shapes=[pltpu.VMEM((HEAD_DIM, KEY_HALF, LANES), F32),
                        pltpu.VMEM((SCAN_OPERANDS - 1, tt, HEAD_DIM, LANES), F32),
                        pltpu.VMEM((tt, KEY_HALF, LANES), F32),
                        pltpu.VMEM((tt, KEY_HALF, LANES), F32)],
        compiler_params=pltpu.CompilerParams(
            dimension_semantics=("arbitrary", "arbitrary"), vmem_limit_bytes=VMEM_LIMIT),
        name="wkv_scan_prompt",
    )(ops)


def _wkv_scan_sample_kernel(q_ref, s0_ref, o_ref, sout_ref, s_ref, *, t_len):
    h = pl.program_id(0)
    s_ref[...] = s0_ref[0]
    for i in range(t_len):
        row = lambda qi, kp, i=i: q_ref[qi, i, pl.ds(kp * HEADS + h, 1), :]
        vv = q_ref[SCAN_OPERANDS - 1, i, pl.ds(h, HEAD_DIM, stride=HEADS), :]
        o_ref[i, pl.ds(h, HEAD_DIM, stride=HEADS), :] = _wkv_step(s_ref, row, vv)
    sout_ref[0] = s_ref[...]


def _wkv_scan_sample(ops, s0):
    t_len = ops.shape[1]
    spec = pl.BlockSpec(ops.shape[1:], lambda h: (0, 0, 0))
    sspec = pl.BlockSpec((1, HEAD_DIM, HEAD_DIM, LANES), lambda h: (h, 0, 0, 0))
    return pl.pallas_call(
        functools.partial(_wkv_scan_sample_kernel, t_len=t_len),
        grid=(HEADS,),
        in_specs=[pl.BlockSpec(ops.shape, lambda h: (0, 0, 0, 0)), sspec],
        out_specs=[spec, sspec],
        out_shape=[jax.ShapeDtypeStruct(ops.shape[1:], F32), jax.ShapeDtypeStruct(s0.shape, F32)],
        scratch_shapes=[pltpu.VMEM((HEAD_DIM, HEAD_DIM, LANES), F32)],
        compiler_params=pltpu.CompilerParams(
            dimension_semantics=("arbitrary",), vmem_limit_bytes=VMEM_LIMIT),
        name="wkv_scan_sample",
    )(ops, s0)


def _post_kernel(o_ref, bon_ref, g_ref, pa_ref, sgb_ref, x_ref, lnw_ref, lnb_ref, ones_ref, wb_ref,
                 wout_ref, nffn_ref, wrh_ref, wrl_ref, br_ref, x1_ref, h2_ref, rt_ref, cnt_ref, carry):
    @pl.when(pl.program_id(0) == 0)
    def _():
        carry[...] = jnp.zeros_like(carry)

    ones_bd = ones_ref[...]
    o = o_ref[0]
    mean = _seg_sum_rows(ones_bd, o) * (1.0 / HEAD_DIM)
    d = o - mean
    var = _seg_sum_rows(ones_bd, d * d) * (1.0 / HEAD_DIM)
    on = d * lax.rsqrt(var + GN_EPS) * lnw_ref[...] + lnb_ref[...]
    yb = ((on + bon_ref[0]) * g_ref[0]).astype(BF16)
    mb = lax.dot_general(yb, wb_ref[...], (((0,), (0,)), ((), ())), preferred_element_type=F32)
    merged = pa_ref[...] + sgb_ref[...] * mb
    x1 = x_ref[...] + _dot(merged.astype(BF16), wout_ref[...])
    x1_ref[...] = x1
    h2 = _rmsnorm(x1, nffn_ref[...])
    h2_ref[...] = h2

    h_hi, h_lo = _split_bf16(h2)
    logits = (_dot(h_hi, wrh_ref[...]) + _dot(h_lo, wrh_ref[...]) + _dot(h_hi, wrl_ref[...])
              + br_ref[...])
    ln = lax.broadcasted_iota(jnp.int32, logits.shape, 1)
    is_group = (ln >= N_EXPERTS) & (ln < N_EXPERTS + N_GROUPS)
    gl = jnp.where(is_group, logits, NEG_BIG)
    gmax = jnp.max(gl, axis=-1, keepdims=True)
    gsel = jnp.min(jnp.where(gl == gmax, ln, ROUTER_LANES), axis=-1, keepdims=True) - N_EXPERTS
    den = jnp.sum(jnp.where(is_group, jnp.exp(gl - gmax), 0.0), axis=-1, keepdims=True)
    pg = 1.0 / den
    in_group = (ln < N_EXPERTS) & ((ln // EXPERTS_PER_GROUP) == gsel)
    el = jnp.where(in_group, logits, NEG_BIG)
    m1 = jnp.max(el, axis=-1, keepdims=True)
    i1 = jnp.min(jnp.where(el == m1, ln, ROUTER_LANES), axis=-1, keepdims=True)
    el2 = jnp.where(ln == i1, NEG_BIG, el)
    m2 = jnp.max(el2, axis=-1, keepdims=True)
    i2 = jnp.min(jnp.where(el2 == m2, ln, ROUTER_LANES), axis=-1, keepdims=True)
    e2 = jnp.exp(m2 - m1)
    p1 = 1.0 / (1.0 + e2)
    p2 = e2 / (1.0 + e2)

    tm = logits.shape[0]
    sel = (ln == i1) | (ln == i2)
    tri = (lax.broadcasted_iota(jnp.int32, (tm, tm), 1)
           < lax.broadcasted_iota(jnp.int32, (tm, tm), 0)).astype(BF16)
    before = carry[0:1, :] + _dot(tri, sel.astype(BF16))
    r1 = jnp.sum(jnp.where(ln == i1, before, 0.0), axis=-1, keepdims=True)
    r2 = jnp.sum(jnp.where(ln == i2, before, 0.0), axis=-1, keepdims=True)
    total = carry[0:1, :] + jnp.sum(sel.astype(F32), axis=0, keepdims=True)
    carry[...] = jnp.broadcast_to(total, carry.shape)
    cnt_ref[...] = jnp.broadcast_to(total, cnt_ref.shape)
    fields = (i1.astype(F32), i2.astype(F32), r1, r2, p1 * pg, p2 * pg)
    rt = jnp.zeros(logits.shape, F32)
    for lane_idx, val in enumerate(fields):
        rt = jnp.where(ln == lane_idx, val, rt)
    rt_ref[...] = rt


def _post(o, bon, g, pa, sgb, x2d, wts, *, tm, cm_index):
    n = x2d.shape[0]
    row = lambda i: (i, 0)
    tok = lambda width: pl.BlockSpec((tm, width), row)
    cm = pl.BlockSpec((1, RWKV_WIDTH, tm), cm_index)
    full = lambda arr: pl.BlockSpec(arr.shape, lambda i: (0, 0))
    return pl.pallas_call(
        _post_kernel,
        grid=(n // tm,),
        in_specs=[cm] * 3 + [tok(D_MODEL)] * 3 + [full(w) for w in wts],
        out_specs=[tok(D_MODEL), tok(D_MODEL), tok(ROUTER_LANES),
                   pl.BlockSpec((SUBLANES, ROUTER_LANES), lambda i: (0, 0))],
        out_shape=[jax.ShapeDtypeStruct((n, D_MODEL), F32),
                   jax.ShapeDtypeStruct((n, D_MODEL), F32),
                   jax.ShapeDtypeStruct((n, ROUTER_LANES), F32),
                   jax.ShapeDtypeStruct((SUBLANES, ROUTER_LANES), F32)],
        scratch_shapes=[pltpu.VMEM((SUBLANES, ROUTER_LANES), F32)],
        compiler_params=pltpu.CompilerParams(
            dimension_semantics=("arbitrary",), vmem_limit_bytes=VMEM_LIMIT),
        name="post",
    )(o, bon, g, pa, sgb, x2d, *wts)


def _route_tables(rt, cnt, n_tiles_max):
    counts = cnt[0, :N_EXPERTS].astype(jnp.int32)
    tiles_e = (counts + EXPERT_TILE - 1) // EXPERT_TILE
    tile_end = jnp.cumsum(tiles_e)
    row0 = (tile_end - tiles_e) * EXPERT_TILE
    tile_ids = jnp.arange(n_tiles_max, dtype=jnp.int32)
    tile_expert = jnp.minimum(jnp.sum(tile_ids[:, None] >= tile_end[None, :], axis=-1),
                              N_EXPERTS - 1).astype(jnp.int32)
    e = rt[:, 0:2].astype(jnp.int32)
    r = rt[:, 2:4].astype(jnp.int32)
    pos = r + jnp.sum(jnp.where(e[..., None] == jnp.arange(N_EXPERTS), row0, 0), axis=-1)
    return pos.reshape(-1), tile_expert, tile_end[N_EXPERTS - 1:].astype(jnp.int32)


def _dispatch_kernel(pos_ref, h2_ref, xs0_ref, xs_ref, sem, *, td):
    del xs0_ref

    def issue(r, c):
        for j in range(2):
            pltpu.make_async_copy(h2_ref.at[pl.ds(r, 1)],
                                  xs_ref.at[pl.ds(pos_ref[2 * r + j], 1)], sem).start()
        return c
    lax.fori_loop(0, td, issue, 0, unroll=8)
    for j in range(2):
        pltpu.make_async_copy(h2_ref, xs_ref.at[pl.ds(0, td)], sem).wait()


def _dispatch(h2, pos, n_rows, *, td):
    n = h2.shape[0]
    return pl.pallas_call(
        functools.partial(_dispatch_kernel, td=td),
        grid=(n // td,),
        in_specs=[pl.BlockSpec((2 * td,), lambda i: (i,), memory_space=pltpu.SMEM),
                  pl.BlockSpec((td, D_MODEL), lambda i: (i, 0)),
                  pl.BlockSpec(memory_space=pl.ANY)],
        out_specs=pl.BlockSpec(memory_space=pl.ANY),
        out_shape=jax.ShapeDtypeStruct((n_rows, D_MODEL), F32),
        scratch_shapes=[pltpu.SemaphoreType.DMA(())],
        input_output_aliases={2: 0},
        compiler_params=pltpu.CompilerParams(
            dimension_semantics=("arbitrary",), vmem_limit_bytes=VMEM_LIMIT),
        name="moe_dispatch",
    )(pos, h2, jnp.zeros((n_rows, D_MODEL), F32))


def _experts_kernel(te_ref, nu_ref, xs_ref, wg_ref, wu_ref, wd_ref, ys_ref, wgu, wdn):
    i = pl.program_id(0)

    @pl.when(i < nu_ref[0])
    def _():
        @pl.when((i == 0) | (te_ref[i] != te_ref[jnp.maximum(i - 1, 0)]))
        def _():
            wgu[:, 0:D_EXPERT] = wg_ref[0].astype(BF16)
            wgu[:, D_EXPERT:2 * D_EXPERT] = wu_ref[0].astype(BF16)
            wdn[...] = wd_ref[0].astype(BF16)

        gu = _dot(xs_ref[...].astype(BF16), wgu[...])
        hid = jax.nn.silu(gu[:, 0:D_EXPERT]) * gu[:, D_EXPERT:2 * D_EXPERT]
        ys_ref[...] = _dot(hid.astype(BF16), wdn[...])

    @pl.when(i >= nu_ref[0])
    def _():
        ys_ref[...] = jnp.zeros_like(ys_ref)


def _experts(xs, tile_expert, n_used, wg, wu, wd):
    n_tiles = xs.shape[0] // EXPERT_TILE
    tile = lambda i, te, nu: (jnp.minimum(i, nu[0] - 1), 0)
    wsel = lambda i, te, nu: (te[jnp.minimum(i, nu[0] - 1)], 0, 0)
    return pl.pallas_call(
        _experts_kernel,
        grid_spec=pltpu.PrefetchScalarGridSpec(
            num_scalar_prefetch=2,
            grid=(n_tiles,),
            in_specs=[pl.BlockSpec((EXPERT_TILE, D_MODEL), tile),
                      pl.BlockSpec((1, D_MODEL, D_EXPERT), wsel),
                      pl.BlockSpec((1, D_MODEL, D_EXPERT), wsel),
                      pl.BlockSpec((1, D_EXPERT, D_MODEL), wsel)],
            out_specs=pl.BlockSpec((EXPERT_TILE, D_MODEL), lambda i, te, nu: (i, 0)),
            scratch_shapes=[pltpu.VMEM((D_MODEL, 2 * D_EXPERT), BF16),
                            pltpu.VMEM((D_EXPERT, D_MODEL), BF16)]),
        out_shape=jax.ShapeDtypeStruct(xs.shape, F32),
        compiler_params=pltpu.CompilerParams(
            dimension_semantics=("arbitrary",), vmem_limit_bytes=VMEM_LIMIT),
        name="moe_experts",
    )(tile_expert, n_used, xs, wg, wu, wd)


def _tail_kernel(pos_ref, x1_ref, rt_ref, p_ref, ys_ref, nple_ref, wpg_ref, wpp_ref, nfin_ref,
                 y_ref, ybuf, sem, *, tm):
    def issue(r, c):
        for j in range(2):
            pltpu.make_async_copy(ys_ref.at[pl.ds(pos_ref[2 * r + j], 1)],
                                  ybuf.at[j, pl.ds(r, 1)], sem).start()
        return c
    lax.fori_loop(0, tm, issue, 0, unroll=8)
    for j in range(2):
        pltpu.make_async_copy(ys_ref.at[pl.ds(0, tm)], ybuf.at[j], sem).wait()
    rt = rt_ref[...]
    x2 = x1_ref[...] + (rt[:, 4:5] * ybuf[0] + rt[:, 5:6] * ybuf[1])
    h3 = _rmsnorm(x2, nple_ref[...]).astype(BF16)
    ple = jax.nn.sigmoid(_dot(h3, wpg_ref[...])) * _dot(p_ref[...].astype(BF16), wpp_ref[...])
    y_ref[...] = _rmsnorm(x2 + ple, nfin_ref[...])


def _tail(pos, x1, rt, p2d, ys, nple, wpg, wpp, nfin, *, tm):
    n = x1.shape[0]
    tok = lambda width: pl.BlockSpec((tm, width), lambda i: (i, 0))
    full = lambda arr: pl.BlockSpec(arr.shape, lambda i: (0, 0))
    return pl.pallas_call(
        functools.partial(_tail_kernel, tm=tm),
        grid=(n // tm,),
        in_specs=[pl.BlockSpec((2 * tm,), lambda i: (i,), memory_space=pltpu.SMEM),
                  tok(D_MODEL), tok(ROUTER_LANES), tok(PLE_DIM),
                  pl.BlockSpec(memory_space=pl.ANY),
                  full(nple), full(wpg), full(wpp), full(nfin)],
        out_specs=tok(D_MODEL),
        out_shape=jax.ShapeDtypeStruct((n, D_MODEL), F32),
        scratch_shapes=[pltpu.VMEM((2, tm, D_MODEL), F32), pltpu.SemaphoreType.DMA(())],
        compiler_params=pltpu.CompilerParams(
            dimension_semantics=("arbitrary",), vmem_limit_bytes=VMEM_LIMIT),
        name="moe_tail",
    )(pos, x1, rt, p2d, ys, nple, wpg, wpp, nfin)


def _moe_tail(h2, rt, cnt, x1, p2d, wg, wu, wd, nple, wpg, wpp, nfin, *, td, tm):
    n = h2.shape[0]
    n_tiles_max = (2 * n) // EXPERT_TILE + N_EXPERTS
    pos, tile_expert, n_used = _route_tables(rt, cnt, n_tiles_max)
    xs = _dispatch(h2, pos, n_tiles_max * EXPERT_TILE, td=td)
    ys = _experts(xs, tile_expert, n_used, wg, wu, wd)
    return _tail(pos, x1, rt, p2d, ys, nple, wpg, wpp, nfin, tm=tm)


def kernel(x_prompt, x_sample, state_pool, state_shift, state_wkv, p_prompt, p_sample, norm_mix, w_in, pool_mix, pool_scale, w_branch_a, shift_mu, decay_w0, decay_w2, iclr_a0, iclr_a2, gate_g2, k_k, k_a, r_k, ln_x_w, ln_x_b, w_branch_b, w_out, norm_ffn, w_route_group, b_route_group, w_route_expert, b_route_expert, expert_gate, expert_up, expert_down, norm_ple, w_ple_gate, w_ple_proj, norm_final):
    l = 0
    bsz, seq, _ = x_prompt.shape
    dbsz, dseq, _ = x_sample.shape
    row = lambda vec: vec.reshape(1, -1).astype(F32)
    col = lambda vec: vec.reshape(-1, 1).astype(F32)
    perm = CHANNEL_PERM

    head_of = perm // HEAD_DIM
    ones_bd = (head_of[:, None] == head_of[None, :]).astype(np.float32)
    ones_bd = jnp.asarray(ones_bd, BF16)
    o1 = POOL_WIDTH
    o2 = o1 + SHIFT_WIDTH
    zperm = np.concatenate([perm, RWKV_WIDTH + perm, 2 * RWKV_WIDTH + perm,
                            np.arange(3 * RWKV_WIDTH, SHIFT_WIDTH)])
    w_zt = w_in[l][:, o1:o2].T[zperm].astype(BF16)
    zeros_lora = jnp.zeros((RWKV_WIDTH, LORA_PAIR // 2), F32)
    w2t_pad = jnp.concatenate([decay_w2[l].T[perm], zeros_lora], axis=1).astype(BF16)
    a2t_pad = jnp.concatenate([zeros_lora, iclr_a2[l].T[perm]], axis=1).astype(BF16)
    w_router = jnp.concatenate(
        [w_route_expert[l], w_route_group[l],
         jnp.zeros((D_MODEL, ROUTER_LANES - N_EXPERTS - N_GROUPS), F32)], axis=1)
    wr_hi = w_router.astype(BF16)
    wr_lo = (w_router - wr_hi.astype(F32)).astype(BF16)
    b_router = jnp.concatenate(
        [b_route_expert[l], b_route_group[l],
         jnp.zeros((ROUTER_LANES - N_EXPERTS - N_GROUPS,), F32)]).reshape(1, -1)
    prep_w = [row(norm_mix[l]), w_in[l][:, :o1].astype(BF16), w_zt, w_in[l][:, o2:].astype(BF16),
              col(shift_mu[l][zperm]), col(decay_w0[l][perm]), w2t_pad, col(iclr_a0[l][perm]), a2t_pad,
              gate_g2[l].T[perm].astype(BF16), col(k_k[l][perm]), col(k_a[l][perm]),
              col(r_k[l].reshape(-1)[perm]), ones_bd, pool_mix[l].astype(BF16), row(pool_scale[l]),
              w_branch_a[l].astype(BF16)]
    post_w = [col(ln_x_w[l][perm]), col(ln_x_b[l][perm]), ones_bd, w_branch_b[l][perm].astype(BF16),
              w_out[l].astype(BF16), row(norm_ffn[l]), wr_hi, wr_lo, b_router]
    moe_w = [expert_gate[l], expert_up[l], expert_down[l],
             row(norm_ple[l]), w_ple_gate[l].astype(BF16), w_ple_proj[l].astype(BF16), row(norm_final)]

    tm_p = 256
    tiles_p = seq // tm_p
    x_p = x_prompt.reshape(bsz * seq, D_MODEL)
    outs = _mix_prep(x_p, jnp.zeros((bsz * SHIFT_WIDTH, LANES), F32),
                     jnp.zeros((bsz * 16, POOL_WIDTH), F32), prep_w,
                     nb=bsz, tiles=tiles_p, tm=tm_p, s=1, pos0=0, cm_index=lambda b, t: (b, 0, t))
    u_p, zl_p, ops, g, bon, pa, sgb = outs
    o_p, s_p = _wkv_scan_prompt(ops, tt=LANES)
    x1, h2, rt, cnt = _post(o_p, bon, g, pa, sgb, x_p, post_w, tm=tm_p,
                            cm_index=lambda i: (i // tiles_p, 0, i % tiles_p))
    y_p = _moe_tail(h2, rt, cnt, x1, p_prompt[l].reshape(bsz * seq, PLE_DIM), *moe_w, td=512, tm=256)
    wkv_p = s_p.reshape(KEY_HALF, 2, KEY_HALF, 2, bsz, HEADS).transpose(4, 5, 3, 2, 1, 0)
    wkv_p = wkv_p.reshape(bsz, HEADS, HEAD_DIM, HEAD_DIM)

    n_s = dbsz * dseq
    x_s = x_sample.transpose(1, 0, 2).reshape(n_s, D_MODEL)
    p_s = p_sample[l].transpose(1, 0, 2).reshape(n_s, PLE_DIM)
    uc0_s = jnp.concatenate(
        [jnp.zeros((dbsz, POOL_WIDTH), F32),
         state_pool[l].transpose(1, 0, 2).reshape(POOL_STATE * dbsz, POOL_WIDTH)], axis=0)
    zc0_s = state_shift[l].T[zperm]
    outs = _mix_prep(x_s, zc0_s, uc0_s, prep_w, nb=1, tiles=dseq, tm=dbsz, s=dbsz, pos0=PAST_LEN,
                     cm_index=lambda b, t: (t, 0, 0))
    u_s, zl_s, ops, g, bon, pa, sgb = outs
    s0_s = state_wkv[l].reshape(dbsz, HEADS, 2, KEY_HALF, 2, KEY_HALF).transpose(1, 5, 4, 3, 2, 0)
    s0_s = s0_s.reshape(HEADS, HEAD_DIM, HEAD_DIM, dbsz)
    o_s, s_s = _wkv_scan_sample(ops, s0_s)
    x1, h2, rt, cnt = _post(o_s, bon, g, pa, sgb, x_s, post_w, tm=dbsz, cm_index=lambda i: (i, 0, 0))
    y_s = _moe_tail(h2, rt, cnt, x1, p_s, *moe_w, td=n_s, tm=256)
    wkv_s = s_s.reshape(HEADS, KEY_HALF, 2, KEY_HALF, 2, dbsz).transpose(5, 0, 4, 3, 2, 1)
    wkv_s = wkv_s.reshape(dbsz, HEADS, HEAD_DIM, HEAD_DIM)

    inv_z = np.argsort(zperm)
    y_prompt = y_p.reshape(bsz, seq, D_MODEL)
    y_sample = y_s.reshape(dseq, dbsz, D_MODEL).transpose(1, 0, 2)
    pool_prompt = u_p.reshape(bsz, seq, POOL_WIDTH)[:, seq - POOL_STATE:]
    u_s_bt = u_s.reshape(dseq, dbsz, POOL_WIDTH).transpose(1, 0, 2)
    pool_sample = jnp.concatenate([state_pool[l][:, dseq:], u_s_bt], axis=1)
    shift_prompt = zl_p[:, :, LANES - 1][:, inv_z]
    shift_sample = zl_s[0].T[:, inv_z]
    return (y_prompt, y_sample, pool_prompt[None], shift_prompt[None], wkv_p[None],
            pool_sample[None], shift_sample[None], wkv_s[None])
```

```python
import functools

import jax
import jax.numpy as jnp
import numpy as np
from jax import lax
from jax.experimental import pallas as pl
from jax.experimental.pallas import tpu as pltpu

F32 = jnp.float32
BF16 = jnp.bfloat16

D_MODEL = 1024
PLE_DIM = 256
POOL_WIDTH = 512
POOL_WINDOWS = (2, 4, 8, 16)
POOL_GROUP_DIM = 128
POOL_STATE = 15
RWKV_WIDTH = 512
HEAD_DIM = 64
HEADS = 8
LORA_PAIR = 128
GATE_LORA = 128
SHIFT_WIDTH = 3 * RWKV_WIDTH + LORA_PAIR + GATE_LORA
IN_WIDTH = POOL_WIDTH + SHIFT_WIDTH + 2 * D_MODEL
N_GROUPS = 4
EXPERTS_PER_GROUP = 8
N_EXPERTS = 32
D_EXPERT = 256
RMS_EPS = 1e-6
GN_EPS = 64e-5
PAST_LEN = 16384

LANES = 128
SUBLANES = 8
KEY_HALF = HEAD_DIM // 2
SCAN_OPERANDS = 6
KEY_GROUP = 16
ROUTER_LANES = 128
EXPERT_TILE = 256
NEG_BIG = -1e30
VMEM_LIMIT = 56 * 1024 * 1024

_J = np.arange(RWKV_WIDTH)
CHANNEL_PERM = (_J % 8) * HEAD_DIM + ((_J // 8) % 2) * KEY_HALF + _J // 16


def _dot(a, b):
    return jnp.dot(a, b, preferred_element_type=F32)


def _split_bf16(x):
    hi = x.astype(BF16)
    return hi, (x - hi.astype(F32)).astype(BF16)


def _seg_sum_rows(ones_bd, x):
    hi, lo = _split_bf16(x)
    return _dot(ones_bd, hi) + _dot(ones_bd, lo)


def _rmsnorm(x, g):
    return x * lax.rsqrt(jnp.mean(x * x, axis=-1, keepdims=True) + RMS_EPS) * g


def _softplus(y):
    return jnp.maximum(y, 0.0) + jnp.log1p(jnp.exp(-jnp.abs(y)))


def _mix_prep_kernel(x_ref, zc0_ref, uc0_ref, nmix_ref, wu_ref, wz_ref, wgab_ref, mu_ref, w0_ref,
                     w2t_ref, a0_ref, a2t_ref, g2t_ref, kk_ref, ka_ref, rk_ref, ones_ref, mix_ref,
                     pscale_ref, wa_ref,
                     u_ref, zl_ref, q_ref, g_ref, bon_ref,
                     pa_ref, sgb_ref, zc, uext, *, tm, s, pos0):
    t = pl.program_id(1)
    up = 16 * s

    @pl.when(t == 0)
    def _():
        zc[...] = zc0_ref[...]
        uext[0:up] = uc0_ref[...]

    x = x_ref[...]
    h = _rmsnorm(x, nmix_ref[...]).astype(BF16)
    u = _dot(h, wu_ref[...])
    gab = _dot(h, wgab_ref[...])
    u_ref[...] = u
    sgb_ref[...] = jax.nn.sigmoid(gab[:, D_MODEL:])
    uext[up:up + tm] = u

    z_t = _dot(h, wz_ref[...]).T
    if s == 1:
        lane = lax.broadcasted_iota(jnp.int32, z_t.shape, 1)
        zprev = jnp.where(lane == 0, zc[:, LANES - 1:LANES], pltpu.roll(z_t, 1, axis=1))
    else:
        zprev = zc[...]
    zc[...] = z_t[:, tm - LANES:tm]
    zl_ref[0] = z_t[:, tm - LANES:tm]
    zm = z_t + (zprev - z_t) * mu_ref[...]
    r = zm[0:RWKV_WIDTH]
    k = zm[RWKV_WIDTH:2 * RWKV_WIDTH]
    v = zm[2 * RWKV_WIDTH:3 * RWKV_WIDTH]
    lora_in = zm[3 * RWKV_WIDTH:3 * RWKV_WIDTH + LORA_PAIR]
    gd = zm[3 * RWKV_WIDTH + LORA_PAIR:SHIFT_WIDTH]
    dw = _dot(w2t_ref[...], jnp.tanh(lora_in).astype(BF16))
    da = _dot(a2t_ref[...], lora_in.astype(BF16))
    w_log = -_softplus(-(w0_ref[...] + dw)) - 0.5
    a = jax.nn.sigmoid(a0_ref[...] + da)
    ones_bd = ones_ref[...]
    kk = k * kk_ref[...]
    kk = kk / jnp.maximum(jnp.sqrt(_seg_sum_rows(ones_bd, kk * kk)), 1e-12)
    k2 = k * (1.0 + (a - 1.0) * ka_ref[...])
    q_ref[0, 0] = -kk
    q_ref[1, 0] = jnp.exp(-jnp.exp(w_log))
    q_ref[2, 0] = kk * a
    q_ref[3, 0] = k2
    q_ref[4, 0] = r
    q_ref[5, 0] = v
    g_ref[0] = _dot(g2t_ref[...], jax.nn.sigmoid(gd).astype(BF16))
    bon_ref[0] = _seg_sum_rows(ones_bd, r * k2 * rk_ref[...]) * v

    rows = lax.broadcasted_iota(jnp.int32, (tm, POOL_GROUP_DIM), 0)
    if s > 1:
        rows = rows // s
    pos = pos0 + t * (tm // s) + rows
    ys = []
    for gi, wnd in enumerate(POOL_WINDOWS):
        lanes = slice(gi * POOL_GROUP_DIM, (gi + 1) * POOL_GROUP_DIM)
        cur = uext[pl.ds(up, tm), lanes]
        acc = cur
        for j in range(1, wnd):
            acc = acc + uext[pl.ds(up - j * s, tm), lanes]
        cnt = jnp.minimum(pos + 1, wnd).astype(F32)
        pooled = acc / cnt - cur
        ys.append(_dot(pooled.astype(BF16), mix_ref[gi]))
    y = jnp.concatenate(ys, axis=-1) * pscale_ref[...]
    pa_ref[...] = jax.nn.sigmoid(gab[:, :D_MODEL]) * _dot(y.astype(BF16), wa_ref[...])

    uext[0:up] = uext[tm:tm + up]


def _mix_prep(x2d, zc0, uc0, wts, *, nb, tiles, tm, s, pos0, cm_index):
    n = x2d.shape[0]
    up = 16 * s
    row = lambda b, t: (b * tiles + t, 0)
    full = lambda arr: pl.BlockSpec(arr.shape, lambda b, t: (0,) * arr.ndim)
    in_specs = [
        pl.BlockSpec((tm, D_MODEL), row),
        pl.BlockSpec((SHIFT_WIDTH, LANES), lambda b, t: (b, 0)),
        pl.BlockSpec((up, POOL_WIDTH), lambda b, t: (b, 0)),
    ] + [full(w) for w in wts]
    tok = lambda width: pl.BlockSpec((tm, width), row)
    cm_shape = (nb, RWKV_WIDTH, tiles * tm) if s == 1 else (tiles, RWKV_WIDTH, tm)
    cm = pl.BlockSpec((1, RWKV_WIDTH, tm), cm_index)
    cm_stack = pl.BlockSpec((SCAN_OPERANDS, 1, RWKV_WIDTH, tm), lambda b, t: (0,) + cm_index(b, t))
    out_specs = [tok(POOL_WIDTH), pl.BlockSpec((1, SHIFT_WIDTH, LANES), lambda b, t: (b, 0, 0)),
                 cm_stack, cm, cm] + [tok(D_MODEL)] * 2
    out_shape = [jax.ShapeDtypeStruct((n, POOL_WIDTH), F32),
                 jax.ShapeDtypeStruct((nb, SHIFT_WIDTH, LANES), F32),
                 jax.ShapeDtypeStruct((SCAN_OPERANDS,) + cm_shape, F32)] \
        + [jax.ShapeDtypeStruct(cm_shape, F32)] * 2 \
        + [jax.ShapeDtypeStruct((n, D_MODEL), F32)] * 2
    return pl.pallas_call(
        functools.partial(_mix_prep_kernel, tm=tm, s=s, pos0=pos0),
        grid=(nb, tiles),
        in_specs=in_specs,
        out_specs=out_specs,
        out_shape=out_shape,
        scratch_shapes=[pltpu.VMEM((SHIFT_WIDTH, LANES), F32),
                        pltpu.VMEM((up + tm, POOL_WIDTH), F32)],
        compiler_params=pltpu.CompilerParams(
            dimension_semantics=("arbitrary", "arbitrary"), vmem_limit_bytes=VMEM_LIMIT),
        name="mix_prep",
    )(x2d, zc0, uc0, *wts)


def _wkv_step(s_ref, row, vv):
    groups = HEAD_DIM // KEY_GROUP

    def sa_pass(g, acc):
        for kk in range(KEY_GROUP):
            kp = g * KEY_GROUP + kk
            acc = acc + s_ref[kp] * row(0, kp)
        return acc
    sa = lax.fori_loop(0, groups, sa_pass, jnp.zeros(vv.shape, F32))

    def update_pass(g, acc):
        for kk in range(KEY_GROUP):
            kp = g * KEY_GROUP + kk
            sn = s_ref[kp] * row(1, kp) + sa * row(2, kp) + vv * row(3, kp)
            s_ref[kp] = sn
            acc = acc + sn * row(4, kp)
        return acc
    return lax.fori_loop(0, groups, update_pass, jnp.zeros(vv.shape, F32))


def _swap_major_sublane(x):
    return jnp.swapaxes(x, 0, 1)


def _wkv_scan_prompt_kernel(q_ref, o_ref, sout_ref, s_ref, kv_ref, vv_ref, ov_ref, *, tt):
    t = pl.program_id(0)
    q = pl.program_id(1)

    @pl.when((t == 0) & (q == 0))
    def _():
        s_ref[...] = jnp.zeros_like(s_ref)

    def gather_t(base_lo, base_hi):
        pieces = [q_ref[0, bb, pl.ds(base, SUBLANES), :] for base in (base_lo, base_hi) for bb in range(8)]
        return jnp.concatenate(pieces, axis=0).T

    @pl.when(q < SCAN_OPERANDS - 1)
    def _():
        def kgroup(gi, c):
            slabs = []
            for kk in range(SUBLANES):
                base = pl.multiple_of((gi * SUBLANES + kk) * SUBLANES, SUBLANES)
                slabs.append(gather_t(base, base))
            kv_ref[q, :, pl.ds(pl.multiple_of(gi * SUBLANES, SUBLANES), SUBLANES), :] = \
                _swap_major_sublane(jnp.stack(slabs))
            return c
        lax.fori_loop(0, HEAD_DIM // SUBLANES, kgroup, 0)

    @pl.when(q == SCAN_OPERANDS - 1)
    def _():
        def vgroup(gi, c):
            slabs = []
            for vi in range(SUBLANES):
                base = pl.multiple_of((gi * SUBLANES + vi) * 2 * SUBLANES, 2 * SUBLANES)
                slabs.append(gather_t(base, base + SUBLANES))
            vv_ref[:, pl.ds(pl.multiple_of(gi * SUBLANES, SUBLANES), SUBLANES), :] = \
                _swap_major_sublane(jnp.stack(slabs))
            return c
        lax.fori_loop(0, KEY_HALF // SUBLANES, vgroup, 0)

        def step(i, c):
            row = lambda qi, kp: kv_ref[qi, i, pl.ds(kp, 1), :]
            ov_ref[i] = _wkv_step(s_ref, row, vv_ref[i])
            return c
        lax.fori_loop(0, tt, step, 0)

        def ogroup(gi, c):
            g0 = pl.multiple_of(gi * SUBLANES, SUBLANES)
            x = _swap_major_sublane(ov_ref[:, pl.ds(g0, SUBLANES), :])
            for vi in range(SUBLANES):
                xt = x[vi].T
                base = pl.multiple_of((gi * SUBLANES + vi) * 2 * SUBLANES, 2 * SUBLANES)
                for vh in range(2):
                    for bb in range(8):
                        r0 = (vh * 8 + bb) * SUBLANES
                        o_ref[bb, pl.ds(base + vh * SUBLANES, SUBLANES), :] = xt[r0:r0 + SUBLANES]
            return c
        lax.fori_loop(0, KEY_HALF // SUBLANES, ogroup, 0)

        @pl.when(t == pl.num_programs(0) - 1)
        def _():
            sout_ref[...] = s_ref[...]


def _wkv_scan_prompt(ops, *, tt):
    _, nb, _, t_len = ops.shape
    sspec = pl.BlockSpec((HEAD_DIM, KEY_HALF, LANES), lambda ti, qi: (0, 0, 0))
    return pl.pallas_call(
        functools.partial(_wkv_scan_prompt_kernel, tt=tt),
        grid=(t_len // tt, SCAN_OPERANDS),
        in_specs=[pl.BlockSpec((1, nb, RWKV_WIDTH, tt), lambda ti, qi: (qi, 0, 0, ti))],
        out_specs=[pl.BlockSpec((nb, RWKV_WIDTH, tt), lambda ti, qi: (0, 0, ti)), sspec],
        out_shape=[jax.ShapeDtypeStruct((nb, RWKV_WIDTH, t_len), F32),
                   jax.ShapeDtypeStruct((HEAD_DIM, KEY_HALF, LANES), F32)],
        scratch_shapes=[pltpu.VMEM((HEAD_DIM, KEY_HALF, LANES), F32),
                        pltpu.VMEM((SCAN_OPERANDS - 1, tt, HEAD_DIM, LANES), F32),
                        pltpu.VMEM((tt, KEY_HALF, LANES), F32),
                        pltpu.VMEM((tt, KEY_HALF, LANES), F32)],
        compiler_params=pltpu.CompilerParams(
            dimension_semantics=("arbitrary", "arbitrary"), vmem_limit_bytes=VMEM_LIMIT),
        name="wkv_scan_prompt",
    )(ops)


def _wkv_scan_sample_kernel(q_ref, s0_ref, o_ref, sout_ref, s_ref, *, t_len):
    h = pl.program_id(0)
    s_ref[...] = s0_ref[0]
    for i in range(t_len):
        row = lambda qi, kp, i=i: q_ref[qi, i, pl.ds(kp * HEADS + h, 1), :]
        vv = q_ref[SCAN_OPERANDS - 1, i, pl.ds(h, HEAD_DIM, stride=HEADS), :]
        o_ref[i, pl.ds(h, HEAD_DIM, stride=HEADS), :] = _wkv_step(s_ref, row, vv)
    sout_ref[0] = s_ref[...]


def _wkv_scan_sample(ops, s0):
    t_len = ops.shape[1]
    spec = pl.BlockSpec(ops.shape[1:], lambda h: (0, 0, 0))
    sspec = pl.BlockSpec((1, HEAD_DIM, HEAD_DIM, LANES), lambda h: (h, 0, 0, 0))
    return pl.pallas_call(
        functools.partial(_wkv_scan_sample_kernel, t_len=t_len),
        grid=(HEADS,),
        in_specs=[pl.BlockSpec(ops.shape, lambda h: (0, 0, 0, 0)), sspec],
        out_specs=[spec, sspec],
        out_shape=[jax.ShapeDtypeStruct(ops.shape[1:], F32), jax.ShapeDtypeStruct(s0.shape, F32)],
        scratch_shapes=[pltpu.VMEM((HEAD_DIM, HEAD_DIM, LANES), F32)],
        compiler_params=pltpu.CompilerParams(
            dimension_semantics=("arbitrary",), vmem_limit_bytes=VMEM_LIMIT),
        name="wkv_scan_sample",
    )(ops, s0)


def _post_kernel(o_ref, bon_ref, g_ref, pa_ref, sgb_ref, x_ref, lnw_ref, lnb_ref, ones_ref, wb_ref,
                 wout_ref, nffn_ref, wrh_ref, wrl_ref, br_ref, x1_ref, h2_ref, rt_ref, cnt_ref, carry):
    @pl.when(pl.program_id(0) == 0)
    def _():
        carry[...] = jnp.zeros_like(carry)

    ones_bd = ones_ref[...]
    o = o_ref[0]
    mean = _seg_sum_rows(ones_bd, o) * (1.0 / HEAD_DIM)
    d = o - mean
    var = _seg_sum_rows(ones_bd, d * d) * (1.0 / HEAD_DIM)
    on = d * lax.rsqrt(var + GN_EPS) * lnw_ref[...] + lnb_ref[...]
    yb = ((on + bon_ref[0]) * g_ref[0]).astype(BF16)
    mb = lax.dot_general(yb, wb_ref[...], (((0,), (0,)), ((), ())), preferred_element_type=F32)
    merged = pa_ref[...] + sgb_ref[...] * mb
    x1 = x_ref[...] + _dot(merged.astype(BF16), wout_ref[...])
    x1_ref[...] = x1
    h2 = _rmsnorm(x1, nffn_ref[...])
    h2_ref[...] = h2

    h_hi, h_lo = _split_bf16(h2)
    logits = (_dot(h_hi, wrh_ref[...]) + _dot(h_lo, wrh_ref[...]) + _dot(h_hi, wrl_ref[...])
              + br_ref[...])
    ln = lax.broadcasted_iota(jnp.int32, logits.shape, 1)
    is_group = (ln >= N_EXPERTS) & (ln < N_EXPERTS + N_GROUPS)
    gl = jnp.where(is_group, logits, NEG_BIG)
    gmax = jnp.max(gl, axis=-1, keepdims=True)
    gsel = jnp.min(jnp.where(gl == gmax, ln, ROUTER_LANES), axis=-1, keepdims=True) - N_EXPERTS
    den = jnp.sum(jnp.where(is_group, jnp.exp(gl - gmax), 0.0), axis=-1, keepdims=True)
    pg = 1.0 / den
    in_group = (ln < N_EXPERTS) & ((ln // EXPERTS_PER_GROUP) == gsel)
    el = jnp.where(in_group, logits, NEG_BIG)
    m1 = jnp.max(el, axis=-1, keepdims=True)
    i1 = jnp.min(jnp.where(el == m1, ln, ROUTER_LANES), axis=-1, keepdims=True)
    el2 = jnp.where(ln == i1, NEG_BIG, el)
    m2 = jnp.max(el2, axis=-1, keepdims=True)
    i2 = jnp.min(jnp.where(el2 == m2, ln, ROUTER_LANES), axis=-1, keepdims=True)
    e2 = jnp.exp(m2 - m1)
    p1 = 1.0 / (1.0 + e2)
    p2 = e2 / (1.0 + e2)

    tm = logits.shape[0]
    sel = (ln == i1) | (ln == i2)
    tri = (lax.broadcasted_iota(jnp.int32, (tm, tm), 1)
           < lax.broadcasted_iota(jnp.int32, (tm, tm), 0)).astype(BF16)
    before = carry[0:1, :] + _dot(tri, sel.astype(BF16))
    r1 = jnp.sum(jnp.where(ln == i1, before, 0.0), axis=-1, keepdims=True)
    r2 = jnp.sum(jnp.where(ln == i2, before, 0.0), axis=-1, keepdims=True)
    total = carry[0:1, :] + jnp.sum(sel.astype(F32), axis=0, keepdims=True)
    carry[...] = jnp.broadcast_to(total, carry.shape)
    cnt_ref[...] = jnp.broadcast_to(total, cnt_ref.shape)
    fields = (i1.astype(F32), i2.astype(F32), r1, r2, p1 * pg, p2 * pg)
    rt = jnp.zeros(logits.shape, F32)
    for lane_idx, val in enumerate(fields):
        rt = jnp.where(ln == lane_idx, val, rt)
    rt_ref[...] = rt


def _post(o, bon, g, pa, sgb, x2d, wts, *, tm, cm_index):
    n = x2d.shape[0]
    row = lambda i: (i, 0)
    tok = lambda width: pl.BlockSpec((tm, width), row)
    cm = pl.BlockSpec((1, RWKV_WIDTH, tm), cm_index)
    full = lambda arr: pl.BlockSpec(arr.shape, lambda i: (0, 0))
    return pl.pallas_call(
        _post_kernel,
        grid=(n // tm,),
        in_specs=[cm] * 3 + [tok(D_MODEL)] * 3 + [full(w) for w in wts],
        out_specs=[tok(D_MODEL), tok(D_MODEL), tok(ROUTER_LANES),
                   pl.BlockSpec((SUBLANES, ROUTER_LANES), lambda i: (0, 0))],
        out_shape=[jax.ShapeDtypeStruct((n, D_MODEL), F32),
                   jax.ShapeDtypeStruct((n, D_MODEL), F32),
                   jax.ShapeDtypeStruct((n, ROUTER_LANES), F32),
                   jax.ShapeDtypeStruct((SUBLANES, ROUTER_LANES), F32)],
        scratch_shapes=[pltpu.VMEM((SUBLANES, ROUTER_LANES), F32)],
        compiler_params=pltpu.CompilerParams(
            dimension_semantics=("arbitrary",), vmem_limit_bytes=VMEM_LIMIT),
        name="post",
    )(o, bon, g, pa, sgb, x2d, *wts)


def _route_tables(rt, cnt, n_tiles_max):
    counts = cnt[0, :N_EXPERTS].astype(jnp.int32)
    tiles_e = (counts + EXPERT_TILE - 1) // EXPERT_TILE
    tile_end = jnp.cumsum(tiles_e)
    row0 = (tile_end - tiles_e) * EXPERT_TILE
    tile_ids = jnp.arange(n_tiles_max, dtype=jnp.int32)
    tile_expert = jnp.minimum(jnp.sum(tile_ids[:, None] >= tile_end[None, :], axis=-1),
                              N_EXPERTS - 1).astype(jnp.int32)
    e = rt[:, 0:2].astype(jnp.int32)
    r = rt[:, 2:4].astype(jnp.int32)
    pos = r + jnp.sum(jnp.where(e[..., None] == jnp.arange(N_EXPERTS), row0, 0), axis=-1)
    return pos.reshape(-1), tile_expert, tile_end[N_EXPERTS - 1:].astype(jnp.int32)


def _dispatch_kernel(pos_ref, h2_ref, xs0_ref, xs_ref, sem, *, td):
    del xs0_ref

    for r in range(td):
        for j in range(2):
            pltpu.make_async_copy(h2_ref.at[pl.ds(r, 1)],
                                  xs_ref.at[pl.ds(pos_ref[2 * r + j], 1)], sem).start()
    for j in range(2):
        pltpu.make_async_copy(h2_ref, xs_ref.at[pl.ds(0, td)], sem).wait()


def _dispatch(h2, pos, n_rows, *, td):
    n = h2.shape[0]
    return pl.pallas_call(
        functools.partial(_dispatch_kernel, td=td),
        grid=(n // td,),
        in_specs=[pl.BlockSpec((2 * td,), lambda i: (i,), memory_space=pltpu.SMEM),
                  pl.BlockSpec((td, D_MODEL), lambda i: (i, 0)),
                  pl.BlockSpec(memory_space=pl.ANY)],
        out_specs=pl.BlockSpec(memory_space=pl.ANY),
        out_shape=jax.ShapeDtypeStruct((n_rows, D_MODEL), F32),
        scratch_shapes=[pltpu.SemaphoreType.DMA(())],
        input_output_aliases={2: 0},
        compiler_params=pltpu.CompilerParams(
            dimension_semantics=("arbitrary",), vmem_limit_bytes=VMEM_LIMIT),
        name="moe_dispatch",
    )(pos, h2, jnp.zeros((n_rows, D_MODEL), F32))


def _experts_kernel(te_ref, nu_ref, xs_ref, wg_ref, wu_ref, wd_ref, ys_ref, wgu, wdn):
    i = pl.program_id(0)

    @pl.when(i < nu_ref[0])
    def _():
        @pl.when((i == 0) | (te_ref[i] != te_ref[jnp.maximum(i - 1, 0)]))
        def _():
            wgu[:, 0:D_EXPERT] = wg_ref[0].astype(BF16)
            wgu[:, D_EXPERT:2 * D_EXPERT] = wu_ref[0].astype(BF16)
            wdn[...] = wd_ref[0].astype(BF16)

        gu = _dot(xs_ref[...].astype(BF16), wgu[...])
        hid = jax.nn.silu(gu[:, 0:D_EXPERT]) * gu[:, D_EXPERT:2 * D_EXPERT]
        ys_ref[...] = _dot(hid.astype(BF16), wdn[...])

    @pl.when(i >= nu_ref[0])
    def _():
        ys_ref[...] = jnp.zeros_like(ys_ref)


def _experts(xs, tile_expert, n_used, wg, wu, wd):
    n_tiles = xs.shape[0] // EXPERT_TILE
    tile = lambda i, te, nu: (jnp.minimum(i, nu[0] - 1), 0)
    wsel = lambda i, te, nu: (te[jnp.minimum(i, nu[0] - 1)], 0, 0)
    return pl.pallas_call(
        _experts_kernel,
        grid_spec=pltpu.PrefetchScalarGridSpec(
            num_scalar_prefetch=2,
            grid=(n_tiles,),
            in_specs=[pl.BlockSpec((EXPERT_TILE, D_MODEL), tile),
                      pl.BlockSpec((1, D_MODEL, D_EXPERT), wsel),
                      pl.BlockSpec((1, D_MODEL, D_EXPERT), wsel),
                      pl.BlockSpec((1, D_EXPERT, D_MODEL), wsel)],
            out_specs=pl.BlockSpec((EXPERT_TILE, D_MODEL), lambda i, te, nu: (i, 0)),
            scratch_shapes=[pltpu.VMEM((D_MODEL, 2 * D_EXPERT), BF16),
                            pltpu.VMEM((D_EXPERT, D_MODEL), BF16)]),
        out_shape=jax.ShapeDtypeStruct(xs.shape, F32),
        compiler_params=pltpu.CompilerParams(
            dimension_semantics=("arbitrary",), vmem_limit_bytes=VMEM_LIMIT),
        name="moe_experts",
    )(tile_expert, n_used, xs, wg, wu, wd)


def _tail_kernel(pos_ref, posn_ref, x1_ref, rt_ref, p_ref, ys_ref, nple_ref, wpg_ref, wpp_ref,
                 nfin_ref, y_ref, ybuf0, ybuf1, sem, *, tm):
    i = pl.program_id(0)
    last = pl.num_programs(0) - 1
    ybuf = (ybuf0, ybuf1)

    def issue(p_ref, p_off, slot):
        for r in range(tm):
            for j in range(2):
                pltpu.make_async_copy(ys_ref.at[pl.ds(p_ref[p_off + 2 * r + j], 1)],
                                      ybuf[slot].at[j, pl.ds(r, 1)], sem.at[slot]).start()

    def wait(slot):
        for j in range(2):
            pltpu.make_async_copy(ys_ref.at[pl.ds(0, tm)], ybuf[slot].at[j], sem.at[slot]).wait()

    def compute(slot):
        rows = pl.ds(slot * tm, tm)
        rt = rt_ref[rows, :]
        x2 = x1_ref[rows, :] + (rt[:, 4:5] * ybuf[slot][0] + rt[:, 5:6] * ybuf[slot][1])
        h3 = _rmsnorm(x2, nple_ref[...]).astype(BF16)
        ple = (jax.nn.sigmoid(_dot(h3, wpg_ref[...]))
               * _dot(p_ref[rows, :].astype(BF16), wpp_ref[...]))
        y_ref[rows, :] = _rmsnorm(x2 + ple, nfin_ref[...])

    @pl.when(i == 0)
    def _():
        issue(pos_ref, 0, 0)

    wait(0)
    issue(pos_ref, 2 * tm, 1)
    compute(0)
    wait(1)
    issue(posn_ref, 0, 0)
    compute(1)

    @pl.when(i == last)
    def _():
        wait(0)


def _tail(pos, x1, rt, p2d, ys, nple, wpg, wpp, nfin, *, tm):
    n = x1.shape[0]
    n_steps = n // (2 * tm)
    tok = lambda width: pl.BlockSpec((2 * tm, width), lambda i: (i, 0))
    full = lambda arr: pl.BlockSpec(arr.shape, lambda i: (0, 0))
    return pl.pallas_call(
        functools.partial(_tail_kernel, tm=tm),
        grid=(n_steps,),
        in_specs=[pl.BlockSpec((4 * tm,), lambda i: (i,), memory_space=pltpu.SMEM),
                  pl.BlockSpec((2 * tm,), lambda i: (2 * jnp.minimum(i + 1, n_steps - 1),),
                               memory_space=pltpu.SMEM),
                  tok(D_MODEL), tok(ROUTER_LANES), tok(PLE_DIM),
                  pl.BlockSpec(memory_space=pl.ANY),
                  full(nple), full(wpg), full(wpp), full(nfin)],
        out_specs=tok(D_MODEL),
        out_shape=jax.ShapeDtypeStruct((n, D_MODEL), F32),
        scratch_shapes=[pltpu.VMEM((2, tm, D_MODEL), F32), pltpu.VMEM((2, tm, D_MODEL), F32),
                        pltpu.SemaphoreType.DMA((2,))],
        compiler_params=pltpu.CompilerParams(
            dimension_semantics=("arbitrary",), vmem_limit_bytes=VMEM_LIMIT),
        name="moe_tail",
    )(pos, pos, x1, rt, p2d, ys, nple, wpg, wpp, nfin)


def _moe_tail(h2, rt, cnt, x1, p2d, wg, wu, wd, nple, wpg, wpp, nfin, *, td, tm):
    n = h2.shape[0]
    n_tiles_max = (2 * n) // EXPERT_TILE + N_EXPERTS
    pos, tile_expert, n_used = _route_tables(rt, cnt, n_tiles_max)
    xs = _dispatch(h2, pos, n_tiles_max * EXPERT_TILE, td=td)
    ys = _experts(xs, tile_expert, n_used, wg, wu, wd)
    return _tail(pos, x1, rt, p2d, ys, nple, wpg, wpp, nfin, tm=tm)


def kernel(x_prompt, x_sample, state_pool, state_shift, state_wkv, p_prompt, p_sample, norm_mix, w_in, pool_mix, pool_scale, w_branch_a, shift_mu, decay_w0, decay_w2, iclr_a0, iclr_a2, gate_g2, k_k, k_a, r_k, ln_x_w, ln_x_b, w_branch_b, w_out, norm_ffn, w_route_group, b_route_group, w_route_expert, b_route_expert, expert_gate, expert_up, expert_down, norm_ple, w_ple_gate, w_ple_proj, norm_final):
    l = 0
    bsz, seq, _ = x_prompt.shape
    dbsz, dseq, _ = x_sample.shape
    row = lambda vec: vec.reshape(1, -1).astype(F32)
    col = lambda vec: vec.reshape(-1, 1).astype(F32)
    perm = CHANNEL_PERM

    head_of = perm // HEAD_DIM
    ones_bd = (head_of[:, None] == head_of[None, :]).astype(np.float32)
    ones_bd = jnp.asarray(ones_bd, BF16)
    o1 = POOL_WIDTH
    o2 = o1 + SHIFT_WIDTH
    zperm = np.concatenate([perm, RWKV_WIDTH + perm, 2 * RWKV_WIDTH + perm,
                            np.arange(3 * RWKV_WIDTH, SHIFT_WIDTH)])
    w_z = w_in[l][:, o1:o2][:, zperm].astype(BF16)
    zeros_lora = jnp.zeros((RWKV_WIDTH, LORA_PAIR // 2), F32)
    w2t_pad = jnp.concatenate([decay_w2[l].T[perm], zeros_lora], axis=1).astype(BF16)
    a2t_pad = jnp.concatenate([zeros_lora, iclr_a2[l].T[perm]], axis=1).astype(BF16)
    w_router = jnp.concatenate(
        [w_route_expert[l], w_route_group[l],
         jnp.zeros((D_MODEL, ROUTER_LANES - N_EXPERTS - N_GROUPS), F32)], axis=1)
    wr_hi = w_router.astype(BF16)
    wr_lo = (w_router - wr_hi.astype(F32)).astype(BF16)
    b_router = jnp.concatenate(
        [b_route_expert[l], b_route_group[l],
         jnp.zeros((ROUTER_LANES - N_EXPERTS - N_GROUPS,), F32)]).reshape(1, -1)
    prep_w = [row(norm_mix[l]), w_in[l][:, :o1].astype(BF16), w_z, w_in[l][:, o2:].astype(BF16),
              col(shift_mu[l][zperm]), col(decay_w0[l][perm]), w2t_pad, col(iclr_a0[l][perm]), a2t_pad,
              gate_g2[l].T[perm].astype(BF16), col(k_k[l][perm]), col(k_a[l][perm]),
              col(r_k[l].reshape(-1)[perm]), ones_bd, pool_mix[l].astype(BF16), row(pool_scale[l]),
              w_branch_a[l].astype(BF16)]
    post_w = [col(ln_x_w[l][perm]), col(ln_x_b[l][perm]), ones_bd, w_branch_b[l][perm].astype(BF16),
              w_out[l].astype(BF16), row(norm_ffn[l]), wr_hi, wr_lo, b_router]
    moe_w = [expert_gate[l], expert_up[l], expert_down[l],
             row(norm_ple[l]), w_ple_gate[l].astype(BF16), w_ple_proj[l].astype(BF16), row(norm_final)]

    tm_p = 256
    tiles_p = seq // tm_p
    x_p = x_prompt.reshape(bsz * seq, D_MODEL)
    outs = _mix_prep(x_p, jnp.zeros((bsz * SHIFT_WIDTH, LANES), F32),
                     jnp.zeros((bsz * 16, POOL_WIDTH), F32), prep_w,
                     nb=bsz, tiles=tiles_p, tm=tm_p, s=1, pos0=0, cm_index=lambda b, t: (b, 0, t))
    u_p, zl_p, ops, g, bon, pa, sgb = outs
    o_p, s_p = _wkv_scan_prompt(ops, tt=LANES)
    x1, h2, rt, cnt = _post(o_p, bon, g, pa, sgb, x_p, post_w, tm=tm_p,
                            cm_index=lambda i: (i // tiles_p, 0, i % tiles_p))
    y_p = _moe_tail(h2, rt, cnt, x1, p_prompt[l].reshape(bsz * seq, PLE_DIM), *moe_w, td=512, tm=256)
    wkv_p = s_p.reshape(KEY_HALF, 2, KEY_HALF, 2, bsz, HEADS).transpose(4, 5, 3, 2, 1, 0)
    wkv_p = wkv_p.reshape(bsz, HEADS, HEAD_DIM, HEAD_DIM)

    n_s = dbsz * dseq
    x_s = x_sample.transpose(1, 0, 2).reshape(n_s, D_MODEL)
    p_s = p_sample[l].transpose(1, 0, 2).reshape(n_s, PLE_DIM)
    uc0_s = jnp.concatenate(
        [jnp.zeros((dbsz, POOL_WIDTH), F32),
         state_pool[l].transpose(1, 0, 2).reshape(POOL_STATE * dbsz, POOL_WIDTH)], axis=0)
    zc0_s = state_shift[l].T[zperm]
    outs = _mix_prep(x_s, zc0_s, uc0_s, prep_w, nb=1, tiles=dseq, tm=dbsz, s=dbsz, pos0=PAST_LEN,
                     cm_index=lambda b, t: (t, 0, 0))
    u_s, zl_s, ops, g, bon, pa, sgb = outs
    s0_s = state_wkv[l].reshape(dbsz, HEADS, 2, KEY_HALF, 2, KEY_HALF).transpose(1, 5, 4, 3, 2, 0)
    s0_s = s0_s.reshape(HEADS, HEAD_DIM, HEAD_DIM, dbsz)
    o_s, s_s = _wkv_scan_sample(ops, s0_s)
    x1, h2, rt, cnt = _post(o_s, bon, g, pa, sgb, x_s, post_w, tm=dbsz, cm_index=lambda i: (i, 0, 0))
    y_s = _moe_tail(h2, rt, cnt, x1, p_s, *moe_w, td=n_s, tm=256)
    wkv_s = s_s.reshape(HEADS, KEY_HALF, 2, KEY_HALF, 2, dbsz).transpose(5, 0, 4, 3, 2, 1)
    wkv_s = wkv_s.reshape(dbsz, HEADS, HEAD_DIM, HEAD_DIM)

    inv_z = np.argsort(zperm)
    y_prompt = y_p.reshape(bsz, seq, D_MODEL)
    y_sample = y_s.reshape(dseq, dbsz, D_MODEL).transpose(1, 0, 2)
    pool_prompt = u_p.reshape(bsz, seq, POOL_WIDTH)[:, seq - POOL_STATE:]
    u_s_bt = u_s.reshape(dseq, dbsz, POOL_WIDTH).transpose(1, 0, 2)
    pool_sample = jnp.concatenate([state_pool[l][:, dseq:], u_s_bt], axis=1)
    shift_prompt = zl_p[:, :, LANES - 1][:, inv_z]
    shift_sample = zl_s[0].T[:, inv_z]
    return (y_prompt, y_sample, pool_prompt[None], shift_prompt[None], wkv_p[None],
            pool_sample[None], shift_sample[None], wkv_s[None])
```

```python
import functools

import jax
import jax.numpy as jnp
import numpy as np
from jax import lax
from jax.experimental import pallas as pl
from jax.experimental.pallas import tpu as pltpu

F32 = jnp.float32
BF16 = jnp.bfloat16

D_MODEL = 1024
PLE_DIM = 256
POOL_WIDTH = 512
POOL_WINDOWS = (2, 4, 8, 16)
POOL_GROUP_DIM = 128
POOL_STATE = 15
RWKV_WIDTH = 512
HEAD_DIM = 64
HEADS = 8
LORA_PAIR = 128
GATE_LORA = 128
SHIFT_WIDTH = 3 * RWKV_WIDTH + LORA_PAIR + GATE_LORA
IN_WIDTH = POOL_WIDTH + SHIFT_WIDTH + 2 * D_MODEL
N_GROUPS = 4
EXPERTS_PER_GROUP = 8
N_EXPERTS = 32
D_EXPERT = 256
RMS_EPS = 1e-6
GN_EPS = 64e-5
PAST_LEN = 16384

LANES = 128
SUBLANES = 8
KEY_HALF = HEAD_DIM // 2
SCAN_OPERANDS = 6
KEY_GROUP = 16
ROUTER_LANES = 128
EXPERT_TILE = 256
NEG_BIG = -1e30
VMEM_LIMIT = 56 * 1024 * 1024

_J = np.arange(RWKV_WIDTH)
CHANNEL_PERM = (_J % 8) * HEAD_DIM + ((_J // 8) % 2) * KEY_HALF + _J // 16


def _dot(a, b):
    return jnp.dot(a, b, preferred_element_type=F32)


def _split_bf16(x):
    hi = x.astype(BF16)
    return hi, (x - hi.astype(F32)).astype(BF16)


def _head_sum(x):
    x3 = x.reshape(HEAD_DIM, HEADS, x.shape[1])
    s = jnp.sum(x3, axis=0, keepdims=True)
    return jnp.broadcast_to(s, x3.shape).reshape(x.shape)


def _rmsnorm(x, g):
    return x * lax.rsqrt(jnp.mean(x * x, axis=-1, keepdims=True) + RMS_EPS) * g


def _softplus(y):
    return jnp.maximum(y, 0.0) + jnp.log1p(jnp.exp(-jnp.abs(y)))


def _mix_prep_kernel(x_ref, zc0_ref, uc0_ref, nmix_ref, wu_ref, wz_ref, wgab_ref, mu_ref, w0_ref,
                     w2t_ref, a0_ref, a2t_ref, g2t_ref, kk_ref, ka_ref, rk_ref, mix_ref,
                     pscale_ref, wa_ref,
                     u_ref, zl_ref, q_ref, g_ref, bon_ref,
                     pa_ref, sgb_ref, zc, uext, *, tm, s, pos0):
    t = pl.program_id(1)
    up = 16 * s

    @pl.when(t == 0)
    def _():
        zc[...] = zc0_ref[...]
        uext[0:up] = uc0_ref[...]

    x = x_ref[...]
    h = _rmsnorm(x, nmix_ref[...]).astype(BF16)
    u = _dot(h, wu_ref[...])
    gab = _dot(h, wgab_ref[...])
    u_ref[...] = u
    sgb_ref[...] = jax.nn.sigmoid(gab[:, D_MODEL:])
    uext[up:up + tm] = u

    z_t = _dot(h, wz_ref[...]).T
    if s == 1:
        lane = lax.broadcasted_iota(jnp.int32, z_t.shape, 1)
        zprev = jnp.where(lane == 0, zc[:, LANES - 1:LANES], pltpu.roll(z_t, 1, axis=1))
    else:
        zprev = zc[...]
    zc[...] = z_t[:, tm - LANES:tm]
    zl_ref[0] = z_t[:, tm - LANES:tm]
    zm = z_t + (zprev - z_t) * mu_ref[...]
    r = zm[0:RWKV_WIDTH]
    k = zm[RWKV_WIDTH:2 * RWKV_WIDTH]
    v = zm[2 * RWKV_WIDTH:3 * RWKV_WIDTH]
    lora_in = zm[3 * RWKV_WIDTH:3 * RWKV_WIDTH + LORA_PAIR]
    gd = zm[3 * RWKV_WIDTH + LORA_PAIR:SHIFT_WIDTH]
    dw = _dot(w2t_ref[...], jnp.tanh(lora_in).astype(BF16))
    da = _dot(a2t_ref[...], lora_in.astype(BF16))
    w_log = -_softplus(-(w0_ref[...] + dw)) - 0.5
    a = jax.nn.sigmoid(a0_ref[...] + da)
    kk = k * kk_ref[...]
    kk = kk / jnp.maximum(jnp.sqrt(_head_sum(kk * kk)), 1e-12)
    k2 = k * (1.0 + (a - 1.0) * ka_ref[...])
    q_ref[0, 0] = -kk
    q_ref[1, 0] = jnp.exp(-jnp.exp(w_log))
    q_ref[2, 0] = kk * a
    q_ref[3, 0] = k2
    q_ref[4, 0] = r
    q_ref[5, 0] = v
    g_ref[0] = _dot(g2t_ref[...], jax.nn.sigmoid(gd).astype(BF16))
    bon_ref[0] = _head_sum(r * k2 * rk_ref[...]) * v

    rows = lax.broadcasted_iota(jnp.int32, (tm, POOL_GROUP_DIM), 0)
    if s > 1:
        rows = rows // s
    pos = pos0 + t * (tm // s) + rows
    ys = []
    for gi, wnd in enumerate(POOL_WINDOWS):
        lanes = slice(gi * POOL_GROUP_DIM, (gi + 1) * POOL_GROUP_DIM)
        cur = uext[pl.ds(up, tm), lanes]
        acc = cur
        for j in range(1, wnd):
            acc = acc + uext[pl.ds(up - j * s, tm), lanes]
        cnt = jnp.minimum(pos + 1, wnd).astype(F32)
        pooled = acc / cnt - cur
        ys.append(_dot(pooled.astype(BF16), mix_ref[gi]))
    y = jnp.concatenate(ys, axis=-1) * pscale_ref[...]
    pa_ref[...] = jax.nn.sigmoid(gab[:, :D_MODEL]) * _dot(y.astype(BF16), wa_ref[...])

    uext[0:up] = uext[tm:tm + up]


def _mix_prep(x2d, zc0, uc0, wts, *, nb, tiles, tm, s, pos0, cm_index):
    n = x2d.shape[0]
    up = 16 * s
    row = lambda b, t: (b * tiles + t, 0)
    full = lambda arr: pl.BlockSpec(arr.shape, lambda b, t: (0,) * arr.ndim)
    in_specs = [
        pl.BlockSpec((tm, D_MODEL), row),
        pl.BlockSpec((SHIFT_WIDTH, LANES), lambda b, t: (b, 0)),
        pl.BlockSpec((up, POOL_WIDTH), lambda b, t: (b, 0)),
    ] + [full(w) for w in wts]
    tok = lambda width: pl.BlockSpec((tm, width), row)
    cm_shape = (nb, RWKV_WIDTH, tiles * tm) if s == 1 else (tiles, RWKV_WIDTH, tm)
    cm = pl.BlockSpec((1, RWKV_WIDTH, tm), cm_index)
    cm_stack = pl.BlockSpec((SCAN_OPERANDS, 1, RWKV_WIDTH, tm), lambda b, t: (0,) + cm_index(b, t))
    out_specs = [tok(POOL_WIDTH), pl.BlockSpec((1, SHIFT_WIDTH, LANES), lambda b, t: (b, 0, 0)),
                 cm_stack, cm, cm] + [tok(D_MODEL)] * 2
    out_shape = [jax.ShapeDtypeStruct((n, POOL_WIDTH), F32),
                 jax.ShapeDtypeStruct((nb, SHIFT_WIDTH, LANES), F32),
                 jax.ShapeDtypeStruct((SCAN_OPERANDS,) + cm_shape, F32)] \
        + [jax.ShapeDtypeStruct(cm_shape, F32)] * 2 \
        + [jax.ShapeDtypeStruct((n, D_MODEL), F32)] * 2
    return pl.pallas_call(
        functools.partial(_mix_prep_kernel, tm=tm, s=s, pos0=pos0),
        grid=(nb, tiles),
        in_specs=in_specs,
        out_specs=out_specs,
        out_shape=out_shape,
        scratch_shapes=[pltpu.VMEM((SHIFT_WIDTH, LANES), F32),
                        pltpu.VMEM((up + tm, POOL_WIDTH), F32)],
        compiler_params=pltpu.CompilerParams(
            dimension_semantics=("arbitrary", "arbitrary"), vmem_limit_bytes=VMEM_LIMIT),
        name="mix_prep",
    )(x2d, zc0, uc0, *wts)


def _wkv_step(s_ref, row, vv):
    groups = HEAD_DIM // KEY_GROUP

    def sa_pass(g, acc):
        for kk in range(KEY_GROUP):
            kp = g * KEY_GROUP + kk
            acc = acc + s_ref[kp] * row(0, kp)
        return acc
    sa = lax.fori_loop(0, groups, sa_pass, jnp.zeros(vv.shape, F32))

    def update_pass(g, acc):
        for kk in range(KEY_GROUP):
            kp = g * KEY_GROUP + kk
            sn = s_ref[kp] * row(1, kp) + sa * row(2, kp) + vv * row(3, kp)
            s_ref[kp] = sn
            acc = acc + sn * row(4, kp)
        return acc
    return lax.fori_loop(0, groups, update_pass, jnp.zeros(vv.shape, F32))


def _swap_major_sublane(x):
    return jnp.swapaxes(x, 0, 1)


def _wkv_scan_prompt_kernel(q_ref, o_ref, sout_ref, s_ref, kv_ref, vv_ref, ov_ref, *, tt):
    t = pl.program_id(0)
    q = pl.program_id(1)

    @pl.when((t == 0) & (q == 0))
    def _():
        s_ref[...] = jnp.zeros_like(s_ref)

    def gather_t(base_lo, base_hi):
        pieces = [q_ref[0, bb, pl.ds(base, SUBLANES), :] for base in (base_lo, base_hi) for bb in range(8)]
        return jnp.concatenate(pieces, axis=0).T

    @pl.when(q < SCAN_OPERANDS - 1)
    def _():
        def kgroup(gi, c):
            slabs = []
            for kk in range(SUBLANES):
                base = pl.multiple_of((gi * SUBLANES + kk) * SUBLANES, SUBLANES)
                slabs.append(gather_t(base, base))
            kv_ref[q, :, pl.ds(pl.multiple_of(gi * SUBLANES, SUBLANES), SUBLANES), :] = \
                _swap_major_sublane(jnp.stack(slabs))
            return c
        lax.fori_loop(0, HEAD_DIM // SUBLANES, kgroup, 0)

    @pl.when(q == SCAN_OPERANDS - 1)
    def _():
        def vgroup(gi, c):
            slabs = []
            for vi in range(SUBLANES):
                base = pl.multiple_of((gi * SUBLANES + vi) * 2 * SUBLANES, 2 * SUBLANES)
                slabs.append(gather_t(base, base + SUBLANES))
            vv_ref[:, pl.ds(pl.multiple_of(gi * SUBLANES, SUBLANES), SUBLANES), :] = \
                _swap_major_sublane(jnp.stack(slabs))
            return c
        lax.fori_loop(0, KEY_HALF // SUBLANES, vgroup, 0)

        def step(i, c):
            row = lambda qi, kp: kv_ref[qi, i, pl.ds(kp, 1), :]
            ov_ref[i] = _wkv_step(s_ref, row, vv_ref[i])
            return c
        lax.fori_loop(0, tt, step, 0)

        def ogroup(gi, c):
            g0 = pl.multiple_of(gi * SUBLANES, SUBLANES)
            x = _swap_major_sublane(ov_ref[:, pl.ds(g0, SUBLANES), :])
            for vi in range(SUBLANES):
                xt = x[vi].T
                base = pl.multiple_of((gi * SUBLANES + vi) * 2 * SUBLANES, 2 * SUBLANES)
                for vh in range(2):
                    for bb in range(8):
                        r0 = (vh * 8 + bb) * SUBLANES
                        o_ref[bb, pl.ds(base + vh * SUBLANES, SUBLANES), :] = xt[r0:r0 + SUBLANES]
            return c
        lax.fori_loop(0, KEY_HALF // SUBLANES, ogroup, 0)

        @pl.when(t == pl.num_programs(0) - 1)
        def _():
            sout_ref[...] = s_ref[...]


def _wkv_scan_prompt(ops, *, tt):
    _, nb, _, t_len = ops.shape
    sspec = pl.BlockSpec((HEAD_DIM, KEY_HALF, LANES), lambda ti, qi: (0, 0, 0))
    return pl.pallas_call(
        functools.partial(_wkv_scan_prompt_kernel, tt=tt),
        grid=(t_len // tt, SCAN_OPERANDS),
        in_specs=[pl.BlockSpec((1, nb, RWKV_WIDTH, tt), lambda ti, qi: (qi, 0, 0, ti))],
        out_specs=[pl.BlockSpec((nb, RWKV_WIDTH, tt), lambda ti, qi: (0, 0, ti)), sspec],
        out_shape=[jax.ShapeDtypeStruct((nb, RWKV_WIDTH, t_len), F32),
                   jax.ShapeDtypeStruct((HEAD_DIM, KEY_HALF, LANES), F32)],
        scratch_shapes=[pltpu.VMEM((HEAD_DIM, KEY_HALF, LANES), F32),
                        pltpu.VMEM((SCAN_OPERANDS - 1, tt, HEAD_DIM, LANES), F32),
                        pltpu.VMEM((tt, KEY_HALF, LANES), F32),
                        pltpu.VMEM((tt, KEY_HALF, LANES), F32)],
        compiler_params=pltpu.CompilerParams(
            dimension_semantics=("arbitrary", "arbitrary"), vmem_limit_bytes=VMEM_LIMIT),
        name="wkv_scan_prompt",
    )(ops)


def _wkv_scan_sample_kernel(q_ref, s0_ref, o_ref, sout_ref, s_ref, *, t_len):
    h = pl.program_id(0)
    s_ref[...] = s0_ref[0]
    for i in range(t_len):
        row = lambda qi, kp, i=i: q_ref[qi, i, pl.ds(kp * HEADS + h, 1), :]
        vv = q_ref[SCAN_OPERANDS - 1, i, pl.ds(h, HEAD_DIM, stride=HEADS), :]
        o_ref[i, pl.ds(h, HEAD_DIM, stride=HEADS), :] = _wkv_step(s_ref, row, vv)
    sout_ref[0] = s_ref[...]


def _wkv_scan_sample(ops, s0):
    t_len = ops.shape[1]
    spec = pl.BlockSpec(ops.shape[1:], lambda h: (0, 0, 0))
    sspec = pl.BlockSpec((1, HEAD_DIM, HEAD_DIM, LANES), lambda h: (h, 0, 0, 0))
    return pl.pallas_call(
        functools.partial(_wkv_scan_sample_kernel, t_len=t_len),
        grid=(HEADS,),
        in_specs=[pl.BlockSpec(ops.shape, lambda h: (0, 0, 0, 0)), sspec],
        out_specs=[spec, sspec],
        out_shape=[jax.ShapeDtypeStruct(ops.shape[1:], F32), jax.ShapeDtypeStruct(s0.shape, F32)],
        scratch_shapes=[pltpu.VMEM((HEAD_DIM, HEAD_DIM, LANES), F32)],
        compiler_params=pltpu.CompilerParams(
            dimension_semantics=("arbitrary",), vmem_limit_bytes=VMEM_LIMIT),
        name="wkv_scan_sample",
    )(ops, s0)


def _post_kernel(o_ref, bon_ref, g_ref, pa_ref, sgb_ref, x_ref, cnt0_ref, lnw_ref, lnb_ref, wb_ref,
                 wout_ref, nffn_ref, wrh_ref, wrl_ref, br_ref, x1_ref, h2_ref, rt_ref, cnt_ref, carry,
                 *, sub):
    @pl.when(pl.program_id(0) == 0)
    def _():
        carry[...] = cnt0_ref[...]

    prev = carry[0:1, :]
    for part in range(o_ref.shape[2] // sub):
        prev = _post_rows(pl.ds(part * sub, sub), prev, o_ref, bon_ref, g_ref, pa_ref, sgb_ref, x_ref,
                          lnw_ref, lnb_ref, wb_ref, wout_ref, nffn_ref, wrh_ref, wrl_ref,
                          br_ref, x1_ref, h2_ref, rt_ref)
    carry[...] = jnp.broadcast_to(prev, carry.shape)
    cnt_ref[...] = jnp.broadcast_to(prev, cnt_ref.shape)


def _post_rows(rows, prev_count, o_ref, bon_ref, g_ref, pa_ref, sgb_ref, x_ref, lnw_ref, lnb_ref,
               wb_ref, wout_ref, nffn_ref, wrh_ref, wrl_ref, br_ref, x1_ref, h2_ref, rt_ref):
    o = o_ref[0, :, rows]
    mean = _head_sum(o) * (1.0 / HEAD_DIM)
    d = o - mean
    var = _head_sum(d * d) * (1.0 / HEAD_DIM)
    on = d * lax.rsqrt(var + GN_EPS) * lnw_ref[...] + lnb_ref[...]
    yb = ((on + bon_ref[0, :, rows]) * g_ref[0, :, rows]).astype(BF16)
    mb = lax.dot_general(yb, wb_ref[...], (((0,), (0,)), ((), ())), preferred_element_type=F32)
    merged = pa_ref[rows, :] + sgb_ref[rows, :] * mb
    x1 = x_ref[rows, :] + _dot(merged.astype(BF16), wout_ref[...])
    x1_ref[rows, :] = x1
    h2 = _rmsnorm(x1, nffn_ref[...])
    h2_ref[rows, :] = h2

    h_hi, h_lo = _split_bf16(h2)
    logits = (_dot(h_hi, wrh_ref[...]) + _dot(h_lo, wrh_ref[...]) + _dot(h_hi, wrl_ref[...])
              + br_ref[...])
    ln = lax.broadcasted_iota(jnp.int32, logits.shape, 1)
    is_group = (ln >= N_EXPERTS) & (ln < N_EXPERTS + N_GROUPS)
    gl = jnp.where(is_group, logits, NEG_BIG)
    gmax = jnp.max(gl, axis=-1, keepdims=True)
    gsel = jnp.min(jnp.where(gl == gmax, ln, ROUTER_LANES), axis=-1, keepdims=True) - N_EXPERTS
    den = jnp.sum(jnp.where(is_group, jnp.exp(gl - gmax), 0.0), axis=-1, keepdims=True)
    pg = 1.0 / den
    in_group = (ln < N_EXPERTS) & ((ln // EXPERTS_PER_GROUP) == gsel)
    el = jnp.where(in_group, logits, NEG_BIG)
    m1 = jnp.max(el, axis=-1, keepdims=True)
    i1 = jnp.min(jnp.where(el == m1, ln, ROUTER_LANES), axis=-1, keepdims=True)
    el2 = jnp.where(ln == i1, NEG_BIG, el)
    m2 = jnp.max(el2, axis=-1, keepdims=True)
    i2 = jnp.min(jnp.where(el2 == m2, ln, ROUTER_LANES), axis=-1, keepdims=True)
    e2 = jnp.exp(m2 - m1)
    p1 = 1.0 / (1.0 + e2)
    p2 = e2 / (1.0 + e2)

    tm = logits.shape[0]
    sel = (ln == i1) | (ln == i2)
    tri = (lax.broadcasted_iota(jnp.int32, (tm, tm), 1)
           < lax.broadcasted_iota(jnp.int32, (tm, tm), 0)).astype(BF16)
    before = prev_count + _dot(tri, sel.astype(BF16))
    r1 = jnp.sum(jnp.where(ln == i1, before, 0.0), axis=-1, keepdims=True)
    r2 = jnp.sum(jnp.where(ln == i2, before, 0.0), axis=-1, keepdims=True)
    fields = (i1.astype(F32), i2.astype(F32), r1, r2, p1 * pg, p2 * pg)
    rt = jnp.zeros(logits.shape, F32)
    for lane_idx, val in enumerate(fields):
        rt = jnp.where(ln == lane_idx, val, rt)
    rt_ref[rows, :] = rt
    return prev_count + jnp.sum(sel.astype(F32), axis=0, keepdims=True)


def _post(o, bon, g, pa, sgb, x2d, cnt0, wts, *, tm, sub, cm_index):
    n = x2d.shape[0]
    row = lambda i: (i, 0)
    tok = lambda width: pl.BlockSpec((tm, width), row)
    cm = pl.BlockSpec((1, RWKV_WIDTH, tm), cm_index)
    full = lambda arr: pl.BlockSpec(arr.shape, lambda i: (0, 0))
    return pl.pallas_call(
        functools.partial(_post_kernel, sub=sub),
        grid=(n // tm,),
        in_specs=[cm] * 3 + [tok(D_MODEL)] * 3 + [full(cnt0)] + [full(w) for w in wts],
        out_specs=[tok(D_MODEL), tok(D_MODEL), tok(ROUTER_LANES),
                   pl.BlockSpec((SUBLANES, ROUTER_LANES), lambda i: (0, 0))],
        out_shape=[jax.ShapeDtypeStruct((n, D_MODEL), F32),
                   jax.ShapeDtypeStruct((n, D_MODEL), F32),
                   jax.ShapeDtypeStruct((n, ROUTER_LANES), F32),
                   jax.ShapeDtypeStruct((SUBLANES, ROUTER_LANES), F32)],
        scratch_shapes=[pltpu.VMEM((SUBLANES, ROUTER_LANES), F32)],
        compiler_params=pltpu.CompilerParams(
            dimension_semantics=("arbitrary",), vmem_limit_bytes=VMEM_LIMIT),
        name="post",
    )(o, bon, g, pa, sgb, x2d, cnt0, *wts)


def _route_tables(cnt, n_tiles_max):
    counts = cnt[0, :N_EXPERTS].astype(jnp.int32)
    tiles_e = (counts + EXPERT_TILE - 1) // EXPERT_TILE
    tile_end = jnp.cumsum(tiles_e)
    row0 = (tile_end - tiles_e) * EXPERT_TILE
    tile_ids = jnp.arange(n_tiles_max, dtype=jnp.int32)
    tile_expert = jnp.minimum(jnp.sum(tile_ids[:, None] >= tile_end[None, :], axis=-1),
                              N_EXPERTS - 1).astype(jnp.int32)
    return tile_expert, tile_end[N_EXPERTS - 1:].astype(jnp.int32), row0


def _assignment_rows(rt, row0):
    e = rt[:, 0:2].astype(jnp.int32)
    r = rt[:, 2:4].astype(jnp.int32)
    pos = r + jnp.sum(jnp.where(e[..., None] == jnp.arange(N_EXPERTS), row0, 0), axis=-1)
    return pos.reshape(-1)


def _dispatch_kernel(pos_ref, h2_ref, xs0_ref, xs_ref, sem, *, td):
    del xs0_ref

    for r in range(td):
        for j in range(2):
            pltpu.make_async_copy(h2_ref.at[pl.ds(r, 1)],
                                  xs_ref.at[pl.ds(pos_ref[2 * r + j], 1)], sem).start()
    for j in range(2):
        pltpu.make_async_copy(h2_ref, xs_ref.at[pl.ds(0, td)], sem).wait()


def _dispatch(h2, pos, xs_in, *, td):
    n = h2.shape[0]
    n_rows = xs_in.shape[0]
    return pl.pallas_call(
        functools.partial(_dispatch_kernel, td=td),
        grid=(n // td,),
        in_specs=[pl.BlockSpec((2 * td,), lambda i: (i,), memory_space=pltpu.SMEM),
                  pl.BlockSpec((td, D_MODEL), lambda i: (i, 0)),
                  pl.BlockSpec(memory_space=pl.ANY)],
        out_specs=pl.BlockSpec(memory_space=pl.ANY),
        out_shape=jax.ShapeDtypeStruct((n_rows, D_MODEL), F32),
        scratch_shapes=[pltpu.SemaphoreType.DMA(())],
        input_output_aliases={2: 0},
        compiler_params=pltpu.CompilerParams(
            dimension_semantics=("arbitrary",), vmem_limit_bytes=VMEM_LIMIT),
        name="moe_dispatch",
    )(pos, h2, xs_in)


def _experts_kernel(te_ref, nu_ref, xs_ref, wg_ref, wu_ref, wd_ref, ys_ref, wgu, wdn):
    i = pl.program_id(0)

    @pl.when(i < nu_ref[0])
    def _():
        @pl.when((i == 0) | (te_ref[i] != te_ref[jnp.maximum(i - 1, 0)]))
        def _():
            wgu[:, 0:D_EXPERT] = wg_ref[0].astype(BF16)
            wgu[:, D_EXPERT:2 * D_EXPERT] = wu_ref[0].astype(BF16)
            wdn[...] = wd_ref[0].astype(BF16)

        gu = _dot(xs_ref[...].astype(BF16), wgu[...])
        hid = jax.nn.silu(gu[:, 0:D_EXPERT]) * gu[:, D_EXPERT:2 * D_EXPERT]
        ys_ref[...] = _dot(hid.astype(BF16), wdn[...])

    @pl.when(i >= nu_ref[0])
    def _():
        ys_ref[...] = jnp.zeros_like(ys_ref)


def _experts(xs, tile_expert, n_used, wg, wu, wd):
    n_tiles = xs.shape[0] // EXPERT_TILE
    tile = lambda i, te, nu: (jnp.minimum(i, nu[0] - 1), 0)
    wsel = lambda i, te, nu: (te[jnp.minimum(i, nu[0] - 1)], 0, 0)
    return pl.pallas_call(
        _experts_kernel,
        grid_spec=pltpu.PrefetchScalarGridSpec(
            num_scalar_prefetch=2,
            grid=(n_tiles,),
            in_specs=[pl.BlockSpec((EXPERT_TILE, D_MODEL), tile),
                      pl.BlockSpec((1, D_MODEL, D_EXPERT), wsel),
                      pl.BlockSpec((1, D_MODEL, D_EXPERT), wsel),
                      pl.BlockSpec((1, D_EXPERT, D_MODEL), wsel)],
            out_specs=pl.BlockSpec((EXPERT_TILE, D_MODEL), lambda i, te, nu: (i, 0)),
            scratch_shapes=[pltpu.VMEM((D_MODEL, 2 * D_EXPERT), BF16),
                            pltpu.VMEM((D_EXPERT, D_MODEL), BF16)]),
        out_shape=jax.ShapeDtypeStruct(xs.shape, F32),
        compiler_params=pltpu.CompilerParams(
            dimension_semantics=("arbitrary",), vmem_limit_bytes=VMEM_LIMIT),
        name="moe_experts",
    )(tile_expert, n_used, xs, wg, wu, wd)


def _tail_kernel(pos_ref, posn_ref, x1_ref, rt_ref, p_ref, ys_ref, nple_ref, wpg_ref, wpp_ref,
                 nfin_ref, y_ref, ybuf0, ybuf1, sem, *, tm):
    i = pl.program_id(0)
    last = pl.num_programs(0) - 1
    ybuf = (ybuf0, ybuf1)

    def issue(p_ref, p_off, slot):
        for r in range(tm):
            for j in range(2):
                pltpu.make_async_copy(ys_ref.at[pl.ds(p_ref[p_off + 2 * r + j], 1)],
                                      ybuf[slot].at[j, pl.ds(r, 1)], sem.at[slot]).start()

    def wait(slot):
        for j in range(2):
            pltpu.make_async_copy(ys_ref.at[pl.ds(0, tm)], ybuf[slot].at[j], sem.at[slot]).wait()

    def compute(slot):
        rows = pl.ds(slot * tm, tm)
        rt = rt_ref[rows, :]
        x2 = x1_ref[rows, :] + (rt[:, 4:5] * ybuf[slot][0] + rt[:, 5:6] * ybuf[slot][1])
        h3 = _rmsnorm(x2, nple_ref[...]).astype(BF16)
        ple = (jax.nn.sigmoid(_dot(h3, wpg_ref[...]))
               * _dot(p_ref[rows, :].astype(BF16), wpp_ref[...]))
        y_ref[rows, :] = _rmsnorm(x2 + ple, nfin_ref[...])

    @pl.when(i == 0)
    def _():
        issue(pos_ref, 0, 0)

    wait(0)
    issue(pos_ref, 2 * tm, 1)
    compute(0)
    wait(1)
    issue(posn_ref, 0, 0)
    compute(1)

    @pl.when(i == last)
    def _():
        wait(0)


def _tail(pos, x1, rt, p2d, ys, nple, wpg, wpp, nfin, *, tm):
    n = x1.shape[0]
    n_steps = n // (2 * tm)
    tok = lambda width: pl.BlockSpec((2 * tm, width), lambda i: (i, 0))
    full = lambda arr: pl.BlockSpec(arr.shape, lambda i: (0, 0))
    return pl.pallas_call(
        functools.partial(_tail_kernel, tm=tm),
        grid=(n_steps,),
        in_specs=[pl.BlockSpec((4 * tm,), lambda i: (i,), memory_space=pltpu.SMEM),
                  pl.BlockSpec((2 * tm,), lambda i: (2 * jnp.minimum(i + 1, n_steps - 1),),
                               memory_space=pltpu.SMEM),
                  tok(D_MODEL), tok(ROUTER_LANES), tok(PLE_DIM),
                  pl.BlockSpec(memory_space=pl.ANY),
                  full(nple), full(wpg), full(wpp), full(nfin)],
        out_specs=tok(D_MODEL),
        out_shape=jax.ShapeDtypeStruct((n, D_MODEL), F32),
        scratch_shapes=[pltpu.VMEM((2, tm, D_MODEL), F32), pltpu.VMEM((2, tm, D_MODEL), F32),
                        pltpu.SemaphoreType.DMA((2,))],
        compiler_params=pltpu.CompilerParams(
            dimension_semantics=("arbitrary",), vmem_limit_bytes=VMEM_LIMIT),
        name="moe_tail",
    )(pos, pos, x1, rt, p2d, ys, nple, wpg, wpp, nfin)


def _moe_tail(groups, cnt, wg, wu, wd, nple, wpg, wpp, nfin, *, tm):
    n_all = sum(g[0].shape[0] for g in groups)
    n_tiles_max = (2 * n_all) // EXPERT_TILE + N_EXPERTS
    tile_expert, n_used, row0 = _route_tables(cnt, n_tiles_max)
    xs = jnp.zeros((n_tiles_max * EXPERT_TILE, D_MODEL), F32)
    poss = []
    for h2, rt, _, _, td in groups:
        poss.append(_assignment_rows(rt, row0))
        xs = _dispatch(h2, poss[-1], xs, td=td)
    ys = _experts(xs, tile_expert, n_used, wg, wu, wd)
    return [_tail(pos, x1, rt, p2d, ys, nple, wpg, wpp, nfin, tm=tm)
            for pos, (_, rt, x1, p2d, _) in zip(poss, groups)]


def kernel(x_prompt, x_sample, state_pool, state_shift, state_wkv, p_prompt, p_sample, norm_mix, w_in, pool_mix, pool_scale, w_branch_a, shift_mu, decay_w0, decay_w2, iclr_a0, iclr_a2, gate_g2, k_k, k_a, r_k, ln_x_w, ln_x_b, w_branch_b, w_out, norm_ffn, w_route_group, b_route_group, w_route_expert, b_route_expert, expert_gate, expert_up, expert_down, norm_ple, w_ple_gate, w_ple_proj, norm_final):
    l = 0
    bsz, seq, _ = x_prompt.shape
    dbsz, dseq, _ = x_sample.shape
    row = lambda vec: vec.reshape(1, -1).astype(F32)
    col = lambda vec: vec.reshape(-1, 1).astype(F32)
    perm = CHANNEL_PERM

    o1 = POOL_WIDTH
    o2 = o1 + SHIFT_WIDTH
    zperm = np.concatenate([perm, RWKV_WIDTH + perm, 2 * RWKV_WIDTH + perm,
                            np.arange(3 * RWKV_WIDTH, SHIFT_WIDTH)])
    w_z = w_in[l][:, o1:o2][:, zperm].astype(BF16)
    zeros_lora = jnp.zeros((RWKV_WIDTH, LORA_PAIR // 2), F32)
    w2t_pad = jnp.concatenate([decay_w2[l].T[perm], zeros_lora], axis=1).astype(BF16)
    a2t_pad = jnp.concatenate([zeros_lora, iclr_a2[l].T[perm]], axis=1).astype(BF16)
    w_router = jnp.concatenate(
        [w_route_expert[l], w_route_group[l],
         jnp.zeros((D_MODEL, ROUTER_LANES - N_EXPERTS - N_GROUPS), F32)], axis=1)
    wr_hi = w_router.astype(BF16)
    wr_lo = (w_router - wr_hi.astype(F32)).astype(BF16)
    b_router = jnp.concatenate(
        [b_route_expert[l], b_route_group[l],
         jnp.zeros((ROUTER_LANES - N_EXPERTS - N_GROUPS,), F32)]).reshape(1, -1)
    prep_w = [row(norm_mix[l]), w_in[l][:, :o1].astype(BF16), w_z, w_in[l][:, o2:].astype(BF16),
              col(shift_mu[l][zperm]), col(decay_w0[l][perm]), w2t_pad, col(iclr_a0[l][perm]), a2t_pad,
              gate_g2[l].T[perm].astype(BF16), col(k_k[l][perm]), col(k_a[l][perm]),
              col(r_k[l].reshape(-1)[perm]), pool_mix[l].astype(BF16), row(pool_scale[l]),
              w_branch_a[l].astype(BF16)]
    post_w = [col(ln_x_w[l][perm]), col(ln_x_b[l][perm]), w_branch_b[l][perm].astype(BF16),
              w_out[l].astype(BF16), row(norm_ffn[l]), wr_hi, wr_lo, b_router]
    moe_w = [expert_gate[l], expert_up[l], expert_down[l],
             row(norm_ple[l]), w_ple_gate[l].astype(BF16), w_ple_proj[l].astype(BF16), row(norm_final)]

    tm_p = 256
    tiles_p = seq // tm_p
    x_p = x_prompt.reshape(bsz * seq, D_MODEL)
    outs = _mix_prep(x_p, jnp.zeros((bsz * SHIFT_WIDTH, LANES), F32),
                     jnp.zeros((bsz * 16, POOL_WIDTH), F32), prep_w,
                     nb=bsz, tiles=tiles_p, tm=tm_p, s=1, pos0=0, cm_index=lambda b, t: (b, 0, t))
    u_p, zl_p, ops, g, bon, pa, sgb = outs
    o_p, s_p = _wkv_scan_prompt(ops, tt=LANES)
    tm_post = 2 * tm_p
    x1_p, h2_p, rt_p, cnt_p = _post(
        o_p, bon, g, pa, sgb, x_p, jnp.zeros((SUBLANES, ROUTER_LANES), F32), post_w,
        tm=tm_post, sub=tm_p, cm_index=lambda i: (i // (seq // tm_post), 0, i % (seq // tm_post)))
    wkv_p = s_p.reshape(KEY_HALF, 2, KEY_HALF, 2, bsz, HEADS).transpose(4, 5, 3, 2, 1, 0)
    wkv_p = wkv_p.reshape(bsz, HEADS, HEAD_DIM, HEAD_DIM)

    n_s = dbsz * dseq
    x_s = x_sample.transpose(1, 0, 2).reshape(n_s, D_MODEL)
    p_s = p_sample[l].transpose(1, 0, 2).reshape(n_s, PLE_DIM)
    uc0_s = jnp.concatenate(
        [jnp.zeros((dbsz, POOL_WIDTH), F32),
         state_pool[l].transpose(1, 0, 2).reshape(POOL_STATE * dbsz, POOL_WIDTH)], axis=0)
    zc0_s = state_shift[l].T[zperm]
    outs = _mix_prep(x_s, zc0_s, uc0_s, prep_w, nb=1, tiles=dseq, tm=dbsz, s=dbsz, pos0=PAST_LEN,
                     cm_index=lambda b, t: (t, 0, 0))
    u_s, zl_s, ops, g, bon, pa, sgb = outs
    s0_s = state_wkv[l].reshape(dbsz, HEADS, 2, KEY_HALF, 2, KEY_HALF).transpose(1, 5, 4, 3, 2, 0)
    s0_s = s0_s.reshape(HEADS, HEAD_DIM, HEAD_DIM, dbsz)
    o_s, s_s = _wkv_scan_sample(ops, s0_s)
    x1_s, h2_s, rt_s, cnt_all = _post(o_s, bon, g, pa, sgb, x_s, cnt_p, post_w, tm=dbsz, sub=dbsz,
                                      cm_index=lambda i: (i, 0, 0))

    y_p, y_s = _moe_tail(
        [(h2_p, rt_p, x1_p, p_prompt[l].reshape(bsz * seq, PLE_DIM), 512),
         (h2_s, rt_s, x1_s, p_s, n_s)], cnt_all, *moe_w, tm=256)
    wkv_s = s_s.reshape(HEADS, KEY_HALF, 2, KEY_HALF, 2, dbsz).transpose(5, 0, 4, 3, 2, 1)
    wkv_s = wkv_s.reshape(dbsz, HEADS, HEAD_DIM, HEAD_DIM)

    inv_z = np.argsort(zperm)
    y_prompt = y_p.reshape(bsz, seq, D_MODEL)
    y_sample = y_s.reshape(dseq, dbsz, D_MODEL).transpose(1, 0, 2)
    pool_prompt = u_p.reshape(bsz, seq, POOL_WIDTH)[:, seq - POOL_STATE:]
    u_s_bt = u_s.reshape(dseq, dbsz, POOL_WIDTH).transpose(1, 0, 2)
    pool_sample = jnp.concatenate([state_pool[l][:, dseq:], u_s_bt], axis=1)
    shift_prompt = zl_p[:, :, LANES - 1][:, inv_z]
    shift_sample = zl_s[0].T[:, inv_z]
    return (y_prompt, y_sample, pool_prompt[None], shift_prompt[None], wkv_p[None],
            pool_sample[None], shift_sample[None], wkv_s[None])
```

```python
import functools

import jax
import jax.numpy as jnp
import numpy as np
from jax import lax
from jax.experimental import pallas as pl
from jax.experimental.pallas import tpu as pltpu

F32 = jnp.float32
BF16 = jnp.bfloat16

D_MODEL = 1024
PLE_DIM = 256
POOL_WIDTH = 512
POOL_WINDOWS = (2, 4, 8, 16)
POOL_GROUP_DIM = 128
POOL_STATE = 15
RWKV_WIDTH = 512
HEAD_DIM = 64
HEADS = 8
LORA_PAIR = 128
GATE_LORA = 128
SHIFT_WIDTH = 3 * RWKV_WIDTH + LORA_PAIR + GATE_LORA
IN_WIDTH = POOL_WIDTH + SHIFT_WIDTH + 2 * D_MODEL
N_GROUPS = 4
EXPERTS_PER_GROUP = 8
N_EXPERTS = 32
D_EXPERT = 256
RMS_EPS = 1e-6
GN_EPS = 64e-5
PAST_LEN = 16384

LANES = 128
SUBLANES = 8
KEY_HALF = HEAD_DIM // 2
SCAN_OPERANDS = 6
KEY_GROUP = 16
ROUTER_LANES = 128
EXPERT_TILE = 256
NEG_BIG = -1e30
VMEM_LIMIT = 56 * 1024 * 1024

_J = np.arange(RWKV_WIDTH)
CHANNEL_PERM = (_J % 8) * HEAD_DIM + ((_J // 8) % 2) * KEY_HALF + _J // 16


def _dot(a, b):
    return jnp.dot(a, b, preferred_element_type=F32)


def _split_bf16(x):
    hi = x.astype(BF16)
    return hi, (x - hi.astype(F32)).astype(BF16)


def _head_sum(x):
    x3 = x.reshape(HEAD_DIM, HEADS, x.shape[1])
    s = jnp.sum(x3, axis=0, keepdims=True)
    return jnp.broadcast_to(s, x3.shape).reshape(x.shape)


def _rmsnorm(x, g):
    return x * lax.rsqrt(jnp.mean(x * x, axis=-1, keepdims=True) + RMS_EPS) * g


def _softplus(y):
    return jnp.maximum(y, 0.0) + jnp.log1p(jnp.exp(-jnp.abs(y)))


def _mix_prep_kernel(x_ref, zc0_ref, uc0_ref, nmix_ref, wu_ref, wz_ref, wgab_ref, mu_ref, w0_ref,
                     w2t_ref, a0_ref, a2t_ref, g2t_ref, kk_ref, ka_ref, rk_ref, mix_ref,
                     pscale_ref, wa_ref,
                     u_ref, zl_ref, q_ref, g_ref, bon_ref,
                     pa_ref, sgb_ref, zc, uext, *, tm, s, pos0):
    t = pl.program_id(1)
    up = 16 * s

    @pl.when(t == 0)
    def _():
        zc[...] = zc0_ref[...]
        uext[0:up] = uc0_ref[...]

    x = x_ref[...]
    h = _rmsnorm(x, nmix_ref[...]).astype(BF16)
    u = _dot(h, wu_ref[...])
    gab = _dot(h, wgab_ref[...])
    u_ref[...] = u
    sgb_ref[...] = jax.nn.sigmoid(gab[:, D_MODEL:])
    uext[up:up + tm] = u

    z_t = _dot(h, wz_ref[...]).T
    if s == 1:
        lane = lax.broadcasted_iota(jnp.int32, z_t.shape, 1)
        zprev = jnp.where(lane == 0, zc[:, LANES - 1:LANES], pltpu.roll(z_t, 1, axis=1))
    else:
        zprev = zc[...]
    zc[...] = z_t[:, tm - LANES:tm]
    zl_ref[0] = z_t[:, tm - LANES:tm]
    zm = z_t + (zprev - z_t) * mu_ref[...]
    r = zm[0:RWKV_WIDTH]
    k = zm[RWKV_WIDTH:2 * RWKV_WIDTH]
    v = zm[2 * RWKV_WIDTH:3 * RWKV_WIDTH]
    lora_in = zm[3 * RWKV_WIDTH:3 * RWKV_WIDTH + LORA_PAIR]
    gd = zm[3 * RWKV_WIDTH + LORA_PAIR:SHIFT_WIDTH]
    dw = _dot(w2t_ref[...], jnp.tanh(lora_in).astype(BF16))
    da = _dot(a2t_ref[...], lora_in.astype(BF16))
    w_log = -_softplus(-(w0_ref[...] + dw)) - 0.5
    a = jax.nn.sigmoid(a0_ref[...] + da)
    kk = k * kk_ref[...]
    kk = kk / jnp.maximum(jnp.sqrt(_head_sum(kk * kk)), 1e-12)
    k2 = k * (1.0 + (a - 1.0) * ka_ref[...])
    q_ref[0, 0] = -kk
    q_ref[1, 0] = jnp.exp(-jnp.exp(w_log))
    q_ref[2, 0] = kk * a
    q_ref[3, 0] = k2
    q_ref[4, 0] = r
    q_ref[5, 0] = v
    g_ref[0] = _dot(g2t_ref[...], jax.nn.sigmoid(gd).astype(BF16))
    bon_ref[0] = _head_sum(r * k2 * rk_ref[...]) * v

    rows = lax.broadcasted_iota(jnp.int32, (tm, POOL_GROUP_DIM), 0)
    if s > 1:
        rows = rows // s
    pos = pos0 + t * (tm // s) + rows
    ys = []
    for gi, wnd in enumerate(POOL_WINDOWS):
        lanes = slice(gi * POOL_GROUP_DIM, (gi + 1) * POOL_GROUP_DIM)
        cur = uext[pl.ds(up, tm), lanes]
        acc = cur
        for j in range(1, wnd):
            acc = acc + uext[pl.ds(up - j * s, tm), lanes]
        cnt = jnp.minimum(pos + 1, wnd).astype(F32)
        pooled = acc / cnt - cur
        ys.append(_dot(pooled.astype(BF16), mix_ref[gi]))
    y = jnp.concatenate(ys, axis=-1) * pscale_ref[...]
    pa_ref[...] = jax.nn.sigmoid(gab[:, :D_MODEL]) * _dot(y.astype(BF16), wa_ref[...])

    uext[0:up] = uext[tm:tm + up]


def _mix_prep(x2d, zc0, uc0, wts, *, nb, tiles, tm, s, pos0, cm_index):
    n = x2d.shape[0]
    up = 16 * s
    row = lambda b, t: (b * tiles + t, 0)
    full = lambda arr: pl.BlockSpec(arr.shape, lambda b, t: (0,) * arr.ndim)
    in_specs = [
        pl.BlockSpec((tm, D_MODEL), row),
        pl.BlockSpec((SHIFT_WIDTH, LANES), lambda b, t: (b, 0)),
        pl.BlockSpec((up, POOL_WIDTH), lambda b, t: (b, 0)),
    ] + [full(w) for w in wts]
    tok = lambda width: pl.BlockSpec((tm, width), row)
    cm_shape = (nb, RWKV_WIDTH, tiles * tm) if s == 1 else (tiles, RWKV_WIDTH, tm)
    cm = pl.BlockSpec((1, RWKV_WIDTH, tm), cm_index)
    cm_stack = pl.BlockSpec((SCAN_OPERANDS, 1, RWKV_WIDTH, tm), lambda b, t: (0,) + cm_index(b, t))
    out_specs = [tok(POOL_WIDTH), pl.BlockSpec((1, SHIFT_WIDTH, LANES), lambda b, t: (b, 0, 0)),
                 cm_stack, cm, cm] + [tok(D_MODEL)] * 2
    out_shape = [jax.ShapeDtypeStruct((n, POOL_WIDTH), F32),
                 jax.ShapeDtypeStruct((nb, SHIFT_WIDTH, LANES), F32),
                 jax.ShapeDtypeStruct((SCAN_OPERANDS,) + cm_shape, F32)] \
        + [jax.ShapeDtypeStruct(cm_shape, F32)] * 2 \
        + [jax.ShapeDtypeStruct((n, D_MODEL), F32)] * 2
    return pl.pallas_call(
        functools.partial(_mix_prep_kernel, tm=tm, s=s, pos0=pos0),
        grid=(nb, tiles),
        in_specs=in_specs,
        out_specs=out_specs,
        out_shape=out_shape,
        scratch_shapes=[pltpu.VMEM((SHIFT_WIDTH, LANES), F32),
                        pltpu.VMEM((up + tm, POOL_WIDTH), F32)],
        compiler_params=pltpu.CompilerParams(
            dimension_semantics=("arbitrary", "arbitrary"), vmem_limit_bytes=VMEM_LIMIT),
        name="mix_prep",
    )(x2d, zc0, uc0, *wts)


def _wkv_step(s_ref, row, vv):
    groups = HEAD_DIM // KEY_GROUP

    def sa_pass(g, acc):
        for kk in range(KEY_GROUP):
            kp = g * KEY_GROUP + kk
            acc = acc + s_ref[kp] * row(0, kp)
        return acc
    sa = lax.fori_loop(0, groups, sa_pass, jnp.zeros(vv.shape, F32))

    def update_pass(g, acc):
        for kk in range(KEY_GROUP):
            kp = g * KEY_GROUP + kk
            sn = s_ref[kp] * row(1, kp) + sa * row(2, kp) + vv * row(3, kp)
            s_ref[kp] = sn
            acc = acc + sn * row(4, kp)
        return acc
    return lax.fori_loop(0, groups, update_pass, jnp.zeros(vv.shape, F32))


def _swap_major_sublane(x):
    return jnp.swapaxes(x, 0, 1)


def _wkv_scan_prompt_kernel(q_ref, o_ref, sout_ref, s_ref, kv_ref, vv_ref, ov_ref, *, tt):
    t = pl.program_id(0)
    q = pl.program_id(1)

    @pl.when((t == 0) & (q == 0))
    def _():
        s_ref[...] = jnp.zeros_like(s_ref)

    def gather_t(base_lo, base_hi):
        pieces = [q_ref[0, bb, pl.ds(base, SUBLANES), :] for base in (base_lo, base_hi) for bb in range(8)]
        return jnp.concatenate(pieces, axis=0).T

    @pl.when(q < SCAN_OPERANDS - 1)
    def _():
        def kgroup(gi, c):
            slabs = []
            for kk in range(SUBLANES):
                base = pl.multiple_of((gi * SUBLANES + kk) * SUBLANES, SUBLANES)
                slabs.append(gather_t(base, base))
            kv_ref[q, :, pl.ds(pl.multiple_of(gi * SUBLANES, SUBLANES), SUBLANES), :] = \
                _swap_major_sublane(jnp.stack(slabs))
            return c
        lax.fori_loop(0, HEAD_DIM // SUBLANES, kgroup, 0, unroll=True)

    @pl.when(q == SCAN_OPERANDS - 1)
    def _():
        def vgroup(gi, c):
            slabs = []
            for vi in range(SUBLANES):
                base = pl.multiple_of((gi * SUBLANES + vi) * 2 * SUBLANES, 2 * SUBLANES)
                slabs.append(gather_t(base, base + SUBLANES))
            vv_ref[:, pl.ds(pl.multiple_of(gi * SUBLANES, SUBLANES), SUBLANES), :] = \
                _swap_major_sublane(jnp.stack(slabs))
            return c
        lax.fori_loop(0, KEY_HALF // SUBLANES, vgroup, 0, unroll=True)

        def step(i, c):
            row = lambda qi, kp: kv_ref[qi, i, pl.ds(kp, 1), :]
            ov_ref[i] = _wkv_step(s_ref, row, vv_ref[i])
            return c
        lax.fori_loop(0, tt, step, 0)

        def ogroup(gi, c):
            g0 = pl.multiple_of(gi * SUBLANES, SUBLANES)
            x = _swap_major_sublane(ov_ref[:, pl.ds(g0, SUBLANES), :])
            for vi in range(SUBLANES):
                xt = x[vi].T
                base = pl.multiple_of((gi * SUBLANES + vi) * 2 * SUBLANES, 2 * SUBLANES)
                for vh in range(2):
                    for bb in range(8):
                        r0 = (vh * 8 + bb) * SUBLANES
                        o_ref[bb, pl.ds(base + vh * SUBLANES, SUBLANES), :] = xt[r0:r0 + SUBLANES]
            return c
        lax.fori_loop(0, KEY_HALF // SUBLANES, ogroup, 0, unroll=True)

        @pl.when(t == pl.num_programs(0) - 1)
        def _():
            sout_ref[...] = s_ref[...]


def _wkv_scan_prompt(ops, *, tt):
    _, nb, _, t_len = ops.shape
    sspec = pl.BlockSpec((HEAD_DIM, KEY_HALF, LANES), lambda ti, qi: (0, 0, 0))
    return pl.pallas_call(
        functools.partial(_wkv_scan_prompt_kernel, tt=tt),
        grid=(t_len // tt, SCAN_OPERANDS),
        in_specs=[pl.BlockSpec((1, nb, RWKV_WIDTH, tt), lambda ti, qi: (qi, 0, 0, ti))],
        out_specs=[pl.BlockSpec((nb, RWKV_WIDTH, tt), lambda ti, qi: (0, 0, ti)), sspec],
        out_shape=[jax.ShapeDtypeStruct((nb, RWKV_WIDTH, t_len), F32),
                   jax.ShapeDtypeStruct((HEAD_DIM, KEY_HALF, LANES), F32)],
        scratch_shapes=[pltpu.VMEM((HEAD_DIM, KEY_HALF, LANES), F32),
                        pltpu.VMEM((SCAN_OPERANDS - 1, tt, HEAD_DIM, LANES), F32),
                        pltpu.VMEM((tt, KEY_HALF, LANES), F32),
                        pltpu.VMEM((tt, KEY_HALF, LANES), F32)],
        compiler_params=pltpu.CompilerParams(
            dimension_semantics=("arbitrary", "arbitrary"), vmem_limit_bytes=VMEM_LIMIT),
        name="wkv_scan_prompt",
    )(ops)


def _wkv_scan_sample_kernel(q_ref, s0_ref, o_ref, sout_ref, s_ref, *, t_len):
    h = pl.program_id(0)
    s_ref[...] = s0_ref[0]
    for i in range(t_len):
        row = lambda qi, kp, i=i: q_ref[qi, i, pl.ds(kp * HEADS + h, 1), :]
        vv = q_ref[SCAN_OPERANDS - 1, i, pl.ds(h, HEAD_DIM, stride=HEADS), :]
        o_ref[i, pl.ds(h, HEAD_DIM, stride=HEADS), :] = _wkv_step(s_ref, row, vv)
    sout_ref[0] = s_ref[...]


def _wkv_scan_sample(ops, s0):
    t_len = ops.shape[1]
    spec = pl.BlockSpec(ops.shape[1:], lambda h: (0, 0, 0))
    sspec = pl.BlockSpec((1, HEAD_DIM, HEAD_DIM, LANES), lambda h: (h, 0, 0, 0))
    return pl.pallas_call(
        functools.partial(_wkv_scan_sample_kernel, t_len=t_len),
        grid=(HEADS,),
        in_specs=[pl.BlockSpec(ops.shape, lambda h: (0, 0, 0, 0)), sspec],
        out_specs=[spec, sspec],
        out_shape=[jax.ShapeDtypeStruct(ops.shape[1:], F32), jax.ShapeDtypeStruct(s0.shape, F32)],
        scratch_shapes=[pltpu.VMEM((HEAD_DIM, HEAD_DIM, LANES), F32)],
        compiler_params=pltpu.CompilerParams(
            dimension_semantics=("arbitrary",), vmem_limit_bytes=VMEM_LIMIT),
        name="wkv_scan_sample",
    )(ops, s0)


def _post_kernel(o_ref, bon_ref, g_ref, pa_ref, sgb_ref, x_ref, cnt0_ref, h2_all_ref, lnw_ref, lnb_ref,
                 wb_ref, wout_ref, nffn_ref, wrh_ref, wrl_ref, br_ref, x1_ref, h2_ref, rt_ref, cnt_ref,
                 carry, *, sub):
    del h2_all_ref
    @pl.when(pl.program_id(0) == 0)
    def _():
        carry[...] = cnt0_ref[...]

    prev = carry[0:1, :]
    for part in range(o_ref.shape[2] // sub):
        prev = _post_rows(pl.ds(part * sub, sub), prev, o_ref, bon_ref, g_ref, pa_ref, sgb_ref, x_ref,
                          lnw_ref, lnb_ref, wb_ref, wout_ref, nffn_ref, wrh_ref, wrl_ref,
                          br_ref, x1_ref, h2_ref, rt_ref)
    carry[...] = jnp.broadcast_to(prev, carry.shape)
    cnt_ref[...] = jnp.broadcast_to(prev, cnt_ref.shape)


def _post_rows(rows, prev_count, o_ref, bon_ref, g_ref, pa_ref, sgb_ref, x_ref, lnw_ref, lnb_ref,
               wb_ref, wout_ref, nffn_ref, wrh_ref, wrl_ref, br_ref, x1_ref, h2_ref, rt_ref):
    o = o_ref[0, :, rows]
    mean = _head_sum(o) * (1.0 / HEAD_DIM)
    d = o - mean
    var = _head_sum(d * d) * (1.0 / HEAD_DIM)
    on = d * lax.rsqrt(var + GN_EPS) * lnw_ref[...] + lnb_ref[...]
    yb = ((on + bon_ref[0, :, rows]) * g_ref[0, :, rows]).astype(BF16)
    mb = lax.dot_general(yb, wb_ref[...], (((0,), (0,)), ((), ())), preferred_element_type=F32)
    merged = pa_ref[rows, :] + sgb_ref[rows, :] * mb
    x1 = x_ref[rows, :] + _dot(merged.astype(BF16), wout_ref[...])
    x1_ref[rows, :] = x1
    h2 = _rmsnorm(x1, nffn_ref[...])
    h2_ref[rows, :] = h2

    h_hi, h_lo = _split_bf16(h2)
    logits = (_dot(h_hi, wrh_ref[...]) + _dot(h_lo, wrh_ref[...]) + _dot(h_hi, wrl_ref[...])
              + br_ref[...])
    ln = lax.broadcasted_iota(jnp.int32, logits.shape, 1)
    is_group = (ln >= N_EXPERTS) & (ln < N_EXPERTS + N_GROUPS)
    gl = jnp.where(is_group, logits, NEG_BIG)
    gmax = jnp.max(gl, axis=-1, keepdims=True)
    gsel = jnp.min(jnp.where(gl == gmax, ln, ROUTER_LANES), axis=-1, keepdims=True) - N_EXPERTS
    den = jnp.sum(jnp.where(is_group, jnp.exp(gl - gmax), 0.0), axis=-1, keepdims=True)
    pg = 1.0 / den
    in_group = (ln < N_EXPERTS) & ((ln // EXPERTS_PER_GROUP) == gsel)
    el = jnp.where(in_group, logits, NEG_BIG)
    m1 = jnp.max(el, axis=-1, keepdims=True)
    i1 = jnp.min(jnp.where(el == m1, ln, ROUTER_LANES), axis=-1, keepdims=True)
    el2 = jnp.where(ln == i1, NEG_BIG, el)
    m2 = jnp.max(el2, axis=-1, keepdims=True)
    i2 = jnp.min(jnp.where(el2 == m2, ln, ROUTER_LANES), axis=-1, keepdims=True)
    e2 = jnp.exp(m2 - m1)
    p1 = 1.0 / (1.0 + e2)
    p2 = e2 / (1.0 + e2)

    tm = logits.shape[0]
    sel = (ln == i1) | (ln == i2)
    tri = (lax.broadcasted_iota(jnp.int32, (tm, tm), 1)
           < lax.broadcasted_iota(jnp.int32, (tm, tm), 0)).astype(BF16)
    before = prev_count + _dot(tri, sel.astype(BF16))
    r1 = jnp.sum(jnp.where(ln == i1, before, 0.0), axis=-1, keepdims=True)
    r2 = jnp.sum(jnp.where(ln == i2, before, 0.0), axis=-1, keepdims=True)
    fields = (i1.astype(F32), i2.astype(F32), r1, r2, p1 * pg, p2 * pg)
    rt = jnp.zeros(logits.shape, F32)
    for lane_idx, val in enumerate(fields):
        rt = jnp.where(ln == lane_idx, val, rt)
    rt_ref[rows, :] = rt
    return prev_count + jnp.sum(sel.astype(F32), axis=0, keepdims=True)


def _post(o, bon, g, pa, sgb, x2d, cnt0, h2_all, row_offset, wts, *, tm, sub, cm_index):
    n = x2d.shape[0]
    row = lambda i: (i, 0)
    tok = lambda width: pl.BlockSpec((tm, width), row)
    cm = pl.BlockSpec((1, RWKV_WIDTH, tm), cm_index)
    full = lambda arr: pl.BlockSpec(arr.shape, lambda i: (0, 0))
    block_offset = row_offset // tm
    return pl.pallas_call(
        functools.partial(_post_kernel, sub=sub),
        grid=(n // tm,),
        in_specs=[cm] * 3 + [tok(D_MODEL)] * 3 + [full(cnt0), pl.BlockSpec(memory_space=pl.ANY)]
        + [full(w) for w in wts],
        out_specs=[tok(D_MODEL), pl.BlockSpec((tm, D_MODEL), lambda i: (i + block_offset, 0)),
                   tok(ROUTER_LANES), pl.BlockSpec((SUBLANES, ROUTER_LANES), lambda i: (0, 0))],
        out_shape=[jax.ShapeDtypeStruct((n, D_MODEL), F32),
                   jax.ShapeDtypeStruct(h2_all.shape, F32),
                   jax.ShapeDtypeStruct((n, ROUTER_LANES), F32),
                   jax.ShapeDtypeStruct((SUBLANES, ROUTER_LANES), F32)],
        scratch_shapes=[pltpu.VMEM((SUBLANES, ROUTER_LANES), F32)],
        input_output_aliases={7: 1},
        compiler_params=pltpu.CompilerParams(
            dimension_semantics=("arbitrary",), vmem_limit_bytes=VMEM_LIMIT),
        name="post",
    )(o, bon, g, pa, sgb, x2d, cnt0, h2_all, *wts)


def _route_tables(cnt, n_tiles_max):
    counts = cnt[0, :N_EXPERTS].astype(jnp.int32)
    tiles_e = (counts + EXPERT_TILE - 1) // EXPERT_TILE
    tile_end = jnp.cumsum(tiles_e)
    row0 = (tile_end - tiles_e) * EXPERT_TILE
    tile_ids = jnp.arange(n_tiles_max, dtype=jnp.int32)
    tile_expert = jnp.minimum(jnp.sum(tile_ids[:, None] >= tile_end[None, :], axis=-1),
                              N_EXPERTS - 1).astype(jnp.int32)
    return tile_expert, tile_end[N_EXPERTS - 1:].astype(jnp.int32), row0


def _assignment_rows(rt, row0):
    e = rt[:, 0:2].astype(jnp.int32)
    r = rt[:, 2:4].astype(jnp.int32)
    pos = r + jnp.sum(jnp.where(e[..., None] == jnp.arange(N_EXPERTS), row0, 0), axis=-1)
    return pos.reshape(-1)


def _experts_kernel(te_ref, nu_ref, inv_ref, invn_ref, h2_ref, wg0_ref, wu0_ref, wd0_ref,
                    wg1_ref, wu1_ref, wd1_ref, ys_ref, xbuf0, xbuf1, wgu0, wdn0, wgu1, wdn1, sem):
    i = pl.program_id(0)
    last = pl.num_programs(0) - 1
    xbuf = (xbuf0, xbuf1)
    wts = ((wg0_ref, wu0_ref, wd0_ref, wgu0, wdn0), (wg1_ref, wu1_ref, wd1_ref, wgu1, wdn1))

    def issue(tok_ref, off, slot):
        for r in range(EXPERT_TILE):
            pltpu.make_async_copy(h2_ref.at[pl.ds(tok_ref[off + r], 1)],
                                  xbuf[slot].at[pl.ds(r, 1)], sem.at[slot]).start(priority=r % 2)

    def wait(slot):
        pltpu.make_async_copy(h2_ref.at[pl.ds(0, EXPERT_TILE)], xbuf[slot], sem.at[slot]).wait()

    def refresh_weights(slot):
        tile = 2 * i + slot
        wg_ref, wu_ref, wd_ref, wgu, wdn = wts[slot]
        e_now = te_ref[jnp.minimum(tile, nu_ref[0] - 1)]
        e_prev = te_ref[jnp.minimum(jnp.maximum(tile - 2, 0), nu_ref[0] - 1)]

        @pl.when((i == 0) | (e_now != e_prev))
        def _():
            wgu[:, 0:D_EXPERT] = wg_ref[0].astype(BF16)
            wgu[:, D_EXPERT:2 * D_EXPERT] = wu_ref[0].astype(BF16)
            wdn[...] = wd_ref[0].astype(BF16)

    def compute(slot):
        rows = pl.ds(slot * EXPERT_TILE, EXPERT_TILE)
        _, _, _, wgu, wdn = wts[slot]
        gu = _dot(xbuf[slot][...].astype(BF16), wgu[...])
        hid = jax.nn.silu(gu[:, 0:D_EXPERT]) * gu[:, D_EXPERT:2 * D_EXPERT]
        ys_ref[rows, :] = _dot(hid.astype(BF16), wdn[...])

    @pl.when(i == 0)
    def _():
        issue(inv_ref, 0, 0)

    refresh_weights(0)
    refresh_weights(1)
    wait(0)
    issue(inv_ref, EXPERT_TILE, 1)
    compute(0)
    wait(1)
    issue(invn_ref, 0, 0)
    compute(1)

    @pl.when(i == last)
    def _():
        wait(0)


def _experts(h2, inv, tile_expert, n_used, wg, wu, wd):
    n_rows = inv.shape[0]
    n_steps = n_rows // (2 * EXPERT_TILE)
    wsel = lambda slot: (lambda i, te, nu: (te[jnp.minimum(2 * i + slot, nu[0] - 1)], 0, 0))
    wspecs = [pl.BlockSpec((1, D_MODEL, D_EXPERT), wsel(s)) for s in (0, 1)]
    dspecs = [pl.BlockSpec((1, D_EXPERT, D_MODEL), wsel(s)) for s in (0, 1)]
    return pl.pallas_call(
        _experts_kernel,
        grid_spec=pltpu.PrefetchScalarGridSpec(
            num_scalar_prefetch=2,
            grid=(n_steps,),
            in_specs=[pl.BlockSpec((2 * EXPERT_TILE,), lambda i, te, nu: (i,), memory_space=pltpu.SMEM),
                      pl.BlockSpec((EXPERT_TILE,),
                                   lambda i, te, nu: (2 * jnp.minimum(i + 1, n_steps - 1),),
                                   memory_space=pltpu.SMEM),
                      pl.BlockSpec(memory_space=pl.ANY),
                      wspecs[0], wspecs[0], dspecs[0], wspecs[1], wspecs[1], dspecs[1]],
            out_specs=pl.BlockSpec((2 * EXPERT_TILE, D_MODEL), lambda i, te, nu: (i, 0)),
            scratch_shapes=[pltpu.VMEM((EXPERT_TILE, D_MODEL), F32),
                            pltpu.VMEM((EXPERT_TILE, D_MODEL), F32),
                            pltpu.VMEM((D_MODEL, 2 * D_EXPERT), BF16),
                            pltpu.VMEM((D_EXPERT, D_MODEL), BF16),
                            pltpu.VMEM((D_MODEL, 2 * D_EXPERT), BF16),
                            pltpu.VMEM((D_EXPERT, D_MODEL), BF16),
                            pltpu.SemaphoreType.DMA((2,))]),
        out_shape=jax.ShapeDtypeStruct((n_rows, D_MODEL), F32),
        compiler_params=pltpu.CompilerParams(
            dimension_semantics=("arbitrary",), vmem_limit_bytes=VMEM_LIMIT),
        name="moe_experts",
    )(tile_expert, n_used, inv, inv, h2, wg, wu, wd, wg, wu, wd)


def _tail_kernel(pos_ref, posn_ref, x1_ref, rt_ref, p_ref, ys_ref, nple_ref, wpg_ref, wpp_ref,
                 nfin_ref, y_ref, ybuf0, ybuf1, sem, *, tm):
    i = pl.program_id(0)
    last = pl.num_programs(0) - 1
    ybuf = (ybuf0, ybuf1)

    def issue(p_ref, p_off, slot):
        for r in range(tm):
            for j in range(2):
                pltpu.make_async_copy(ys_ref.at[pl.ds(p_ref[p_off + 2 * r + j], 1)],
                                      ybuf[slot].at[j, pl.ds(r, 1)], sem.at[slot]).start(priority=j)

    def wait(slot):
        for j in range(2):
            pltpu.make_async_copy(ys_ref.at[pl.ds(0, tm)], ybuf[slot].at[j], sem.at[slot]).wait()

    def compute(slot):
        rows = pl.ds(slot * tm, tm)
        rt = rt_ref[rows, :]
        x2 = x1_ref[rows, :] + (rt[:, 4:5] * ybuf[slot][0] + rt[:, 5:6] * ybuf[slot][1])
        h3 = _rmsnorm(x2, nple_ref[...]).astype(BF16)
        ple = (jax.nn.sigmoid(_dot(h3, wpg_ref[...]))
               * _dot(p_ref[rows, :].astype(BF16), wpp_ref[...]))
        y_ref[rows, :] = _rmsnorm(x2 + ple, nfin_ref[...])

    @pl.when(i == 0)
    def _():
        issue(pos_ref, 0, 0)

    wait(0)
    issue(pos_ref, 2 * tm, 1)
    compute(0)
    wait(1)
    issue(posn_ref, 0, 0)
    compute(1)

    @pl.when(i == last)
    def _():
        wait(0)


def _tail(pos, x1, rt, p2d, ys, nple, wpg, wpp, nfin, *, tm):
    n = x1.shape[0]
    n_steps = n // (2 * tm)
    tok = lambda width: pl.BlockSpec((2 * tm, width), lambda i: (i, 0))
    full = lambda arr: pl.BlockSpec(arr.shape, lambda i: (0, 0))
    return pl.pallas_call(
        functools.partial(_tail_kernel, tm=tm),
        grid=(n_steps,),
        in_specs=[pl.BlockSpec((4 * tm,), lambda i: (i,), memory_space=pltpu.SMEM),
                  pl.BlockSpec((2 * tm,), lambda i: (2 * jnp.minimum(i + 1, n_steps - 1),),
                               memory_space=pltpu.SMEM),
                  tok(D_MODEL), tok(ROUTER_LANES), tok(PLE_DIM),
                  pl.BlockSpec(memory_space=pl.ANY),
                  full(nple), full(wpg), full(wpp), full(nfin)],
        out_specs=tok(D_MODEL),
        out_shape=jax.ShapeDtypeStruct((n, D_MODEL), F32),
        scratch_shapes=[pltpu.VMEM((2, tm, D_MODEL), F32), pltpu.VMEM((2, tm, D_MODEL), F32),
                        pltpu.SemaphoreType.DMA((2,))],
        compiler_params=pltpu.CompilerParams(
            dimension_semantics=("arbitrary",), vmem_limit_bytes=VMEM_LIMIT),
        name="moe_tail",
    )(pos, pos, x1, rt, p2d, ys, nple, wpg, wpp, nfin)


def _moe_tail(h2_all, groups, cnt, wg, wu, wd, nple, wpg, wpp, nfin, *, tm):
    n_all = h2_all.shape[0]
    n_tiles_max = (2 * n_all) // EXPERT_TILE + N_EXPERTS
    tile_expert, n_used, row0 = _route_tables(cnt, n_tiles_max)
    poss = [_assignment_rows(rt, row0) for rt, _, _ in groups]
    token_of = jnp.arange(2 * n_all, dtype=jnp.int32) // 2
    inv = jnp.zeros((n_tiles_max * EXPERT_TILE,), jnp.int32).at[jnp.concatenate(poss)].set(token_of)
    ys = _experts(h2_all, inv, tile_expert, n_used, wg, wu, wd)
    return [_tail(pos, x1, rt, p2d, ys, nple, wpg, wpp, nfin, tm=tm)
            for pos, (rt, x1, p2d) in zip(poss, groups)]


def kernel(x_prompt, x_sample, state_pool, state_shift, state_wkv, p_prompt, p_sample, norm_mix, w_in, pool_mix, pool_scale, w_branch_a, shift_mu, decay_w0, decay_w2, iclr_a0, iclr_a2, gate_g2, k_k, k_a, r_k, ln_x_w, ln_x_b, w_branch_b, w_out, norm_ffn, w_route_group, b_route_group, w_route_expert, b_route_expert, expert_gate, expert_up, expert_down, norm_ple, w_ple_gate, w_ple_proj, norm_final):
    l = 0
    bsz, seq, _ = x_prompt.shape
    dbsz, dseq, _ = x_sample.shape
    row = lambda vec: vec.reshape(1, -1).astype(F32)
    col = lambda vec: vec.reshape(-1, 1).astype(F32)
    perm = CHANNEL_PERM

    o1 = POOL_WIDTH
    o2 = o1 + SHIFT_WIDTH
    zperm = np.concatenate([perm, RWKV_WIDTH + perm, 2 * RWKV_WIDTH + perm,
                            np.arange(3 * RWKV_WIDTH, SHIFT_WIDTH)])
    w_z = w_in[l][:, o1:o2][:, zperm].astype(BF16)
    zeros_lora = jnp.zeros((RWKV_WIDTH, LORA_PAIR // 2), F32)
    w2t_pad = jnp.concatenate([decay_w2[l].T[perm], zeros_lora], axis=1).astype(BF16)
    a2t_pad = jnp.concatenate([zeros_lora, iclr_a2[l].T[perm]], axis=1).astype(BF16)
    w_router = jnp.concatenate(
        [w_route_expert[l], w_route_group[l],
         jnp.zeros((D_MODEL, ROUTER_LANES - N_EXPERTS - N_GROUPS), F32)], axis=1)
    wr_hi = w_router.astype(BF16)
    wr_lo = (w_router - wr_hi.astype(F32)).astype(BF16)
    b_router = jnp.concatenate(
        [b_route_expert[l], b_route_group[l],
         jnp.zeros((ROUTER_LANES - N_EXPERTS - N_GROUPS,), F32)]).reshape(1, -1)
    prep_w = [row(norm_mix[l]), w_in[l][:, :o1].astype(BF16), w_z, w_in[l][:, o2:].astype(BF16),
              col(shift_mu[l][zperm]), col(decay_w0[l][perm]), w2t_pad, col(iclr_a0[l][perm]), a2t_pad,
              gate_g2[l].T[perm].astype(BF16), col(k_k[l][perm]), col(k_a[l][perm]),
              col(r_k[l].reshape(-1)[perm]), pool_mix[l].astype(BF16), row(pool_scale[l]),
              w_branch_a[l].astype(BF16)]
    post_w = [col(ln_x_w[l][perm]), col(ln_x_b[l][perm]), w_branch_b[l][perm].astype(BF16),
              w_out[l].astype(BF16), row(norm_ffn[l]), wr_hi, wr_lo, b_router]
    moe_w = [expert_gate[l], expert_up[l], expert_down[l],
             row(norm_ple[l]), w_ple_gate[l].astype(BF16), w_ple_proj[l].astype(BF16), row(norm_final)]

    tm_p = 256
    tiles_p = seq // tm_p
    x_p = x_prompt.reshape(bsz * seq, D_MODEL)
    outs = _mix_prep(x_p, jnp.zeros((bsz * SHIFT_WIDTH, LANES), F32),
                     jnp.zeros((bsz * 16, POOL_WIDTH), F32), prep_w,
                     nb=bsz, tiles=tiles_p, tm=tm_p, s=1, pos0=0, cm_index=lambda b, t: (b, 0, t))
    u_p, zl_p, ops, g, bon, pa, sgb = outs
    o_p, s_p = _wkv_scan_prompt(ops, tt=LANES)
    tm_post = 2 * tm_p
    n_p = bsz * seq
    n_s = dbsz * dseq
    x1_p, h2_all, rt_p, cnt_p = _post(
        o_p, bon, g, pa, sgb, x_p, jnp.zeros((SUBLANES, ROUTER_LANES), F32),
        jnp.zeros((n_p + n_s, D_MODEL), F32), 0, post_w,
        tm=tm_post, sub=tm_p, cm_index=lambda i: (i // (seq // tm_post), 0, i % (seq // tm_post)))
    wkv_p = s_p.reshape(KEY_HALF, 2, KEY_HALF, 2, bsz, HEADS).transpose(4, 5, 3, 2, 1, 0)
    wkv_p = wkv_p.reshape(bsz, HEADS, HEAD_DIM, HEAD_DIM)

    n_s = dbsz * dseq
    x_s = x_sample.transpose(1, 0, 2).reshape(n_s, D_MODEL)
    p_s = p_sample[l].transpose(1, 0, 2).reshape(n_s, PLE_DIM)
    uc0_s = jnp.concatenate(
        [jnp.zeros((dbsz, POOL_WIDTH), F32),
         state_pool[l].transpose(1, 0, 2).reshape(POOL_STATE * dbsz, POOL_WIDTH)], axis=0)
    zc0_s = state_shift[l].T[zperm]
    outs = _mix_prep(x_s, zc0_s, uc0_s, prep_w, nb=1, tiles=dseq, tm=dbsz, s=dbsz, pos0=PAST_LEN,
                     cm_index=lambda b, t: (t, 0, 0))
    u_s, zl_s, ops, g, bon, pa, sgb = outs
    s0_s = state_wkv[l].reshape(dbsz, HEADS, 2, KEY_HALF, 2, KEY_HALF).transpose(1, 5, 4, 3, 2, 0)
    s0_s = s0_s.reshape(HEADS, HEAD_DIM, HEAD_DIM, dbsz)
    o_s, s_s = _wkv_scan_sample(ops, s0_s)
    x1_s, h2_all, rt_s, cnt_all = _post(o_s, bon, g, pa, sgb, x_s, cnt_p, h2_all, n_p, post_w,
                                        tm=dbsz, sub=dbsz, cm_index=lambda i: (i, 0, 0))

    y_p, y_s = _moe_tail(
        h2_all, [(rt_p, x1_p, p_prompt[l].reshape(n_p, PLE_DIM)), (rt_s, x1_s, p_s)],
        cnt_all, *moe_w, tm=256)
    wkv_s = s_s.reshape(HEADS, KEY_HALF, 2, KEY_HALF, 2, dbsz).transpose(5, 0, 4, 3, 2, 1)
    wkv_s = wkv_s.reshape(dbsz, HEADS, HEAD_DIM, HEAD_DIM)

    inv_z = np.argsort(zperm)
    y_prompt = y_p.reshape(bsz, seq, D_MODEL)
    y_sample = y_s.reshape(dseq, dbsz, D_MODEL).transpose(1, 0, 2)
    pool_prompt = u_p.reshape(bsz, seq, POOL_WIDTH)[:, seq - POOL_STATE:]
    u_s_bt = u_s.reshape(dseq, dbsz, POOL_WIDTH).transpose(1, 0, 2)
    pool_sample = jnp.concatenate([state_pool[l][:, dseq:], u_s_bt], axis=1)
    shift_prompt = zl_p[:, :, LANES - 1][:, inv_z]
    shift_sample = zl_s[0].T[:, inv_z]
    return (y_prompt, y_sample, pool_prompt[None], shift_prompt[None], wkv_p[None],
            pool_sample[None], shift_sample[None], wkv_s[None])
```

```python
import functools

import jax
import jax.numpy as jnp
import numpy as np
from jax import lax
from jax.experimental import pallas as pl
from jax.experimental.pallas import tpu as pltpu

F32 = jnp.float32
BF16 = jnp.bfloat16

D_MODEL = 1024
PLE_DIM = 256
POOL_WIDTH = 512
POOL_WINDOWS = (2, 4, 8, 16)
POOL_GROUP_DIM = 128
POOL_STATE = 15
RWKV_WIDTH = 512
HEAD_DIM = 64
HEADS = 8
LORA_PAIR = 128
GATE_LORA = 128
SHIFT_WIDTH = 3 * RWKV_WIDTH + LORA_PAIR + GATE_LORA
IN_WIDTH = POOL_WIDTH + SHIFT_WIDTH + 2 * D_MODEL
N_GROUPS = 4
EXPERTS_PER_GROUP = 8
N_EXPERTS = 32
D_EXPERT = 256
RMS_EPS = 1e-6
GN_EPS = 64e-5
PAST_LEN = 16384

LANES = 128
SUBLANES = 8
KEY_HALF = HEAD_DIM // 2
SCAN_OPERANDS = 6
KEY_GROUP = 16
ROUTER_LANES = 128
GROUP_TILE = 1024
COMB_LANE0 = 8
MOE_ROW = D_MODEL + ROUTER_LANES
NEG_BIG = -1e30
VMEM_LIMIT = 56 * 1024 * 1024

_J = np.arange(RWKV_WIDTH)
CHANNEL_PERM = (_J % 8) * HEAD_DIM + ((_J // 8) % 2) * KEY_HALF + _J // 16


def _dot(a, b):
    return jnp.dot(a, b, preferred_element_type=F32)


def _split_bf16(x):
    hi = x.astype(BF16)
    return hi, (x - hi.astype(F32)).astype(BF16)


def _head_sum(x):
    x3 = x.reshape(HEAD_DIM, HEADS, x.shape[1])
    s = jnp.sum(x3, axis=0, keepdims=True)
    return jnp.broadcast_to(s, x3.shape).reshape(x.shape)


def _rmsnorm(x, g):
    return x * lax.rsqrt(jnp.mean(x * x, axis=-1, keepdims=True) + RMS_EPS) * g


def _softplus(y):
    return jnp.maximum(y, 0.0) + jnp.log1p(jnp.exp(-jnp.abs(y)))


def _mix_prep_kernel(x_ref, zc0_ref, uc0_ref, nmix_ref, wu_ref, wz_ref, wgab_ref, mu_ref, w0_ref,
                     w2t_ref, a0_ref, a2t_ref, g2t_ref, kk_ref, ka_ref, rk_ref, mix_ref,
                     pscale_ref, wa_ref,
                     u_ref, zl_ref, q_ref, g_ref, bon_ref,
                     pa_ref, sgb_ref, zc, uext, *, tm, s, pos0):
    t = pl.program_id(1)
    up = 16 * s

    @pl.when(t == 0)
    def _():
        zc[...] = zc0_ref[...]
        uext[0:up] = uc0_ref[...]

    x = x_ref[...]
    h = _rmsnorm(x, nmix_ref[...]).astype(BF16)
    u = _dot(h, wu_ref[...])
    gab = _dot(h, wgab_ref[...])
    u_ref[...] = u
    sgb_ref[...] = jax.nn.sigmoid(gab[:, D_MODEL:])
    uext[up:up + tm] = u

    z_t = _dot(h, wz_ref[...]).T
    if s == 1:
        lane = lax.broadcasted_iota(jnp.int32, z_t.shape, 1)
        zprev = jnp.where(lane == 0, zc[:, LANES - 1:LANES], pltpu.roll(z_t, 1, axis=1))
    else:
        zprev = zc[...]
    zc[...] = z_t[:, tm - LANES:tm]
    zl_ref[0] = z_t[:, tm - LANES:tm]
    zm = z_t + (zprev - z_t) * mu_ref[...]
    r = zm[0:RWKV_WIDTH]
    k = zm[RWKV_WIDTH:2 * RWKV_WIDTH]
    v = zm[2 * RWKV_WIDTH:3 * RWKV_WIDTH]
    lora_in = zm[3 * RWKV_WIDTH:3 * RWKV_WIDTH + LORA_PAIR]
    gd = zm[3 * RWKV_WIDTH + LORA_PAIR:SHIFT_WIDTH]
    dw = _dot(w2t_ref[...], jnp.tanh(lora_in).astype(BF16))
    da = _dot(a2t_ref[...], lora_in.astype(BF16))
    w_log = -_softplus(-(w0_ref[...] + dw)) - 0.5
    a = jax.nn.sigmoid(a0_ref[...] + da)
    kk = k * kk_ref[...]
    kk = kk / jnp.maximum(jnp.sqrt(_head_sum(kk * kk)), 1e-12)
    k2 = k * (1.0 + (a - 1.0) * ka_ref[...])
    q_ref[0, 0] = -kk
    q_ref[1, 0] = jnp.exp(-jnp.exp(w_log))
    q_ref[2, 0] = kk * a
    q_ref[3, 0] = k2
    q_ref[4, 0] = r
    q_ref[5, 0] = v
    g_ref[0] = _dot(g2t_ref[...], jax.nn.sigmoid(gd).astype(BF16))
    bon_ref[0] = _head_sum(r * k2 * rk_ref[...]) * v

    rows = lax.broadcasted_iota(jnp.int32, (tm, POOL_GROUP_DIM), 0)
    if s > 1:
        rows = rows // s
    pos = pos0 + t * (tm // s) + rows
    ys = []
    for gi, wnd in enumerate(POOL_WINDOWS):
        lanes = slice(gi * POOL_GROUP_DIM, (gi + 1) * POOL_GROUP_DIM)
        cur = uext[pl.ds(up, tm), lanes]
        acc = cur
        for j in range(1, wnd):
            acc = acc + uext[pl.ds(up - j * s, tm), lanes]
        cnt = jnp.minimum(pos + 1, wnd).astype(F32)
        pooled = acc / cnt - cur
        ys.append(_dot(pooled.astype(BF16), mix_ref[gi]))
    y = jnp.concatenate(ys, axis=-1) * pscale_ref[...]
    pa_ref[...] = jax.nn.sigmoid(gab[:, :D_MODEL]) * _dot(y.astype(BF16), wa_ref[...])

    uext[0:up] = uext[tm:tm + up]


def _mix_prep(x2d, zc0, uc0, wts, *, nb, tiles, tm, s, pos0, cm_index):
    n = x2d.shape[0]
    up = 16 * s
    row = lambda b, t: (b * tiles + t, 0)
    full = lambda arr: pl.BlockSpec(arr.shape, lambda b, t: (0,) * arr.ndim)
    in_specs = [
        pl.BlockSpec((tm, D_MODEL), row),
        pl.BlockSpec((SHIFT_WIDTH, LANES), lambda b, t: (b, 0)),
        pl.BlockSpec((up, POOL_WIDTH), lambda b, t: (b, 0)),
    ] + [full(w) for w in wts]
    tok = lambda width: pl.BlockSpec((tm, width), row)
    cm_shape = (nb, RWKV_WIDTH, tiles * tm) if s == 1 else (tiles, RWKV_WIDTH, tm)
    cm = pl.BlockSpec((1, RWKV_WIDTH, tm), cm_index)
    cm_stack = pl.BlockSpec((SCAN_OPERANDS, 1, RWKV_WIDTH, tm), lambda b, t: (0,) + cm_index(b, t))
    out_specs = [tok(POOL_WIDTH), pl.BlockSpec((1, SHIFT_WIDTH, LANES), lambda b, t: (b, 0, 0)),
                 cm_stack, cm, cm] + [tok(D_MODEL)] * 2
    out_shape = [jax.ShapeDtypeStruct((n, POOL_WIDTH), F32),
                 jax.ShapeDtypeStruct((nb, SHIFT_WIDTH, LANES), F32),
                 jax.ShapeDtypeStruct((SCAN_OPERANDS,) + cm_shape, F32)] \
        + [jax.ShapeDtypeStruct(cm_shape, F32)] * 2 \
        + [jax.ShapeDtypeStruct((n, D_MODEL), F32)] * 2
    return pl.pallas_call(
        functools.partial(_mix_prep_kernel, tm=tm, s=s, pos0=pos0),
        grid=(nb, tiles),
        in_specs=in_specs,
        out_specs=out_specs,
        out_shape=out_shape,
        scratch_shapes=[pltpu.VMEM((SHIFT_WIDTH, LANES), F32),
                        pltpu.VMEM((up + tm, POOL_WIDTH), F32)],
        compiler_params=pltpu.CompilerParams(
            dimension_semantics=("arbitrary", "arbitrary"), vmem_limit_bytes=VMEM_LIMIT),
        name="mix_prep",
    )(x2d, zc0, uc0, *wts)


def _wkv_step(s_ref, row, vv):
    groups = HEAD_DIM // KEY_GROUP

    def sa_pass(g, acc):
        for kk in range(KEY_GROUP):
            kp = g * KEY_GROUP + kk
            acc = acc + s_ref[kp] * row(0, kp)
        return acc
    sa = lax.fori_loop(0, groups, sa_pass, jnp.zeros(vv.shape, F32))

    def update_pass(g, acc):
        for kk in range(KEY_GROUP):
            kp = g * KEY_GROUP + kk
            sn = s_ref[kp] * row(1, kp) + sa * row(2, kp) + vv * row(3, kp)
            s_ref[kp] = sn
            acc = acc + sn * row(4, kp)
        return acc
    return lax.fori_loop(0, groups, update_pass, jnp.zeros(vv.shape, F32))


def _swap_major_sublane(x):
    return jnp.swapaxes(x, 0, 1)


def _wkv_scan_prompt_kernel(q_ref, o_ref, sout_ref, s_ref, kv_ref, vv_ref, ov_ref, *, tt):
    t = pl.program_id(0)
    q = pl.program_id(1)

    @pl.when((t == 0) & (q == 0))
    def _():
        s_ref[...] = jnp.zeros_like(s_ref)

    def gather_t(base_lo, base_hi):
        pieces = [q_ref[0, bb, pl.ds(base, SUBLANES), :] for base in (base_lo, base_hi) for bb in range(8)]
        return jnp.concatenate(pieces, axis=0).T

    @pl.when(q < SCAN_OPERANDS - 1)
    def _():
        def kgroup(gi, c):
            slabs = []
            for kk in range(SUBLANES):
                base = pl.multiple_of((gi * SUBLANES + kk) * SUBLANES, SUBLANES)
                slabs.append(gather_t(base, base))
            kv_ref[q, :, pl.ds(pl.multiple_of(gi * SUBLANES, SUBLANES), SUBLANES), :] = \
                _swap_major_sublane(jnp.stack(slabs))
            return c
        lax.fori_loop(0, HEAD_DIM // SUBLANES, kgroup, 0, unroll=True)

    @pl.when(q == SCAN_OPERANDS - 1)
    def _():
        def vgroup(gi, c):
            slabs = []
            for vi in range(SUBLANES):
                base = pl.multiple_of((gi * SUBLANES + vi) * 2 * SUBLANES, 2 * SUBLANES)
                slabs.append(gather_t(base, base + SUBLANES))
            vv_ref[:, pl.ds(pl.multiple_of(gi * SUBLANES, SUBLANES), SUBLANES), :] = \
                _swap_major_sublane(jnp.stack(slabs))
            return c
        lax.fori_loop(0, KEY_HALF // SUBLANES, vgroup, 0, unroll=True)

        def step(i, c):
            row = lambda qi, kp: kv_ref[qi, i, pl.ds(kp, 1), :]
            ov_ref[i] = _wkv_step(s_ref, row, vv_ref[i])
            return c
        lax.fori_loop(0, tt, step, 0)

        def ogroup(gi, c):
            g0 = pl.multiple_of(gi * SUBLANES, SUBLANES)
            x = _swap_major_sublane(ov_ref[:, pl.ds(g0, SUBLANES), :])
            for vi in range(SUBLANES):
                xt = x[vi].T
                base = pl.multiple_of((gi * SUBLANES + vi) * 2 * SUBLANES, 2 * SUBLANES)
                for vh in range(2):
                    for bb in range(8):
                        r0 = (vh * 8 + bb) * SUBLANES
                        o_ref[bb, pl.ds(base + vh * SUBLANES, SUBLANES), :] = xt[r0:r0 + SUBLANES]
            return c
        lax.fori_loop(0, KEY_HALF // SUBLANES, ogroup, 0, unroll=True)

        @pl.when(t == pl.num_programs(0) - 1)
        def _():
            sout_ref[...] = s_ref[...]


def _wkv_scan_prompt(ops, *, tt):
    _, nb, _, t_len = ops.shape
    sspec = pl.BlockSpec((HEAD_DIM, KEY_HALF, LANES), lambda ti, qi: (0, 0, 0))
    return pl.pallas_call(
        functools.partial(_wkv_scan_prompt_kernel, tt=tt),
        grid=(t_len // tt, SCAN_OPERANDS),
        in_specs=[pl.BlockSpec((1, nb, RWKV_WIDTH, tt), lambda ti, qi: (qi, 0, 0, ti))],
        out_specs=[pl.BlockSpec((nb, RWKV_WIDTH, tt), lambda ti, qi: (0, 0, ti)), sspec],
        out_shape=[jax.ShapeDtypeStruct((nb, RWKV_WIDTH, t_len), F32),
                   jax.ShapeDtypeStruct((HEAD_DIM, KEY_HALF, LANES), F32)],
        scratch_shapes=[pltpu.VMEM((HEAD_DIM, KEY_HALF, LANES), F32),
                        pltpu.VMEM((SCAN_OPERANDS - 1, tt, HEAD_DIM, LANES), F32),
                        pltpu.VMEM((tt, KEY_HALF, LANES), F32),
                        pltpu.VMEM((tt, KEY_HALF, LANES), F32)],
        compiler_params=pltpu.CompilerParams(
            dimension_semantics=("arbitrary", "arbitrary"), vmem_limit_bytes=VMEM_LIMIT),
        name="wkv_scan_prompt",
    )(ops)


def _wkv_scan_sample_kernel(q_ref, s0_ref, o_ref, sout_ref, s_ref, *, t_len):
    h = pl.program_id(0)
    s_ref[...] = s0_ref[0]
    for i in range(t_len):
        row = lambda qi, kp, i=i: q_ref[qi, i, pl.ds(kp * HEADS + h, 1), :]
        vv = q_ref[SCAN_OPERANDS - 1, i, pl.ds(h, HEAD_DIM, stride=HEADS), :]
        o_ref[i, pl.ds(h, HEAD_DIM, stride=HEADS), :] = _wkv_step(s_ref, row, vv)
    sout_ref[0] = s_ref[...]


def _wkv_scan_sample(ops, s0):
    t_len = ops.shape[1]
    spec = pl.BlockSpec(ops.shape[1:], lambda h: (0, 0, 0))
    sspec = pl.BlockSpec((1, HEAD_DIM, HEAD_DIM, LANES), lambda h: (h, 0, 0, 0))
    return pl.pallas_call(
        functools.partial(_wkv_scan_sample_kernel, t_len=t_len),
        grid=(HEADS,),
        in_specs=[pl.BlockSpec(ops.shape, lambda h: (0, 0, 0, 0)), sspec],
        out_specs=[spec, sspec],
        out_shape=[jax.ShapeDtypeStruct(ops.shape[1:], F32), jax.ShapeDtypeStruct(s0.shape, F32)],
        scratch_shapes=[pltpu.VMEM((HEAD_DIM, HEAD_DIM, LANES), F32)],
        compiler_params=pltpu.CompilerParams(
            dimension_semantics=("arbitrary",), vmem_limit_bytes=VMEM_LIMIT),
        name="wkv_scan_sample",
    )(ops, s0)


def _post_kernel(o_ref, bon_ref, g_ref, pa_ref, sgb_ref, x_ref, cnt0_ref, lnw_ref, lnb_ref,
                 wb_ref, wout_ref, nffn_ref, wrh_ref, wrl_ref, br_ref, x1_ref, h2_ref, rt_ref, cnt_ref,
                 carry, *, sub):
    @pl.when(pl.program_id(0) == 0)
    def _():
        carry[...] = cnt0_ref[...]

    prev = carry[0:1, :]
    for part in range(o_ref.shape[2] // sub):
        prev = _post_rows(pl.ds(part * sub, sub), prev, o_ref, bon_ref, g_ref, pa_ref, sgb_ref, x_ref,
                          lnw_ref, lnb_ref, wb_ref, wout_ref, nffn_ref, wrh_ref, wrl_ref,
                          br_ref, x1_ref, h2_ref, rt_ref)
    carry[...] = jnp.broadcast_to(prev, carry.shape)
    cnt_ref[...] = jnp.broadcast_to(prev, cnt_ref.shape)


def _post_rows(rows, prev_count, o_ref, bon_ref, g_ref, pa_ref, sgb_ref, x_ref, lnw_ref, lnb_ref,
               wb_ref, wout_ref, nffn_ref, wrh_ref, wrl_ref, br_ref, x1_ref, h2_ref, rt_ref):
    o = o_ref[0, :, rows]
    mean = _head_sum(o) * (1.0 / HEAD_DIM)
    d = o - mean
    var = _head_sum(d * d) * (1.0 / HEAD_DIM)
    on = d * lax.rsqrt(var + GN_EPS) * lnw_ref[...] + lnb_ref[...]
    yb = ((on + bon_ref[0, :, rows]) * g_ref[0, :, rows]).astype(BF16)
    mb = lax.dot_general(yb, wb_ref[...], (((0,), (0,)), ((), ())), preferred_element_type=F32)
    merged = pa_ref[rows, :] + sgb_ref[rows, :] * mb
    x1 = x_ref[rows, :] + _dot(merged.astype(BF16), wout_ref[...])
    x1_ref[rows, :] = x1
    h2 = _rmsnorm(x1, nffn_ref[...])
    h2_ref[rows, 0:D_MODEL] = h2

    h_hi, h_lo = _split_bf16(h2)
    logits = (_dot(h_hi, wrh_ref[...]) + _dot(h_lo, wrh_ref[...]) + _dot(h_hi, wrl_ref[...])
              + br_ref[...])
    ln = lax.broadcasted_iota(jnp.int32, logits.shape, 1)
    is_group = (ln >= N_EXPERTS) & (ln < N_EXPERTS + N_GROUPS)
    gl = jnp.where(is_group, logits, NEG_BIG)
    gmax = jnp.max(gl, axis=-1, keepdims=True)
    gsel = jnp.min(jnp.where(gl == gmax, ln, ROUTER_LANES), axis=-1, keepdims=True) - N_EXPERTS
    den = jnp.sum(jnp.where(is_group, jnp.exp(gl - gmax), 0.0), axis=-1, keepdims=True)
    pg = 1.0 / den
    in_group = (ln < N_EXPERTS) & ((ln // EXPERTS_PER_GROUP) == gsel)
    el = jnp.where(in_group, logits, NEG_BIG)
    m1 = jnp.max(el, axis=-1, keepdims=True)
    i1 = jnp.min(jnp.where(el == m1, ln, ROUTER_LANES), axis=-1, keepdims=True)
    el2 = jnp.where(ln == i1, NEG_BIG, el)
    m2 = jnp.max(el2, axis=-1, keepdims=True)
    i2 = jnp.min(jnp.where(el2 == m2, ln, ROUTER_LANES), axis=-1, keepdims=True)
    e2 = jnp.exp(m2 - m1)
    p1 = 1.0 / (1.0 + e2)
    p2 = e2 / (1.0 + e2)

    tm = logits.shape[0]
    sel = ln == gsel
    tri = (lax.broadcasted_iota(jnp.int32, (tm, tm), 1)
           < lax.broadcasted_iota(jnp.int32, (tm, tm), 0)).astype(BF16)
    before = prev_count + _dot(tri, sel.astype(BF16))
    rank = jnp.sum(jnp.where(sel, before, 0.0), axis=-1, keepdims=True)
    first = gsel * EXPERTS_PER_GROUP - COMB_LANE0
    rt = (jnp.where(ln == 0, gsel.astype(F32), 0.0) + jnp.where(ln == 1, rank, 0.0)
          + jnp.where(ln == i1 - first, p1 * pg, 0.0) + jnp.where(ln == i2 - first, p2 * pg, 0.0))
    rt_ref[rows, :] = rt
    h2_ref[rows, D_MODEL:MOE_ROW] = rt
    return prev_count + jnp.sum(sel.astype(F32), axis=0, keepdims=True)


def _post(o, bon, g, pa, sgb, x2d, cnt0, wts, *, tm, sub, cm_index):
    n = x2d.shape[0]
    row = lambda i: (i, 0)
    tok = lambda width: pl.BlockSpec((tm, width), row)
    cm = pl.BlockSpec((1, RWKV_WIDTH, tm), cm_index)
    full = lambda arr: pl.BlockSpec(arr.shape, lambda i: (0, 0))
    return pl.pallas_call(
        functools.partial(_post_kernel, sub=sub),
        grid=(n // tm,),
        in_specs=[cm] * 3 + [tok(D_MODEL)] * 3 + [full(cnt0)] + [full(w) for w in wts],
        out_specs=[tok(D_MODEL), tok(MOE_ROW), tok(ROUTER_LANES),
                   pl.BlockSpec((SUBLANES, ROUTER_LANES), lambda i: (0, 0))],
        out_shape=[jax.ShapeDtypeStruct((n, D_MODEL), F32),
                   jax.ShapeDtypeStruct((n, MOE_ROW), F32),
                   jax.ShapeDtypeStruct((n, ROUTER_LANES), F32),
                   jax.ShapeDtypeStruct((SUBLANES, ROUTER_LANES), F32)],
        scratch_shapes=[pltpu.VMEM((SUBLANES, ROUTER_LANES), F32)],
        compiler_params=pltpu.CompilerParams(
            dimension_semantics=("arbitrary",), vmem_limit_bytes=VMEM_LIMIT),
        name="post",
    )(o, bon, g, pa, sgb, x2d, cnt0, *wts)


def _route_tables(cnt, n_tiles_max):
    counts = cnt[0, :N_GROUPS].astype(jnp.int32)
    tiles_g = (counts + GROUP_TILE - 1) // GROUP_TILE
    tile_end = jnp.cumsum(tiles_g)
    row0 = (tile_end - tiles_g) * GROUP_TILE
    tile_ids = jnp.arange(n_tiles_max, dtype=jnp.int32)
    tile_group = jnp.minimum(jnp.sum(tile_ids[:, None] >= tile_end[None, :], axis=-1),
                             N_GROUPS - 1).astype(jnp.int32)
    return tile_group, tile_end[N_GROUPS - 1:].astype(jnp.int32), row0


def _sorted_rows(rt, row0):
    g = rt[:, 0].astype(jnp.int32)
    r = rt[:, 1].astype(jnp.int32)
    return r + jnp.sum(jnp.where(g[:, None] == jnp.arange(N_GROUPS), row0, 0), axis=-1)


def _dispatch_kernel(pos_ref, h_ref, xs0_ref, xs_ref, sem, *, td):
    del xs0_ref
    for r in range(td):
        pltpu.make_async_copy(h_ref.at[pl.ds(r, 1)], xs_ref.at[pl.ds(pos_ref[r], 1)],
                              sem).start(priority=r % 2)
    pltpu.make_async_copy(h_ref, xs_ref.at[pl.ds(0, td)], sem).wait()


def _dispatch(h, pos, xs_in, *, td):
    n = h.shape[0]
    return pl.pallas_call(
        functools.partial(_dispatch_kernel, td=td),
        grid=(n // td,),
        in_specs=[pl.BlockSpec((td,), lambda i: (i,), memory_space=pltpu.SMEM),
                  pl.BlockSpec((td, MOE_ROW), lambda i: (i, 0)),
                  pl.BlockSpec(memory_space=pl.ANY)],
        out_specs=pl.BlockSpec(memory_space=pl.ANY),
        out_shape=jax.ShapeDtypeStruct(xs_in.shape, F32),
        scratch_shapes=[pltpu.SemaphoreType.DMA(())],
        input_output_aliases={2: 0},
        compiler_params=pltpu.CompilerParams(
            dimension_semantics=("arbitrary",), vmem_limit_bytes=VMEM_LIMIT),
        name="moe_dispatch",
    )(pos, h, xs_in)


def _experts_kernel(tg_ref, nu_ref, xs_ref, wg_ref, wu_ref, wd_ref, ys_ref, acc, xb):
    t = pl.program_id(0)
    e = pl.program_id(1)
    last_e = pl.num_programs(1) - 1

    @pl.when(t < nu_ref[0])
    def _():
        @pl.when(e == 0)
        def _():
            acc[...] = jnp.zeros_like(acc)
            xb[...] = xs_ref[:, 0:D_MODEL].astype(BF16)

        x = xb[...]
        hid = jax.nn.silu(_dot(x, wg_ref[0].astype(BF16))) * _dot(x, wu_ref[0].astype(BF16))
        ye = _dot(hid.astype(BF16), wd_ref[0].astype(BF16))
        rt = xs_ref[:, D_MODEL:MOE_ROW]
        ln = lax.broadcasted_iota(jnp.int32, rt.shape, 1)
        ce = jnp.sum(jnp.where(ln == COMB_LANE0 + e, rt, 0.0), axis=-1, keepdims=True)
        acc[...] += ce * ye

        @pl.when(e == last_e)
        def _():
            ys_ref[...] = acc[...]

    @pl.when((t >= nu_ref[0]) & (e == last_e))
    def _():
        ys_ref[...] = jnp.zeros_like(ys_ref)


def _experts(xs, tile_group, n_used, wg, wu, wd):
    n_rows = xs.shape[0]
    tile = lambda t, e, tg, nu: (jnp.minimum(t, nu[0] - 1), 0)
    wsel = lambda t, e, tg, nu: (tg[jnp.minimum(t, nu[0] - 1)] * EXPERTS_PER_GROUP + e, 0, 0)
    return pl.pallas_call(
        _experts_kernel,
        grid_spec=pltpu.PrefetchScalarGridSpec(
            num_scalar_prefetch=2,
            grid=(n_rows // GROUP_TILE, EXPERTS_PER_GROUP),
            in_specs=[pl.BlockSpec((GROUP_TILE, MOE_ROW), tile),
                      pl.BlockSpec((1, D_MODEL, D_EXPERT), wsel),
                      pl.BlockSpec((1, D_MODEL, D_EXPERT), wsel),
                      pl.BlockSpec((1, D_EXPERT, D_MODEL), wsel)],
            out_specs=pl.BlockSpec((GROUP_TILE, D_MODEL), lambda t, e, tg, nu: (t, 0)),
            scratch_shapes=[pltpu.VMEM((GROUP_TILE, D_MODEL), F32),
                            pltpu.VMEM((GROUP_TILE, D_MODEL), BF16)]),
        out_shape=jax.ShapeDtypeStruct((n_rows, D_MODEL), F32),
        compiler_params=pltpu.CompilerParams(
            dimension_semantics=("arbitrary", "arbitrary"), vmem_limit_bytes=VMEM_LIMIT),
        name="moe_experts",
    )(tile_group, n_used, xs, wg, wu, wd)


def _tail_kernel(pos_ref, posn_ref, x1_ref, p_ref, ys_ref, nple_ref, wpg_ref, wpp_ref,
                 nfin_ref, y_ref, ybuf0, ybuf1, sem, *, tm):
    i = pl.program_id(0)
    last = pl.num_programs(0) - 1
    ybuf = (ybuf0, ybuf1)

    def issue(p_ref, p_off, slot):
        for r in range(tm):
            pltpu.make_async_copy(ys_ref.at[pl.ds(p_ref[p_off + r], 1)],
                                  ybuf[slot].at[pl.ds(r, 1)], sem.at[slot]).start(priority=r % 2)

    def wait(slot):
        pltpu.make_async_copy(ys_ref.at[pl.ds(0, tm)], ybuf[slot], sem.at[slot]).wait()

    def compute(slot):
        rows = pl.ds(slot * tm, tm)
        x2 = x1_ref[rows, :] + ybuf[slot][...]
        h3 = _rmsnorm(x2, nple_ref[...]).astype(BF16)
        ple = (jax.nn.sigmoid(_dot(h3, wpg_ref[...]))
               * _dot(p_ref[rows, :].astype(BF16), wpp_ref[...]))
        y_ref[rows, :] = _rmsnorm(x2 + ple, nfin_ref[...])

    @pl.when(i == 0)
    def _():
        issue(pos_ref, 0, 0)

    wait(0)
    issue(pos_ref, tm, 1)
    compute(0)
    wait(1)
    issue(posn_ref, 0, 0)
    compute(1)

    @pl.when(i == last)
    def _():
        wait(0)


def _tail(pos, x1, p2d, ys, nple, wpg, wpp, nfin, *, tm):
    n = x1.shape[0]
    n_steps = n // (2 * tm)
    tok = lambda width: pl.BlockSpec((2 * tm, width), lambda i: (i, 0))
    full = lambda arr: pl.BlockSpec(arr.shape, lambda i: (0, 0))
    return pl.pallas_call(
        functools.partial(_tail_kernel, tm=tm),
        grid=(n_steps,),
        in_specs=[pl.BlockSpec((2 * tm,), lambda i: (i,), memory_space=pltpu.SMEM),
                  pl.BlockSpec((tm,), lambda i: (2 * jnp.minimum(i + 1, n_steps - 1),),
                               memory_space=pltpu.SMEM),
                  tok(D_MODEL), tok(PLE_DIM),
                  pl.BlockSpec(memory_space=pl.ANY),
                  full(nple), full(wpg), full(wpp), full(nfin)],
        out_specs=tok(D_MODEL),
        out_shape=jax.ShapeDtypeStruct((n, D_MODEL), F32),
        scratch_shapes=[pltpu.VMEM((tm, D_MODEL), F32), pltpu.VMEM((tm, D_MODEL), F32),
                        pltpu.SemaphoreType.DMA((2,))],
        compiler_params=pltpu.CompilerParams(
            dimension_semantics=("arbitrary",), vmem_limit_bytes=VMEM_LIMIT),
        name="moe_tail",
    )(pos, pos, x1, p2d, ys, nple, wpg, wpp, nfin)


def _moe_tail(groups, cnt, wg, wu, wd, nple, wpg, wpp, nfin, *, tm):
    n_all = sum(g[0].shape[0] for g in groups)
    n_tiles_max = (n_all + GROUP_TILE - 1) // GROUP_TILE + N_GROUPS
    tile_group, n_used, row0 = _route_tables(cnt, n_tiles_max)
    xs = jnp.zeros((n_tiles_max * GROUP_TILE, MOE_ROW), F32)
    poss = []
    for hx, rt, _, _, td in groups:
        poss.append(_sorted_rows(rt, row0))
        xs = _dispatch(hx, poss[-1], xs, td=td)
    ys = _experts(xs, tile_group, n_used, wg, wu, wd)
    return [_tail(pos, x1, p2d, ys, nple, wpg, wpp, nfin, tm=tm)
            for pos, (_, _, x1, p2d, _) in zip(poss, groups)]


def kernel(x_prompt, x_sample, state_pool, state_shift, state_wkv, p_prompt, p_sample, norm_mix, w_in, pool_mix, pool_scale, w_branch_a, shift_mu, decay_w0, decay_w2, iclr_a0, iclr_a2, gate_g2, k_k, k_a, r_k, ln_x_w, ln_x_b, w_branch_b, w_out, norm_ffn, w_route_group, b_route_group, w_route_expert, b_route_expert, expert_gate, expert_up, expert_down, norm_ple, w_ple_gate, w_ple_proj, norm_final):
    l = 0
    bsz, seq, _ = x_prompt.shape
    dbsz, dseq, _ = x_sample.shape
    row = lambda vec: vec.reshape(1, -1).astype(F32)
    col = lambda vec: vec.reshape(-1, 1).astype(F32)
    perm = CHANNEL_PERM

    o1 = POOL_WIDTH
    o2 = o1 + SHIFT_WIDTH
    zperm = np.concatenate([perm, RWKV_WIDTH + perm, 2 * RWKV_WIDTH + perm,
                            np.arange(3 * RWKV_WIDTH, SHIFT_WIDTH)])
    w_z = w_in[l][:, o1:o2][:, zperm].astype(BF16)
    zeros_lora = jnp.zeros((RWKV_WIDTH, LORA_PAIR // 2), F32)
    w2t_pad = jnp.concatenate([decay_w2[l].T[perm], zeros_lora], axis=1).astype(BF16)
    a2t_pad = jnp.concatenate([zeros_lora, iclr_a2[l].T[perm]], axis=1).astype(BF16)
    w_router = jnp.concatenate(
        [w_route_expert[l], w_route_group[l],
         jnp.zeros((D_MODEL, ROUTER_LANES - N_EXPERTS - N_GROUPS), F32)], axis=1)
    wr_hi = w_router.astype(BF16)
    wr_lo = (w_router - wr_hi.astype(F32)).astype(BF16)
    b_router = jnp.concatenate(
        [b_route_expert[l], b_route_group[l],
         jnp.zeros((ROUTER_LANES - N_EXPERTS - N_GROUPS,), F32)]).reshape(1, -1)
    prep_w = [row(norm_mix[l]), w_in[l][:, :o1].astype(BF16), w_z, w_in[l][:, o2:].astype(BF16),
              col(shift_mu[l][zperm]), col(decay_w0[l][perm]), w2t_pad, col(iclr_a0[l][perm]), a2t_pad,
              gate_g2[l].T[perm].astype(BF16), col(k_k[l][perm]), col(k_a[l][perm]),
              col(r_k[l].reshape(-1)[perm]), pool_mix[l].astype(BF16), row(pool_scale[l]),
              w_branch_a[l].astype(BF16)]
    post_w = [col(ln_x_w[l][perm]), col(ln_x_b[l][perm]), w_branch_b[l][perm].astype(BF16),
              w_out[l].astype(BF16), row(norm_ffn[l]), wr_hi, wr_lo, b_router]
    moe_w = [expert_gate[l], expert_up[l], expert_down[l],
             row(norm_ple[l]), w_ple_gate[l].astype(BF16), w_ple_proj[l].astype(BF16), row(norm_final)]

    tm_p = 256
    tiles_p = seq // tm_p
    x_p = x_prompt.reshape(bsz * seq, D_MODEL)
    outs = _mix_prep(x_p, jnp.zeros((bsz * SHIFT_WIDTH, LANES), F32),
                     jnp.zeros((bsz * 16, POOL_WIDTH), F32), prep_w,
                     nb=bsz, tiles=tiles_p, tm=tm_p, s=1, pos0=0, cm_index=lambda b, t: (b, 0, t))
    u_p, zl_p, ops, g, bon, pa, sgb = outs
    o_p, s_p = _wkv_scan_prompt(ops, tt=LANES)
    tm_post = 2 * tm_p
    n_p = bsz * seq
    n_s = dbsz * dseq
    x1_p, hx_p, rt_p, cnt_p = _post(
        o_p, bon, g, pa, sgb, x_p, jnp.zeros((SUBLANES, ROUTER_LANES), F32), post_w,
        tm=tm_post, sub=tm_p, cm_index=lambda i: (i // (seq // tm_post), 0, i % (seq // tm_post)))
    wkv_p = s_p.reshape(KEY_HALF, 2, KEY_HALF, 2, bsz, HEADS).transpose(4, 5, 3, 2, 1, 0)
    wkv_p = wkv_p.reshape(bsz, HEADS, HEAD_DIM, HEAD_DIM)

    n_s = dbsz * dseq
    x_s = x_sample.transpose(1, 0, 2).reshape(n_s, D_MODEL)
    p_s = p_sample[l].transpose(1, 0, 2).reshape(n_s, PLE_DIM)
    uc0_s = jnp.concatenate(
        [jnp.zeros((dbsz, POOL_WIDTH), F32),
         state_pool[l].transpose(1, 0, 2).reshape(POOL_STATE * dbsz, POOL_WIDTH)], axis=0)
    zc0_s = state_shift[l].T[zperm]
    outs = _mix_prep(x_s, zc0_s, uc0_s, prep_w, nb=1, tiles=dseq, tm=dbsz, s=dbsz, pos0=PAST_LEN,
                     cm_index=lambda b, t: (t, 0, 0))
    u_s, zl_s, ops, g, bon, pa, sgb = outs
    s0_s = state_wkv[l].reshape(dbsz, HEADS, 2, KEY_HALF, 2, KEY_HALF).transpose(1, 5, 4, 3, 2, 0)
    s0_s = s0_s.reshape(HEADS, HEAD_DIM, HEAD_DIM, dbsz)
    o_s, s_s = _wkv_scan_sample(ops, s0_s)
    x1_s, hx_s, rt_s, cnt_all = _post(o_s, bon, g, pa, sgb, x_s, cnt_p, post_w,
                                      tm=dbsz, sub=dbsz, cm_index=lambda i: (i, 0, 0))

    y_p, y_s = _moe_tail(
        [(hx_p, rt_p, x1_p, p_prompt[l].reshape(n_p, PLE_DIM), 512), (hx_s, rt_s, x1_s, p_s, n_s)],
        cnt_all, *moe_w, tm=256)
    wkv_s = s_s.reshape(HEADS, KEY_HALF, 2, KEY_HALF, 2, dbsz).transpose(5, 0, 4, 3, 2, 1)
    wkv_s = wkv_s.reshape(dbsz, HEADS, HEAD_DIM, HEAD_DIM)

    inv_z = np.argsort(zperm)
    y_prompt = y_p.reshape(bsz, seq, D_MODEL)
    y_sample = y_s.reshape(dseq, dbsz, D_MODEL).transpose(1, 0, 2)
    pool_prompt = u_p.reshape(bsz, seq, POOL_WIDTH)[:, seq - POOL_STATE:]
    u_s_bt = u_s.reshape(dseq, dbsz, POOL_WIDTH).transpose(1, 0, 2)
    pool_sample = jnp.concatenate([state_pool[l][:, dseq:], u_s_bt], axis=1)
    shift_prompt = zl_p[:, :, LANES - 1][:, inv_z]
    shift_sample = zl_s[0].T[:, inv_z]
    return (y_prompt, y_sample, pool_prompt[None], shift_prompt[None], wkv_p[None],
            pool_sample[None], shift_sample[None], wkv_s[None])
```

```python
import functools
import math

import jax
import jax.numpy as jnp
import numpy as np
from jax import lax
from jax.experimental import pallas as pl
from jax.experimental.pallas import tpu as pltpu

F32 = jnp.float32
BF16 = jnp.bfloat16

D_MODEL = 1024
PLE_DIM = 256
POOL_WIDTH = 512
POOL_WINDOWS = (2, 4, 8, 16)
POOL_GROUP_DIM = 128
POOL_STATE = 15
RWKV_WIDTH = 512
HEAD_DIM = 64
HEADS = 8
LORA_PAIR = 128
GATE_LORA = 128
SHIFT_WIDTH = 3 * RWKV_WIDTH + LORA_PAIR + GATE_LORA
IN_WIDTH = POOL_WIDTH + SHIFT_WIDTH + 2 * D_MODEL
N_GROUPS = 4
EXPERTS_PER_GROUP = 8
N_EXPERTS = 32
D_EXPERT = 256
RMS_EPS = 1e-6
GN_EPS = 64e-5
PAST_LEN = 16384

LANES = 128
SUBLANES = 8
KEY_HALF = HEAD_DIM // 2
SCAN_OPERANDS = 6
KEY_GROUP = 16
ROUTER_LANES = 128
GROUP_TILE = 1024
EXPERTS_PER_STEP = 2
DECAY_SCALE = math.exp(-0.5)
COMB_LANE0 = 8
MOE_ROW = D_MODEL + ROUTER_LANES
NEG_BIG = -1e30
VMEM_LIMIT = 56 * 1024 * 1024

_J = np.arange(RWKV_WIDTH)
CHANNEL_PERM = (_J % 8) * HEAD_DIM + ((_J // 8) % 2) * KEY_HALF + _J // 16


def _dot(a, b):
    return jnp.dot(a, b, preferred_element_type=F32)


def _split_bf16(x):
    hi = x.astype(BF16)
    return hi, (x - hi.astype(F32)).astype(BF16)


def _head_sum(x):
    x3 = x.reshape(HEAD_DIM, HEADS, x.shape[1])
    s = jnp.sum(x3, axis=0, keepdims=True)
    return jnp.broadcast_to(s, x3.shape).reshape(x.shape)


def _rmsnorm(x, g):
    return x * lax.rsqrt(jnp.mean(x * x, axis=-1, keepdims=True) + RMS_EPS) * g


def _sigmoid(x):
    return 0.5 * jnp.tanh(0.5 * x) + 0.5


def _mix_prep_kernel(x_ref, zc0_ref, uc0_ref, nmix_ref, wu_ref, wz_ref, wgab_ref, mu_ref, w0_ref,
                     w2t_ref, a0_ref, a2t_ref, g2t_ref, kk_ref, ka_ref, rk_ref, mix_ref,
                     pscale_ref, wa_ref,
                     u_ref, zl_ref, q_ref, g_ref, bon_ref,
                     pa_ref, sgb_ref, zc, uext, *, tm, s, pos0):
    t = pl.program_id(1)
    up = 16 * s

    @pl.when(t == 0)
    def _():
        zc[...] = zc0_ref[...]
        uext[0:up] = uc0_ref[...]

    x = x_ref[...]
    h = _rmsnorm(x, nmix_ref[...]).astype(BF16)
    u = _dot(h, wu_ref[...])
    gab = _dot(h, wgab_ref[...])
    u_ref[...] = u
    sgb_ref[...] = _sigmoid(gab[:, D_MODEL:])
    uext[up:up + tm] = u

    z_t = _dot(h, wz_ref[...]).T
    if s == 1:
        lane = lax.broadcasted_iota(jnp.int32, z_t.shape, 1)
        zprev = jnp.where(lane == 0, zc[:, LANES - 1:LANES], pltpu.roll(z_t, 1, axis=1))
    else:
        zprev = zc[...]
    zc[...] = z_t[:, tm - LANES:tm]
    zl_ref[0] = z_t[:, tm - LANES:tm]
    zm = z_t + (zprev - z_t) * mu_ref[...]
    r = zm[0:RWKV_WIDTH]
    k = zm[RWKV_WIDTH:2 * RWKV_WIDTH]
    v = zm[2 * RWKV_WIDTH:3 * RWKV_WIDTH]
    lora_in = zm[3 * RWKV_WIDTH:3 * RWKV_WIDTH + LORA_PAIR]
    gd = zm[3 * RWKV_WIDTH + LORA_PAIR:SHIFT_WIDTH]
    dw = _dot(w2t_ref[...], jnp.tanh(lora_in).astype(BF16))
    da = _dot(a2t_ref[...], lora_in.astype(BF16))
    decay = jnp.exp(-DECAY_SCALE * _sigmoid(w0_ref[...] + dw))
    a = _sigmoid(a0_ref[...] + da)
    kk = k * kk_ref[...]
    kk = kk / jnp.maximum(jnp.sqrt(_head_sum(kk * kk)), 1e-12)
    k2 = k * (1.0 + (a - 1.0) * ka_ref[...])
    q_ref[0, 0] = -kk
    q_ref[1, 0] = decay
    q_ref[2, 0] = kk * a
    q_ref[3, 0] = k2
    q_ref[4, 0] = r
    q_ref[5, 0] = v
    g_ref[0] = _dot(g2t_ref[...], _sigmoid(gd).astype(BF16))
    bon_ref[0] = _head_sum(r * k2 * rk_ref[...]) * v

    rows = lax.broadcasted_iota(jnp.int32, (tm, POOL_GROUP_DIM), 0)
    if s > 1:
        rows = rows // s
    pos = pos0 + t * (tm // s) + rows
    ys = []
    for gi, wnd in enumerate(POOL_WINDOWS):
        lanes = slice(gi * POOL_GROUP_DIM, (gi + 1) * POOL_GROUP_DIM)
        wsum = uext[:, lanes]
        span = 1
        while span < wnd:
            wsum = wsum + pltpu.roll(wsum, span * s, axis=0)
            span *= 2
        cur = uext[pl.ds(up, tm), lanes]
        cnt = jnp.minimum(pos + 1, wnd).astype(F32)
        pooled = wsum[up:up + tm] / cnt - cur
        ys.append(_dot(pooled.astype(BF16), mix_ref[gi]))
    y = jnp.concatenate(ys, axis=-1) * pscale_ref[...]
    pa_ref[...] = _sigmoid(gab[:, :D_MODEL]) * _dot(y.astype(BF16), wa_ref[...])

    uext[0:up] = uext[tm:tm + up]


def _mix_prep(x2d, zc0, uc0, wts, *, nb, tiles, tm, s, pos0, cm_index):
    n = x2d.shape[0]
    up = 16 * s
    row = lambda b, t: (b * tiles + t, 0)
    full = lambda arr: pl.BlockSpec(arr.shape, lambda b, t: (0,) * arr.ndim)
    in_specs = [
        pl.BlockSpec((tm, D_MODEL), row),
        pl.BlockSpec((SHIFT_WIDTH, LANES), lambda b, t: (b, 0)),
        pl.BlockSpec((up, POOL_WIDTH), lambda b, t: (b, 0)),
    ] + [full(w) for w in wts]
    tok = lambda width: pl.BlockSpec((tm, width), row)
    cm_shape = (nb, RWKV_WIDTH, tiles * tm) if s == 1 else (tiles, RWKV_WIDTH, tm)
    cm = pl.BlockSpec((1, RWKV_WIDTH, tm), cm_index)
    cm_stack = pl.BlockSpec((SCAN_OPERANDS, 1, RWKV_WIDTH, tm), lambda b, t: (0,) + cm_index(b, t))
    out_specs = [tok(POOL_WIDTH), pl.BlockSpec((1, SHIFT_WIDTH, LANES), lambda b, t: (b, 0, 0)),
                 cm_stack, cm, cm] + [tok(D_MODEL)] * 2
    out_shape = [jax.ShapeDtypeStruct((n, POOL_WIDTH), F32),
                 jax.ShapeDtypeStruct((nb, SHIFT_WIDTH, LANES), F32),
                 jax.ShapeDtypeStruct((SCAN_OPERANDS,) + cm_shape, F32)] \
        + [jax.ShapeDtypeStruct(cm_shape, F32)] * 2 \
        + [jax.ShapeDtypeStruct((n, D_MODEL), F32)] * 2
    return pl.pallas_call(
        functools.partial(_mix_prep_kernel, tm=tm, s=s, pos0=pos0),
        grid=(nb, tiles),
        in_specs=in_specs,
        out_specs=out_specs,
        out_shape=out_shape,
        scratch_shapes=[pltpu.VMEM((SHIFT_WIDTH, LANES), F32),
                        pltpu.VMEM((up + tm, POOL_WIDTH), F32)],
        compiler_params=pltpu.CompilerParams(
            dimension_semantics=("arbitrary", "arbitrary"), vmem_limit_bytes=VMEM_LIMIT),
        name="mix_prep",
    )(x2d, zc0, uc0, *wts)


def _wkv_step(s_ref, row, vv):
    groups = HEAD_DIM // KEY_GROUP

    def sa_pass(g, acc):
        for kk in range(KEY_GROUP):
            kp = g * KEY_GROUP + kk
            acc = acc + s_ref[kp] * row(0, kp)
        return acc
    sa = lax.fori_loop(0, groups, sa_pass, jnp.zeros(vv.shape, F32))

    def update_pass(g, acc):
        for kk in range(KEY_GROUP):
            kp = g * KEY_GROUP + kk
            sn = s_ref[kp] * row(1, kp) + sa * row(2, kp) + vv * row(3, kp)
            s_ref[kp] = sn
            acc = acc + sn * row(4, kp)
        return acc
    return lax.fori_loop(0, groups, update_pass, jnp.zeros(vv.shape, F32))


def _swap_major_sublane(x):
    return jnp.swapaxes(x, 0, 1)


def _wkv_scan_prompt_kernel(q_ref, o_ref, sout_ref, s_ref, kv_ref, vv_ref, ov_ref, *, tt):
    t = pl.program_id(0)
    q = pl.program_id(1)

    @pl.when((t == 0) & (q == 0))
    def _():
        s_ref[...] = jnp.zeros_like(s_ref)

    def gather_t(base_lo, base_hi):
        pieces = [q_ref[0, bb, pl.ds(base, SUBLANES), :] for base in (base_lo, base_hi) for bb in range(8)]
        return jnp.concatenate(pieces, axis=0).T

    @pl.when(q < SCAN_OPERANDS - 1)
    def _():
        def kgroup(gi, c):
            slabs = []
            for kk in range(SUBLANES):
                base = pl.multiple_of((gi * SUBLANES + kk) * SUBLANES, SUBLANES)
                slabs.append(gather_t(base, base))
            kv_ref[q, :, pl.ds(pl.multiple_of(gi * SUBLANES, SUBLANES), SUBLANES), :] = \
                _swap_major_sublane(jnp.stack(slabs))
            return c
        lax.fori_loop(0, HEAD_DIM // SUBLANES, kgroup, 0, unroll=True)

    @pl.when(q == SCAN_OPERANDS - 1)
    def _():
        def vgroup(gi, c):
            slabs = []
            for vi in range(SUBLANES):
                base = pl.multiple_of((gi * SUBLANES + vi) * 2 * SUBLANES, 2 * SUBLANES)
                slabs.append(gather_t(base, base + SUBLANES))
            vv_ref[:, pl.ds(pl.multiple_of(gi * SUBLANES, SUBLANES), SUBLANES), :] = \
                _swap_major_sublane(jnp.stack(slabs))
            return c
        lax.fori_loop(0, KEY_HALF // SUBLANES, vgroup, 0, unroll=True)

        def step(i, c):
            row = lambda qi, kp: kv_ref[qi, i, pl.ds(kp, 1), :]
            ov_ref[i] = _wkv_step(s_ref, row, vv_ref[i])
            return c
        lax.fori_loop(0, tt, step, 0)

        def ogroup(gi, c):
            g0 = pl.multiple_of(gi * SUBLANES, SUBLANES)
            x = _swap_major_sublane(ov_ref[:, pl.ds(g0, SUBLANES), :])
            for vi in range(SUBLANES):
                xt = x[vi].T
                base = pl.multiple_of((gi * SUBLANES + vi) * 2 * SUBLANES, 2 * SUBLANES)
                for vh in range(2):
                    for bb in range(8):
                        r0 = (vh * 8 + bb) * SUBLANES
                        o_ref[bb, pl.ds(base + vh * SUBLANES, SUBLANES), :] = xt[r0:r0 + SUBLANES]
            return c
        lax.fori_loop(0, KEY_HALF // SUBLANES, ogroup, 0, unroll=True)

        @pl.when(t == pl.num_programs(0) - 1)
        def _():
            sout_ref[...] = s_ref[...]


def _wkv_scan_prompt(ops, *, tt):
    _, nb, _, t_len = ops.shape
    sspec = pl.BlockSpec((HEAD_DIM, KEY_HALF, LANES), lambda ti, qi: (0, 0, 0))
    return pl.pallas_call(
        functools.partial(_wkv_scan_prompt_kernel, tt=tt),
        grid=(t_len // tt, SCAN_OPERANDS),
        in_specs=[pl.BlockSpec((1, nb, RWKV_WIDTH, tt), lambda ti, qi: (qi, 0, 0, ti))],
        out_specs=[pl.BlockSpec((nb, RWKV_WIDTH, tt), lambda ti, qi: (0, 0, ti)), sspec],
        out_shape=[jax.ShapeDtypeStruct((nb, RWKV_WIDTH, t_len), F32),
                   jax.ShapeDtypeStruct((HEAD_DIM, KEY_HALF, LANES), F32)],
        scratch_shapes=[pltpu.VMEM((HEAD_DIM, KEY_HALF, LANES), F32),
                        pltpu.VMEM((SCAN_OPERANDS - 1, tt, HEAD_DIM, LANES), F32),
                        pltpu.VMEM((tt, KEY_HALF, LANES), F32),
                        pltpu.VMEM((tt, KEY_HALF, LANES), F32)],
        compiler_params=pltpu.CompilerParams(
            dimension_semantics=("arbitrary", "arbitrary"), vmem_limit_bytes=VMEM_LIMIT),
        name="wkv_scan_prompt",
    )(ops)


def _wkv_scan_sample_kernel(q_ref, s0_ref, o_ref, sout_ref, s_ref, *, t_len):
    h = pl.program_id(0)
    s_ref[...] = s0_ref[0]
    for i in range(t_len):
        row = lambda qi, kp, i=i: q_ref[qi, i, pl.ds(kp * HEADS + h, 1), :]
        vv = q_ref[SCAN_OPERANDS - 1, i, pl.ds(h, HEAD_DIM, stride=HEADS), :]
        o_ref[i, pl.ds(h, HEAD_DIM, stride=HEADS), :] = _wkv_step(s_ref, row, vv)
    sout_ref[0] = s_ref[...]


def _wkv_scan_sample(ops, s0):
    t_len = ops.shape[1]
    spec = pl.BlockSpec(ops.shape[1:], lambda h: (0, 0, 0))
    sspec = pl.BlockSpec((1, HEAD_DIM, HEAD_DIM, LANES), lambda h: (h, 0, 0, 0))
    return pl.pallas_call(
        functools.partial(_wkv_scan_sample_kernel, t_len=t_len),
        grid=(HEADS,),
        in_specs=[pl.BlockSpec(ops.shape, lambda h: (0, 0, 0, 0)), sspec],
        out_specs=[spec, sspec],
        out_shape=[jax.ShapeDtypeStruct(ops.shape[1:], F32), jax.ShapeDtypeStruct(s0.shape, F32)],
        scratch_shapes=[pltpu.VMEM((HEAD_DIM, HEAD_DIM, LANES), F32)],
        compiler_params=pltpu.CompilerParams(
            dimension_semantics=("arbitrary",), vmem_limit_bytes=VMEM_LIMIT),
        name="wkv_scan_sample",
    )(ops, s0)


def _post_kernel(o_ref, bon_ref, g_ref, pa_ref, sgb_ref, x_ref, cnt0_ref, lnw_ref, lnb_ref,
                 wb_ref, wout_ref, nffn_ref, wrh_ref, wrl_ref, br_ref, x1_ref, h2_ref, rt_ref, cnt_ref,
                 carry, *, sub):
    @pl.when(pl.program_id(0) == 0)
    def _():
        carry[...] = cnt0_ref[...]

    prev = carry[0:1, :]
    for part in range(o_ref.shape[2] // sub):
        prev = _post_rows(pl.ds(part * sub, sub), prev, o_ref, bon_ref, g_ref, pa_ref, sgb_ref, x_ref,
                          lnw_ref, lnb_ref, wb_ref, wout_ref, nffn_ref, wrh_ref, wrl_ref,
                          br_ref, x1_ref, h2_ref, rt_ref)
    carry[...] = jnp.broadcast_to(prev, carry.shape)
    cnt_ref[...] = jnp.broadcast_to(prev, cnt_ref.shape)


def _post_rows(rows, prev_count, o_ref, bon_ref, g_ref, pa_ref, sgb_ref, x_ref, lnw_ref, lnb_ref,
               wb_ref, wout_ref, nffn_ref, wrh_ref, wrl_ref, br_ref, x1_ref, h2_ref, rt_ref):
    o = o_ref[0, :, rows]
    mean = _head_sum(o) * (1.0 / HEAD_DIM)
    d = o - mean
    var = _head_sum(d * d) * (1.0 / HEAD_DIM)
    on = d * lax.rsqrt(var + GN_EPS) * lnw_ref[...] + lnb_ref[...]
    yb = ((on + bon_ref[0, :, rows]) * g_ref[0, :, rows]).astype(BF16)
    mb = lax.dot_general(yb, wb_ref[...], (((0,), (0,)), ((), ())), preferred_element_type=F32)
    merged = pa_ref[rows, :] + sgb_ref[rows, :] * mb
    x1 = x_ref[rows, :] + _dot(merged.astype(BF16), wout_ref[...])
    x1_ref[rows, :] = x1
    h2 = _rmsnorm(x1, nffn_ref[...])
    h2_ref[rows, 0:D_MODEL] = h2

    h_hi, h_lo = _split_bf16(h2)
    logits = (_dot(h_hi, wrh_ref[...]) + _dot(h_lo, wrh_ref[...]) + _dot(h_hi, wrl_ref[...])
              + br_ref[...])
    ln = lax.broadcasted_iota(jnp.int32, logits.shape, 1)
    is_group = (ln >= N_EXPERTS) & (ln < N_EXPERTS + N_GROUPS)
    gl = jnp.where(is_group, logits, NEG_BIG)
    gmax = jnp.max(gl, axis=-1, keepdims=True)
    gsel = jnp.min(jnp.where(gl == gmax, ln, ROUTER_LANES), axis=-1, keepdims=True) - N_EXPERTS
    den = jnp.sum(jnp.where(is_group, jnp.exp(gl - gmax), 0.0), axis=-1, keepdims=True)
    pg = 1.0 / den
    in_group = (ln < N_EXPERTS) & ((ln // EXPERTS_PER_GROUP) == gsel)
    el = jnp.where(in_group, logits, NEG_BIG)
    m1 = jnp.max(el, axis=-1, keepdims=True)
    i1 = jnp.min(jnp.where(el == m1, ln, ROUTER_LANES), axis=-1, keepdims=True)
    el2 = jnp.where(ln == i1, NEG_BIG, el)
    m2 = jnp.max(el2, axis=-1, keepdims=True)
    i2 = jnp.min(jnp.where(el2 == m2, ln, ROUTER_LANES), axis=-1, keepdims=True)
    e2 = jnp.exp(m2 - m1)
    p1 = 1.0 / (1.0 + e2)
    p2 = e2 / (1.0 + e2)

    tm = logits.shape[0]
    sel = ln == gsel
    tri = (lax.broadcasted_iota(jnp.int32, (tm, tm), 1)
           < lax.broadcasted_iota(jnp.int32, (tm, tm), 0)).astype(BF16)
    before = prev_count + _dot(tri, sel.astype(BF16))
    rank = jnp.sum(jnp.where(sel, before, 0.0), axis=-1, keepdims=True)
    first = gsel * EXPERTS_PER_GROUP - COMB_LANE0
    rt = (jnp.where(ln == 0, gsel.astype(F32), 0.0) + jnp.where(ln == 1, rank, 0.0)
          + jnp.where(ln == i1 - first, p1 * pg, 0.0) + jnp.where(ln == i2 - first, p2 * pg, 0.0))
    rt_ref[rows, :] = rt
    h2_ref[rows, D_MODEL:MOE_ROW] = rt
    return prev_count + jnp.sum(sel.astype(F32), axis=0, keepdims=True)


def _post(o, bon, g, pa, sgb, x2d, cnt0, wts, *, tm, sub, cm_index):
    n = x2d.shape[0]
    row = lambda i: (i, 0)
    tok = lambda width: pl.BlockSpec((tm, width), row)
    cm = pl.BlockSpec((1, RWKV_WIDTH, tm), cm_index)
    full = lambda arr: pl.BlockSpec(arr.shape, lambda i: (0, 0))
    return pl.pallas_call(
        functools.partial(_post_kernel, sub=sub),
        grid=(n // tm,),
        in_specs=[cm] * 3 + [tok(D_MODEL)] * 3 + [full(cnt0)] + [full(w) for w in wts],
        out_specs=[tok(D_MODEL), tok(MOE_ROW), tok(ROUTER_LANES),
                   pl.BlockSpec((SUBLANES, ROUTER_LANES), lambda i: (0, 0))],
        out_shape=[jax.ShapeDtypeStruct((n, D_MODEL), F32),
                   jax.ShapeDtypeStruct((n, MOE_ROW), F32),
                   jax.ShapeDtypeStruct((n, ROUTER_LANES), F32),
                   jax.ShapeDtypeStruct((SUBLANES, ROUTER_LANES), F32)],
        scratch_shapes=[pltpu.VMEM((SUBLANES, ROUTER_LANES), F32)],
        compiler_params=pltpu.CompilerParams(
            dimension_semantics=("arbitrary",), vmem_limit_bytes=VMEM_LIMIT),
        name="post",
    )(o, bon, g, pa, sgb, x2d, cnt0, *wts)


def _route_tables(cnt, n_tiles_max):
    counts = cnt[0, :N_GROUPS].astype(jnp.int32)
    tiles_g = (counts + GROUP_TILE - 1) // GROUP_TILE
    tile_end = jnp.cumsum(tiles_g)
    row0 = (tile_end - tiles_g) * GROUP_TILE
    tile_ids = jnp.arange(n_tiles_max, dtype=jnp.int32)
    tile_group = jnp.minimum(jnp.sum(tile_ids[:, None] >= tile_end[None, :], axis=-1),
                             N_GROUPS - 1).astype(jnp.int32)
    return tile_group, tile_end[N_GROUPS - 1:].astype(jnp.int32), row0


def _sorted_rows(rt, row0):
    g = rt[:, 0].astype(jnp.int32)
    r = rt[:, 1].astype(jnp.int32)
    return r + jnp.sum(jnp.where(g[:, None] == jnp.arange(N_GROUPS), row0, 0), axis=-1)


def _dispatch_kernel(pos_ref, h_ref, xs0_ref, xs_ref, sem, *, td):
    del xs0_ref
    for r in range(td):
        pltpu.make_async_copy(h_ref.at[pl.ds(r, 1)], xs_ref.at[pl.ds(pos_ref[r], 1)],
                              sem).start(priority=r % 2)
    pltpu.make_async_copy(h_ref, xs_ref.at[pl.ds(0, td)], sem).wait()


def _dispatch(h, pos, xs_in, *, td):
    n = h.shape[0]
    return pl.pallas_call(
        functools.partial(_dispatch_kernel, td=td),
        grid=(n // td,),
        in_specs=[pl.BlockSpec((td,), lambda i: (i,), memory_space=pltpu.SMEM),
                  pl.BlockSpec((td, MOE_ROW), lambda i: (i, 0)),
                  pl.BlockSpec(memory_space=pl.ANY)],
        out_specs=pl.BlockSpec(memory_space=pl.ANY),
        out_shape=jax.ShapeDtypeStruct(xs_in.shape, F32),
        scratch_shapes=[pltpu.SemaphoreType.DMA(())],
        input_output_aliases={2: 0},
        compiler_params=pltpu.CompilerParams(
            dimension_semantics=("arbitrary",), vmem_limit_bytes=VMEM_LIMIT),
        name="moe_dispatch",
    )(pos, h, xs_in)


def _experts_kernel(tg_ref, nu_ref, xs_ref, wg_ref, wu_ref, wd_ref, ys_ref, acc, xb):
    t = pl.program_id(0)
    e = pl.program_id(1)
    last_e = pl.num_programs(1) - 1

    @pl.when(t < nu_ref[0])
    def _():
        @pl.when(e == 0)
        def _():
            acc[...] = jnp.zeros_like(acc)
            xb[...] = xs_ref[:, 0:D_MODEL].astype(BF16)

        x = xb[...]
        rt = xs_ref[:, D_MODEL:MOE_ROW]
        ln = lax.broadcasted_iota(jnp.int32, rt.shape, 1)
        y = acc[...]
        for j in range(EXPERTS_PER_STEP):
            gate = _dot(x, wg_ref[j].astype(BF16))
            hid = gate * _sigmoid(gate) * _dot(x, wu_ref[j].astype(BF16))
            ye = _dot(hid.astype(BF16), wd_ref[j].astype(BF16))
            lane = COMB_LANE0 + e * EXPERTS_PER_STEP + j
            ce = jnp.sum(jnp.where(ln == lane, rt, 0.0), axis=-1, keepdims=True)
            y = y + ce * ye
        acc[...] = y

        @pl.when(e == last_e)
        def _():
            ys_ref[...] = acc[...]

    @pl.when((t >= nu_ref[0]) & (e == last_e))
    def _():
        ys_ref[...] = jnp.zeros_like(ys_ref)


def _experts(xs, tile_group, n_used, wg, wu, wd):
    n_rows = xs.shape[0]
    tile = lambda t, e, tg, nu: (jnp.minimum(t, nu[0] - 1), 0)
    steps = EXPERTS_PER_GROUP // EXPERTS_PER_STEP
    wsel = lambda t, e, tg, nu: (tg[jnp.minimum(t, nu[0] - 1)] * steps + e, 0, 0)
    return pl.pallas_call(
        _experts_kernel,
        grid_spec=pltpu.PrefetchScalarGridSpec(
            num_scalar_prefetch=2,
            grid=(n_rows // GROUP_TILE, steps),
            in_specs=[pl.BlockSpec((GROUP_TILE, MOE_ROW), tile),
                      pl.BlockSpec((EXPERTS_PER_STEP, D_MODEL, D_EXPERT), wsel),
                      pl.BlockSpec((EXPERTS_PER_STEP, D_MODEL, D_EXPERT), wsel),
                      pl.BlockSpec((EXPERTS_PER_STEP, D_EXPERT, D_MODEL), wsel)],
            out_specs=pl.BlockSpec((GROUP_TILE, D_MODEL), lambda t, e, tg, nu: (t, 0)),
            scratch_shapes=[pltpu.VMEM((GROUP_TILE, D_MODEL), F32),
                            pltpu.VMEM((GROUP_TILE, D_MODEL), BF16)]),
        out_shape=jax.ShapeDtypeStruct((n_rows, D_MODEL), F32),
        compiler_params=pltpu.CompilerParams(
            dimension_semantics=("arbitrary", "arbitrary"), vmem_limit_bytes=VMEM_LIMIT),
        name="moe_experts",
    )(tile_group, n_used, xs, wg, wu, wd)


def _tail_kernel(pos_ref, posn_ref, x1_ref, p_ref, ys_ref, nple_ref, wpg_ref, wpp_ref,
                 nfin_ref, y_ref, ybuf0, ybuf1, sem, *, tm):
    i = pl.program_id(0)
    last = pl.num_programs(0) - 1
    ybuf = (ybuf0, ybuf1)

    def issue(p_ref, p_off, slot):
        for r in range(tm):
            pltpu.make_async_copy(ys_ref.at[pl.ds(p_ref[p_off + r], 1)],
                                  ybuf[slot].at[pl.ds(r, 1)], sem.at[slot]).start(priority=r % 2)

    def wait(slot):
        pltpu.make_async_copy(ys_ref.at[pl.ds(0, tm)], ybuf[slot], sem.at[slot]).wait()

    def compute(slot):
        rows = pl.ds(slot * tm, tm)
        x2 = x1_ref[rows, :] + ybuf[slot][...]
        h3 = _rmsnorm(x2, nple_ref[...]).astype(BF16)
        ple = (_sigmoid(_dot(h3, wpg_ref[...]))
               * _dot(p_ref[rows, :].astype(BF16), wpp_ref[...]))
        y_ref[rows, :] = _rmsnorm(x2 + ple, nfin_ref[...])

    @pl.when(i == 0)
    def _():
        issue(pos_ref, 0, 0)

    wait(0)
    issue(pos_ref, tm, 1)
    compute(0)
    wait(1)
    issue(posn_ref, 0, 0)
    compute(1)

    @pl.when(i == last)
    def _():
        wait(0)


def _tail(pos, x1, p2d, ys, nple, wpg, wpp, nfin, *, tm):
    n = x1.shape[0]
    n_steps = n // (2 * tm)
    tok = lambda width: pl.BlockSpec((2 * tm, width), lambda i: (i, 0))
    full = lambda arr: pl.BlockSpec(arr.shape, lambda i: (0, 0))
    return pl.pallas_call(
        functools.partial(_tail_kernel, tm=tm),
        grid=(n_steps,),
        in_specs=[pl.BlockSpec((2 * tm,), lambda i: (i,), memory_space=pltpu.SMEM),
                  pl.BlockSpec((tm,), lambda i: (2 * jnp.minimum(i + 1, n_steps - 1),),
                               memory_space=pltpu.SMEM),
                  tok(D_MODEL), tok(PLE_DIM),
                  pl.BlockSpec(memory_space=pl.ANY),
                  full(nple), full(wpg), full(wpp), full(nfin)],
        out_specs=tok(D_MODEL),
        out_shape=jax.ShapeDtypeStruct((n, D_MODEL), F32),
        scratch_shapes=[pltpu.VMEM((tm, D_MODEL), F32), pltpu.VMEM((tm, D_MODEL), F32),
                        pltpu.SemaphoreType.DMA((2,))],
        compiler_params=pltpu.CompilerParams(
            dimension_semantics=("arbitrary",), vmem_limit_bytes=VMEM_LIMIT),
        name="moe_tail",
    )(pos, pos, x1, p2d, ys, nple, wpg, wpp, nfin)


def _moe_tail(groups, cnt, wg, wu, wd, nple, wpg, wpp, nfin, *, tm):
    n_all = sum(g[0].shape[0] for g in groups)
    n_tiles_max = (n_all + GROUP_TILE - 1) // GROUP_TILE + N_GROUPS
    tile_group, n_used, row0 = _route_tables(cnt, n_tiles_max)
    xs = jnp.zeros((n_tiles_max * GROUP_TILE, MOE_ROW), F32)
    poss = []
    for hx, rt, _, _, td in groups:
        poss.append(_sorted_rows(rt, row0))
        xs = _dispatch(hx, poss[-1], xs, td=td)
    ys = _experts(xs, tile_group, n_used, wg, wu, wd)
    return [_tail(pos, x1, p2d, ys, nple, wpg, wpp, nfin, tm=tm)
            for pos, (_, _, x1, p2d, _) in zip(poss, groups)]


def kernel(x_prompt, x_sample, state_pool, state_shift, state_wkv, p_prompt, p_sample, norm_mix, w_in, pool_mix, pool_scale, w_branch_a, shift_mu, decay_w0, decay_w2, iclr_a0, iclr_a2, gate_g2, k_k, k_a, r_k, ln_x_w, ln_x_b, w_branch_b, w_out, norm_ffn, w_route_group, b_route_group, w_route_expert, b_route_expert, expert_gate, expert_up, expert_down, norm_ple, w_ple_gate, w_ple_proj, norm_final):
    l = 0
    bsz, seq, _ = x_prompt.shape
    dbsz, dseq, _ = x_sample.shape
    row = lambda vec: vec.reshape(1, -1).astype(F32)
    col = lambda vec: vec.reshape(-1, 1).astype(F32)
    perm = CHANNEL_PERM

    o1 = POOL_WIDTH
    o2 = o1 + SHIFT_WIDTH
    zperm = np.concatenate([perm, RWKV_WIDTH + perm, 2 * RWKV_WIDTH + perm,
                            np.arange(3 * RWKV_WIDTH, SHIFT_WIDTH)])
    w_z = w_in[l][:, o1:o2][:, zperm].astype(BF16)
    zeros_lora = jnp.zeros((RWKV_WIDTH, LORA_PAIR // 2), F32)
    w2t_pad = jnp.concatenate([decay_w2[l].T[perm], zeros_lora], axis=1).astype(BF16)
    a2t_pad = jnp.concatenate([zeros_lora, iclr_a2[l].T[perm]], axis=1).astype(BF16)
    w_router = jnp.concatenate(
        [w_route_expert[l], w_route_group[l],
         jnp.zeros((D_MODEL, ROUTER_LANES - N_EXPERTS - N_GROUPS), F32)], axis=1)
    wr_hi = w_router.astype(BF16)
    wr_lo = (w_router - wr_hi.astype(F32)).astype(BF16)
    b_router = jnp.concatenate(
        [b_route_expert[l], b_route_group[l],
         jnp.zeros((ROUTER_LANES - N_EXPERTS - N_GROUPS,), F32)]).reshape(1, -1)
    prep_w = [row(norm_mix[l]), w_in[l][:, :o1].astype(BF16), w_z, w_in[l][:, o2:].astype(BF16),
              col(shift_mu[l][zperm]), col(decay_w0[l][perm]), w2t_pad, col(iclr_a0[l][perm]), a2t_pad,
              gate_g2[l].T[perm].astype(BF16), col(k_k[l][perm]), col(k_a[l][perm]),
              col(r_k[l].reshape(-1)[perm]), pool_mix[l].astype(BF16), row(pool_scale[l]),
              w_branch_a[l].astype(BF16)]
    post_w = [col(ln_x_w[l][perm]), col(ln_x_b[l][perm]), w_branch_b[l][perm].astype(BF16),
              w_out[l].astype(BF16), row(norm_ffn[l]), wr_hi, wr_lo, b_router]
    moe_w = [expert_gate[l], expert_up[l], expert_down[l],
             row(norm_ple[l]), w_ple_gate[l].astype(BF16), w_ple_proj[l].astype(BF16), row(norm_final)]

    tm_p = 256
    tiles_p = seq // tm_p
    x_p = x_prompt.reshape(bsz * seq, D_MODEL)
    outs = _mix_prep(x_p, jnp.zeros((bsz * SHIFT_WIDTH, LANES), F32),
                     jnp.zeros((bsz * 16, POOL_WIDTH), F32), prep_w,
                     nb=bsz, tiles=tiles_p, tm=tm_p, s=1, pos0=0, cm_index=lambda b, t: (b, 0, t))
    u_p, zl_p, ops, g, bon, pa, sgb = outs
    o_p, s_p = _wkv_scan_prompt(ops, tt=LANES)
    tm_post = 2 * tm_p
    n_p = bsz * seq
    n_s = dbsz * dseq
    x1_p, hx_p, rt_p, cnt_p = _post(
        o_p, bon, g, pa, sgb, x_p, jnp.zeros((SUBLANES, ROUTER_LANES), F32), post_w,
        tm=tm_post, sub=tm_p, cm_index=lambda i: (i // (seq // tm_post), 0, i % (seq // tm_post)))
    wkv_p = s_p.reshape(KEY_HALF, 2, KEY_HALF, 2, bsz, HEADS).transpose(4, 5, 3, 2, 1, 0)
    wkv_p = wkv_p.reshape(bsz, HEADS, HEAD_DIM, HEAD_DIM)

    n_s = dbsz * dseq
    x_s = x_sample.transpose(1, 0, 2).reshape(n_s, D_MODEL)
    p_s = p_sample[l].transpose(1, 0, 2).reshape(n_s, PLE_DIM)
    uc0_s = jnp.concatenate(
        [jnp.zeros((dbsz, POOL_WIDTH), F32),
         state_pool[l].transpose(1, 0, 2).reshape(POOL_STATE * dbsz, POOL_WIDTH)], axis=0)
    zc0_s = state_shift[l].T[zperm]
    outs = _mix_prep(x_s, zc0_s, uc0_s, prep_w, nb=1, tiles=dseq, tm=dbsz, s=dbsz, pos0=PAST_LEN,
                     cm_index=lambda b, t: (t, 0, 0))
    u_s, zl_s, ops, g, bon, pa, sgb = outs
    s0_s = state_wkv[l].reshape(dbsz, HEADS, 2, KEY_HALF, 2, KEY_HALF).transpose(1, 5, 4, 3, 2, 0)
    s0_s = s0_s.reshape(HEADS, HEAD_DIM, HEAD_DIM, dbsz)
    o_s, s_s = _wkv_scan_sample(ops, s0_s)
    x1_s, hx_s, rt_s, cnt_all = _post(o_s, bon, g, pa, sgb, x_s, cnt_p, post_w,
                                      tm=dbsz, sub=dbsz, cm_index=lambda i: (i, 0, 0))

    y_p, y_s = _moe_tail(
        [(hx_p, rt_p, x1_p, p_prompt[l].reshape(n_p, PLE_DIM), 512), (hx_s, rt_s, x1_s, p_s, n_s)],
        cnt_all, *moe_w, tm=256)
    wkv_s = s_s.reshape(HEADS, KEY_HALF, 2, KEY_HALF, 2, dbsz).transpose(5, 0, 4, 3, 2, 1)
    wkv_s = wkv_s.reshape(dbsz, HEADS, HEAD_DIM, HEAD_DIM)

    inv_z = np.argsort(zperm)
    y_prompt = y_p.reshape(bsz, seq, D_MODEL)
    y_sample = y_s.reshape(dseq, dbsz, D_MODEL).transpose(1, 0, 2)
    pool_prompt = u_p.reshape(bsz, seq, POOL_WIDTH)[:, seq - POOL_STATE:]
    u_s_bt = u_s.reshape(dseq, dbsz, POOL_WIDTH).transpose(1, 0, 2)
    pool_sample = jnp.concatenate([state_pool[l][:, dseq:], u_s_bt], axis=1)
    shift_prompt = zl_p[:, :, LANES - 1][:, inv_z]
    shift_sample = zl_s[0].T[:, inv_z]
    return (y_prompt, y_sample, pool_prompt[None], shift_prompt[None], wkv_p[None],
            pool_sample[None], shift_sample[None], wkv_s[None])
```

```python
import functools
import math

import jax
import jax.numpy as jnp
import numpy as np
from jax import lax
from jax.experimental import pallas as pl
from jax.experimental.pallas import tpu as pltpu

F32 = jnp.float32
BF16 = jnp.bfloat16

D_MODEL = 1024
PLE_DIM = 256
POOL_WIDTH = 512
POOL_WINDOWS = (2, 4, 8, 16)
POOL_GROUP_DIM = 128
POOL_STATE = 15
RWKV_WIDTH = 512
HEAD_DIM = 64
HEADS = 8
LORA_PAIR = 128
GATE_LORA = 128
SHIFT_WIDTH = 3 * RWKV_WIDTH + LORA_PAIR + GATE_LORA
IN_WIDTH = POOL_WIDTH + SHIFT_WIDTH + 2 * D_MODEL
N_GROUPS = 4
EXPERTS_PER_GROUP = 8
N_EXPERTS = 32
D_EXPERT = 256
RMS_EPS = 1e-6
GN_EPS = 64e-5
PAST_LEN = 16384

LANES = 128
SUBLANES = 8
KEY_HALF = HEAD_DIM // 2
SCAN_OPERANDS = 6
KEY_GROUP = 32
ROUTER_LANES = 128
GROUP_TILE = 1024
EXPERTS_PER_STEP = 2
DECAY_SCALE = math.exp(-0.5)
COMB_LANE0 = 8
MOE_ROW = D_MODEL + ROUTER_LANES
NEG_BIG = -1e30
VMEM_LIMIT = 56 * 1024 * 1024

_J = np.arange(RWKV_WIDTH)
CHANNEL_PERM = (_J % 8) * HEAD_DIM + ((_J // 8) % 2) * KEY_HALF + _J // 16


def _dot(a, b):
    return jnp.dot(a, b, preferred_element_type=F32)


def _split_bf16(x):
    hi = x.astype(BF16)
    return hi, (x - hi.astype(F32)).astype(BF16)


def _head_sum(x):
    x3 = x.reshape(HEAD_DIM, HEADS, x.shape[1])
    s = jnp.sum(x3, axis=0, keepdims=True)
    return jnp.broadcast_to(s, x3.shape).reshape(x.shape)


def _rmsnorm(x, g):
    return x * lax.rsqrt(jnp.mean(x * x, axis=-1, keepdims=True) + RMS_EPS) * g


def _sigmoid(x):
    return 0.5 * jnp.tanh(0.5 * x) + 0.5


def _mix_prep_kernel(x_ref, zc0_ref, uc0_ref, nmix_ref, wu_ref, wz_ref, wgab_ref, mu_ref, w0_ref,
                     w2t_ref, a0_ref, a2t_ref, g2t_ref, kk_ref, ka_ref, rk_ref, mix_ref,
                     pscale_ref, wa_ref,
                     u_ref, zl_ref, q_ref, g_ref, bon_ref,
                     pa_ref, sgb_ref, zc, uext, *, tm, s, pos0):
    t = pl.program_id(1)
    up = 16 * s

    @pl.when(t == 0)
    def _():
        zc[...] = zc0_ref[...]
        uext[0:up] = uc0_ref[...]

    x = x_ref[...]
    h = _rmsnorm(x, nmix_ref[...]).astype(BF16)
    u = _dot(h, wu_ref[...])
    gab = _dot(h, wgab_ref[...])
    u_ref[...] = u
    sgb_ref[...] = _sigmoid(gab[:, D_MODEL:])
    uext[up:up + tm] = u

    z_t = _dot(h, wz_ref[...]).T
    if s == 1:
        lane = lax.broadcasted_iota(jnp.int32, z_t.shape, 1)
        zprev = jnp.where(lane == 0, zc[:, LANES - 1:LANES], pltpu.roll(z_t, 1, axis=1))
    else:
        zprev = zc[...]
    zc[...] = z_t[:, tm - LANES:tm]
    zl_ref[0] = z_t[:, tm - LANES:tm]
    zm = z_t + (zprev - z_t) * mu_ref[...]
    r = zm[0:RWKV_WIDTH]
    k = zm[RWKV_WIDTH:2 * RWKV_WIDTH]
    v = zm[2 * RWKV_WIDTH:3 * RWKV_WIDTH]
    lora_in = zm[3 * RWKV_WIDTH:3 * RWKV_WIDTH + LORA_PAIR]
    gd = zm[3 * RWKV_WIDTH + LORA_PAIR:SHIFT_WIDTH]
    dw = _dot(w2t_ref[...], jnp.tanh(lora_in).astype(BF16))
    da = _dot(a2t_ref[...], lora_in.astype(BF16))
    decay = jnp.exp(-DECAY_SCALE * _sigmoid(w0_ref[...] + dw))
    a = _sigmoid(a0_ref[...] + da)
    kk = k * kk_ref[...]
    kk = kk / jnp.maximum(jnp.sqrt(_head_sum(kk * kk)), 1e-12)
    k2 = k * (1.0 + (a - 1.0) * ka_ref[...])
    q_ref[0, 0] = -kk
    q_ref[1, 0] = decay
    q_ref[2, 0] = kk * a
    q_ref[3, 0] = k2
    q_ref[4, 0] = r
    q_ref[5, 0] = v
    g_ref[0] = _dot(g2t_ref[...], _sigmoid(gd).astype(BF16))
    bon_ref[0] = _head_sum(r * k2 * rk_ref[...]) * v

    rows = lax.broadcasted_iota(jnp.int32, (tm, POOL_GROUP_DIM), 0)
    if s > 1:
        rows = rows // s
    pos = pos0 + t * (tm // s) + rows
    ys = []
    for gi, wnd in enumerate(POOL_WINDOWS):
        lanes = slice(gi * POOL_GROUP_DIM, (gi + 1) * POOL_GROUP_DIM)
        wsum = uext[:, lanes]
        span = 1
        while span < wnd:
            wsum = wsum + pltpu.roll(wsum, span * s, axis=0)
            span *= 2
        cur = uext[pl.ds(up, tm), lanes]
        cnt = jnp.minimum(pos + 1, wnd).astype(F32)
        pooled = wsum[up:up + tm] / cnt - cur
        ys.append(_dot(pooled.astype(BF16), mix_ref[gi]))
    y = jnp.concatenate(ys, axis=-1) * pscale_ref[...]
    pa_ref[...] = _sigmoid(gab[:, :D_MODEL]) * _dot(y.astype(BF16), wa_ref[...])

    uext[0:up] = uext[tm:tm + up]


def _mix_prep(x2d, zc0, uc0, wts, *, nb, tiles, tm, s, pos0, cm_index):
    n = x2d.shape[0]
    up = 16 * s
    row = lambda b, t: (b * tiles + t, 0)
    full = lambda arr: pl.BlockSpec(arr.shape, lambda b, t: (0,) * arr.ndim)
    in_specs = [
        pl.BlockSpec((tm, D_MODEL), row),
        pl.BlockSpec((SHIFT_WIDTH, LANES), lambda b, t: (b, 0)),
        pl.BlockSpec((up, POOL_WIDTH), lambda b, t: (b, 0)),
    ] + [full(w) for w in wts]
    tok = lambda width: pl.BlockSpec((tm, width), row)
    cm_shape = (nb, RWKV_WIDTH, tiles * tm) if s == 1 else (tiles, RWKV_WIDTH, tm)
    cm = pl.BlockSpec((1, RWKV_WIDTH, tm), cm_index)
    cm_stack = pl.BlockSpec((SCAN_OPERANDS, 1, RWKV_WIDTH, tm), lambda b, t: (0,) + cm_index(b, t))
    out_specs = [tok(POOL_WIDTH), pl.BlockSpec((1, SHIFT_WIDTH, LANES), lambda b, t: (b, 0, 0)),
                 cm_stack, cm, cm] + [tok(D_MODEL)] * 2
    out_shape = [jax.ShapeDtypeStruct((n, POOL_WIDTH), F32),
                 jax.ShapeDtypeStruct((nb, SHIFT_WIDTH, LANES), F32),
                 jax.ShapeDtypeStruct((SCAN_OPERANDS,) + cm_shape, F32)] \
        + [jax.ShapeDtypeStruct(cm_shape, F32)] * 2 \
        + [jax.ShapeDtypeStruct((n, D_MODEL), F32)] * 2
    return pl.pallas_call(
        functools.partial(_mix_prep_kernel, tm=tm, s=s, pos0=pos0),
        grid=(nb, tiles),
        in_specs=in_specs,
        out_specs=out_specs,
        out_shape=out_shape,
        scratch_shapes=[pltpu.VMEM((SHIFT_WIDTH, LANES), F32),
                        pltpu.VMEM((up + tm, POOL_WIDTH), F32)],
        compiler_params=pltpu.CompilerParams(
            dimension_semantics=("arbitrary", "arbitrary"), vmem_limit_bytes=VMEM_LIMIT),
        name="mix_prep",
    )(x2d, zc0, uc0, *wts)


def _wkv_step(s_ref, row, vv):
    groups = HEAD_DIM // KEY_GROUP

    def sa_pass(g, acc):
        for kk in range(KEY_GROUP):
            kp = g * KEY_GROUP + kk
            acc = acc + s_ref[kp] * row(0, kp)
        return acc
    sa = lax.fori_loop(0, groups, sa_pass, jnp.zeros(vv.shape, F32))

    def update_pass(g, acc):
        for kk in range(KEY_GROUP):
            kp = g * KEY_GROUP + kk
            sn = s_ref[kp] * row(1, kp) + sa * row(2, kp) + vv * row(3, kp)
            s_ref[kp] = sn
            acc = acc + sn * row(4, kp)
        return acc
    return lax.fori_loop(0, groups, update_pass, jnp.zeros(vv.shape, F32))


def _swap_major_sublane(x):
    return jnp.swapaxes(x, 0, 1)


def _wkv_scan_prompt_kernel(q_ref, o_ref, sout_ref, s_ref, kv_ref, vv_ref, ov_ref, *, tt):
    t = pl.program_id(0)
    q = pl.program_id(1)

    @pl.when((t == 0) & (q == 0))
    def _():
        s_ref[...] = jnp.zeros_like(s_ref)

    def gather_t(base_lo, base_hi):
        pieces = [q_ref[0, bb, pl.ds(base, SUBLANES), :] for base in (base_lo, base_hi) for bb in range(8)]
        return jnp.concatenate(pieces, axis=0).T

    @pl.when(q < SCAN_OPERANDS - 1)
    def _():
        def kgroup(gi, c):
            slabs = []
            for kk in range(SUBLANES):
                base = pl.multiple_of((gi * SUBLANES + kk) * SUBLANES, SUBLANES)
                slabs.append(gather_t(base, base))
            kv_ref[q, :, pl.ds(pl.multiple_of(gi * SUBLANES, SUBLANES), SUBLANES), :] = \
                _swap_major_sublane(jnp.stack(slabs))
            return c
        lax.fori_loop(0, HEAD_DIM // SUBLANES, kgroup, 0, unroll=True)

    @pl.when(q == SCAN_OPERANDS - 1)
    def _():
        def vgroup(gi, c):
            slabs = []
            for vi in range(SUBLANES):
                base = pl.multiple_of((gi * SUBLANES + vi) * 2 * SUBLANES, 2 * SUBLANES)
                slabs.append(gather_t(base, base + SUBLANES))
            vv_ref[:, pl.ds(pl.multiple_of(gi * SUBLANES, SUBLANES), SUBLANES), :] = \
                _swap_major_sublane(jnp.stack(slabs))
            return c
        lax.fori_loop(0, KEY_HALF // SUBLANES, vgroup, 0, unroll=True)

        def step(i, c):
            row = lambda qi, kp: kv_ref[qi, i, pl.ds(kp, 1), :]
            ov_ref[i] = _wkv_step(s_ref, row, vv_ref[i])
            return c
        lax.fori_loop(0, tt, step, 0)

        def ogroup(gi, c):
            g0 = pl.multiple_of(gi * SUBLANES, SUBLANES)
            x = _swap_major_sublane(ov_ref[:, pl.ds(g0, SUBLANES), :])
            for vi in range(SUBLANES):
                xt = x[vi].T
                base = pl.multiple_of((gi * SUBLANES + vi) * 2 * SUBLANES, 2 * SUBLANES)
                for vh in range(2):
                    for bb in range(8):
                        r0 = (vh * 8 + bb) * SUBLANES
                        o_ref[bb, pl.ds(base + vh * SUBLANES, SUBLANES), :] = xt[r0:r0 + SUBLANES]
            return c
        lax.fori_loop(0, KEY_HALF // SUBLANES, ogroup, 0, unroll=True)

        @pl.when(t == pl.num_programs(0) - 1)
        def _():
            sout_ref[...] = s_ref[...]


def _wkv_scan_prompt(ops, *, tt):
    _, nb, _, t_len = ops.shape
    sspec = pl.BlockSpec((HEAD_DIM, KEY_HALF, LANES), lambda ti, qi: (0, 0, 0))
    return pl.pallas_call(
        functools.partial(_wkv_scan_prompt_kernel, tt=tt),
        grid=(t_len // tt, SCAN_OPERANDS),
        in_specs=[pl.BlockSpec((1, nb, RWKV_WIDTH, tt), lambda ti, qi: (qi, 0, 0, ti))],
        out_specs=[pl.BlockSpec((nb, RWKV_WIDTH, tt), lambda ti, qi: (0, 0, ti)), sspec],
        out_shape=[jax.ShapeDtypeStruct((nb, RWKV_WIDTH, t_len), F32),
                   jax.ShapeDtypeStruct((HEAD_DIM, KEY_HALF, LANES), F32)],
        scratch_shapes=[pltpu.VMEM((HEAD_DIM, KEY_HALF, LANES), F32),
                        pltpu.VMEM((SCAN_OPERANDS - 1, tt, HEAD_DIM, LANES), F32),
                        pltpu.VMEM((tt, KEY_HALF, LANES), F32),
                        pltpu.VMEM((tt, KEY_HALF, LANES), F32)],
        compiler_params=pltpu.CompilerParams(
            dimension_semantics=("arbitrary", "arbitrary"), vmem_limit_bytes=VMEM_LIMIT),
        name="wkv_scan_prompt",
    )(ops)


def _wkv_scan_sample_kernel(q_ref, s0_ref, o_ref, sout_ref, s_ref, *, t_len):
    h = pl.program_id(0)
    s_ref[...] = s0_ref[0]
    for i in range(t_len):
        row = lambda qi, kp, i=i: q_ref[qi, i, pl.ds(kp * HEADS + h, 1), :]
        vv = q_ref[SCAN_OPERANDS - 1, i, pl.ds(h, HEAD_DIM, stride=HEADS), :]
        o_ref[i, pl.ds(h, HEAD_DIM, stride=HEADS), :] = _wkv_step(s_ref, row, vv)
    sout_ref[0] = s_ref[...]


def _wkv_scan_sample(ops, s0):
    t_len = ops.shape[1]
    spec = pl.BlockSpec(ops.shape[1:], lambda h: (0, 0, 0))
    sspec = pl.BlockSpec((1, HEAD_DIM, HEAD_DIM, LANES), lambda h: (h, 0, 0, 0))
    return pl.pallas_call(
        functools.partial(_wkv_scan_sample_kernel, t_len=t_len),
        grid=(HEADS,),
        in_specs=[pl.BlockSpec(ops.shape, lambda h: (0, 0, 0, 0)), sspec],
        out_specs=[spec, sspec],
        out_shape=[jax.ShapeDtypeStruct(ops.shape[1:], F32), jax.ShapeDtypeStruct(s0.shape, F32)],
        scratch_shapes=[pltpu.VMEM((HEAD_DIM, HEAD_DIM, LANES), F32)],
        compiler_params=pltpu.CompilerParams(
            dimension_semantics=("arbitrary",), vmem_limit_bytes=VMEM_LIMIT),
        name="wkv_scan_sample",
    )(ops, s0)


def _post_kernel(o_ref, bon_ref, g_ref, pa_ref, sgb_ref, x_ref, cnt0_ref, lnw_ref, lnb_ref,
                 wb_ref, wout_ref, nffn_ref, wrh_ref, wrl_ref, br_ref, x1_ref, h2_ref, rt_ref, cnt_ref,
                 carry, *, sub):
    @pl.when(pl.program_id(0) == 0)
    def _():
        carry[...] = cnt0_ref[...]

    prev = carry[0:1, :]
    for part in range(o_ref.shape[2] // sub):
        prev = _post_rows(pl.ds(part * sub, sub), prev, o_ref, bon_ref, g_ref, pa_ref, sgb_ref, x_ref,
                          lnw_ref, lnb_ref, wb_ref, wout_ref, nffn_ref, wrh_ref, wrl_ref,
                          br_ref, x1_ref, h2_ref, rt_ref)
    carry[...] = jnp.broadcast_to(prev, carry.shape)
    cnt_ref[...] = jnp.broadcast_to(prev, cnt_ref.shape)


def _post_rows(rows, prev_count, o_ref, bon_ref, g_ref, pa_ref, sgb_ref, x_ref, lnw_ref, lnb_ref,
               wb_ref, wout_ref, nffn_ref, wrh_ref, wrl_ref, br_ref, x1_ref, h2_ref, rt_ref):
    o = o_ref[0, :, rows]
    mean = _head_sum(o) * (1.0 / HEAD_DIM)
    d = o - mean
    var = _head_sum(d * d) * (1.0 / HEAD_DIM)
    on = d * lax.rsqrt(var + GN_EPS) * lnw_ref[...] + lnb_ref[...]
    yb = ((on + bon_ref[0, :, rows]) * g_ref[0, :, rows]).astype(BF16)
    mb = lax.dot_general(yb, wb_ref[...], (((0,), (0,)), ((), ())), preferred_element_type=F32)
    merged = pa_ref[rows, :] + sgb_ref[rows, :] * mb
    x1 = x_ref[rows, :] + _dot(merged.astype(BF16), wout_ref[...])
    x1_ref[rows, :] = x1
    h2 = _rmsnorm(x1, nffn_ref[...])
    h2_ref[rows, 0:D_MODEL] = h2

    h_hi, h_lo = _split_bf16(h2)
    logits = (_dot(h_hi, wrh_ref[...]) + _dot(h_lo, wrh_ref[...]) + _dot(h_hi, wrl_ref[...])
              + br_ref[...])
    ln = lax.broadcasted_iota(jnp.int32, logits.shape, 1)
    is_group = (ln >= N_EXPERTS) & (ln < N_EXPERTS + N_GROUPS)
    gl = jnp.where(is_group, logits, NEG_BIG)
    gmax = jnp.max(gl, axis=-1, keepdims=True)
    gsel = jnp.min(jnp.where(gl == gmax, ln, ROUTER_LANES), axis=-1, keepdims=True) - N_EXPERTS
    den = jnp.sum(jnp.where(is_group, jnp.exp(gl - gmax), 0.0), axis=-1, keepdims=True)
    pg = 1.0 / den
    in_group = (ln < N_EXPERTS) & ((ln // EXPERTS_PER_GROUP) == gsel)
    el = jnp.where(in_group, logits, NEG_BIG)
    m1 = jnp.max(el, axis=-1, keepdims=True)
    i1 = jnp.min(jnp.where(el == m1, ln, ROUTER_LANES), axis=-1, keepdims=True)
    el2 = jnp.where(ln == i1, NEG_BIG, el)
    m2 = jnp.max(el2, axis=-1, keepdims=True)
    i2 = jnp.min(jnp.where(el2 == m2, ln, ROUTER_LANES), axis=-1, keepdims=True)
    e2 = jnp.exp(m2 - m1)
    p1 = 1.0 / (1.0 + e2)
    p2 = e2 / (1.0 + e2)

    tm = logits.shape[0]
    sel = ln == gsel
    tri = (lax.broadcasted_iota(jnp.int32, (tm, tm), 1)
           < lax.broadcasted_iota(jnp.int32, (tm, tm), 0)).astype(BF16)
    before = prev_count + _dot(tri, sel.astype(BF16))
    rank = jnp.sum(jnp.where(sel, before, 0.0), axis=-1, keepdims=True)
    first = gsel * EXPERTS_PER_GROUP - COMB_LANE0
    rt = (jnp.where(ln == 0, gsel.astype(F32), 0.0) + jnp.where(ln == 1, rank, 0.0)
          + jnp.where(ln == i1 - first, p1 * pg, 0.0) + jnp.where(ln == i2 - first, p2 * pg, 0.0))
    rt_ref[rows, :] = rt
    h2_ref[rows, D_MODEL:MOE_ROW] = rt
    return prev_count + jnp.sum(sel.astype(F32), axis=0, keepdims=True)


def _post(o, bon, g, pa, sgb, x2d, cnt0, wts, *, tm, sub, cm_index):
    n = x2d.shape[0]
    row = lambda i: (i, 0)
    tok = lambda width: pl.BlockSpec((tm, width), row)
    cm = pl.BlockSpec((1, RWKV_WIDTH, tm), cm_index)
    full = lambda arr: pl.BlockSpec(arr.shape, lambda i: (0, 0))
    return pl.pallas_call(
        functools.partial(_post_kernel, sub=sub),
        grid=(n // tm,),
        in_specs=[cm] * 3 + [tok(D_MODEL)] * 3 + [full(cnt0)] + [full(w) for w in wts],
        out_specs=[tok(D_MODEL), tok(MOE_ROW), tok(ROUTER_LANES),
                   pl.BlockSpec((SUBLANES, ROUTER_LANES), lambda i: (0, 0))],
        out_shape=[jax.ShapeDtypeStruct((n, D_MODEL), F32),
                   jax.ShapeDtypeStruct((n, MOE_ROW), F32),
                   jax.ShapeDtypeStruct((n, ROUTER_LANES), F32),
                   jax.ShapeDtypeStruct((SUBLANES, ROUTER_LANES), F32)],
        scratch_shapes=[pltpu.VMEM((SUBLANES, ROUTER_LANES), F32)],
        compiler_params=pltpu.CompilerParams(
            dimension_semantics=("arbitrary",), vmem_limit_bytes=VMEM_LIMIT),
        name="post",
    )(o, bon, g, pa, sgb, x2d, cnt0, *wts)


def _route_tables(cnt, n_tiles_max):
    counts = cnt[0, :N_GROUPS].astype(jnp.int32)
    tiles_g = (counts + GROUP_TILE - 1) // GROUP_TILE
    tile_end = jnp.cumsum(tiles_g)
    row0 = (tile_end - tiles_g) * GROUP_TILE
    tile_ids = jnp.arange(n_tiles_max, dtype=jnp.int32)
    tile_group = jnp.minimum(jnp.sum(tile_ids[:, None] >= tile_end[None, :], axis=-1),
                             N_GROUPS - 1).astype(jnp.int32)
    return tile_group, tile_end[N_GROUPS - 1:].astype(jnp.int32), row0


def _sorted_rows(rt, row0):
    g = rt[:, 0].astype(jnp.int32)
    r = rt[:, 1].astype(jnp.int32)
    return r + jnp.sum(jnp.where(g[:, None] == jnp.arange(N_GROUPS), row0, 0), axis=-1)


def _dispatch_kernel(pos_ref, h_ref, xs0_ref, xs_ref, sem, *, td):
    del xs0_ref
    for r in range(td):
        pltpu.make_async_copy(h_ref.at[pl.ds(r, 1)], xs_ref.at[pl.ds(pos_ref[r], 1)],
                              sem).start(priority=r % 2)
    pltpu.make_async_copy(h_ref, xs_ref.at[pl.ds(0, td)], sem).wait()


def _dispatch(h, pos, xs_in, *, td):
    n = h.shape[0]
    return pl.pallas_call(
        functools.partial(_dispatch_kernel, td=td),
        grid=(n // td,),
        in_specs=[pl.BlockSpec((td,), lambda i: (i,), memory_space=pltpu.SMEM),
                  pl.BlockSpec((td, MOE_ROW), lambda i: (i, 0)),
                  pl.BlockSpec(memory_space=pl.ANY)],
        out_specs=pl.BlockSpec(memory_space=pl.ANY),
        out_shape=jax.ShapeDtypeStruct(xs_in.shape, F32),
        scratch_shapes=[pltpu.SemaphoreType.DMA(())],
        input_output_aliases={2: 0},
        compiler_params=pltpu.CompilerParams(
            dimension_semantics=("arbitrary",), vmem_limit_bytes=VMEM_LIMIT),
        name="moe_dispatch",
    )(pos, h, xs_in)


def _experts_kernel(tg_ref, nu_ref, xs_ref, wg_ref, wu_ref, wd_ref, ys_ref, acc, xb):
    t = pl.program_id(0)
    e = pl.program_id(1)
    last_e = pl.num_programs(1) - 1

    @pl.when(t < nu_ref[0])
    def _():
        @pl.when(e == 0)
        def _():
            acc[...] = jnp.zeros_like(acc)
            xb[...] = xs_ref[:, 0:D_MODEL].astype(BF16)

        x = xb[...]
        rt = xs_ref[:, D_MODEL:MOE_ROW]
        ln = lax.broadcasted_iota(jnp.int32, rt.shape, 1)
        y = acc[...]
        for j in range(EXPERTS_PER_STEP):
            gate = _dot(x, wg_ref[j].astype(BF16))
            hid = gate * _sigmoid(gate) * _dot(x, wu_ref[j].astype(BF16))
            ye = _dot(hid.astype(BF16), wd_ref[j].astype(BF16))
            lane = COMB_LANE0 + e * EXPERTS_PER_STEP + j
            ce = jnp.sum(jnp.where(ln == lane, rt, 0.0), axis=-1, keepdims=True)
            y = y + ce * ye
        acc[...] = y

        @pl.when(e == last_e)
        def _():
            ys_ref[...] = acc[...]

    @pl.when((t >= nu_ref[0]) & (e == last_e))
    def _():
        ys_ref[...] = jnp.zeros_like(ys_ref)


def _experts(xs, tile_group, n_used, wg, wu, wd):
    n_rows = xs.shape[0]
    tile = lambda t, e, tg, nu: (jnp.minimum(t, nu[0] - 1), 0)
    steps = EXPERTS_PER_GROUP // EXPERTS_PER_STEP
    wsel = lambda t, e, tg, nu: (tg[jnp.minimum(t, nu[0] - 1)] * steps + e, 0, 0)
    return pl.pallas_call(
        _experts_kernel,
        grid_spec=pltpu.PrefetchScalarGridSpec(
            num_scalar_prefetch=2,
            grid=(n_rows // GROUP_TILE, steps),
            in_specs=[pl.BlockSpec((GROUP_TILE, MOE_ROW), tile),
                      pl.BlockSpec((EXPERTS_PER_STEP, D_MODEL, D_EXPERT), wsel),
                      pl.BlockSpec((EXPERTS_PER_STEP, D_MODEL, D_EXPERT), wsel),
                      pl.BlockSpec((EXPERTS_PER_STEP, D_EXPERT, D_MODEL), wsel)],
            out_specs=pl.BlockSpec((GROUP_TILE, D_MODEL), lambda t, e, tg, nu: (t, 0)),
            scratch_shapes=[pltpu.VMEM((GROUP_TILE, D_MODEL), F32),
                            pltpu.VMEM((GROUP_TILE, D_MODEL), BF16)]),
        out_shape=jax.ShapeDtypeStruct((n_rows, D_MODEL), F32),
        compiler_params=pltpu.CompilerParams(
            dimension_semantics=("arbitrary", "arbitrary"), vmem_limit_bytes=VMEM_LIMIT),
        name="moe_experts",
    )(tile_group, n_used, xs, wg, wu, wd)


def _tail_kernel(pos_ref, posn_ref, x1_ref, p_ref, ys_ref, nple_ref, wpg_ref, wpp_ref,
                 nfin_ref, y_ref, ybuf0, ybuf1, sem, *, tm):
    i = pl.program_id(0)
    last = pl.num_programs(0) - 1
    ybuf = (ybuf0, ybuf1)

    def issue(p_ref, p_off, slot):
        for r in range(tm):
            pltpu.make_async_copy(ys_ref.at[pl.ds(p_ref[p_off + r], 1)],
                                  ybuf[slot].at[pl.ds(r, 1)], sem.at[slot]).start(priority=r % 2)

    def wait(slot):
        pltpu.make_async_copy(ys_ref.at[pl.ds(0, tm)], ybuf[slot], sem.at[slot]).wait()

    def compute(slot):
        rows = pl.ds(slot * tm, tm)
        x2 = x1_ref[rows, :] + ybuf[slot][...]
        h3 = _rmsnorm(x2, nple_ref[...]).astype(BF16)
        ple = (_sigmoid(_dot(h3, wpg_ref[...]))
               * _dot(p_ref[rows, :].astype(BF16), wpp_ref[...]))
        y_ref[rows, :] = _rmsnorm(x2 + ple, nfin_ref[...])

    @pl.when(i == 0)
    def _():
        issue(pos_ref, 0, 0)

    wait(0)
    issue(pos_ref, tm, 1)
    compute(0)
    wait(1)
    issue(posn_ref, 0, 0)
    compute(1)

    @pl.when(i == last)
    def _():
        wait(0)


def _tail(pos, x1, p2d, ys, nple, wpg, wpp, nfin, *, tm):
    n = x1.shape[0]
    n_steps = n // (2 * tm)
    tok = lambda width: pl.BlockSpec((2 * tm, width), lambda i: (i, 0))
    full = lambda arr: pl.BlockSpec(arr.shape, lambda i: (0, 0))
    return pl.pallas_call(
        functools.partial(_tail_kernel, tm=tm),
        grid=(n_steps,),
        in_specs=[pl.BlockSpec((2 * tm,), lambda i: (i,), memory_space=pltpu.SMEM),
                  pl.BlockSpec((tm,), lambda i: (2 * jnp.minimum(i + 1, n_steps - 1),),
                               memory_space=pltpu.SMEM),
                  tok(D_MODEL), tok(PLE_DIM),
                  pl.BlockSpec(memory_space=pl.ANY),
                  full(nple), full(wpg), full(wpp), full(nfin)],
        out_specs=tok(D_MODEL),
        out_shape=jax.ShapeDtypeStruct((n, D_MODEL), F32),
        scratch_shapes=[pltpu.VMEM((tm, D_MODEL), F32), pltpu.VMEM((tm, D_MODEL), F32),
                        pltpu.SemaphoreType.DMA((2,))],
        compiler_params=pltpu.CompilerParams(
            dimension_semantics=("arbitrary",), vmem_limit_bytes=VMEM_LIMIT),
        name="moe_tail",
    )(pos, pos, x1, p2d, ys, nple, wpg, wpp, nfin)


def _moe_tail(groups, cnt, wg, wu, wd, nple, wpg, wpp, nfin, *, tm):
    n_all = sum(g[0].shape[0] for g in groups)
    n_tiles_max = (n_all + GROUP_TILE - 1) // GROUP_TILE + N_GROUPS
    tile_group, n_used, row0 = _route_tables(cnt, n_tiles_max)
    xs = jnp.zeros((n_tiles_max * GROUP_TILE, MOE_ROW), F32)
    poss = []
    for hx, rt, _, _, td in groups:
        poss.append(_sorted_rows(rt, row0))
        xs = _dispatch(hx, poss[-1], xs, td=td)
    ys = _experts(xs, tile_group, n_used, wg, wu, wd)
    return [_tail(pos, x1, p2d, ys, nple, wpg, wpp, nfin, tm=tm)
            for pos, (_, _, x1, p2d, _) in zip(poss, groups)]


def kernel(x_prompt, x_sample, state_pool, state_shift, state_wkv, p_prompt, p_sample, norm_mix, w_in, pool_mix, pool_scale, w_branch_a, shift_mu, decay_w0, decay_w2, iclr_a0, iclr_a2, gate_g2, k_k, k_a, r_k, ln_x_w, ln_x_b, w_branch_b, w_out, norm_ffn, w_route_group, b_route_group, w_route_expert, b_route_expert, expert_gate, expert_up, expert_down, norm_ple, w_ple_gate, w_ple_proj, norm_final):
    l = 0
    bsz, seq, _ = x_prompt.shape
    dbsz, dseq, _ = x_sample.shape
    row = lambda vec: vec.reshape(1, -1).astype(F32)
    col = lambda vec: vec.reshape(-1, 1).astype(F32)
    perm = CHANNEL_PERM

    o1 = POOL_WIDTH
    o2 = o1 + SHIFT_WIDTH
    zperm = np.concatenate([perm, RWKV_WIDTH + perm, 2 * RWKV_WIDTH + perm,
                            np.arange(3 * RWKV_WIDTH, SHIFT_WIDTH)])
    w_z = w_in[l][:, o1:o2][:, zperm].astype(BF16)
    zeros_lora = jnp.zeros((RWKV_WIDTH, LORA_PAIR // 2), F32)
    w2t_pad = jnp.concatenate([decay_w2[l].T[perm], zeros_lora], axis=1).astype(BF16)
    a2t_pad = jnp.concatenate([zeros_lora, iclr_a2[l].T[perm]], axis=1).astype(BF16)
    w_router = jnp.concatenate(
        [w_route_expert[l], w_route_group[l],
         jnp.zeros((D_MODEL, ROUTER_LANES - N_EXPERTS - N_GROUPS), F32)], axis=1)
    wr_hi = w_router.astype(BF16)
    wr_lo = (w_router - wr_hi.astype(F32)).astype(BF16)
    b_router = jnp.concatenate(
        [b_route_expert[l], b_route_group[l],
         jnp.zeros((ROUTER_LANES - N_EXPERTS - N_GROUPS,), F32)]).reshape(1, -1)
    prep_w = [row(norm_mix[l]), w_in[l][:, :o1].astype(BF16), w_z, w_in[l][:, o2:].astype(BF16),
              col(shift_mu[l][zperm]), col(decay_w0[l][perm]), w2t_pad, col(iclr_a0[l][perm]), a2t_pad,
              gate_g2[l].T[perm].astype(BF16), col(k_k[l][perm]), col(k_a[l][perm]),
              col(r_k[l].reshape(-1)[perm]), pool_mix[l].astype(BF16), row(pool_scale[l]),
              w_branch_a[l].astype(BF16)]
    post_w = [col(ln_x_w[l][perm]), col(ln_x_b[l][perm]), w_branch_b[l][perm].astype(BF16),
              w_out[l].astype(BF16), row(norm_ffn[l]), wr_hi, wr_lo, b_router]
    moe_w = [expert_gate[l], expert_up[l], expert_down[l],
             row(norm_ple[l]), w_ple_gate[l].astype(BF16), w_ple_proj[l].astype(BF16), row(norm_final)]

    tm_p = 256
    tiles_p = seq // tm_p
    x_p = x_prompt.reshape(bsz * seq, D_MODEL)
    outs = _mix_prep(x_p, jnp.zeros((bsz * SHIFT_WIDTH, LANES), F32),
                     jnp.zeros((bsz * 16, POOL_WIDTH), F32), prep_w,
                     nb=bsz, tiles=tiles_p, tm=tm_p, s=1, pos0=0, cm_index=lambda b, t: (b, 0, t))
    u_p, zl_p, ops, g, bon, pa, sgb = outs
    o_p, s_p = _wkv_scan_prompt(ops, tt=LANES)
    tm_post = 512
    n_p = bsz * seq
    n_s = dbsz * dseq
    x1_p, hx_p, rt_p, cnt_p = _post(
        o_p, bon, g, pa, sgb, x_p, jnp.zeros((SUBLANES, ROUTER_LANES), F32), post_w,
        tm=tm_post, sub=256, cm_index=lambda i: (i // (seq // tm_post), 0, i % (seq // tm_post)))
    wkv_p = s_p.reshape(KEY_HALF, 2, KEY_HALF, 2, bsz, HEADS).transpose(4, 5, 3, 2, 1, 0)
    wkv_p = wkv_p.reshape(bsz, HEADS, HEAD_DIM, HEAD_DIM)

    n_s = dbsz * dseq
    x_s = x_sample.transpose(1, 0, 2).reshape(n_s, D_MODEL)
    p_s = p_sample[l].transpose(1, 0, 2).reshape(n_s, PLE_DIM)
    uc0_s = jnp.concatenate(
        [jnp.zeros((dbsz, POOL_WIDTH), F32),
         state_pool[l].transpose(1, 0, 2).reshape(POOL_STATE * dbsz, POOL_WIDTH)], axis=0)
    zc0_s = state_shift[l].T[zperm]
    outs = _mix_prep(x_s, zc0_s, uc0_s, prep_w, nb=1, tiles=dseq, tm=dbsz, s=dbsz, pos0=PAST_LEN,
                     cm_index=lambda b, t: (t, 0, 0))
    u_s, zl_s, ops, g, bon, pa, sgb = outs
    s0_s = state_wkv[l].reshape(dbsz, HEADS, 2, KEY_HALF, 2, KEY_HALF).transpose(1, 5, 4, 3, 2, 0)
    s0_s = s0_s.reshape(HEADS, HEAD_DIM, HEAD_DIM, dbsz)
    o_s, s_s = _wkv_scan_sample(ops, s0_s)
    x1_s, hx_s, rt_s, cnt_all = _post(o_s, bon, g, pa, sgb, x_s, cnt_p, post_w,
                                      tm=dbsz, sub=dbsz, cm_index=lambda i: (i, 0, 0))

    y_p, y_s = _moe_tail(
        [(hx_p, rt_p, x1_p, p_prompt[l].reshape(n_p, PLE_DIM), 512), (hx_s, rt_s, x1_s, p_s, n_s)],
        cnt_all, *moe_w, tm=256)
    wkv_s = s_s.reshape(HEADS, KEY_HALF, 2, KEY_HALF, 2, dbsz).transpose(5, 0, 4, 3, 2, 1)
    wkv_s = wkv_s.reshape(dbsz, HEADS, HEAD_DIM, HEAD_DIM)

    inv_z = np.argsort(zperm)
    y_prompt = y_p.reshape(bsz, seq, D_MODEL)
    y_sample = y_s.reshape(dseq, dbsz, D_MODEL).transpose(1, 0, 2)
    pool_prompt = u_p.reshape(bsz, seq, POOL_WIDTH)[:, seq - POOL_STATE:]
    u_s_bt = u_s.reshape(dseq, dbsz, POOL_WIDTH).transpose(1, 0, 2)
    pool_sample = jnp.concatenate([state_pool[l][:, dseq:], u_s_bt], axis=1)
    shift_prompt = zl_p[:, :, LANES - 1][:, inv_z]
    shift_sample = zl_s[0].T[:, inv_z]
    return (y_prompt, y_sample, pool_prompt[None], shift_prompt[None], wkv_p[None],
            pool_sample[None], shift_sample[None], wkv_s[None])
```

```python
import functools
import math

import jax
import jax.numpy as jnp
import numpy as np
from jax import lax
from jax.experimental import pallas as pl
from jax.experimental.pallas import tpu as pltpu

F32 = jnp.float32
BF16 = jnp.bfloat16

D_MODEL = 1024
PLE_DIM = 256
POOL_WIDTH = 512
POOL_WINDOWS = (2, 4, 8, 16)
POOL_GROUP_DIM = 128
POOL_STATE = 15
RWKV_WIDTH = 512
HEAD_DIM = 64
HEADS = 8
LORA_PAIR = 128
GATE_LORA = 128
SHIFT_WIDTH = 3 * RWKV_WIDTH + LORA_PAIR + GATE_LORA
IN_WIDTH = POOL_WIDTH + SHIFT_WIDTH + 2 * D_MODEL
N_GROUPS = 4
EXPERTS_PER_GROUP = 8
N_EXPERTS = 32
D_EXPERT = 256
RMS_EPS = 1e-6
GN_EPS = 64e-5
PAST_LEN = 16384

LANES = 128
SUBLANES = 8
KEY_HALF = HEAD_DIM // 2
SCAN_OPERANDS = 6
KEY_GROUP = 32
ROUTER_LANES = 128
GROUP_TILE = 1024
EXPERTS_PER_STEP = 2
DECAY_SCALE = math.exp(-0.5)
COMB_LANE0 = 8
MOE_ROW = D_MODEL + ROUTER_LANES
NEG_BIG = -1e30
VMEM_LIMIT = 56 * 1024 * 1024

_J = np.arange(RWKV_WIDTH)
CHANNEL_PERM = (_J % 8) * HEAD_DIM + ((_J // 8) % 2) * KEY_HALF + _J // 16


def _dot(a, b):
    return jnp.dot(a, b, preferred_element_type=F32)


def _split_bf16(x):
    hi = x.astype(BF16)
    return hi, (x - hi.astype(F32)).astype(BF16)


def _head_sum(x):
    x3 = x.reshape(HEAD_DIM, HEADS, x.shape[1])
    s = jnp.sum(x3, axis=0, keepdims=True)
    return jnp.broadcast_to(s, x3.shape).reshape(x.shape)


def _rmsnorm(x, g):
    return x * lax.rsqrt(jnp.mean(x * x, axis=-1, keepdims=True) + RMS_EPS) * g


def _sigmoid(x):
    return 0.5 * jnp.tanh(0.5 * x) + 0.5


def _mix_prep_kernel(x_ref, zc0_ref, uc0_ref, nmix_ref, wu_ref, wz_ref, wgab_ref, mu_ref, w0_ref,
                     w2t_ref, a0_ref, a2t_ref, g2t_ref, kk_ref, ka_ref, rk_ref, mix_ref,
                     pscale_ref, wa_ref,
                     u_ref, zl_ref, q_ref, g_ref, bon_ref,
                     pa_ref, sgb_ref, zc, uext, *, tm, s, pos0):
    t = pl.program_id(1)
    up = 16 * s

    @pl.when(t == 0)
    def _():
        zc[...] = zc0_ref[...]
        uext[0:up] = uc0_ref[...]

    x = x_ref[...]
    h = _rmsnorm(x, nmix_ref[...]).astype(BF16)
    u = _dot(h, wu_ref[...])
    gab = _dot(h, wgab_ref[...])
    u_ref[...] = u
    sgb_ref[...] = _sigmoid(gab[:, D_MODEL:])
    uext[up:up + tm] = u

    z_t = _dot(h, wz_ref[...]).T
    if s == 1:
        lane = lax.broadcasted_iota(jnp.int32, z_t.shape, 1)
        zprev = jnp.where(lane == 0, zc[:, LANES - 1:LANES], pltpu.roll(z_t, 1, axis=1))
    else:
        zprev = zc[...]
    zc[...] = z_t[:, tm - LANES:tm]
    zl_ref[0] = z_t[:, tm - LANES:tm]
    zm = z_t + (zprev - z_t) * mu_ref[...]
    r = zm[0:RWKV_WIDTH]
    k = zm[RWKV_WIDTH:2 * RWKV_WIDTH]
    v = zm[2 * RWKV_WIDTH:3 * RWKV_WIDTH]
    lora_in = zm[3 * RWKV_WIDTH:3 * RWKV_WIDTH + LORA_PAIR]
    gd = zm[3 * RWKV_WIDTH + LORA_PAIR:SHIFT_WIDTH]
    dw = _dot(w2t_ref[...], jnp.tanh(lora_in).astype(BF16))
    da = _dot(a2t_ref[...], lora_in.astype(BF16))
    decay = jnp.exp(-DECAY_SCALE * _sigmoid(w0_ref[...] + dw))
    a = _sigmoid(a0_ref[...] + da)
    kk = k * kk_ref[...]
    kk = kk / jnp.maximum(jnp.sqrt(_head_sum(kk * kk)), 1e-12)
    k2 = k * (1.0 + (a - 1.0) * ka_ref[...])
    q_ref[0, 0] = -kk
    q_ref[1, 0] = decay
    q_ref[2, 0] = kk * a
    q_ref[3, 0] = k2
    q_ref[4, 0] = r
    q_ref[5, 0] = v
    g_ref[0] = _dot(g2t_ref[...], _sigmoid(gd).astype(BF16))
    bon_ref[0] = _head_sum(r * k2 * rk_ref[...]) * v

    rows = lax.broadcasted_iota(jnp.int32, (tm, POOL_GROUP_DIM), 0)
    if s > 1:
        rows = rows // s
    pos = pos0 + t * (tm // s) + rows
    ys = []
    for gi, wnd in enumerate(POOL_WINDOWS):
        lanes = slice(gi * POOL_GROUP_DIM, (gi + 1) * POOL_GROUP_DIM)
        wsum = uext[:, lanes]
        span = 1
        while span < wnd:
            wsum = wsum + pltpu.roll(wsum, span * s, axis=0)
            span *= 2
        cur = uext[pl.ds(up, tm), lanes]
        cnt = jnp.minimum(pos + 1, wnd).astype(F32)
        pooled = wsum[up:up + tm] / cnt - cur
        ys.append(_dot(pooled.astype(BF16), mix_ref[gi]))
    y = jnp.concatenate(ys, axis=-1) * pscale_ref[...]
    pa_ref[...] = _sigmoid(gab[:, :D_MODEL]) * _dot(y.astype(BF16), wa_ref[...])

    uext[0:up] = uext[tm:tm + up]


def _mix_prep(x2d, zc0, uc0, wts, *, nb, tiles, tm, s, pos0, cm_index):
    n = x2d.shape[0]
    up = 16 * s
    row = lambda b, t: (b * tiles + t, 0)
    full = lambda arr: pl.BlockSpec(arr.shape, lambda b, t: (0,) * arr.ndim)
    in_specs = [
        pl.BlockSpec((tm, D_MODEL), row),
        pl.BlockSpec((SHIFT_WIDTH, LANES), lambda b, t: (b, 0)),
        pl.BlockSpec((up, POOL_WIDTH), lambda b, t: (b, 0)),
    ] + [full(w) for w in wts]
    tok = lambda width: pl.BlockSpec((tm, width), row)
    cm_shape = (nb, RWKV_WIDTH, tiles * tm) if s == 1 else (tiles, RWKV_WIDTH, tm)
    cm = pl.BlockSpec((1, RWKV_WIDTH, tm), cm_index)
    cm_stack = pl.BlockSpec((SCAN_OPERANDS, 1, RWKV_WIDTH, tm), lambda b, t: (0,) + cm_index(b, t))
    out_specs = [tok(POOL_WIDTH), pl.BlockSpec((1, SHIFT_WIDTH, LANES), lambda b, t: (b, 0, 0)),
                 cm_stack, cm, cm] + [tok(D_MODEL)] * 2
    out_shape = [jax.ShapeDtypeStruct((n, POOL_WIDTH), F32),
                 jax.ShapeDtypeStruct((nb, SHIFT_WIDTH, LANES), F32),
                 jax.ShapeDtypeStruct((SCAN_OPERANDS,) + cm_shape, F32)] \
        + [jax.ShapeDtypeStruct(cm_shape, F32)] * 2 \
        + [jax.ShapeDtypeStruct((n, D_MODEL), F32)] * 2
    return pl.pallas_call(
        functools.partial(_mix_prep_kernel, tm=tm, s=s, pos0=pos0),
        grid=(nb, tiles),
        in_specs=in_specs,
        out_specs=out_specs,
        out_shape=out_shape,
        scratch_shapes=[pltpu.VMEM((SHIFT_WIDTH, LANES), F32),
                        pltpu.VMEM((up + tm, POOL_WIDTH), F32)],
        compiler_params=pltpu.CompilerParams(
            dimension_semantics=("arbitrary", "arbitrary"), vmem_limit_bytes=VMEM_LIMIT),
        name="mix_prep",
    )(x2d, zc0, uc0, *wts)


def _wkv_step(s_ref, row, vv):
    groups = HEAD_DIM // KEY_GROUP

    def sa_pass(g, acc):
        for kk in range(KEY_GROUP):
            kp = g * KEY_GROUP + kk
            acc = acc + s_ref[kp] * row(0, kp)
        return acc
    sa = lax.fori_loop(0, groups, sa_pass, jnp.zeros(vv.shape, F32))

    def update_pass(g, acc):
        for kk in range(KEY_GROUP):
            kp = g * KEY_GROUP + kk
            sn = s_ref[kp] * row(1, kp) + sa * row(2, kp) + vv * row(3, kp)
            s_ref[kp] = sn
            acc = acc + sn * row(4, kp)
        return acc
    return lax.fori_loop(0, groups, update_pass, jnp.zeros(vv.shape, F32))


def _swap_major_sublane(x):
    return jnp.swapaxes(x, 0, 1)


def _wkv_scan_prompt_kernel(q_ref, o_ref, sout_ref, s_ref, kv_ref, vv_ref, ov_ref, *, tt):
    t = pl.program_id(0)
    q = pl.program_id(1)

    @pl.when((t == 0) & (q == 0))
    def _():
        s_ref[...] = jnp.zeros_like(s_ref)

    def gather_t(base_lo, base_hi):
        pieces = [q_ref[0, bb, pl.ds(base, SUBLANES), :] for base in (base_lo, base_hi) for bb in range(8)]
        return jnp.concatenate(pieces, axis=0).T

    @pl.when(q < SCAN_OPERANDS - 1)
    def _():
        def kgroup(gi, c):
            slabs = []
            for kk in range(SUBLANES):
                base = pl.multiple_of((gi * SUBLANES + kk) * SUBLANES, SUBLANES)
                slabs.append(gather_t(base, base))
            kv_ref[q, :, pl.ds(pl.multiple_of(gi * SUBLANES, SUBLANES), SUBLANES), :] = \
                _swap_major_sublane(jnp.stack(slabs))
            return c
        lax.fori_loop(0, HEAD_DIM // SUBLANES, kgroup, 0, unroll=True)

    @pl.when(q == SCAN_OPERANDS - 1)
    def _():
        def vgroup(gi, c):
            slabs = []
            for vi in range(SUBLANES):
                base = pl.multiple_of((gi * SUBLANES + vi) * 2 * SUBLANES, 2 * SUBLANES)
                slabs.append(gather_t(base, base + SUBLANES))
            vv_ref[:, pl.ds(pl.multiple_of(gi * SUBLANES, SUBLANES), SUBLANES), :] = \
                _swap_major_sublane(jnp.stack(slabs))
            return c
        lax.fori_loop(0, KEY_HALF // SUBLANES, vgroup, 0, unroll=True)

        def step(i, c):
            row = lambda qi, kp: kv_ref[qi, i, pl.ds(kp, 1), :]
            ov_ref[i] = _wkv_step(s_ref, row, vv_ref[i])
            return c
        lax.fori_loop(0, tt, step, 0)

        def ogroup(gi, c):
            g0 = pl.multiple_of(gi * SUBLANES, SUBLANES)
            x = _swap_major_sublane(ov_ref[:, pl.ds(g0, SUBLANES), :])
            for vi in range(SUBLANES):
                xt = x[vi].T
                base = pl.multiple_of((gi * SUBLANES + vi) * 2 * SUBLANES, 2 * SUBLANES)
                for vh in range(2):
                    for bb in range(8):
                        r0 = (vh * 8 + bb) * SUBLANES
                        o_ref[bb, pl.ds(base + vh * SUBLANES, SUBLANES), :] = xt[r0:r0 + SUBLANES]
            return c
        lax.fori_loop(0, KEY_HALF // SUBLANES, ogroup, 0, unroll=True)

        @pl.when(t == pl.num_programs(0) - 1)
        def _():
            sout_ref[...] = s_ref[...]


def _wkv_scan_prompt(ops, *, tt):
    _, nb, _, t_len = ops.shape
    sspec = pl.BlockSpec((HEAD_DIM, KEY_HALF, LANES), lambda ti, qi: (0, 0, 0))
    return pl.pallas_call(
        functools.partial(_wkv_scan_prompt_kernel, tt=tt),
        grid=(t_len // tt, SCAN_OPERANDS),
        in_specs=[pl.BlockSpec((1, nb, RWKV_WIDTH, tt), lambda ti, qi: (qi, 0, 0, ti))],
        out_specs=[pl.BlockSpec((nb, RWKV_WIDTH, tt), lambda ti, qi: (0, 0, ti)), sspec],
        out_shape=[jax.ShapeDtypeStruct((nb, RWKV_WIDTH, t_len), F32),
                   jax.ShapeDtypeStruct((HEAD_DIM, KEY_HALF, LANES), F32)],
        scratch_shapes=[pltpu.VMEM((HEAD_DIM, KEY_HALF, LANES), F32),
                        pltpu.VMEM((SCAN_OPERANDS - 1, tt, HEAD_DIM, LANES), F32),
                        pltpu.VMEM((tt, KEY_HALF, LANES), F32),
                        pltpu.VMEM((tt, KEY_HALF, LANES), F32)],
        compiler_params=pltpu.CompilerParams(
            dimension_semantics=("arbitrary", "arbitrary"), vmem_limit_bytes=VMEM_LIMIT),
        name="wkv_scan_prompt",
    )(ops)


def _wkv_scan_sample_kernel(q_ref, s0_ref, o_ref, sout_ref, s_ref, *, t_len):
    h = pl.program_id(0)
    s_ref[...] = s0_ref[0]
    for i in range(t_len):
        row = lambda qi, kp, i=i: q_ref[qi, i, pl.ds(kp * HEADS + h, 1), :]
        vv = q_ref[SCAN_OPERANDS - 1, i, pl.ds(h, HEAD_DIM, stride=HEADS), :]
        o_ref[i, pl.ds(h, HEAD_DIM, stride=HEADS), :] = _wkv_step(s_ref, row, vv)
    sout_ref[0] = s_ref[...]


def _wkv_scan_sample(ops, s0):
    t_len = ops.shape[1]
    spec = pl.BlockSpec(ops.shape[1:], lambda h: (0, 0, 0))
    sspec = pl.BlockSpec((1, HEAD_DIM, HEAD_DIM, LANES), lambda h: (h, 0, 0, 0))
    return pl.pallas_call(
        functools.partial(_wkv_scan_sample_kernel, t_len=t_len),
        grid=(HEADS,),
        in_specs=[pl.BlockSpec(ops.shape, lambda h: (0, 0, 0, 0)), sspec],
        out_specs=[spec, sspec],
        out_shape=[jax.ShapeDtypeStruct(ops.shape[1:], F32), jax.ShapeDtypeStruct(s0.shape, F32)],
        scratch_shapes=[pltpu.VMEM((HEAD_DIM, HEAD_DIM, LANES), F32)],
        compiler_params=pltpu.CompilerParams(
            dimension_semantics=("arbitrary",), vmem_limit_bytes=VMEM_LIMIT),
        name="wkv_scan_sample",
    )(ops, s0)


def _post_kernel(o_ref, bon_ref, g_ref, pa_ref, sgb_ref, x_ref, cnt0_ref, lnw_ref, lnb_ref,
                 wb_ref, wout_ref, nffn_ref, wrh_ref, wrl_ref, br_ref, x1_ref, h2_ref, rt_ref, cnt_ref,
                 carry, *, sub):
    @pl.when(pl.program_id(0) == 0)
    def _():
        carry[...] = cnt0_ref[...]

    parts = [pl.ds(part * sub, sub) for part in range(o_ref.shape[2] // sub)]
    logits = [_post_project(rows, o_ref, bon_ref, g_ref, pa_ref, sgb_ref, x_ref, lnw_ref, lnb_ref,
                            wb_ref, wout_ref, nffn_ref, wrh_ref, wrl_ref, br_ref, x1_ref, h2_ref)
              for rows in parts]
    prev = carry[0:1, :]
    for rows, lg in zip(parts, logits):
        prev = _post_route(rows, prev, lg, h2_ref, rt_ref)
    carry[...] = jnp.broadcast_to(prev, carry.shape)
    cnt_ref[...] = jnp.broadcast_to(prev, cnt_ref.shape)


def _post_project(rows, o_ref, bon_ref, g_ref, pa_ref, sgb_ref, x_ref, lnw_ref, lnb_ref,
                  wb_ref, wout_ref, nffn_ref, wrh_ref, wrl_ref, br_ref, x1_ref, h2_ref):
    o = o_ref[0, :, rows]
    mean = _head_sum(o) * (1.0 / HEAD_DIM)
    d = o - mean
    var = _head_sum(d * d) * (1.0 / HEAD_DIM)
    on = d * lax.rsqrt(var + GN_EPS) * lnw_ref[...] + lnb_ref[...]
    yb = ((on + bon_ref[0, :, rows]) * g_ref[0, :, rows]).astype(BF16)
    mb = lax.dot_general(yb, wb_ref[...], (((0,), (0,)), ((), ())), preferred_element_type=F32)
    merged = pa_ref[rows, :] + sgb_ref[rows, :] * mb
    x1 = x_ref[rows, :] + _dot(merged.astype(BF16), wout_ref[...])
    x1_ref[rows, :] = x1
    h2 = _rmsnorm(x1, nffn_ref[...])
    h2_ref[rows, 0:D_MODEL] = h2

    h_hi, h_lo = _split_bf16(h2)
    return (_dot(h_hi, wrh_ref[...]) + _dot(h_lo, wrh_ref[...]) + _dot(h_hi, wrl_ref[...])
            + br_ref[...])


def _post_route(rows, prev_count, logits, h2_ref, rt_ref):
    ln = lax.broadcasted_iota(jnp.int32, logits.shape, 1)
    is_group = (ln >= N_EXPERTS) & (ln < N_EXPERTS + N_GROUPS)
    gl = jnp.where(is_group, logits, NEG_BIG)
    gmax = jnp.max(gl, axis=-1, keepdims=True)
    gsel = jnp.min(jnp.where(gl == gmax, ln, ROUTER_LANES), axis=-1, keepdims=True) - N_EXPERTS
    den = jnp.sum(jnp.where(is_group, jnp.exp(gl - gmax), 0.0), axis=-1, keepdims=True)
    pg = 1.0 / den
    in_group = (ln < N_EXPERTS) & ((ln // EXPERTS_PER_GROUP) == gsel)
    el = jnp.where(in_group, logits, NEG_BIG)
    m1 = jnp.max(el, axis=-1, keepdims=True)
    i1 = jnp.min(jnp.where(el == m1, ln, ROUTER_LANES), axis=-1, keepdims=True)
    el2 = jnp.where(ln == i1, NEG_BIG, el)
    m2 = jnp.max(el2, axis=-1, keepdims=True)
    i2 = jnp.min(jnp.where(el2 == m2, ln, ROUTER_LANES), axis=-1, keepdims=True)
    e2 = jnp.exp(m2 - m1)
    p1 = 1.0 / (1.0 + e2)
    p2 = e2 / (1.0 + e2)

    tm = logits.shape[0]
    sel = ln == gsel
    tri = (lax.broadcasted_iota(jnp.int32, (tm, tm), 1)
           < lax.broadcasted_iota(jnp.int32, (tm, tm), 0)).astype(BF16)
    before = prev_count + _dot(tri, sel.astype(BF16))
    rank = jnp.sum(jnp.where(sel, before, 0.0), axis=-1, keepdims=True)
    first = gsel * EXPERTS_PER_GROUP - COMB_LANE0
    rt = (jnp.where(ln == 0, gsel.astype(F32), 0.0) + jnp.where(ln == 1, rank, 0.0)
          + jnp.where(ln == i1 - first, p1 * pg, 0.0) + jnp.where(ln == i2 - first, p2 * pg, 0.0))
    rt_ref[rows, :] = rt
    h2_ref[rows, D_MODEL:MOE_ROW] = rt
    return prev_count + jnp.sum(sel.astype(F32), axis=0, keepdims=True)


def _post(o, bon, g, pa, sgb, x2d, cnt0, wts, *, tm, sub, cm_index):
    n = x2d.shape[0]
    row = lambda i: (i, 0)
    tok = lambda width: pl.BlockSpec((tm, width), row)
    cm = pl.BlockSpec((1, RWKV_WIDTH, tm), cm_index)
    full = lambda arr: pl.BlockSpec(arr.shape, lambda i: (0, 0))
    return pl.pallas_call(
        functools.partial(_post_kernel, sub=sub),
        grid=(n // tm,),
        in_specs=[cm] * 3 + [tok(D_MODEL)] * 3 + [full(cnt0)] + [full(w) for w in wts],
        out_specs=[tok(D_MODEL), tok(MOE_ROW), tok(ROUTER_LANES),
                   pl.BlockSpec((SUBLANES, ROUTER_LANES), lambda i: (0, 0))],
        out_shape=[jax.ShapeDtypeStruct((n, D_MODEL), F32),
                   jax.ShapeDtypeStruct((n, MOE_ROW), F32),
                   jax.ShapeDtypeStruct((n, ROUTER_LANES), F32),
                   jax.ShapeDtypeStruct((SUBLANES, ROUTER_LANES), F32)],
        scratch_shapes=[pltpu.VMEM((SUBLANES, ROUTER_LANES), F32)],
        compiler_params=pltpu.CompilerParams(
            dimension_semantics=("arbitrary",), vmem_limit_bytes=VMEM_LIMIT),
        name="post",
    )(o, bon, g, pa, sgb, x2d, cnt0, *wts)


def _route_tables(cnt, n_tiles_max):
    counts = cnt[0, :N_GROUPS].astype(jnp.int32)
    tiles_g = (counts + GROUP_TILE - 1) // GROUP_TILE
    tile_end = jnp.cumsum(tiles_g)
    row0 = (tile_end - tiles_g) * GROUP_TILE
    tile_ids = jnp.arange(n_tiles_max, dtype=jnp.int32)
    tile_group = jnp.minimum(jnp.sum(tile_ids[:, None] >= tile_end[None, :], axis=-1),
                             N_GROUPS - 1).astype(jnp.int32)
    return tile_group, tile_end[N_GROUPS - 1:].astype(jnp.int32), row0


def _sorted_rows(rt, row0):
    g = rt[:, 0].astype(jnp.int32)
    r = rt[:, 1].astype(jnp.int32)
    return r + jnp.sum(jnp.where(g[:, None] == jnp.arange(N_GROUPS), row0, 0), axis=-1)


def _dispatch_kernel(pos_ref, h_ref, xs0_ref, xs_ref, sem, *, td):
    del xs0_ref
    for r in range(td):
        pltpu.make_async_copy(h_ref.at[pl.ds(r, 1)], xs_ref.at[pl.ds(pos_ref[r], 1)],
                              sem).start(priority=r % 2)
    pltpu.make_async_copy(h_ref, xs_ref.at[pl.ds(0, td)], sem).wait()


def _dispatch(h, pos, xs_in, *, td):
    n = h.shape[0]
    return pl.pallas_call(
        functools.partial(_dispatch_kernel, td=td),
        grid=(n // td,),
        in_specs=[pl.BlockSpec((td,), lambda i: (i,), memory_space=pltpu.SMEM),
                  pl.BlockSpec((td, MOE_ROW), lambda i: (i, 0)),
                  pl.BlockSpec(memory_space=pl.ANY)],
        out_specs=pl.BlockSpec(memory_space=pl.ANY),
        out_shape=jax.ShapeDtypeStruct(xs_in.shape, F32),
        scratch_shapes=[pltpu.SemaphoreType.DMA(())],
        input_output_aliases={2: 0},
        compiler_params=pltpu.CompilerParams(
            dimension_semantics=("arbitrary",), vmem_limit_bytes=VMEM_LIMIT),
        name="moe_dispatch",
    )(pos, h, xs_in)


def _experts_kernel(tg_ref, nu_ref, xs_ref, wg_ref, wu_ref, wd_ref, ys_ref, acc, xb):
    t = pl.program_id(0)
    e = pl.program_id(1)
    last_e = pl.num_programs(1) - 1

    @pl.when(t < nu_ref[0])
    def _():
        @pl.when(e == 0)
        def _():
            acc[...] = jnp.zeros_like(acc)
            xb[...] = xs_ref[:, 0:D_MODEL].astype(BF16)

        x = xb[...]
        rt = xs_ref[:, D_MODEL:MOE_ROW]
        ln = lax.broadcasted_iota(jnp.int32, rt.shape, 1)
        y = acc[...]
        for j in range(EXPERTS_PER_STEP):
            gate = _dot(x, wg_ref[j].astype(BF16))
            hid = gate * _sigmoid(gate) * _dot(x, wu_ref[j].astype(BF16))
            ye = _dot(hid.astype(BF16), wd_ref[j].astype(BF16))
            lane = COMB_LANE0 + e * EXPERTS_PER_STEP + j
            ce = jnp.sum(jnp.where(ln == lane, rt, 0.0), axis=-1, keepdims=True)
            y = y + ce * ye
        acc[...] = y

        @pl.when(e == last_e)
        def _():
            ys_ref[...] = acc[...]

    @pl.when((t >= nu_ref[0]) & (e == last_e))
    def _():
        ys_ref[...] = jnp.zeros_like(ys_ref)


def _experts(xs, tile_group, n_used, wg, wu, wd):
    n_rows = xs.shape[0]
    tile = lambda t, e, tg, nu: (jnp.minimum(t, nu[0] - 1), 0)
    steps = EXPERTS_PER_GROUP // EXPERTS_PER_STEP
    wsel = lambda t, e, tg, nu: (tg[jnp.minimum(t, nu[0] - 1)] * steps + e, 0, 0)
    return pl.pallas_call(
        _experts_kernel,
        grid_spec=pltpu.PrefetchScalarGridSpec(
            num_scalar_prefetch=2,
            grid=(n_rows // GROUP_TILE, steps),
            in_specs=[pl.BlockSpec((GROUP_TILE, MOE_ROW), tile),
                      pl.BlockSpec((EXPERTS_PER_STEP, D_MODEL, D_EXPERT), wsel),
                      pl.BlockSpec((EXPERTS_PER_STEP, D_MODEL, D_EXPERT), wsel),
                      pl.BlockSpec((EXPERTS_PER_STEP, D_EXPERT, D_MODEL), wsel)],
            out_specs=pl.BlockSpec((GROUP_TILE, D_MODEL), lambda t, e, tg, nu: (t, 0)),
            scratch_shapes=[pltpu.VMEM((GROUP_TILE, D_MODEL), F32),
                            pltpu.VMEM((GROUP_TILE, D_MODEL), BF16)]),
        out_shape=jax.ShapeDtypeStruct((n_rows, D_MODEL), F32),
        compiler_params=pltpu.CompilerParams(
            dimension_semantics=("arbitrary", "arbitrary"), vmem_limit_bytes=VMEM_LIMIT),
        name="moe_experts",
    )(tile_group, n_used, xs, wg, wu, wd)


def _tail_kernel(pos_ref, posn_ref, x1_ref, p_ref, ys_ref, nple_ref, wpg_ref, wpp_ref,
                 nfin_ref, y_ref, ybuf0, ybuf1, sem, *, tm):
    i = pl.program_id(0)
    last = pl.num_programs(0) - 1
    ybuf = (ybuf0, ybuf1)

    def issue(p_ref, p_off, slot):
        for r in range(tm):
            pltpu.make_async_copy(ys_ref.at[pl.ds(p_ref[p_off + r], 1)],
                                  ybuf[slot].at[pl.ds(r, 1)], sem.at[slot]).start(priority=r % 2)

    def wait(slot):
        pltpu.make_async_copy(ys_ref.at[pl.ds(0, tm)], ybuf[slot], sem.at[slot]).wait()

    def compute(slot):
        rows = pl.ds(slot * tm, tm)
        x2 = x1_ref[rows, :] + ybuf[slot][...]
        h3 = _rmsnorm(x2, nple_ref[...]).astype(BF16)
        ple = (_sigmoid(_dot(h3, wpg_ref[...]))
               * _dot(p_ref[rows, :].astype(BF16), wpp_ref[...]))
        y_ref[rows, :] = _rmsnorm(x2 + ple, nfin_ref[...])

    @pl.when(i == 0)
    def _():
        issue(pos_ref, 0, 0)

    wait(0)
    issue(pos_ref, tm, 1)
    compute(0)
    wait(1)
    issue(posn_ref, 0, 0)
    compute(1)

    @pl.when(i == last)
    def _():
        wait(0)


def _tail(pos, x1, p2d, ys, nple, wpg, wpp, nfin, *, tm):
    n = x1.shape[0]
    n_steps = n // (2 * tm)
    tok = lambda width: pl.BlockSpec((2 * tm, width), lambda i: (i, 0))
    full = lambda arr: pl.BlockSpec(arr.shape, lambda i: (0, 0))
    return pl.pallas_call(
        functools.partial(_tail_kernel, tm=tm),
        grid=(n_steps,),
        in_specs=[pl.BlockSpec((2 * tm,), lambda i: (i,), memory_space=pltpu.SMEM),
                  pl.BlockSpec((tm,), lambda i: (2 * jnp.minimum(i + 1, n_steps - 1),),
                               memory_space=pltpu.SMEM),
                  tok(D_MODEL), tok(PLE_DIM),
                  pl.BlockSpec(memory_space=pl.ANY),
                  full(nple), full(wpg), full(wpp), full(nfin)],
        out_specs=tok(D_MODEL),
        out_shape=jax.ShapeDtypeStruct((n, D_MODEL), F32),
        scratch_shapes=[pltpu.VMEM((tm, D_MODEL), F32), pltpu.VMEM((tm, D_MODEL), F32),
                        pltpu.SemaphoreType.DMA((2,))],
        compiler_params=pltpu.CompilerParams(
            dimension_semantics=("arbitrary",), vmem_limit_bytes=VMEM_LIMIT),
        name="moe_tail",
    )(pos, pos, x1, p2d, ys, nple, wpg, wpp, nfin)


def _moe_tail(groups, cnt, wg, wu, wd, nple, wpg, wpp, nfin, *, tm):
    n_all = sum(g[0].shape[0] for g in groups)
    n_tiles_max = (n_all + GROUP_TILE - 1) // GROUP_TILE + N_GROUPS
    tile_group, n_used, row0 = _route_tables(cnt, n_tiles_max)
    xs = jnp.zeros((n_tiles_max * GROUP_TILE, MOE_ROW), F32)
    poss = []
    for hx, rt, _, _, td in groups:
        poss.append(_sorted_rows(rt, row0))
        xs = _dispatch(hx, poss[-1], xs, td=td)
    ys = _experts(xs, tile_group, n_used, wg, wu, wd)
    return [_tail(pos, x1, p2d, ys, nple, wpg, wpp, nfin, tm=tm)
            for pos, (_, _, x1, p2d, _) in zip(poss, groups)]


def kernel(x_prompt, x_sample, state_pool, state_shift, state_wkv, p_prompt, p_sample, norm_mix, w_in, pool_mix, pool_scale, w_branch_a, shift_mu, decay_w0, decay_w2, iclr_a0, iclr_a2, gate_g2, k_k, k_a, r_k, ln_x_w, ln_x_b, w_branch_b, w_out, norm_ffn, w_route_group, b_route_group, w_route_expert, b_route_expert, expert_gate, expert_up, expert_down, norm_ple, w_ple_gate, w_ple_proj, norm_final):
    l = 0
    bsz, seq, _ = x_prompt.shape
    dbsz, dseq, _ = x_sample.shape
    row = lambda vec: vec.reshape(1, -1).astype(F32)
    col = lambda vec: vec.reshape(-1, 1).astype(F32)
    perm = CHANNEL_PERM

    o1 = POOL_WIDTH
    o2 = o1 + SHIFT_WIDTH
    zperm = np.concatenate([perm, RWKV_WIDTH + perm, 2 * RWKV_WIDTH + perm,
                            np.arange(3 * RWKV_WIDTH, SHIFT_WIDTH)])
    w_z = w_in[l][:, o1:o2][:, zperm].astype(BF16)
    zeros_lora = jnp.zeros((RWKV_WIDTH, LORA_PAIR // 2), F32)
    w2t_pad = jnp.concatenate([decay_w2[l].T[perm], zeros_lora], axis=1).astype(BF16)
    a2t_pad = jnp.concatenate([zeros_lora, iclr_a2[l].T[perm]], axis=1).astype(BF16)
    w_router = jnp.concatenate(
        [w_route_expert[l], w_route_group[l],
         jnp.zeros((D_MODEL, ROUTER_LANES - N_EXPERTS - N_GROUPS), F32)], axis=1)
    wr_hi = w_router.astype(BF16)
    wr_lo = (w_router - wr_hi.astype(F32)).astype(BF16)
    b_router = jnp.concatenate(
        [b_route_expert[l], b_route_group[l],
         jnp.zeros((ROUTER_LANES - N_EXPERTS - N_GROUPS,), F32)]).reshape(1, -1)
    prep_w = [row(norm_mix[l]), w_in[l][:, :o1].astype(BF16), w_z, w_in[l][:, o2:].astype(BF16),
              col(shift_mu[l][zperm]), col(decay_w0[l][perm]), w2t_pad, col(iclr_a0[l][perm]), a2t_pad,
              gate_g2[l].T[perm].astype(BF16), col(k_k[l][perm]), col(k_a[l][perm]),
              col(r_k[l].reshape(-1)[perm]), pool_mix[l].astype(BF16), row(pool_scale[l]),
              w_branch_a[l].astype(BF16)]
    post_w = [col(ln_x_w[l][perm]), col(ln_x_b[l][perm]), w_branch_b[l][perm].astype(BF16),
              w_out[l].astype(BF16), row(norm_ffn[l]), wr_hi, wr_lo, b_router]
    moe_w = [expert_gate[l], expert_up[l], expert_down[l],
             row(norm_ple[l]), w_ple_gate[l].astype(BF16), w_ple_proj[l].astype(BF16), row(norm_final)]

    tm_p = 256
    tiles_p = seq // tm_p
    x_p = x_prompt.reshape(bsz * seq, D_MODEL)
    outs = _mix_prep(x_p, jnp.zeros((bsz * SHIFT_WIDTH, LANES), F32),
                     jnp.zeros((bsz * 16, POOL_WIDTH), F32), prep_w,
                     nb=bsz, tiles=tiles_p, tm=tm_p, s=1, pos0=0, cm_index=lambda b, t: (b, 0, t))
    u_p, zl_p, ops, g, bon, pa, sgb = outs
    o_p, s_p = _wkv_scan_prompt(ops, tt=LANES)
    tm_post = 512
    n_p = bsz * seq
    n_s = dbsz * dseq
    x1_p, hx_p, rt_p, cnt_p = _post(
        o_p, bon, g, pa, sgb, x_p, jnp.zeros((SUBLANES, ROUTER_LANES), F32), post_w,
        tm=tm_post, sub=256, cm_index=lambda i: (i // (seq // tm_post), 0, i % (seq // tm_post)))
    wkv_p = s_p.reshape(KEY_HALF, 2, KEY_HALF, 2, bsz, HEADS).transpose(4, 5, 3, 2, 1, 0)
    wkv_p = wkv_p.reshape(bsz, HEADS, HEAD_DIM, HEAD_DIM)

    n_s = dbsz * dseq
    x_s = x_sample.transpose(1, 0, 2).reshape(n_s, D_MODEL)
    p_s = p_sample[l].transpose(1, 0, 2).reshape(n_s, PLE_DIM)
    uc0_s = jnp.concatenate(
        [jnp.zeros((dbsz, POOL_WIDTH), F32),
         state_pool[l].transpose(1, 0, 2).reshape(POOL_STATE * dbsz, POOL_WIDTH)], axis=0)
    zc0_s = state_shift[l].T[zperm]
    outs = _mix_prep(x_s, zc0_s, uc0_s, prep_w, nb=1, tiles=dseq, tm=dbsz, s=dbsz, pos0=PAST_LEN,
                     cm_index=lambda b, t: (t, 0, 0))
    u_s, zl_s, ops, g, bon, pa, sgb = outs
    s0_s = state_wkv[l].reshape(dbsz, HEADS, 2, KEY_HALF, 2, KEY_HALF).transpose(1, 5, 4, 3, 2, 0)
    s0_s = s0_s.reshape(HEADS, HEAD_DIM, HEAD_DIM, dbsz)
    o_s, s_s = _wkv_scan_sample(ops, s0_s)
    x1_s, hx_s, rt_s, cnt_all = _post(o_s, bon, g, pa, sgb, x_s, cnt_p, post_w,
                                      tm=dbsz, sub=dbsz, cm_index=lambda i: (i, 0, 0))

    y_p, y_s = _moe_tail(
        [(hx_p, rt_p, x1_p, p_prompt[l].reshape(n_p, PLE_DIM), 512), (hx_s, rt_s, x1_s, p_s, n_s)],
        cnt_all, *moe_w, tm=256)
    wkv_s = s_s.reshape(HEADS, KEY_HALF, 2, KEY_HALF, 2, dbsz).transpose(5, 0, 4, 3, 2, 1)
    wkv_s = wkv_s.reshape(dbsz, HEADS, HEAD_DIM, HEAD_DIM)

    inv_z = np.argsort(zperm)
    y_prompt = y_p.reshape(bsz, seq, D_MODEL)
    y_sample = y_s.reshape(dseq, dbsz, D_MODEL).transpose(1, 0, 2)
    pool_prompt = u_p.reshape(bsz, seq, POOL_WIDTH)[:, seq - POOL_STATE:]
    u_s_bt = u_s.reshape(dseq, dbsz, POOL_WIDTH).transpose(1, 0, 2)
    pool_sample = jnp.concatenate([state_pool[l][:, dseq:], u_s_bt], axis=1)
    shift_prompt = zl_p[:, :, LANES - 1][:, inv_z]
    shift_sample = zl_s[0].T[:, inv_z]
    return (y_prompt, y_sample, pool_prompt[None], shift_prompt[None], wkv_p[None],
            pool_sample[None], shift_sample[None], wkv_s[None])
```

```python
import functools
import math

import jax
import jax.numpy as jnp
from jax import lax
from jax.experimental import pallas as pl
from jax.experimental.pallas import tpu as pltpu

F32 = jnp.float32
BF16 = jnp.bfloat16

D_MODEL = 1024
PLE_DIM = 256
POOL_WIDTH = 512
POOL_WINDOWS = (2, 4, 8, 16)
POOL_GROUP_DIM = 128
POOL_STATE = 15
RWKV_WIDTH = 512
HEAD_DIM = 64
HEADS = 8
LORA_PAIR = 128
GATE_LORA = 128
SHIFT_WIDTH = 3 * RWKV_WIDTH + LORA_PAIR + GATE_LORA
IN_WIDTH = POOL_WIDTH + SHIFT_WIDTH + 2 * D_MODEL
N_GROUPS = 4
EXPERTS_PER_GROUP = 8
N_EXPERTS = 32
D_EXPERT = 256
RMS_EPS = 1e-6
GN_EPS = 64e-5
PAST_LEN = 16384

LANES = 128
SUBLANES = 8
KEY_HALF = HEAD_DIM // 2
SCAN_OPERANDS = 6
KEY_GROUP = 32
ROUTER_LANES = 128
GROUP_TILE = 1024
EXPERTS_PER_STEP = 2
DECAY_SCALE = math.exp(-0.5)
COMB_LANE0 = 8
MOE_ROW = D_MODEL + ROUTER_LANES
NEG_BIG = -1e30
VMEM_LIMIT = 56 * 1024 * 1024

def _swap_channel_order(x, axis, to_scan):
    x = jnp.moveaxis(x, axis, -1)
    lead = x.shape[:-1]
    split = (HEADS, 2, KEY_HALF) if to_scan else (KEY_HALF, 2, HEADS)
    x = jnp.swapaxes(x.reshape(lead + split), -1, -3).reshape(lead + (RWKV_WIDTH,))
    return jnp.moveaxis(x, -1, axis)


def _swap_shift_order(x, axis, to_scan):
    x = jnp.moveaxis(x, axis, -1)
    lead = x.shape[:-1]
    rkv = x[..., :3 * RWKV_WIDTH].reshape(lead + (3, RWKV_WIDTH))
    rkv = _swap_channel_order(rkv, -1, to_scan).reshape(lead + (3 * RWKV_WIDTH,))
    return jnp.moveaxis(jnp.concatenate([rkv, x[..., 3 * RWKV_WIDTH:]], axis=-1), -1, axis)


def _dot(a, b):
    return jnp.dot(a, b, preferred_element_type=F32)


def _split_bf16(x):
    hi = x.astype(BF16)
    return hi, (x - hi.astype(F32)).astype(BF16)


def _head_sum(x):
    x3 = x.reshape(HEAD_DIM, HEADS, x.shape[1])
    s = jnp.sum(x3, axis=0, keepdims=True)
    return jnp.broadcast_to(s, x3.shape).reshape(x.shape)


def _rmsnorm(x, g):
    return x * lax.rsqrt(jnp.mean(x * x, axis=-1, keepdims=True) + RMS_EPS) * g


def _sigmoid(x):
    return 0.5 * jnp.tanh(0.5 * x) + 0.5


def _mix_prep_kernel(x_ref, zc0_ref, uc0_ref, nmix_ref, wu_ref, wz_ref, wgab_ref, mu_ref, w0_ref,
                     w2t_ref, a0_ref, a2t_ref, g2t_ref, kk_ref, ka_ref, rk_ref, mix_ref,
                     pscale_ref, wa_ref,
                     u_ref, zl_ref, q_ref, g_ref, bon_ref,
                     pa_ref, sgb_ref, zc, uext, *, tm, s, pos0):
    t = pl.program_id(1)
    up = 16 * s

    @pl.when(t == 0)
    def _():
        zc[...] = zc0_ref[...]
        uext[0:up] = uc0_ref[...]

    x = x_ref[...]
    h = _rmsnorm(x, nmix_ref[...]).astype(BF16)
    u = _dot(h, wu_ref[...])
    gab = _dot(h, wgab_ref[...])
    u_ref[...] = u
    sgb_ref[...] = _sigmoid(gab[:, D_MODEL:])
    uext[up:up + tm] = u

    z = _dot(h, wz_ref[...])
    zl_ref[0] = z[tm - max(SUBLANES, s):tm]
    z_t = z.T
    if s == 1:
        lane = lax.broadcasted_iota(jnp.int32, z_t.shape, 1)
        zprev = jnp.where(lane == 0, zc[:, LANES - 1:LANES], pltpu.roll(z_t, 1, axis=1))
    else:
        zprev = zc[...]
    zc[...] = z_t[:, tm - LANES:tm]
    zm = z_t + (zprev - z_t) * mu_ref[...]
    r = zm[0:RWKV_WIDTH]
    k = zm[RWKV_WIDTH:2 * RWKV_WIDTH]
    v = zm[2 * RWKV_WIDTH:3 * RWKV_WIDTH]
    lora_in = zm[3 * RWKV_WIDTH:3 * RWKV_WIDTH + LORA_PAIR]
    gd = zm[3 * RWKV_WIDTH + LORA_PAIR:SHIFT_WIDTH]
    dw = _dot(w2t_ref[...], jnp.tanh(lora_in).astype(BF16))
    da = _dot(a2t_ref[...], lora_in.astype(BF16))
    decay = jnp.exp(-DECAY_SCALE * _sigmoid(w0_ref[...] + dw))
    a = _sigmoid(a0_ref[...] + da)
    kk = k * kk_ref[...]
    kk = kk / jnp.maximum(jnp.sqrt(_head_sum(kk * kk)), 1e-12)
    k2 = k * (1.0 + (a - 1.0) * ka_ref[...])
    q_ref[0, 0] = -kk
    q_ref[1, 0] = decay
    q_ref[2, 0] = kk * a
    q_ref[3, 0] = k2
    q_ref[4, 0] = r
    q_ref[5, 0] = v
    g_ref[0] = _dot(g2t_ref[...], _sigmoid(gd).astype(BF16))
    bon_ref[0] = _head_sum(r * k2 * rk_ref[...]) * v

    rows = lax.broadcasted_iota(jnp.int32, (tm, POOL_GROUP_DIM), 0)
    if s > 1:
        rows = rows // s
    pos = pos0 + t * (tm // s) + rows
    ys = []
    for gi, wnd in enumerate(POOL_WINDOWS):
        lanes = slice(gi * POOL_GROUP_DIM, (gi + 1) * POOL_GROUP_DIM)
        wsum = uext[:, lanes]
        span = 1
        while span < wnd:
            wsum = wsum + pltpu.roll(wsum, span * s, axis=0)
            span *= 2
        cur = uext[pl.ds(up, tm), lanes]
        cnt = jnp.minimum(pos + 1, wnd).astype(F32)
        pooled = wsum[up:up + tm] / cnt - cur
        ys.append(_dot(pooled.astype(BF16), mix_ref[gi]))
    y = jnp.concatenate(ys, axis=-1) * pscale_ref[...]
    pa_ref[...] = _sigmoid(gab[:, :D_MODEL]) * _dot(y.astype(BF16), wa_ref[...])

    uext[0:up] = uext[tm:tm + up]


def _mix_prep(x2d, zc0, uc0, wts, *, nb, tiles, tm, s, pos0, cm_index):
    n = x2d.shape[0]
    up = 16 * s
    row = lambda b, t: (b * tiles + t, 0)
    full = lambda arr: pl.BlockSpec(arr.shape, lambda b, t: (0,) * arr.ndim)
    in_specs = [
        pl.BlockSpec((tm, D_MODEL), row),
        pl.BlockSpec((SHIFT_WIDTH, LANES), lambda b, t: (b, 0)),
        pl.BlockSpec((up, POOL_WIDTH), lambda b, t: (b, 0)),
    ] + [full(w) for w in wts]
    tok = lambda width: pl.BlockSpec((tm, width), row)
    cm_shape = (nb, RWKV_WIDTH, tiles * tm) if s == 1 else (tiles, RWKV_WIDTH, tm)
    cm = pl.BlockSpec((1, RWKV_WIDTH, tm), cm_index)
    cm_stack = pl.BlockSpec((SCAN_OPERANDS, 1, RWKV_WIDTH, tm), lambda b, t: (0,) + cm_index(b, t))
    zrows = max(SUBLANES, s)
    out_specs = [tok(POOL_WIDTH), pl.BlockSpec((1, zrows, SHIFT_WIDTH), lambda b, t: (b, 0, 0)),
                 cm_stack, cm, cm] + [tok(D_MODEL)] * 2
    out_shape = [jax.ShapeDtypeStruct((n, POOL_WIDTH), F32),
                 jax.ShapeDtypeStruct((nb, zrows, SHIFT_WIDTH), F32),
                 jax.ShapeDtypeStruct((SCAN_OPERANDS,) + cm_shape, F32)] \
        + [jax.ShapeDtypeStruct(cm_shape, F32)] * 2 \
        + [jax.ShapeDtypeStruct((n, D_MODEL), F32)] * 2
    return pl.pallas_call(
        functools.partial(_mix_prep_kernel, tm=tm, s=s, pos0=pos0),
        grid=(nb, tiles),
        in_specs=in_specs,
        out_specs=out_specs,
        out_shape=out_shape,
        scratch_shapes=[pltpu.VMEM((SHIFT_WIDTH, LANES), F32),
                        pltpu.VMEM((up + tm, POOL_WIDTH), F32)],
        compiler_params=pltpu.CompilerParams(
            dimension_semantics=("arbitrary", "arbitrary"), vmem_limit_bytes=VMEM_LIMIT),
        name="mix_prep",
    )(x2d, zc0, uc0, *wts)


def _wkv_step(s_ref, row, vv):
    groups = HEAD_DIM // KEY_GROUP

    def sa_pass(g, acc):
        for kk in range(KEY_GROUP):
            kp = g * KEY_GROUP + kk
            acc = acc + s_ref[kp] * row(0, kp)
        return acc
    sa = lax.fori_loop(0, groups, sa_pass, jnp.zeros(vv.shape, F32))

    def update_pass(g, acc):
        for kk in range(KEY_GROUP):
            kp = g * KEY_GROUP + kk
            sn = s_ref[kp] * row(1, kp) + sa * row(2, kp) + vv * row(3, kp)
            s_ref[kp] = sn
            acc = acc + sn * row(4, kp)
        return acc
    return lax.fori_loop(0, groups, update_pass, jnp.zeros(vv.shape, F32))


def _swap_major_sublane(x):
    return jnp.swapaxes(x, 0, 1)


def _wkv_scan_prompt_kernel(q_ref, o_ref, sout_ref, s_ref, kv_ref, vv_ref, ov_ref, *, tt):
    t = pl.program_id(0)
    q = pl.program_id(1)

    @pl.when((t == 0) & (q == 0))
    def _():
        s_ref[...] = jnp.zeros_like(s_ref)

    def gather_t(base_lo, base_hi):
        pieces = [q_ref[0, bb, pl.ds(base, SUBLANES), :] for base in (base_lo, base_hi) for bb in range(8)]
        return jnp.concatenate(pieces, axis=0).T

    @pl.when(q < SCAN_OPERANDS - 1)
    def _():
        def kgroup(gi, c):
            slabs = []
            for kk in range(SUBLANES):
                base = pl.multiple_of((gi * SUBLANES + kk) * SUBLANES, SUBLANES)
                slabs.append(gather_t(base, base))
            kv_ref[q, :, pl.ds(pl.multiple_of(gi * SUBLANES, SUBLANES), SUBLANES), :] = \
                _swap_major_sublane(jnp.stack(slabs))
            return c
        lax.fori_loop(0, HEAD_DIM // SUBLANES, kgroup, 0, unroll=True)

    @pl.when(q == SCAN_OPERANDS - 1)
    def _():
        def vgroup(gi, c):
            slabs = []
            for vi in range(SUBLANES):
                base = pl.multiple_of((gi * SUBLANES + vi) * 2 * SUBLANES, 2 * SUBLANES)
                slabs.append(gather_t(base, base + SUBLANES))
            vv_ref[:, pl.ds(pl.multiple_of(gi * SUBLANES, SUBLANES), SUBLANES), :] = \
                _swap_major_sublane(jnp.stack(slabs))
            return c
        lax.fori_loop(0, KEY_HALF // SUBLANES, vgroup, 0, unroll=True)

        def step(i, c):
            row = lambda qi, kp: kv_ref[qi, i, pl.ds(kp, 1), :]
            ov_ref[i] = _wkv_step(s_ref, row, vv_ref[i])
            return c
        lax.fori_loop(0, tt, step, 0)

        def ogroup(gi, c):
            g0 = pl.multiple_of(gi * SUBLANES, SUBLANES)
            x = _swap_major_sublane(ov_ref[:, pl.ds(g0, SUBLANES), :])
            for vi in range(SUBLANES):
                xt = x[vi].T
                base = pl.multiple_of((gi * SUBLANES + vi) * 2 * SUBLANES, 2 * SUBLANES)
                for vh in range(2):
                    for bb in range(8):
                        r0 = (vh * 8 + bb) * SUBLANES
                        o_ref[bb, pl.ds(base + vh * SUBLANES, SUBLANES), :] = xt[r0:r0 + SUBLANES]
            return c
        lax.fori_loop(0, KEY_HALF // SUBLANES, ogroup, 0, unroll=True)

        @pl.when(t == pl.num_programs(0) - 1)
        def _():
            sout_ref[...] = s_ref[...]


def _wkv_scan_prompt(ops, *, tt):
    _, nb, _, t_len = ops.shape
    sspec = pl.BlockSpec((HEAD_DIM, KEY_HALF, LANES), lambda ti, qi: (0, 0, 0))
    return pl.pallas_call(
        functools.partial(_wkv_scan_prompt_kernel, tt=tt),
        grid=(t_len // tt, SCAN_OPERANDS),
        in_specs=[pl.BlockSpec((1, nb, RWKV_WIDTH, tt), lambda ti, qi: (qi, 0, 0, ti))],
        out_specs=[pl.BlockSpec((nb, RWKV_WIDTH, tt), lambda ti, qi: (0, 0, ti)), sspec],
        out_shape=[jax.ShapeDtypeStruct((nb, RWKV_WIDTH, t_len), F32),
                   jax.ShapeDtypeStruct((HEAD_DIM, KEY_HALF, LANES), F32)],
        scratch_shapes=[pltpu.VMEM((HEAD_DIM, KEY_HALF, LANES), F32),
                        pltpu.VMEM((SCAN_OPERANDS - 1, tt, HEAD_DIM, LANES), F32),
                        pltpu.VMEM((tt, KEY_HALF, LANES), F32),
                        pltpu.VMEM((tt, KEY_HALF, LANES), F32)],
        compiler_params=pltpu.CompilerParams(
            dimension_semantics=("arbitrary", "arbitrary"), vmem_limit_bytes=VMEM_LIMIT),
        name="wkv_scan_prompt",
    )(ops)


def _wkv_scan_sample_kernel(q_ref, s0_ref, o_ref, sout_ref, s_ref, *, t_len):
    h = pl.program_id(0)
    s_ref[...] = s0_ref[0]
    for i in range(t_len):
        row = lambda qi, kp, i=i: q_ref[qi, i, pl.ds(kp * HEADS + h, 1), :]
        vv = q_ref[SCAN_OPERANDS - 1, i, pl.ds(h, HEAD_DIM, stride=HEADS), :]
        o_ref[i, pl.ds(h, HEAD_DIM, stride=HEADS), :] = _wkv_step(s_ref, row, vv)
    sout_ref[0] = s_ref[...]


def _wkv_scan_sample(ops, s0):
    t_len = ops.shape[1]
    spec = pl.BlockSpec(ops.shape[1:], lambda h: (0, 0, 0))
    sspec = pl.BlockSpec((1, HEAD_DIM, HEAD_DIM, LANES), lambda h: (h, 0, 0, 0))
    return pl.pallas_call(
        functools.partial(_wkv_scan_sample_kernel, t_len=t_len),
        grid=(HEADS,),
        in_specs=[pl.BlockSpec(ops.shape, lambda h: (0, 0, 0, 0)), sspec],
        out_specs=[spec, sspec],
        out_shape=[jax.ShapeDtypeStruct(ops.shape[1:], F32), jax.ShapeDtypeStruct(s0.shape, F32)],
        scratch_shapes=[pltpu.VMEM((HEAD_DIM, HEAD_DIM, LANES), F32)],
        compiler_params=pltpu.CompilerParams(
            dimension_semantics=("arbitrary",), vmem_limit_bytes=VMEM_LIMIT),
        name="wkv_scan_sample",
    )(ops, s0)


def _post_kernel(o_ref, bon_ref, g_ref, pa_ref, sgb_ref, x_ref, cnt0_ref, lnw_ref, lnb_ref,
                 wb_ref, wout_ref, nffn_ref, wrh_ref, wrl_ref, br_ref, x1_ref, h2_ref, rt_ref, cnt_ref,
                 carry, *, sub):
    @pl.when(pl.program_id(0) == 0)
    def _():
        carry[...] = cnt0_ref[...]

    parts = [pl.ds(part * sub, sub) for part in range(o_ref.shape[2] // sub)]
    logits = [_post_project(rows, o_ref, bon_ref, g_ref, pa_ref, sgb_ref, x_ref, lnw_ref, lnb_ref,
                            wb_ref, wout_ref, nffn_ref, wrh_ref, wrl_ref, br_ref, x1_ref, h2_ref)
              for rows in parts]
    prev = carry[0:1, :]
    for rows, lg in zip(parts, logits):
        prev = _post_route(rows, prev, lg, h2_ref, rt_ref)
    carry[...] = jnp.broadcast_to(prev, carry.shape)
    cnt_ref[...] = jnp.broadcast_to(prev, cnt_ref.shape)


def _post_project(rows, o_ref, bon_ref, g_ref, pa_ref, sgb_ref, x_ref, lnw_ref, lnb_ref,
                  wb_ref, wout_ref, nffn_ref, wrh_ref, wrl_ref, br_ref, x1_ref, h2_ref):
    o = o_ref[0, :, rows]
    mean = _head_sum(o) * (1.0 / HEAD_DIM)
    d = o - mean
    var = _head_sum(d * d) * (1.0 / HEAD_DIM)
    on = d * lax.rsqrt(var + GN_EPS) * lnw_ref[...] + lnb_ref[...]
    yb = ((on + bon_ref[0, :, rows]) * g_ref[0, :, rows]).astype(BF16)
    mb = lax.dot_general(yb, wb_ref[...], (((0,), (0,)), ((), ())), preferred_element_type=F32)
    merged = pa_ref[rows, :] + sgb_ref[rows, :] * mb
    x1 = x_ref[rows, :] + _dot(merged.astype(BF16), wout_ref[...])
    x1_ref[rows, :] = x1
    h2 = _rmsnorm(x1, nffn_ref[...])
    h2_ref[rows, 0:D_MODEL] = h2

    h_hi, h_lo = _split_bf16(h2)
    return (_dot(h_hi, wrh_ref[...]) + _dot(h_lo, wrh_ref[...]) + _dot(h_hi, wrl_ref[...])
            + br_ref[...])


def _post_route(rows, prev_count, logits, h2_ref, rt_ref):
    ln = lax.broadcasted_iota(jnp.int32, logits.shape, 1)
    is_group = (ln >= N_EXPERTS) & (ln < N_EXPERTS + N_GROUPS)
    gl = jnp.where(is_group, logits, NEG_BIG)
    gmax = jnp.max(gl, axis=-1, keepdims=True)
    gsel = jnp.min(jnp.where(gl == gmax, ln, ROUTER_LANES), axis=-1, keepdims=True) - N_EXPERTS
    den = jnp.sum(jnp.where(is_group, jnp.exp(gl - gmax), 0.0), axis=-1, keepdims=True)
    pg = 1.0 / den
    in_group = (ln < N_EXPERTS) & ((ln // EXPERTS_PER_GROUP) == gsel)
    el = jnp.where(in_group, logits, NEG_BIG)
    m1 = jnp.max(el, axis=-1, keepdims=True)
    i1 = jnp.min(jnp.where(el == m1, ln, ROUTER_LANES), axis=-1, keepdims=True)
    el2 = jnp.where(ln == i1, NEG_BIG, el)
    m2 = jnp.max(el2, axis=-1, keepdims=True)
    i2 = jnp.min(jnp.where(el2 == m2, ln, ROUTER_LANES), axis=-1, keepdims=True)
    e2 = jnp.exp(m2 - m1)
    p1 = 1.0 / (1.0 + e2)
    p2 = e2 / (1.0 + e2)

    tm = logits.shape[0]
    sel = ln == gsel
    tri = (lax.broadcasted_iota(jnp.int32, (tm, tm), 1)
           < lax.broadcasted_iota(jnp.int32, (tm, tm), 0)).astype(BF16)
    before = prev_count + _dot(tri, sel.astype(BF16))
    rank = jnp.sum(jnp.where(sel, before, 0.0), axis=-1, keepdims=True)
    first = gsel * EXPERTS_PER_GROUP - COMB_LANE0
    rt = (jnp.where(ln == 0, gsel.astype(F32), 0.0) + jnp.where(ln == 1, rank, 0.0)
          + jnp.where(ln == i1 - first, p1 * pg, 0.0) + jnp.where(ln == i2 - first, p2 * pg, 0.0))
    rt_ref[rows, :] = rt
    h2_ref[rows, D_MODEL:MOE_ROW] = rt
    return prev_count + jnp.sum(sel.astype(F32), axis=0, keepdims=True)


def _post(o, bon, g, pa, sgb, x2d, cnt0, wts, *, tm, sub, cm_index):
    n = x2d.shape[0]
    row = lambda i: (i, 0)
    tok = lambda width: pl.BlockSpec((tm, width), row)
    cm = pl.BlockSpec((1, RWKV_WIDTH, tm), cm_index)
    full = lambda arr: pl.BlockSpec(arr.shape, lambda i: (0, 0))
    return pl.pallas_call(
        functools.partial(_post_kernel, sub=sub),
        grid=(n // tm,),
        in_specs=[cm] * 3 + [tok(D_MODEL)] * 3 + [full(cnt0)] + [full(w) for w in wts],
        out_specs=[tok(D_MODEL), tok(MOE_ROW), tok(ROUTER_LANES),
                   pl.BlockSpec((SUBLANES, ROUTER_LANES), lambda i: (0, 0))],
        out_shape=[jax.ShapeDtypeStruct((n, D_MODEL), F32),
                   jax.ShapeDtypeStruct((n, MOE_ROW), F32),
                   jax.ShapeDtypeStruct((n, ROUTER_LANES), F32),
                   jax.ShapeDtypeStruct((SUBLANES, ROUTER_LANES), F32)],
        scratch_shapes=[pltpu.VMEM((SUBLANES, ROUTER_LANES), F32)],
        compiler_params=pltpu.CompilerParams(
            dimension_semantics=("arbitrary",), vmem_limit_bytes=VMEM_LIMIT),
        name="post",
    )(o, bon, g, pa, sgb, x2d, cnt0, *wts)


def _route_tables(cnt, n_tiles_max):
    counts = cnt[0, :N_GROUPS].astype(jnp.int32)
    tiles_g = (counts + GROUP_TILE - 1) // GROUP_TILE
    tile_end = jnp.cumsum(tiles_g)
    row0 = (tile_end - tiles_g) * GROUP_TILE
    tile_ids = jnp.arange(n_tiles_max, dtype=jnp.int32)
    tile_group = jnp.minimum(jnp.sum(tile_ids[:, None] >= tile_end[None, :], axis=-1),
                             N_GROUPS - 1).astype(jnp.int32)
    return tile_group, tile_end[N_GROUPS - 1:].astype(jnp.int32), row0


def _sorted_rows(rt, row0):
    g = rt[:, 0].astype(jnp.int32)
    r = rt[:, 1].astype(jnp.int32)
    return r + jnp.sum(jnp.where(g[:, None] == jnp.arange(N_GROUPS), row0, 0), axis=-1)


def _dispatch_kernel(pos_ref, h_ref, xs0_ref, xs_ref, sem, *, td):
    del xs0_ref
    for r in range(td):
        pltpu.make_async_copy(h_ref.at[pl.ds(r, 1)], xs_ref.at[pl.ds(pos_ref[r], 1)],
                              sem).start(priority=r % 2)
    pltpu.make_async_copy(h_ref, xs_ref.at[pl.ds(0, td)], sem).wait()


def _dispatch(h, pos, xs_in, *, td):
    n = h.shape[0]
    return pl.pallas_call(
        functools.partial(_dispatch_kernel, td=td),
        grid=(n // td,),
        in_specs=[pl.BlockSpec((td,), lambda i: (i,), memory_space=pltpu.SMEM),
                  pl.BlockSpec((td, MOE_ROW), lambda i: (i, 0)),
                  pl.BlockSpec(memory_space=pl.ANY)],
        out_specs=pl.BlockSpec(memory_space=pl.ANY),
        out_shape=jax.ShapeDtypeStruct(xs_in.shape, F32),
        scratch_shapes=[pltpu.SemaphoreType.DMA(())],
        input_output_aliases={2: 0},
        compiler_params=pltpu.CompilerParams(
            dimension_semantics=("arbitrary",), vmem_limit_bytes=VMEM_LIMIT),
        name="moe_dispatch",
    )(pos, h, xs_in)


def _experts_kernel(tg_ref, nu_ref, xs_ref, wg_ref, wu_ref, wd_ref, ys_ref, acc, xb):
    t = pl.program_id(0)
    e = pl.program_id(1)
    last_e = pl.num_programs(1) - 1

    @pl.when(t < nu_ref[0])
    def _():
        @pl.when(e == 0)
        def _():
            acc[...] = jnp.zeros_like(acc)
            xb[...] = xs_ref[:, 0:D_MODEL].astype(BF16)

        x = xb[...]
        rt = xs_ref[:, D_MODEL:MOE_ROW]
        ln = lax.broadcasted_iota(jnp.int32, rt.shape, 1)
        y = acc[...]
        for j in range(EXPERTS_PER_STEP):
            gate = _dot(x, wg_ref[j].astype(BF16))
            hid = gate * _sigmoid(gate) * _dot(x, wu_ref[j].astype(BF16))
            ye = _dot(hid.astype(BF16), wd_ref[j].astype(BF16))
            lane = COMB_LANE0 + e * EXPERTS_PER_STEP + j
            ce = jnp.sum(jnp.where(ln == lane, rt, 0.0), axis=-1, keepdims=True)
            y = y + ce * ye
        acc[...] = y

        @pl.when(e == last_e)
        def _():
            ys_ref[...] = acc[...]

    @pl.when((t >= nu_ref[0]) & (e == last_e))
    def _():
        ys_ref[...] = jnp.zeros_like(ys_ref)


def _experts(xs, tile_group, n_used, wg, wu, wd):
    n_rows = xs.shape[0]
    tile = lambda t, e, tg, nu: (jnp.minimum(t, nu[0] - 1), 0)
    steps = EXPERTS_PER_GROUP // EXPERTS_PER_STEP
    wsel = lambda t, e, tg, nu: (tg[jnp.minimum(t, nu[0] - 1)] * steps + e, 0, 0)
    return pl.pallas_call(
        _experts_kernel,
        grid_spec=pltpu.PrefetchScalarGridSpec(
            num_scalar_prefetch=2,
            grid=(n_rows // GROUP_TILE, steps),
            in_specs=[pl.BlockSpec((GROUP_TILE, MOE_ROW), tile),
                      pl.BlockSpec((EXPERTS_PER_STEP, D_MODEL, D_EXPERT), wsel),
                      pl.BlockSpec((EXPERTS_PER_STEP, D_MODEL, D_EXPERT), wsel),
                      pl.BlockSpec((EXPERTS_PER_STEP, D_EXPERT, D_MODEL), wsel)],
            out_specs=pl.BlockSpec((GROUP_TILE, D_MODEL), lambda t, e, tg, nu: (t, 0)),
            scratch_shapes=[pltpu.VMEM((GROUP_TILE, D_MODEL), F32),
                            pltpu.VMEM((GROUP_TILE, D_MODEL), BF16)]),
        out_shape=jax.ShapeDtypeStruct((n_rows, D_MODEL), F32),
        compiler_params=pltpu.CompilerParams(
            dimension_semantics=("arbitrary", "arbitrary"), vmem_limit_bytes=VMEM_LIMIT),
        name="moe_experts",
    )(tile_group, n_used, xs, wg, wu, wd)


def _tail_kernel(pos_ref, posn_ref, x1_ref, p_ref, ys_ref, nple_ref, wpg_ref, wpp_ref,
                 nfin_ref, y_ref, ybuf0, ybuf1, sem, *, tm):
    i = pl.program_id(0)
    last = pl.num_programs(0) - 1
    ybuf = (ybuf0, ybuf1)

    def issue(p_ref, p_off, slot):
        for r in range(tm):
            pltpu.make_async_copy(ys_ref.at[pl.ds(p_ref[p_off + r], 1)],
                                  ybuf[slot].at[pl.ds(r, 1)], sem.at[slot]).start(priority=r % 2)

    def wait(slot):
        pltpu.make_async_copy(ys_ref.at[pl.ds(0, tm)], ybuf[slot], sem.at[slot]).wait()

    def compute(slot):
        rows = pl.ds(slot * tm, tm)
        x2 = x1_ref[rows, :] + ybuf[slot][...]
        h3 = _rmsnorm(x2, nple_ref[...]).astype(BF16)
        ple = (_sigmoid(_dot(h3, wpg_ref[...]))
               * _dot(p_ref[rows, :].astype(BF16), wpp_ref[...]))
        y_ref[rows, :] = _rmsnorm(x2 + ple, nfin_ref[...])

    @pl.when(i == 0)
    def _():
        issue(pos_ref, 0, 0)

    wait(0)
    issue(pos_ref, tm, 1)
    compute(0)
    wait(1)
    issue(posn_ref, 0, 0)
    compute(1)

    @pl.when(i == last)
    def _():
        wait(0)


def _tail(pos, x1, p2d, ys, nple, wpg, wpp, nfin, *, tm):
    n = x1.shape[0]
    n_steps = n // (2 * tm)
    tok = lambda width: pl.BlockSpec((2 * tm, width), lambda i: (i, 0))
    full = lambda arr: pl.BlockSpec(arr.shape, lambda i: (0, 0))
    return pl.pallas_call(
        functools.partial(_tail_kernel, tm=tm),
        grid=(n_steps,),
        in_specs=[pl.BlockSpec((2 * tm,), lambda i: (i,), memory_space=pltpu.SMEM),
                  pl.BlockSpec((tm,), lambda i: (2 * jnp.minimum(i + 1, n_steps - 1),),
                               memory_space=pltpu.SMEM),
                  tok(D_MODEL), tok(PLE_DIM),
                  pl.BlockSpec(memory_space=pl.ANY),
                  full(nple), full(wpg), full(wpp), full(nfin)],
        out_specs=tok(D_MODEL),
        out_shape=jax.ShapeDtypeStruct((n, D_MODEL), F32),
        scratch_shapes=[pltpu.VMEM((tm, D_MODEL), F32), pltpu.VMEM((tm, D_MODEL), F32),
                        pltpu.SemaphoreType.DMA((2,))],
        compiler_params=pltpu.CompilerParams(
            dimension_semantics=("arbitrary",), vmem_limit_bytes=VMEM_LIMIT),
        name="moe_tail",
    )(pos, pos, x1, p2d, ys, nple, wpg, wpp, nfin)


def _moe_tail(groups, cnt, wg, wu, wd, nple, wpg, wpp, nfin, *, tm):
    n_all = sum(g[0].shape[0] for g in groups)
    n_tiles_max = (n_all + GROUP_TILE - 1) // GROUP_TILE + N_GROUPS
    tile_group, n_used, row0 = _route_tables(cnt, n_tiles_max)
    xs = jnp.zeros((n_tiles_max * GROUP_TILE, MOE_ROW), F32)
    poss = []
    for hx, rt, _, _, td in groups:
        poss.append(_sorted_rows(rt, row0))
        xs = _dispatch(hx, poss[-1], xs, td=td)
    ys = _experts(xs, tile_group, n_used, wg, wu, wd)
    return [_tail(pos, x1, p2d, ys, nple, wpg, wpp, nfin, tm=tm)
            for pos, (_, _, x1, p2d, _) in zip(poss, groups)]


def kernel(x_prompt, x_sample, state_pool, state_shift, state_wkv, p_prompt, p_sample, norm_mix, w_in, pool_mix, pool_scale, w_branch_a, shift_mu, decay_w0, decay_w2, iclr_a0, iclr_a2, gate_g2, k_k, k_a, r_k, ln_x_w, ln_x_b, w_branch_b, w_out, norm_ffn, w_route_group, b_route_group, w_route_expert, b_route_expert, expert_gate, expert_up, expert_down, norm_ple, w_ple_gate, w_ple_proj, norm_final):
    l = 0
    bsz, seq, _ = x_prompt.shape
    dbsz, dseq, _ = x_sample.shape
    row = lambda vec: vec.reshape(1, -1).astype(F32)
    col = lambda vec: vec.reshape(-1, 1).astype(F32)
    scan = lambda arr, axis=0: _swap_channel_order(arr, axis, True)

    o1 = POOL_WIDTH
    o2 = o1 + SHIFT_WIDTH
    w_z = _swap_shift_order(w_in[l][:, o1:o2], 1, True).astype(BF16)
    zeros_lora = jnp.zeros((RWKV_WIDTH, LORA_PAIR // 2), F32)
    w2t_pad = jnp.concatenate([scan(decay_w2[l], 1).T, zeros_lora], axis=1).astype(BF16)
    a2t_pad = jnp.concatenate([zeros_lora, scan(iclr_a2[l], 1).T], axis=1).astype(BF16)
    w_router = jnp.concatenate(
        [w_route_expert[l], w_route_group[l],
         jnp.zeros((D_MODEL, ROUTER_LANES - N_EXPERTS - N_GROUPS), F32)], axis=1)
    wr_hi = w_router.astype(BF16)
    wr_lo = (w_router - wr_hi.astype(F32)).astype(BF16)
    b_router = jnp.concatenate(
        [b_route_expert[l], b_route_group[l],
         jnp.zeros((ROUTER_LANES - N_EXPERTS - N_GROUPS,), F32)]).reshape(1, -1)
    prep_w = [row(norm_mix[l]), w_in[l][:, :o1].astype(BF16), w_z, w_in[l][:, o2:].astype(BF16),
              col(_swap_shift_order(shift_mu[l], 0, True)), col(scan(decay_w0[l])), w2t_pad,
              col(scan(iclr_a0[l])), a2t_pad, scan(gate_g2[l], 1).T.astype(BF16), col(scan(k_k[l])),
              col(scan(k_a[l])), col(scan(r_k[l].reshape(-1))), pool_mix[l].astype(BF16),
              row(pool_scale[l]), w_branch_a[l].astype(BF16)]
    post_w = [col(scan(ln_x_w[l])), col(scan(ln_x_b[l])), scan(w_branch_b[l]).astype(BF16),
              w_out[l].astype(BF16), row(norm_ffn[l]), wr_hi, wr_lo, b_router]
    moe_w = [expert_gate[l], expert_up[l], expert_down[l],
             row(norm_ple[l]), w_ple_gate[l].astype(BF16), w_ple_proj[l].astype(BF16), row(norm_final)]

    tm_p = 256
    tiles_p = seq // tm_p
    x_p = x_prompt.reshape(bsz * seq, D_MODEL)
    outs = _mix_prep(x_p, jnp.zeros((bsz * SHIFT_WIDTH, LANES), F32),
                     jnp.zeros((bsz * 16, POOL_WIDTH), F32), prep_w,
                     nb=bsz, tiles=tiles_p, tm=tm_p, s=1, pos0=0, cm_index=lambda b, t: (b, 0, t))
    u_p, zl_p, ops, g, bon, pa, sgb = outs
    o_p, s_p = _wkv_scan_prompt(ops, tt=LANES)
    tm_post = 512
    n_p = bsz * seq
    n_s = dbsz * dseq
    x1_p, hx_p, rt_p, cnt_p = _post(
        o_p, bon, g, pa, sgb, x_p, jnp.zeros((SUBLANES, ROUTER_LANES), F32), post_w,
        tm=tm_post, sub=256, cm_index=lambda i: (i // (seq // tm_post), 0, i % (seq // tm_post)))
    wkv_p = s_p.reshape(KEY_HALF, 2, KEY_HALF, 2, bsz, HEADS).transpose(4, 5, 3, 2, 1, 0)
    wkv_p = wkv_p.reshape(bsz, HEADS, HEAD_DIM, HEAD_DIM)

    n_s = dbsz * dseq
    x_s = x_sample.transpose(1, 0, 2).reshape(n_s, D_MODEL)
    p_s = p_sample[l].transpose(1, 0, 2).reshape(n_s, PLE_DIM)
    uc0_s = jnp.concatenate(
        [jnp.zeros((dbsz, POOL_WIDTH), F32),
         state_pool[l].transpose(1, 0, 2).reshape(POOL_STATE * dbsz, POOL_WIDTH)], axis=0)
    zc0_s = _swap_shift_order(state_shift[l], 1, True).T
    outs = _mix_prep(x_s, zc0_s, uc0_s, prep_w, nb=1, tiles=dseq, tm=dbsz, s=dbsz, pos0=PAST_LEN,
                     cm_index=lambda b, t: (t, 0, 0))
    u_s, zl_s, ops, g, bon, pa, sgb = outs
    s0_s = state_wkv[l].reshape(dbsz, HEADS, 2, KEY_HALF, 2, KEY_HALF).transpose(1, 5, 4, 3, 2, 0)
    s0_s = s0_s.reshape(HEADS, HEAD_DIM, HEAD_DIM, dbsz)
    o_s, s_s = _wkv_scan_sample(ops, s0_s)
    x1_s, hx_s, rt_s, cnt_all = _post(o_s, bon, g, pa, sgb, x_s, cnt_p, post_w,
                                      tm=dbsz, sub=dbsz, cm_index=lambda i: (i, 0, 0))

    y_p, y_s = _moe_tail(
        [(hx_p, rt_p, x1_p, p_prompt[l].reshape(n_p, PLE_DIM), 512), (hx_s, rt_s, x1_s, p_s, n_s)],
        cnt_all, *moe_w, tm=256)
    wkv_s = s_s.reshape(HEADS, KEY_HALF, 2, KEY_HALF, 2, dbsz).transpose(5, 0, 4, 3, 2, 1)
    wkv_s = wkv_s.reshape(dbsz, HEADS, HEAD_DIM, HEAD_DIM)

    y_prompt = y_p.reshape(bsz, seq, D_MODEL)
    y_sample = y_s.reshape(dseq, dbsz, D_MODEL).transpose(1, 0, 2)
    pool_prompt = u_p.reshape(bsz, seq, POOL_WIDTH)[:, seq - POOL_STATE:]
    u_s_bt = u_s.reshape(dseq, dbsz, POOL_WIDTH).transpose(1, 0, 2)
    pool_sample = jnp.concatenate([state_pool[l][:, dseq:], u_s_bt], axis=1)
    shift_prompt = _swap_shift_order(zl_p[:, SUBLANES - 1, :], 1, False)
    shift_sample = _swap_shift_order(zl_s[0], 1, False)
    return (y_prompt, y_sample, pool_prompt[None], shift_prompt[None], wkv_p[None],
            pool_sample[None], shift_sample[None], wkv_s[None])
```

```python
import functools
import math

import jax
import jax.numpy as jnp
from jax import lax
from jax.experimental import pallas as pl
from jax.experimental.pallas import tpu as pltpu

F32 = jnp.float32
BF16 = jnp.bfloat16

D_MODEL = 1024
PLE_DIM = 256
POOL_WIDTH = 512
POOL_WINDOWS = (2, 4, 8, 16)
POOL_GROUP_DIM = 128
POOL_STATE = 15
RWKV_WIDTH = 512
HEAD_DIM = 64
HEADS = 8
LORA_PAIR = 128
GATE_LORA = 128
SHIFT_WIDTH = 3 * RWKV_WIDTH + LORA_PAIR + GATE_LORA
IN_WIDTH = POOL_WIDTH + SHIFT_WIDTH + 2 * D_MODEL
N_GROUPS = 4
EXPERTS_PER_GROUP = 8
N_EXPERTS = 32
D_EXPERT = 256
RMS_EPS = 1e-6
GN_EPS = 64e-5
PAST_LEN = 16384

LANES = 128
SUBLANES = 8
KEY_HALF = HEAD_DIM // 2
SCAN_OPERANDS = 6
KEY_GROUP = 32
ROUTER_LANES = 128
GROUP_TILE = 1024
EXPERTS_PER_STEP = 2
DECAY_SCALE = math.exp(-0.5)
COMB_LANE0 = 8
MOE_ROW = D_MODEL + ROUTER_LANES
NEG_BIG = -1e30
VMEM_LIMIT = 56 * 1024 * 1024

def _swap_channel_order(x, axis, to_scan):
    x = jnp.moveaxis(x, axis, -1)
    lead = x.shape[:-1]
    split = (HEADS, 2, KEY_HALF) if to_scan else (KEY_HALF, 2, HEADS)
    x = jnp.swapaxes(x.reshape(lead + split), -1, -3).reshape(lead + (RWKV_WIDTH,))
    return jnp.moveaxis(x, -1, axis)


def _swap_shift_order(x, axis, to_scan):
    x = jnp.moveaxis(x, axis, -1)
    lead = x.shape[:-1]
    rkv = x[..., :3 * RWKV_WIDTH].reshape(lead + (3, RWKV_WIDTH))
    rkv = _swap_channel_order(rkv, -1, to_scan).reshape(lead + (3 * RWKV_WIDTH,))
    return jnp.moveaxis(jnp.concatenate([rkv, x[..., 3 * RWKV_WIDTH:]], axis=-1), -1, axis)


def _dot(a, b):
    return jnp.dot(a, b, preferred_element_type=F32)


def _split_bf16(x):
    hi = x.astype(BF16)
    return hi, (x - hi.astype(F32)).astype(BF16)


def _head_sum(x):
    x3 = x.reshape(HEAD_DIM, HEADS, x.shape[1])
    s = jnp.sum(x3, axis=0, keepdims=True)
    return jnp.broadcast_to(s, x3.shape).reshape(x.shape)


def _rmsnorm(x, g):
    return x * lax.rsqrt(jnp.mean(x * x, axis=-1, keepdims=True) + RMS_EPS) * g


def _sigmoid(x):
    return 0.5 * jnp.tanh(0.5 * x) + 0.5


def _mix_prep_kernel(x_ref, zc0_ref, uc0_ref, nmix_ref, wu_ref, wz_ref, wgab_ref, mu_ref, w0_ref,
                     w2t_ref, a0_ref, a2t_ref, g2t_ref, kk_ref, ka_ref, rk_ref, mix_ref,
                     pscale_ref, wa_ref,
                     u_ref, zl_ref, q_ref, g_ref, bon_ref,
                     pa_ref, sgb_ref, zc, uext, *, tm, s, pos0):
    t = pl.program_id(1)
    up = 16 * s

    @pl.when(t == 0)
    def _():
        zc[...] = zc0_ref[...]
        uext[0:up] = uc0_ref[...]

    x = x_ref[...]
    h = _rmsnorm(x, nmix_ref[...]).astype(BF16)
    u = _dot(h, wu_ref[...])
    gab = _dot(h, wgab_ref[...])
    u_ref[...] = u
    sgb_ref[...] = _sigmoid(gab[:, D_MODEL:])
    uext[up:up + tm] = u

    z = _dot(h, wz_ref[...])
    zl_ref[0] = z[tm - max(SUBLANES, s):tm]
    z_t = z.T
    if s == 1:
        lane = lax.broadcasted_iota(jnp.int32, z_t.shape, 1)
        zprev = jnp.where(lane == 0, zc[:, LANES - 1:LANES], pltpu.roll(z_t, 1, axis=1))
    else:
        zprev = zc[...]
    zc[...] = z_t[:, tm - LANES:tm]
    zm = z_t + (zprev - z_t) * mu_ref[...]
    r = zm[0:RWKV_WIDTH]
    k = zm[RWKV_WIDTH:2 * RWKV_WIDTH]
    v = zm[2 * RWKV_WIDTH:3 * RWKV_WIDTH]
    lora_in = zm[3 * RWKV_WIDTH:3 * RWKV_WIDTH + LORA_PAIR]
    gd = zm[3 * RWKV_WIDTH + LORA_PAIR:SHIFT_WIDTH]
    dw = _dot(w2t_ref[...], jnp.tanh(lora_in).astype(BF16))
    da = _dot(a2t_ref[...], lora_in.astype(BF16))
    decay = jnp.exp(-DECAY_SCALE * _sigmoid(w0_ref[...] + dw))
    a = _sigmoid(a0_ref[...] + da)
    kk = k * kk_ref[...]
    kk = kk / jnp.maximum(jnp.sqrt(_head_sum(kk * kk)), 1e-12)
    k2 = k * (1.0 + (a - 1.0) * ka_ref[...])
    q_ref[0, 0] = -kk
    q_ref[1, 0] = decay
    q_ref[2, 0] = kk * a
    q_ref[3, 0] = k2
    q_ref[4, 0] = r
    q_ref[5, 0] = v
    g_ref[0] = _dot(g2t_ref[...], _sigmoid(gd).astype(BF16))
    bon_ref[0] = _head_sum(r * k2 * rk_ref[...]) * v

    rows = lax.broadcasted_iota(jnp.int32, (tm, POOL_GROUP_DIM), 0)
    if s > 1:
        rows = rows // s
    pos = pos0 + t * (tm // s) + rows
    ys = []
    for gi, wnd in enumerate(POOL_WINDOWS):
        lanes = slice(gi * POOL_GROUP_DIM, (gi + 1) * POOL_GROUP_DIM)
        wsum = uext[:, lanes]
        span = 1
        while span < wnd:
            wsum = wsum + pltpu.roll(wsum, span * s, axis=0)
            span *= 2
        cur = uext[pl.ds(up, tm), lanes]
        cnt = jnp.minimum(pos + 1, wnd).astype(F32)
        pooled = wsum[up:up + tm] / cnt - cur
        ys.append(_dot(pooled.astype(BF16), mix_ref[gi]))
    y = jnp.concatenate(ys, axis=-1) * pscale_ref[...]
    pa_ref[...] = _sigmoid(gab[:, :D_MODEL]) * _dot(y.astype(BF16), wa_ref[...])

    uext[0:up] = uext[tm:tm + up]


def _mix_prep(x2d, zc0, uc0, wts, *, nb, tiles, tm, s, pos0, cm_index):
    n = x2d.shape[0]
    up = 16 * s
    row = lambda b, t: (b * tiles + t, 0)
    full = lambda arr: pl.BlockSpec(arr.shape, lambda b, t: (0,) * arr.ndim)
    in_specs = [
        pl.BlockSpec((tm, D_MODEL), row),
        pl.BlockSpec((SHIFT_WIDTH, LANES), lambda b, t: (b, 0)),
        pl.BlockSpec((up, POOL_WIDTH), lambda b, t: (b, 0)),
    ] + [full(w) for w in wts]
    tok = lambda width: pl.BlockSpec((tm, width), row)
    cm_shape = (nb, RWKV_WIDTH, tiles * tm) if s == 1 else (tiles, RWKV_WIDTH, tm)
    cm = pl.BlockSpec((1, RWKV_WIDTH, tm), cm_index)
    cm_stack = pl.BlockSpec((SCAN_OPERANDS, 1, RWKV_WIDTH, tm), lambda b, t: (0,) + cm_index(b, t))
    zrows = max(SUBLANES, s)
    out_specs = [tok(POOL_WIDTH), pl.BlockSpec((1, zrows, SHIFT_WIDTH), lambda b, t: (b, 0, 0)),
                 cm_stack, cm, cm] + [tok(D_MODEL)] * 2
    out_shape = [jax.ShapeDtypeStruct((n, POOL_WIDTH), F32),
                 jax.ShapeDtypeStruct((nb, zrows, SHIFT_WIDTH), F32),
                 jax.ShapeDtypeStruct((SCAN_OPERANDS,) + cm_shape, F32)] \
        + [jax.ShapeDtypeStruct(cm_shape, F32)] * 2 \
        + [jax.ShapeDtypeStruct((n, D_MODEL), F32)] * 2
    return pl.pallas_call(
        functools.partial(_mix_prep_kernel, tm=tm, s=s, pos0=pos0),
        grid=(nb, tiles),
        in_specs=in_specs,
        out_specs=out_specs,
        out_shape=out_shape,
        scratch_shapes=[pltpu.VMEM((SHIFT_WIDTH, LANES), F32),
                        pltpu.VMEM((up + tm, POOL_WIDTH), F32)],
        compiler_params=pltpu.CompilerParams(
            dimension_semantics=("arbitrary", "arbitrary"), vmem_limit_bytes=VMEM_LIMIT),
        name="mix_prep",
    )(x2d, zc0, uc0, *wts)


def _wkv_step(s_ref, row, vv):
    groups = HEAD_DIM // KEY_GROUP

    def sa_pass(g, acc):
        for kk in range(KEY_GROUP):
            acc = acc + s_ref[g * KEY_GROUP + kk] * row(0, g, kk)
        return acc
    sa = lax.fori_loop(0, groups, sa_pass, jnp.zeros(vv.shape, F32))

    def update_pass(g, acc):
        for kk in range(KEY_GROUP):
            kp = g * KEY_GROUP + kk
            sn = s_ref[kp] * row(1, g, kk) + sa * row(2, g, kk) + vv * row(3, g, kk)
            s_ref[kp] = sn
            acc = acc + sn * row(4, g, kk)
        return acc
    return lax.fori_loop(0, groups, update_pass, jnp.zeros(vv.shape, F32))


def _swap_major_sublane(x):
    return jnp.swapaxes(x, 0, 1)


def _wkv_scan_prompt_kernel(q_ref, o_ref, sout_ref, s_ref, kv_ref, vv_ref, ov_ref, *, tt):
    t = pl.program_id(0)
    q = pl.program_id(1)

    @pl.when((t == 0) & (q == 0))
    def _():
        s_ref[...] = jnp.zeros_like(s_ref)

    def gather_t(base_lo, base_hi):
        pieces = [q_ref[0, bb, pl.ds(base, SUBLANES), :] for base in (base_lo, base_hi) for bb in range(8)]
        return jnp.concatenate(pieces, axis=0).T

    @pl.when(q < SCAN_OPERANDS - 1)
    def _():
        def kgroup(gi, c):
            slabs = []
            for kk in range(SUBLANES):
                base = pl.multiple_of((gi * SUBLANES + kk) * SUBLANES, SUBLANES)
                slabs.append(gather_t(base, base))
            kv_ref[q, :, pl.ds(pl.multiple_of(gi * SUBLANES, SUBLANES), SUBLANES), :] = \
                _swap_major_sublane(jnp.stack(slabs))
            return c
        lax.fori_loop(0, HEAD_DIM // SUBLANES, kgroup, 0, unroll=True)

    @pl.when(q == SCAN_OPERANDS - 1)
    def _():
        def vgroup(gi, c):
            slabs = []
            for vi in range(SUBLANES):
                base = pl.multiple_of((gi * SUBLANES + vi) * 2 * SUBLANES, 2 * SUBLANES)
                slabs.append(gather_t(base, base + SUBLANES))
            vv_ref[:, pl.ds(pl.multiple_of(gi * SUBLANES, SUBLANES), SUBLANES), :] = \
                _swap_major_sublane(jnp.stack(slabs))
            return c
        lax.fori_loop(0, KEY_HALF // SUBLANES, vgroup, 0, unroll=True)

        def step(i, c):
            row = lambda qi, g, kk: kv_ref[qi, i, pl.ds(g * KEY_GROUP + kk, 1), :]
            ov_ref[i] = _wkv_step(s_ref, row, vv_ref[i])
            return c
        lax.fori_loop(0, tt, step, 0)

        def ogroup(gi, c):
            g0 = pl.multiple_of(gi * SUBLANES, SUBLANES)
            x = _swap_major_sublane(ov_ref[:, pl.ds(g0, SUBLANES), :])
            for vi in range(SUBLANES):
                xt = x[vi].T
                base = pl.multiple_of((gi * SUBLANES + vi) * 2 * SUBLANES, 2 * SUBLANES)
                for vh in range(2):
                    for bb in range(8):
                        r0 = (vh * 8 + bb) * SUBLANES
                        o_ref[bb, pl.ds(base + vh * SUBLANES, SUBLANES), :] = xt[r0:r0 + SUBLANES]
            return c
        lax.fori_loop(0, KEY_HALF // SUBLANES, ogroup, 0, unroll=True)

        @pl.when(t == pl.num_programs(0) - 1)
        def _():
            sout_ref[...] = s_ref[...]


def _wkv_scan_prompt(ops, *, tt):
    _, nb, _, t_len = ops.shape
    sspec = pl.BlockSpec((HEAD_DIM, KEY_HALF, LANES), lambda ti, qi: (0, 0, 0))
    return pl.pallas_call(
        functools.partial(_wkv_scan_prompt_kernel, tt=tt),
        grid=(t_len // tt, SCAN_OPERANDS),
        in_specs=[pl.BlockSpec((1, nb, RWKV_WIDTH, tt), lambda ti, qi: (qi, 0, 0, ti))],
        out_specs=[pl.BlockSpec((nb, RWKV_WIDTH, tt), lambda ti, qi: (0, 0, ti)), sspec],
        out_shape=[jax.ShapeDtypeStruct((nb, RWKV_WIDTH, t_len), F32),
                   jax.ShapeDtypeStruct((HEAD_DIM, KEY_HALF, LANES), F32)],
        scratch_shapes=[pltpu.VMEM((HEAD_DIM, KEY_HALF, LANES), F32),
                        pltpu.VMEM((SCAN_OPERANDS - 1, tt, HEAD_DIM, LANES), F32),
                        pltpu.VMEM((tt, KEY_HALF, LANES), F32),
                        pltpu.VMEM((tt, KEY_HALF, LANES), F32)],
        compiler_params=pltpu.CompilerParams(
            dimension_semantics=("arbitrary", "arbitrary"), vmem_limit_bytes=VMEM_LIMIT),
        name="wkv_scan_prompt",
    )(ops)


def _wkv_scan_sample_kernel(q_ref, s0_ref, o_ref, sout_ref, s_ref, *, t_len):
    assert KEY_GROUP == KEY_HALF
    h = pl.program_id(0)

    def batch_to_lanes(x):
        y = _swap_major_sublane(x)
        y = jnp.stack([y[j].T for j in range(y.shape[0])])
        return _swap_major_sublane(y)

    def lanes_to_batch(x):
        y = _swap_major_sublane(x)
        y = jnp.stack([y[j].T for j in range(y.shape[0])])
        return _swap_major_sublane(y)

    s_ref[...] = batch_to_lanes(s0_ref[:, 0])
    halves = lambda ref_at: [ref_at(pl.ds(half * SUBLANES + h, KEY_HALF, stride=2 * SUBLANES))
                             for half in range(2)]
    for i in range(t_len):
        row = lambda qi, g, kk, i=i: q_ref[qi, i, pl.ds(kk * 2 * SUBLANES + g * SUBLANES + h, 1), :]
        vv = jnp.concatenate(halves(lambda rows, i=i: q_ref[SCAN_OPERANDS - 1, i, rows, :]), axis=0)
        o = _wkv_step(s_ref, row, vv)
        o_ref[i, pl.ds(h, KEY_HALF, stride=2 * SUBLANES), :] = o[0:KEY_HALF]
        o_ref[i, pl.ds(SUBLANES + h, KEY_HALF, stride=2 * SUBLANES), :] = o[KEY_HALF:HEAD_DIM]
    sout_ref[:, 0] = lanes_to_batch(s_ref[...])


def _wkv_scan_sample(ops, s0):
    t_len = ops.shape[1]
    spec = pl.BlockSpec(ops.shape[1:], lambda h: (0, 0, 0))
    sspec = pl.BlockSpec((s0.shape[0], 1, HEAD_DIM, HEAD_DIM), lambda h: (0, h, 0, 0))
    return pl.pallas_call(
        functools.partial(_wkv_scan_sample_kernel, t_len=t_len),
        grid=(HEADS,),
        in_specs=[pl.BlockSpec(ops.shape, lambda h: (0, 0, 0, 0)), sspec],
        out_specs=[spec, sspec],
        out_shape=[jax.ShapeDtypeStruct(ops.shape[1:], F32), jax.ShapeDtypeStruct(s0.shape, F32)],
        scratch_shapes=[pltpu.VMEM((HEAD_DIM, HEAD_DIM, LANES), F32)],
        compiler_params=pltpu.CompilerParams(
            dimension_semantics=("arbitrary",), vmem_limit_bytes=VMEM_LIMIT),
        name="wkv_scan_sample",
    )(ops, s0)


def _post_kernel(o_ref, bon_ref, g_ref, pa_ref, sgb_ref, x_ref, cnt0_ref, lnw_ref, lnb_ref,
                 wb_ref, wout_ref, nffn_ref, wrh_ref, wrl_ref, br_ref, x1_ref, h2_ref, rt_ref, cnt_ref,
                 carry, *, sub):
    @pl.when(pl.program_id(0) == 0)
    def _():
        carry[...] = cnt0_ref[...]

    parts = [pl.ds(part * sub, sub) for part in range(o_ref.shape[2] // sub)]
    logits = [_post_project(rows, o_ref, bon_ref, g_ref, pa_ref, sgb_ref, x_ref, lnw_ref, lnb_ref,
                            wb_ref, wout_ref, nffn_ref, wrh_ref, wrl_ref, br_ref, x1_ref, h2_ref)
              for rows in parts]
    prev = carry[0:1, :]
    for rows, lg in zip(parts, logits):
        prev = _post_route(rows, prev, lg, h2_ref, rt_ref)
    carry[...] = jnp.broadcast_to(prev, carry.shape)
    cnt_ref[...] = jnp.broadcast_to(prev, cnt_ref.shape)


def _post_project(rows, o_ref, bon_ref, g_ref, pa_ref, sgb_ref, x_ref, lnw_ref, lnb_ref,
                  wb_ref, wout_ref, nffn_ref, wrh_ref, wrl_ref, br_ref, x1_ref, h2_ref):
    o = o_ref[0, :, rows]
    mean = _head_sum(o) * (1.0 / HEAD_DIM)
    d = o - mean
    var = _head_sum(d * d) * (1.0 / HEAD_DIM)
    on = d * lax.rsqrt(var + GN_EPS) * lnw_ref[...] + lnb_ref[...]
    yb = ((on + bon_ref[0, :, rows]) * g_ref[0, :, rows]).astype(BF16)
    mb = lax.dot_general(yb, wb_ref[...], (((0,), (0,)), ((), ())), preferred_element_type=F32)
    merged = pa_ref[rows, :] + sgb_ref[rows, :] * mb
    x1 = x_ref[rows, :] + _dot(merged.astype(BF16), wout_ref[...])
    x1_ref[rows, :] = x1
    h2 = _rmsnorm(x1, nffn_ref[...])
    h2_ref[rows, 0:D_MODEL] = h2

    h_hi, h_lo = _split_bf16(h2)
    return (_dot(h_hi, wrh_ref[...]) + _dot(h_lo, wrh_ref[...]) + _dot(h_hi, wrl_ref[...])
            + br_ref[...])


def _post_route(rows, prev_count, logits, h2_ref, rt_ref):
    ln = lax.broadcasted_iota(jnp.int32, logits.shape, 1)
    is_group = (ln >= N_EXPERTS) & (ln < N_EXPERTS + N_GROUPS)
    gl = jnp.where(is_group, logits, NEG_BIG)
    gmax = jnp.max(gl, axis=-1, keepdims=True)
    gsel = jnp.min(jnp.where(gl == gmax, ln, ROUTER_LANES), axis=-1, keepdims=True) - N_EXPERTS
    den = jnp.sum(jnp.where(is_group, jnp.exp(gl - gmax), 0.0), axis=-1, keepdims=True)
    pg = 1.0 / den
    in_group = (ln < N_EXPERTS) & ((ln // EXPERTS_PER_GROUP) == gsel)
    el = jnp.where(in_group, logits, NEG_BIG)
    m1 = jnp.max(el, axis=-1, keepdims=True)
    i1 = jnp.min(jnp.where(el == m1, ln, ROUTER_LANES), axis=-1, keepdims=True)
    el2 = jnp.where(ln == i1, NEG_BIG, el)
    m2 = jnp.max(el2, axis=-1, keepdims=True)
    i2 = jnp.min(jnp.where(el2 == m2, ln, ROUTER_LANES), axis=-1, keepdims=True)
    e2 = jnp.exp(m2 - m1)
    p1 = 1.0 / (1.0 + e2)
    p2 = e2 / (1.0 + e2)

    tm = logits.shape[0]
    sel = ln == gsel
    tri = (lax.broadcasted_iota(jnp.int32, (tm, tm), 1)
           < lax.broadcasted_iota(jnp.int32, (tm, tm), 0)).astype(BF16)
    before = prev_count + _dot(tri, sel.astype(BF16))
    rank = jnp.sum(jnp.where(sel, before, 0.0), axis=-1, keepdims=True)
    first = gsel * EXPERTS_PER_GROUP - COMB_LANE0
    rt = (jnp.where(ln == 0, gsel.astype(F32), 0.0) + jnp.where(ln == 1, rank, 0.0)
          + jnp.where(ln == i1 - first, p1 * pg, 0.0) + jnp.where(ln == i2 - first, p2 * pg, 0.0))
    rt_ref[rows, :] = rt
    h2_ref[rows, D_MODEL:MOE_ROW] = rt
    return prev_count + jnp.sum(sel.astype(F32), axis=0, keepdims=True)


def _post(o, bon, g, pa, sgb, x2d, cnt0, wts, *, tm, sub, cm_index):
    n = x2d.shape[0]
    row = lambda i: (i, 0)
    tok = lambda width: pl.BlockSpec((tm, width), row)
    cm = pl.BlockSpec((1, RWKV_WIDTH, tm), cm_index)
    full = lambda arr: pl.BlockSpec(arr.shape, lambda i: (0, 0))
    return pl.pallas_call(
        functools.partial(_post_kernel, sub=sub),
        grid=(n // tm,),
        in_specs=[cm] * 3 + [tok(D_MODEL)] * 3 + [full(cnt0)] + [full(w) for w in wts],
        out_specs=[tok(D_MODEL), tok(MOE_ROW), tok(ROUTER_LANES),
                   pl.BlockSpec((SUBLANES, ROUTER_LANES), lambda i: (0, 0))],
        out_shape=[jax.ShapeDtypeStruct((n, D_MODEL), F32),
                   jax.ShapeDtypeStruct((n, MOE_ROW), F32),
                   jax.ShapeDtypeStruct((n, ROUTER_LANES), F32),
                   jax.ShapeDtypeStruct((SUBLANES, ROUTER_LANES), F32)],
        scratch_shapes=[pltpu.VMEM((SUBLANES, ROUTER_LANES), F32)],
        compiler_params=pltpu.CompilerParams(
            dimension_semantics=("arbitrary",), vmem_limit_bytes=VMEM_LIMIT),
        name="post",
    )(o, bon, g, pa, sgb, x2d, cnt0, *wts)


def _route_tables(cnt, n_tiles_max):
    counts = cnt[0, :N_GROUPS].astype(jnp.int32)
    tiles_g = (counts + GROUP_TILE - 1) // GROUP_TILE
    tile_end = jnp.cumsum(tiles_g)
    row0 = (tile_end - tiles_g) * GROUP_TILE
    tile_ids = jnp.arange(n_tiles_max, dtype=jnp.int32)
    tile_group = jnp.minimum(jnp.sum(tile_ids[:, None] >= tile_end[None, :], axis=-1),
                             N_GROUPS - 1).astype(jnp.int32)
    return tile_group, tile_end[N_GROUPS - 1:].astype(jnp.int32), row0


def _sorted_rows(rt, row0):
    g = rt[:, 0].astype(jnp.int32)
    r = rt[:, 1].astype(jnp.int32)
    return r + jnp.sum(jnp.where(g[:, None] == jnp.arange(N_GROUPS), row0, 0), axis=-1)


def _dispatch_kernel(pos_ref, h_ref, xs0_ref, xs_ref, sem, *, td):
    del xs0_ref
    for r in range(td):
        pltpu.make_async_copy(h_ref.at[pl.ds(r, 1)], xs_ref.at[pl.ds(pos_ref[r], 1)],
                              sem).start(priority=r % 2)
    pltpu.make_async_copy(h_ref, xs_ref.at[pl.ds(0, td)], sem).wait()


def _dispatch(h, pos, xs_in, *, td):
    n = h.shape[0]
    return pl.pallas_call(
        functools.partial(_dispatch_kernel, td=td),
        grid=(n // td,),
        in_specs=[pl.BlockSpec((td,), lambda i: (i,), memory_space=pltpu.SMEM),
                  pl.BlockSpec((td, MOE_ROW), lambda i: (i, 0)),
                  pl.BlockSpec(memory_space=pl.ANY)],
        out_specs=pl.BlockSpec(memory_space=pl.ANY),
        out_shape=jax.ShapeDtypeStruct(xs_in.shape, F32),
        scratch_shapes=[pltpu.SemaphoreType.DMA(())],
        input_output_aliases={2: 0},
        compiler_params=pltpu.CompilerParams(
            dimension_semantics=("arbitrary",), vmem_limit_bytes=VMEM_LIMIT),
        name="moe_dispatch",
    )(pos, h, xs_in)


def _experts_kernel(tg_ref, nu_ref, xs_ref, wg_ref, wu_ref, wd_ref, ys_ref, acc, xb):
    t = pl.program_id(0)
    e = pl.program_id(1)
    last_e = pl.num_programs(1) - 1

    @pl.when(t < nu_ref[0])
    def _():
        @pl.when(e == 0)
        def _():
            acc[...] = jnp.zeros_like(acc)
            xb[...] = xs_ref[:, 0:D_MODEL].astype(BF16)

        x = xb[...]
        rt = xs_ref[:, D_MODEL:MOE_ROW]
        ln = lax.broadcasted_iota(jnp.int32, rt.shape, 1)
        y = acc[...]
        for j in range(EXPERTS_PER_STEP):
            gate = _dot(x, wg_ref[j].astype(BF16))
            hid = gate * _sigmoid(gate) * _dot(x, wu_ref[j].astype(BF16))
            ye = _dot(hid.astype(BF16), wd_ref[j].astype(BF16))
            lane = COMB_LANE0 + e * EXPERTS_PER_STEP + j
            ce = jnp.sum(jnp.where(ln == lane, rt, 0.0), axis=-1, keepdims=True)
            y = y + ce * ye
        acc[...] = y

        @pl.when(e == last_e)
        def _():
            ys_ref[...] = acc[...]

    @pl.when((t >= nu_ref[0]) & (e == last_e))
    def _():
        ys_ref[...] = jnp.zeros_like(ys_ref)


def _experts(xs, tile_group, n_used, wg, wu, wd):
    n_rows = xs.shape[0]
    tile = lambda t, e, tg, nu: (jnp.minimum(t, nu[0] - 1), 0)
    steps = EXPERTS_PER_GROUP // EXPERTS_PER_STEP
    wsel = lambda t, e, tg, nu: (tg[jnp.minimum(t, nu[0] - 1)] * steps + e, 0, 0)
    return pl.pallas_call(
        _experts_kernel,
        grid_spec=pltpu.PrefetchScalarGridSpec(
            num_scalar_prefetch=2,
            grid=(n_rows // GROUP_TILE, steps),
            in_specs=[pl.BlockSpec((GROUP_TILE, MOE_ROW), tile),
                      pl.BlockSpec((EXPERTS_PER_STEP, D_MODEL, D_EXPERT), wsel),
                      pl.BlockSpec((EXPERTS_PER_STEP, D_MODEL, D_EXPERT), wsel),
                      pl.BlockSpec((EXPERTS_PER_STEP, D_EXPERT, D_MODEL), wsel)],
            out_specs=pl.BlockSpec((GROUP_TILE, D_MODEL), lambda t, e, tg, nu: (t, 0)),
            scratch_shapes=[pltpu.VMEM((GROUP_TILE, D_MODEL), F32),
                            pltpu.VMEM((GROUP_TILE, D_MODEL), BF16)]),
        out_shape=jax.ShapeDtypeStruct((n_rows, D_MODEL), F32),
        compiler_params=pltpu.CompilerParams(
            dimension_semantics=("arbitrary", "arbitrary"), vmem_limit_bytes=VMEM_LIMIT),
        name="moe_experts",
    )(tile_group, n_used, xs, wg, wu, wd)


def _tail_kernel(pos_ref, posn_ref, x1_ref, p_ref, ys_ref, nple_ref, wpg_ref, wpp_ref,
                 nfin_ref, y_ref, ybuf0, ybuf1, sem, *, tm):
    i = pl.program_id(0)
    last = pl.num_programs(0) - 1
    ybuf = (ybuf0, ybuf1)

    def issue(p_ref, p_off, slot):
        for r in range(tm):
            pltpu.make_async_copy(ys_ref.at[pl.ds(p_ref[p_off + r], 1)],
                                  ybuf[slot].at[pl.ds(r, 1)], sem.at[slot]).start(priority=r % 2)

    def wait(slot):
        pltpu.make_async_copy(ys_ref.at[pl.ds(0, tm)], ybuf[slot], sem.at[slot]).wait()

    def compute(slot):
        rows = pl.ds(slot * tm, tm)
        x2 = x1_ref[rows, :] + ybuf[slot][...]
        h3 = _rmsnorm(x2, nple_ref[...]).astype(BF16)
        ple = (_sigmoid(_dot(h3, wpg_ref[...]))
               * _dot(p_ref[rows, :].astype(BF16), wpp_ref[...]))
        y_ref[rows, :] = _rmsnorm(x2 + ple, nfin_ref[...])

    @pl.when(i == 0)
    def _():
        issue(pos_ref, 0, 0)

    wait(0)
    issue(pos_ref, tm, 1)
    compute(0)
    wait(1)
    issue(posn_ref, 0, 0)
    compute(1)

    @pl.when(i == last)
    def _():
        wait(0)


def _tail(pos, x1, p2d, ys, nple, wpg, wpp, nfin, *, tm):
    n = x1.shape[0]
    n_steps = n // (2 * tm)
    tok = lambda width: pl.BlockSpec((2 * tm, width), lambda i: (i, 0))
    full = lambda arr: pl.BlockSpec(arr.shape, lambda i: (0, 0))
    return pl.pallas_call(
        functools.partial(_tail_kernel, tm=tm),
        grid=(n_steps,),
        in_specs=[pl.BlockSpec((2 * tm,), lambda i: (i,), memory_space=pltpu.SMEM),
                  pl.BlockSpec((tm,), lambda i: (2 * jnp.minimum(i + 1, n_steps - 1),),
                               memory_space=pltpu.SMEM),
                  tok(D_MODEL), tok(PLE_DIM),
                  pl.BlockSpec(memory_space=pl.ANY),
                  full(nple), full(wpg), full(wpp), full(nfin)],
        out_specs=tok(D_MODEL),
        out_shape=jax.ShapeDtypeStruct((n, D_MODEL), F32),
        scratch_shapes=[pltpu.VMEM((tm, D_MODEL), F32), pltpu.VMEM((tm, D_MODEL), F32),
                        pltpu.SemaphoreType.DMA((2,))],
        compiler_params=pltpu.CompilerParams(
            dimension_semantics=("arbitrary",), vmem_limit_bytes=VMEM_LIMIT),
        name="moe_tail",
    )(pos, pos, x1, p2d, ys, nple, wpg, wpp, nfin)


def _moe_tail(groups, cnt, wg, wu, wd, nple, wpg, wpp, nfin, *, tm):
    n_all = sum(g[0].shape[0] for g in groups)
    n_tiles_max = (n_all + GROUP_TILE - 1) // GROUP_TILE + N_GROUPS
    tile_group, n_used, row0 = _route_tables(cnt, n_tiles_max)
    xs = jnp.zeros((n_tiles_max * GROUP_TILE, MOE_ROW), F32)
    poss = []
    for hx, rt, _, _, td in groups:
        poss.append(_sorted_rows(rt, row0))
        xs = _dispatch(hx, poss[-1], xs, td=td)
    ys = _experts(xs, tile_group, n_used, wg, wu, wd)
    return [_tail(pos, x1, p2d, ys, nple, wpg, wpp, nfin, tm=tm)
            for pos, (_, _, x1, p2d, _) in zip(poss, groups)]


def kernel(x_prompt, x_sample, state_pool, state_shift, state_wkv, p_prompt, p_sample, norm_mix, w_in, pool_mix, pool_scale, w_branch_a, shift_mu, decay_w0, decay_w2, iclr_a0, iclr_a2, gate_g2, k_k, k_a, r_k, ln_x_w, ln_x_b, w_branch_b, w_out, norm_ffn, w_route_group, b_route_group, w_route_expert, b_route_expert, expert_gate, expert_up, expert_down, norm_ple, w_ple_gate, w_ple_proj, norm_final):
    l = 0
    bsz, seq, _ = x_prompt.shape
    dbsz, dseq, _ = x_sample.shape
    row = lambda vec: vec.reshape(1, -1).astype(F32)
    col = lambda vec: vec.reshape(-1, 1).astype(F32)
    scan = lambda arr, axis=0: _swap_channel_order(arr, axis, True)

    o1 = POOL_WIDTH
    o2 = o1 + SHIFT_WIDTH
    w_z = _swap_shift_order(w_in[l][:, o1:o2], 1, True).astype(BF16)
    zeros_lora = jnp.zeros((RWKV_WIDTH, LORA_PAIR // 2), F32)
    w2t_pad = jnp.concatenate([scan(decay_w2[l], 1).T, zeros_lora], axis=1).astype(BF16)
    a2t_pad = jnp.concatenate([zeros_lora, scan(iclr_a2[l], 1).T], axis=1).astype(BF16)
    w_router = jnp.concatenate(
        [w_route_expert[l], w_route_group[l],
         jnp.zeros((D_MODEL, ROUTER_LANES - N_EXPERTS - N_GROUPS), F32)], axis=1)
    wr_hi = w_router.astype(BF16)
    wr_lo = (w_router - wr_hi.astype(F32)).astype(BF16)
    b_router = jnp.concatenate(
        [b_route_expert[l], b_route_group[l],
         jnp.zeros((ROUTER_LANES - N_EXPERTS - N_GROUPS,), F32)]).reshape(1, -1)
    prep_w = [row(norm_mix[l]), w_in[l][:, :o1].astype(BF16), w_z, w_in[l][:, o2:].astype(BF16),
              col(_swap_shift_order(shift_mu[l], 0, True)), col(scan(decay_w0[l])), w2t_pad,
              col(scan(iclr_a0[l])), a2t_pad, scan(gate_g2[l], 1).T.astype(BF16), col(scan(k_k[l])),
              col(scan(k_a[l])), col(scan(r_k[l].reshape(-1))), pool_mix[l].astype(BF16),
              row(pool_scale[l]), w_branch_a[l].astype(BF16)]
    post_w = [col(scan(ln_x_w[l])), col(scan(ln_x_b[l])), scan(w_branch_b[l]).astype(BF16),
              w_out[l].astype(BF16), row(norm_ffn[l]), wr_hi, wr_lo, b_router]
    moe_w = [expert_gate[l], expert_up[l], expert_down[l],
             row(norm_ple[l]), w_ple_gate[l].astype(BF16), w_ple_proj[l].astype(BF16), row(norm_final)]

    tm_p = 256
    tiles_p = seq // tm_p
    x_p = x_prompt.reshape(bsz * seq, D_MODEL)
    outs = _mix_prep(x_p, jnp.zeros((bsz * SHIFT_WIDTH, LANES), F32),
                     jnp.zeros((bsz * 16, POOL_WIDTH), F32), prep_w,
                     nb=bsz, tiles=tiles_p, tm=tm_p, s=1, pos0=0, cm_index=lambda b, t: (b, 0, t))
    u_p, zl_p, ops, g, bon, pa, sgb = outs
    o_p, s_p = _wkv_scan_prompt(ops, tt=LANES)
    tm_post = 512
    n_p = bsz * seq
    n_s = dbsz * dseq
    x1_p, hx_p, rt_p, cnt_p = _post(
        o_p, bon, g, pa, sgb, x_p, jnp.zeros((SUBLANES, ROUTER_LANES), F32), post_w,
        tm=tm_post, sub=256, cm_index=lambda i: (i // (seq // tm_post), 0, i % (seq // tm_post)))
    wkv_p = s_p.reshape(KEY_HALF, 2, KEY_HALF, 2, bsz, HEADS).transpose(4, 5, 3, 2, 1, 0)
    wkv_p = wkv_p.reshape(bsz, HEADS, HEAD_DIM, HEAD_DIM)

    n_s = dbsz * dseq
    x_s = x_sample.transpose(1, 0, 2).reshape(n_s, D_MODEL)
    p_s = p_sample[l].transpose(1, 0, 2).reshape(n_s, PLE_DIM)
    uc0_s = jnp.concatenate(
        [jnp.zeros((dbsz, POOL_WIDTH), F32),
         state_pool[l].transpose(1, 0, 2).reshape(POOL_STATE * dbsz, POOL_WIDTH)], axis=0)
    zc0_s = _swap_shift_order(state_shift[l], 1, True).T
    outs = _mix_prep(x_s, zc0_s, uc0_s, prep_w, nb=1, tiles=dseq, tm=dbsz, s=dbsz, pos0=PAST_LEN,
                     cm_index=lambda b, t: (t, 0, 0))
    u_s, zl_s, ops, g, bon, pa, sgb = outs
    o_s, wkv_s = _wkv_scan_sample(ops, state_wkv[l])
    x1_s, hx_s, rt_s, cnt_all = _post(o_s, bon, g, pa, sgb, x_s, cnt_p, post_w,
                                      tm=dbsz, sub=dbsz, cm_index=lambda i: (i, 0, 0))

    y_p, y_s = _moe_tail(
        [(hx_p, rt_p, x1_p, p_prompt[l].reshape(n_p, PLE_DIM), 512), (hx_s, rt_s, x1_s, p_s, n_s)],
        cnt_all, *moe_w, tm=256)

    y_prompt = y_p.reshape(bsz, seq, D_MODEL)
    y_sample = y_s.reshape(dseq, dbsz, D_MODEL).transpose(1, 0, 2)
    pool_prompt = u_p.reshape(bsz, seq, POOL_WIDTH)[:, seq - POOL_STATE:]
    u_s_bt = u_s.reshape(dseq, dbsz, POOL_WIDTH).transpose(1, 0, 2)
    pool_sample = jnp.concatenate([state_pool[l][:, dseq:], u_s_bt], axis=1)
    shift_prompt = _swap_shift_order(zl_p[:, SUBLANES - 1, :], 1, False)
    shift_sample = _swap_shift_order(zl_s[0], 1, False)
    return (y_prompt, y_sample, pool_prompt[None], shift_prompt[None], wkv_p[None],
            pool_sample[None], shift_sample[None], wkv_s[None])
```

```python
import functools
import math

import jax
import jax.numpy as jnp
from jax import lax
from jax.experimental import pallas as pl
from jax.experimental.pallas import tpu as pltpu

F32 = jnp.float32
BF16 = jnp.bfloat16

D_MODEL = 1024
PLE_DIM = 256
POOL_WIDTH = 512
POOL_WINDOWS = (2, 4, 8, 16)
POOL_GROUP_DIM = 128
POOL_STATE = 15
RWKV_WIDTH = 512
HEAD_DIM = 64
HEADS = 8
LORA_PAIR = 128
GATE_LORA = 128
SHIFT_WIDTH = 3 * RWKV_WIDTH + LORA_PAIR + GATE_LORA
IN_WIDTH = POOL_WIDTH + SHIFT_WIDTH + 2 * D_MODEL
N_GROUPS = 4
EXPERTS_PER_GROUP = 8
N_EXPERTS = 32
D_EXPERT = 256
RMS_EPS = 1e-6
GN_EPS = 64e-5
PAST_LEN = 16384

LANES = 128
SUBLANES = 8
KEY_HALF = HEAD_DIM // 2
SCAN_OPERANDS = 6
KEY_GROUP = 32
ROUTER_LANES = 128
PROMPT_TILE = 256
POST_TILE = 512
POST_SUB_TILE = 256
DISPATCH_TILE = 512
TAIL_TILE = 256
GROUP_TILE = 1024
EXPERTS_PER_STEP = 2
DECAY_SCALE = math.exp(-0.5)
COMB_LANE0 = 8
MOE_ROW = D_MODEL + ROUTER_LANES
NEG_BIG = -1e30
VMEM_LIMIT = 56 * 1024 * 1024


def _swap_channel_order(x, axis, to_scan):
    x = jnp.moveaxis(x, axis, -1)
    lead = x.shape[:-1]
    split = (HEADS, 2, KEY_HALF) if to_scan else (KEY_HALF, 2, HEADS)
    x = jnp.swapaxes(x.reshape(lead + split), -1, -3).reshape(lead + (RWKV_WIDTH,))
    return jnp.moveaxis(x, -1, axis)


def _swap_shift_order(x, axis, to_scan):
    x = jnp.moveaxis(x, axis, -1)
    lead = x.shape[:-1]
    rkv = x[..., :3 * RWKV_WIDTH].reshape(lead + (3, RWKV_WIDTH))
    rkv = _swap_channel_order(rkv, -1, to_scan).reshape(lead + (3 * RWKV_WIDTH,))
    return jnp.moveaxis(jnp.concatenate([rkv, x[..., 3 * RWKV_WIDTH:]], axis=-1), -1, axis)


def _dot(a, b):
    return jnp.dot(a, b, preferred_element_type=F32)


def _split_bf16(x):
    hi = x.astype(BF16)
    return hi, (x - hi.astype(F32)).astype(BF16)


def _head_sum(x):
    x3 = x.reshape(HEAD_DIM, HEADS, x.shape[1])
    s = jnp.sum(x3, axis=0, keepdims=True)
    return jnp.broadcast_to(s, x3.shape).reshape(x.shape)


def _rmsnorm(x, g):
    return x * lax.rsqrt(jnp.mean(x * x, axis=-1, keepdims=True) + RMS_EPS) * g


def _sigmoid(x):
    return 0.5 * jnp.tanh(0.5 * x) + 0.5


def _mix_prep_kernel(x_ref, zc0_ref, uc0_ref, nmix_ref, wu_ref, wz_ref, wgab_ref, mu_ref, w0_ref,
                     w2t_ref, a0_ref, a2t_ref, g2t_ref, kk_ref, ka_ref, rk_ref, mix_ref,
                     pscale_ref, wa_ref,
                     u_ref, zl_ref, q_ref, g_ref, bon_ref,
                     pa_ref, sgb_ref, zc, uext, *, tm, s, pos0):
    t = pl.program_id(1)
    up = 16 * s

    @pl.when(t == 0)
    def _():
        zc[...] = zc0_ref[...]
        uext[0:up] = uc0_ref[...]

    x = x_ref[...]
    h = _rmsnorm(x, nmix_ref[...]).astype(BF16)
    u = _dot(h, wu_ref[...])
    gab = _dot(h, wgab_ref[...])
    u_ref[...] = u
    sgb_ref[...] = _sigmoid(gab[:, D_MODEL:])
    uext[up:up + tm] = u

    z = _dot(h, wz_ref[...])
    zl_ref[0] = z[tm - max(SUBLANES, s):tm]
    z_t = z.T
    if s == 1:
        lane = lax.broadcasted_iota(jnp.int32, z_t.shape, 1)
        zprev = jnp.where(lane == 0, zc[:, LANES - 1:LANES], pltpu.roll(z_t, 1, axis=1))
    else:
        zprev = zc[...]
    zc[...] = z_t[:, tm - LANES:tm]
    zm = z_t + (zprev - z_t) * mu_ref[...]
    r = zm[0:RWKV_WIDTH]
    k = zm[RWKV_WIDTH:2 * RWKV_WIDTH]
    v = zm[2 * RWKV_WIDTH:3 * RWKV_WIDTH]
    lora_in = zm[3 * RWKV_WIDTH:3 * RWKV_WIDTH + LORA_PAIR]
    gd = zm[3 * RWKV_WIDTH + LORA_PAIR:SHIFT_WIDTH]
    dw = _dot(w2t_ref[...], jnp.tanh(lora_in).astype(BF16))
    da = _dot(a2t_ref[...], lora_in.astype(BF16))
    decay = jnp.exp(-DECAY_SCALE * _sigmoid(w0_ref[...] + dw))
    a = _sigmoid(a0_ref[...] + da)
    kk = k * kk_ref[...]
    kk = kk / jnp.maximum(jnp.sqrt(_head_sum(kk * kk)), 1e-12)
    k2 = k * (1.0 + (a - 1.0) * ka_ref[...])
    q_ref[0, 0] = -kk
    q_ref[1, 0] = decay
    q_ref[2, 0] = kk * a
    q_ref[3, 0] = k2
    q_ref[4, 0] = r
    q_ref[5, 0] = v
    g_ref[0] = _dot(g2t_ref[...], _sigmoid(gd).astype(BF16))
    bon_ref[0] = _head_sum(r * k2 * rk_ref[...]) * v

    rows = lax.broadcasted_iota(jnp.int32, (tm, POOL_GROUP_DIM), 0)
    if s > 1:
        rows = rows // s
    pos = pos0 + t * (tm // s) + rows
    ys = []
    for gi, wnd in enumerate(POOL_WINDOWS):
        lanes = slice(gi * POOL_GROUP_DIM, (gi + 1) * POOL_GROUP_DIM)
        wsum = uext[:, lanes]
        span = 1
        while span < wnd:
            wsum = wsum + pltpu.roll(wsum, span * s, axis=0)
            span *= 2
        cur = uext[pl.ds(up, tm), lanes]
        cnt = jnp.minimum(pos + 1, wnd).astype(F32)
        pooled = wsum[up:up + tm] / cnt - cur
        ys.append(_dot(pooled.astype(BF16), mix_ref[gi]))
    y = jnp.concatenate(ys, axis=-1) * pscale_ref[...]
    pa_ref[...] = _sigmoid(gab[:, :D_MODEL]) * _dot(y.astype(BF16), wa_ref[...])

    uext[0:up] = uext[tm:tm + up]


def _mix_prep(x2d, zc0, uc0, wts, *, nb, tiles, tm, s, pos0, cm_index):
    n = x2d.shape[0]
    up = 16 * s
    row = lambda b, t: (b * tiles + t, 0)
    full = lambda arr: pl.BlockSpec(arr.shape, lambda b, t: (0,) * arr.ndim)
    in_specs = [
        pl.BlockSpec((tm, D_MODEL), row),
        pl.BlockSpec((SHIFT_WIDTH, LANES), lambda b, t: (b, 0)),
        pl.BlockSpec((up, POOL_WIDTH), lambda b, t: (b, 0)),
    ] + [full(w) for w in wts]
    tok = lambda width: pl.BlockSpec((tm, width), row)
    cm_shape = (nb, RWKV_WIDTH, tiles * tm) if s == 1 else (tiles, RWKV_WIDTH, tm)
    cm = pl.BlockSpec((1, RWKV_WIDTH, tm), cm_index)
    cm_stack = pl.BlockSpec((SCAN_OPERANDS, 1, RWKV_WIDTH, tm), lambda b, t: (0,) + cm_index(b, t))
    zrows = max(SUBLANES, s)
    out_specs = [tok(POOL_WIDTH), pl.BlockSpec((1, zrows, SHIFT_WIDTH), lambda b, t: (b, 0, 0)),
                 cm_stack, cm, cm] + [tok(D_MODEL)] * 2
    out_shape = [jax.ShapeDtypeStruct((n, POOL_WIDTH), F32),
                 jax.ShapeDtypeStruct((nb, zrows, SHIFT_WIDTH), F32),
                 jax.ShapeDtypeStruct((SCAN_OPERANDS,) + cm_shape, F32)] \
        + [jax.ShapeDtypeStruct(cm_shape, F32)] * 2 \
        + [jax.ShapeDtypeStruct((n, D_MODEL), F32)] * 2
    return pl.pallas_call(
        functools.partial(_mix_prep_kernel, tm=tm, s=s, pos0=pos0),
        grid=(nb, tiles),
        in_specs=in_specs,
        out_specs=out_specs,
        out_shape=out_shape,
        scratch_shapes=[pltpu.VMEM((SHIFT_WIDTH, LANES), F32),
                        pltpu.VMEM((up + tm, POOL_WIDTH), F32)],
        compiler_params=pltpu.CompilerParams(
            dimension_semantics=("arbitrary", "arbitrary"), vmem_limit_bytes=VMEM_LIMIT),
        name="mix_prep",
    )(x2d, zc0, uc0, *wts)


def _wkv_step(s_ref, row, vv):
    groups = HEAD_DIM // KEY_GROUP

    def sa_pass(g, acc):
        for kk in range(KEY_GROUP):
            kp = g * KEY_GROUP + kk
            acc = acc + s_ref[kp] * row(0, kp)
        return acc
    sa = lax.fori_loop(0, groups, sa_pass, jnp.zeros(vv.shape, F32))

    def update_pass(g, acc):
        for kk in range(KEY_GROUP):
            kp = g * KEY_GROUP + kk
            sn = s_ref[kp] * row(1, kp) + sa * row(2, kp) + vv * row(3, kp)
            s_ref[kp] = sn
            acc = acc + sn * row(4, kp)
        return acc
    return lax.fori_loop(0, groups, update_pass, jnp.zeros(vv.shape, F32))


def _swap_major_sublane(x):
    return jnp.swapaxes(x, 0, 1)


def _wkv_scan_prompt_kernel(q_ref, o_ref, sout_ref, s_ref, kv_ref, vv_ref, ov_ref, *, tt):
    t = pl.program_id(0)
    q = pl.program_id(1)
    nb = q_ref.shape[1]

    @pl.when((t == 0) & (q == 0))
    def _():
        s_ref[...] = jnp.zeros_like(s_ref)

    def gather_t(base_lo, base_hi):
        pieces = [q_ref[0, bb, pl.ds(base, SUBLANES), :]
                  for base in (base_lo, base_hi) for bb in range(nb)]
        return jnp.concatenate(pieces, axis=0).T

    @pl.when(q < SCAN_OPERANDS - 1)
    def _():
        def kgroup(gi, c):
            slabs = []
            for kk in range(SUBLANES):
                base = pl.multiple_of((gi * SUBLANES + kk) * SUBLANES, SUBLANES)
                slabs.append(gather_t(base, base))
            kv_ref[q, :, pl.ds(pl.multiple_of(gi * SUBLANES, SUBLANES), SUBLANES), :] = \
                _swap_major_sublane(jnp.stack(slabs))
            return c
        lax.fori_loop(0, HEAD_DIM // SUBLANES, kgroup, 0, unroll=True)

    @pl.when(q == SCAN_OPERANDS - 1)
    def _():
        def vgroup(gi, c):
            slabs = []
            for vi in range(SUBLANES):
                base = pl.multiple_of((gi * SUBLANES + vi) * 2 * SUBLANES, 2 * SUBLANES)
                slabs.append(gather_t(base, base + SUBLANES))
            vv_ref[:, pl.ds(pl.multiple_of(gi * SUBLANES, SUBLANES), SUBLANES), :] = \
                _swap_major_sublane(jnp.stack(slabs))
            return c
        lax.fori_loop(0, KEY_HALF // SUBLANES, vgroup, 0, unroll=True)

        def step(i, c):
            row = lambda qi, kp: kv_ref[qi, i, pl.ds(kp, 1), :]
            ov_ref[i] = _wkv_step(s_ref, row, vv_ref[i])
            return c
        lax.fori_loop(0, tt, step, 0)

        def ogroup(gi, c):
            g0 = pl.multiple_of(gi * SUBLANES, SUBLANES)
            x = _swap_major_sublane(ov_ref[:, pl.ds(g0, SUBLANES), :])
            for vi in range(SUBLANES):
                xt = x[vi].T
                base = pl.multiple_of((gi * SUBLANES + vi) * 2 * SUBLANES, 2 * SUBLANES)
                for vh in range(2):
                    for bb in range(nb):
                        r0 = (vh * nb + bb) * SUBLANES
                        o_ref[bb, pl.ds(base + vh * SUBLANES, SUBLANES), :] = xt[r0:r0 + SUBLANES]
            return c
        lax.fori_loop(0, KEY_HALF // SUBLANES, ogroup, 0, unroll=True)

        @pl.when(t == pl.num_programs(0) - 1)
        def _():
            sout_ref[...] = s_ref[...]


def _wkv_scan_prompt(ops, *, tt):
    _, nb, _, t_len = ops.shape
    assert 2 * nb * HEADS == LANES
    sspec = pl.BlockSpec((HEAD_DIM, KEY_HALF, LANES), lambda ti, qi: (0, 0, 0))
    return pl.pallas_call(
        functools.partial(_wkv_scan_prompt_kernel, tt=tt),
        grid=(t_len // tt, SCAN_OPERANDS),
        in_specs=[pl.BlockSpec((1, nb, RWKV_WIDTH, tt), lambda ti, qi: (qi, 0, 0, ti))],
        out_specs=[pl.BlockSpec((nb, RWKV_WIDTH, tt), lambda ti, qi: (0, 0, ti)), sspec],
        out_shape=[jax.ShapeDtypeStruct((nb, RWKV_WIDTH, t_len), F32),
                   jax.ShapeDtypeStruct((HEAD_DIM, KEY_HALF, LANES), F32)],
        scratch_shapes=[pltpu.VMEM((HEAD_DIM, KEY_HALF, LANES), F32),
                        pltpu.VMEM((SCAN_OPERANDS - 1, tt, HEAD_DIM, LANES), F32),
                        pltpu.VMEM((tt, KEY_HALF, LANES), F32),
                        pltpu.VMEM((tt, KEY_HALF, LANES), F32)],
        compiler_params=pltpu.CompilerParams(
            dimension_semantics=("arbitrary", "arbitrary"), vmem_limit_bytes=VMEM_LIMIT),
        name="wkv_scan_prompt",
    )(ops)


def _wkv_scan_sample_kernel(q_ref, s0_ref, o_ref, sout_ref, s_ref, *, t_len):
    h = pl.program_id(0)
    s_ref[...] = s0_ref[0]
    for i in range(t_len):
        row = lambda qi, kp, i=i: q_ref[qi, i, pl.ds(kp * HEADS + h, 1), :]
        vv = q_ref[SCAN_OPERANDS - 1, i, pl.ds(h, HEAD_DIM, stride=HEADS), :]
        o_ref[i, pl.ds(h, HEAD_DIM, stride=HEADS), :] = _wkv_step(s_ref, row, vv)
    sout_ref[0] = s_ref[...]


def _wkv_scan_sample(ops, s0):
    t_len = ops.shape[1]
    spec = pl.BlockSpec(ops.shape[1:], lambda h: (0, 0, 0))
    sspec = pl.BlockSpec((1, HEAD_DIM, HEAD_DIM, LANES), lambda h: (h, 0, 0, 0))
    return pl.pallas_call(
        functools.partial(_wkv_scan_sample_kernel, t_len=t_len),
        grid=(HEADS,),
        in_specs=[pl.BlockSpec(ops.shape, lambda h: (0, 0, 0, 0)), sspec],
        out_specs=[spec, sspec],
        out_shape=[jax.ShapeDtypeStruct(ops.shape[1:], F32), jax.ShapeDtypeStruct(s0.shape, F32)],
        scratch_shapes=[pltpu.VMEM((HEAD_DIM, HEAD_DIM, LANES), F32)],
        compiler_params=pltpu.CompilerParams(
            dimension_semantics=("arbitrary",), vmem_limit_bytes=VMEM_LIMIT),
        name="wkv_scan_sample",
    )(ops, s0)


def _post_kernel(o_ref, bon_ref, g_ref, pa_ref, sgb_ref, x_ref, cnt0_ref, lnw_ref, lnb_ref,
                 wb_ref, wout_ref, nffn_ref, wrh_ref, wrl_ref, br_ref, x1_ref, h2_ref, rt_ref, cnt_ref,
                 carry, *, sub):
    @pl.when(pl.program_id(0) == 0)
    def _():
        carry[...] = cnt0_ref[...]

    parts = [pl.ds(part * sub, sub) for part in range(o_ref.shape[2] // sub)]
    logits = [_post_project(rows, o_ref, bon_ref, g_ref, pa_ref, sgb_ref, x_ref, lnw_ref, lnb_ref,
                            wb_ref, wout_ref, nffn_ref, wrh_ref, wrl_ref, br_ref, x1_ref, h2_ref)
              for rows in parts]
    prev = carry[0:1, :]
    for rows, lg in zip(parts, logits):
        prev = _post_route(rows, prev, lg, h2_ref, rt_ref)
    carry[...] = jnp.broadcast_to(prev, carry.shape)
    cnt_ref[...] = jnp.broadcast_to(prev, cnt_ref.shape)


def _post_project(rows, o_ref, bon_ref, g_ref, pa_ref, sgb_ref, x_ref, lnw_ref, lnb_ref,
                  wb_ref, wout_ref, nffn_ref, wrh_ref, wrl_ref, br_ref, x1_ref, h2_ref):
    o = o_ref[0, :, rows]
    mean = _head_sum(o) * (1.0 / HEAD_DIM)
    d = o - mean
    var = _head_sum(d * d) * (1.0 / HEAD_DIM)
    on = d * lax.rsqrt(var + GN_EPS) * lnw_ref[...] + lnb_ref[...]
    yb = ((on + bon_ref[0, :, rows]) * g_ref[0, :, rows]).astype(BF16)
    mb = lax.dot_general(yb, wb_ref[...], (((0,), (0,)), ((), ())), preferred_element_type=F32)
    merged = pa_ref[rows, :] + sgb_ref[rows, :] * mb
    x1 = x_ref[rows, :] + _dot(merged.astype(BF16), wout_ref[...])
    x1_ref[rows, :] = x1
    h2 = _rmsnorm(x1, nffn_ref[...])
    h2_ref[rows, 0:D_MODEL] = h2

    h_hi, h_lo = _split_bf16(h2)
    return (_dot(h_hi, wrh_ref[...]) + _dot(h_lo, wrh_ref[...]) + _dot(h_hi, wrl_ref[...])
            + br_ref[...])


def _post_route(rows, prev_count, logits, h2_ref, rt_ref):
    ln = lax.broadcasted_iota(jnp.int32, logits.shape, 1)
    is_group = (ln >= N_EXPERTS) & (ln < N_EXPERTS + N_GROUPS)
    gl = jnp.where(is_group, logits, NEG_BIG)
    gmax = jnp.max(gl, axis=-1, keepdims=True)
    gsel = jnp.min(jnp.where(gl == gmax, ln, ROUTER_LANES), axis=-1, keepdims=True) - N_EXPERTS
    den = jnp.sum(jnp.where(is_group, jnp.exp(gl - gmax), 0.0), axis=-1, keepdims=True)
    pg = 1.0 / den
    in_group = (ln < N_EXPERTS) & ((ln // EXPERTS_PER_GROUP) == gsel)
    el = jnp.where(in_group, logits, NEG_BIG)
    m1 = jnp.max(el, axis=-1, keepdims=True)
    i1 = jnp.min(jnp.where(el == m1, ln, ROUTER_LANES), axis=-1, keepdims=True)
    el2 = jnp.where(ln == i1, NEG_BIG, el)
    m2 = jnp.max(el2, axis=-1, keepdims=True)
    i2 = jnp.min(jnp.where(el2 == m2, ln, ROUTER_LANES), axis=-1, keepdims=True)
    e2 = jnp.exp(m2 - m1)
    p1 = 1.0 / (1.0 + e2)
    p2 = e2 / (1.0 + e2)

    tm = logits.shape[0]
    sel = ln == gsel
    tri = (lax.broadcasted_iota(jnp.int32, (tm, tm), 1)
           < lax.broadcasted_iota(jnp.int32, (tm, tm), 0)).astype(BF16)
    before = prev_count + _dot(tri, sel.astype(BF16))
    rank = jnp.sum(jnp.where(sel, before, 0.0), axis=-1, keepdims=True)
    first = gsel * EXPERTS_PER_GROUP - COMB_LANE0
    rt = (jnp.where(ln == 0, gsel.astype(F32), 0.0) + jnp.where(ln == 1, rank, 0.0)
          + jnp.where(ln == i1 - first, p1 * pg, 0.0) + jnp.where(ln == i2 - first, p2 * pg, 0.0))
    rt_ref[rows, :] = rt
    h2_ref[rows, D_MODEL:MOE_ROW] = rt
    return prev_count + jnp.sum(sel.astype(F32), axis=0, keepdims=True)


def _post(o, bon, g, pa, sgb, x2d, cnt0, wts, *, tm, sub, cm_index):
    n = x2d.shape[0]
    row = lambda i: (i, 0)
    tok = lambda width: pl.BlockSpec((tm, width), row)
    cm = pl.BlockSpec((1, RWKV_WIDTH, tm), cm_index)
    full = lambda arr: pl.BlockSpec(arr.shape, lambda i: (0, 0))
    return pl.pallas_call(
        functools.partial(_post_kernel, sub=sub),
        grid=(n // tm,),
        in_specs=[cm] * 3 + [tok(D_MODEL)] * 3 + [full(cnt0)] + [full(w) for w in wts],
        out_specs=[tok(D_MODEL), tok(MOE_ROW), tok(ROUTER_LANES),
                   pl.BlockSpec((SUBLANES, ROUTER_LANES), lambda i: (0, 0))],
        out_shape=[jax.ShapeDtypeStruct((n, D_MODEL), F32),
                   jax.ShapeDtypeStruct((n, MOE_ROW), F32),
                   jax.ShapeDtypeStruct((n, ROUTER_LANES), F32),
                   jax.ShapeDtypeStruct((SUBLANES, ROUTER_LANES), F32)],
        scratch_shapes=[pltpu.VMEM((SUBLANES, ROUTER_LANES), F32)],
        compiler_params=pltpu.CompilerParams(
            dimension_semantics=("arbitrary",), vmem_limit_bytes=VMEM_LIMIT),
        name="post",
    )(o, bon, g, pa, sgb, x2d, cnt0, *wts)


def _route_tables(cnt, n_tiles_max):
    counts = cnt[0, :N_GROUPS].astype(jnp.int32)
    tiles_g = (counts + GROUP_TILE - 1) // GROUP_TILE
    tile_end = jnp.cumsum(tiles_g)
    row0 = (tile_end - tiles_g) * GROUP_TILE
    tile_ids = jnp.arange(n_tiles_max, dtype=jnp.int32)
    tile_group = jnp.minimum(jnp.sum(tile_ids[:, None] >= tile_end[None, :], axis=-1),
                             N_GROUPS - 1).astype(jnp.int32)
    return tile_group, tile_end[N_GROUPS - 1:].astype(jnp.int32), row0


def _sorted_rows(rt, row0):
    g = rt[:, 0].astype(jnp.int32)
    r = rt[:, 1].astype(jnp.int32)
    return r + jnp.sum(jnp.where(g[:, None] == jnp.arange(N_GROUPS), row0, 0), axis=-1)


def _dispatch_kernel(pos_ref, h_ref, xs0_ref, xs_ref, sem, *, td):
    del xs0_ref
    for r in range(td):
        pltpu.make_async_copy(h_ref.at[pl.ds(r, 1)], xs_ref.at[pl.ds(pos_ref[r], 1)],
                              sem).start(priority=r % 2)
    pltpu.make_async_copy(h_ref, xs_ref.at[pl.ds(0, td)], sem).wait()


def _dispatch(h, pos, xs_in, *, td):
    n = h.shape[0]
    return pl.pallas_call(
        functools.partial(_dispatch_kernel, td=td),
        grid=(n // td,),
        in_specs=[pl.BlockSpec((td,), lambda i: (i,), memory_space=pltpu.SMEM),
                  pl.BlockSpec((td, MOE_ROW), lambda i: (i, 0)),
                  pl.BlockSpec(memory_space=pl.ANY)],
        out_specs=pl.BlockSpec(memory_space=pl.ANY),
        out_shape=jax.ShapeDtypeStruct(xs_in.shape, F32),
        scratch_shapes=[pltpu.SemaphoreType.DMA(())],
        input_output_aliases={2: 0},
        compiler_params=pltpu.CompilerParams(
            dimension_semantics=("arbitrary",), vmem_limit_bytes=VMEM_LIMIT),
        name="moe_dispatch",
    )(pos, h, xs_in)


def _experts_kernel(tg_ref, nu_ref, xs_ref, wg_ref, wu_ref, wd_ref, ys_ref, acc, xb):
    t = pl.program_id(0)
    e = pl.program_id(1)
    last_e = pl.num_programs(1) - 1

    @pl.when(t < nu_ref[0])
    def _():
        @pl.when(e == 0)
        def _():
            acc[...] = jnp.zeros_like(acc)
            xb[...] = xs_ref[:, 0:D_MODEL].astype(BF16)

        x = xb[...]
        rt = xs_ref[:, D_MODEL:MOE_ROW]
        ln = lax.broadcasted_iota(jnp.int32, rt.shape, 1)
        y = acc[...]
        for j in range(EXPERTS_PER_STEP):
            gate = _dot(x, wg_ref[j].astype(BF16))
            hid = gate * _sigmoid(gate) * _dot(x, wu_ref[j].astype(BF16))
            ye = _dot(hid.astype(BF16), wd_ref[j].astype(BF16))
            lane = COMB_LANE0 + e * EXPERTS_PER_STEP + j
            ce = jnp.sum(jnp.where(ln == lane, rt, 0.0), axis=-1, keepdims=True)
            y = y + ce * ye
        acc[...] = y

        @pl.when(e == last_e)
        def _():
            ys_ref[...] = acc[...]

    @pl.when((t >= nu_ref[0]) & (e == last_e))
    def _():
        ys_ref[...] = jnp.zeros_like(ys_ref)


def _experts(xs, tile_group, n_used, wg, wu, wd):
    n_rows = xs.shape[0]
    tile = lambda t, e, tg, nu: (jnp.minimum(t, nu[0] - 1), 0)
    steps = EXPERTS_PER_GROUP // EXPERTS_PER_STEP
    wsel = lambda t, e, tg, nu: (tg[jnp.minimum(t, nu[0] - 1)] * steps + e, 0, 0)
    return pl.pallas_call(
        _experts_kernel,
        grid_spec=pltpu.PrefetchScalarGridSpec(
            num_scalar_prefetch=2,
            grid=(n_rows // GROUP_TILE, steps),
            in_specs=[pl.BlockSpec((GROUP_TILE, MOE_ROW), tile),
                      pl.BlockSpec((EXPERTS_PER_STEP, D_MODEL, D_EXPERT), wsel),
                      pl.BlockSpec((EXPERTS_PER_STEP, D_MODEL, D_EXPERT), wsel),
                      pl.BlockSpec((EXPERTS_PER_STEP, D_EXPERT, D_MODEL), wsel)],
            out_specs=pl.BlockSpec((GROUP_TILE, D_MODEL), lambda t, e, tg, nu: (t, 0)),
            scratch_shapes=[pltpu.VMEM((GROUP_TILE, D_MODEL), F32),
                            pltpu.VMEM((GROUP_TILE, D_MODEL), BF16)]),
        out_shape=jax.ShapeDtypeStruct((n_rows, D_MODEL), F32),
        compiler_params=pltpu.CompilerParams(
            dimension_semantics=("arbitrary", "arbitrary"), vmem_limit_bytes=VMEM_LIMIT),
        name="moe_experts",
    )(tile_group, n_used, xs, wg, wu, wd)


def _tail_kernel(pos_ref, posn_ref, x1_ref, p_ref, ys_ref, nple_ref, wpg_ref, wpp_ref,
                 nfin_ref, y_ref, ybuf0, ybuf1, sem, *, tm):
    i = pl.program_id(0)
    last = pl.num_programs(0) - 1
    ybuf = (ybuf0, ybuf1)

    def issue(p_ref, p_off, slot):
        for r in range(tm):
            pltpu.make_async_copy(ys_ref.at[pl.ds(p_ref[p_off + r], 1)],
                                  ybuf[slot].at[pl.ds(r, 1)], sem.at[slot]).start(priority=r % 2)

    def wait(slot):
        pltpu.make_async_copy(ys_ref.at[pl.ds(0, tm)], ybuf[slot], sem.at[slot]).wait()

    def compute(slot):
        rows = pl.ds(slot * tm, tm)
        x2 = x1_ref[rows, :] + ybuf[slot][...]
        h3 = _rmsnorm(x2, nple_ref[...]).astype(BF16)
        ple = (_sigmoid(_dot(h3, wpg_ref[...]))
               * _dot(p_ref[rows, :].astype(BF16), wpp_ref[...]))
        y_ref[rows, :] = _rmsnorm(x2 + ple, nfin_ref[...])

    @pl.when(i == 0)
    def _():
        issue(pos_ref, 0, 0)

    wait(0)
    issue(pos_ref, tm, 1)
    compute(0)
    wait(1)
    issue(posn_ref, 0, 0)
    compute(1)

    @pl.when(i == last)
    def _():
        wait(0)


def _tail(pos, x1, p2d, ys, nple, wpg, wpp, nfin, *, tm):
    n = x1.shape[0]
    n_steps = n // (2 * tm)
    tok = lambda width: pl.BlockSpec((2 * tm, width), lambda i: (i, 0))
    full = lambda arr: pl.BlockSpec(arr.shape, lambda i: (0, 0))
    return pl.pallas_call(
        functools.partial(_tail_kernel, tm=tm),
        grid=(n_steps,),
        in_specs=[pl.BlockSpec((2 * tm,), lambda i: (i,), memory_space=pltpu.SMEM),
                  pl.BlockSpec((tm,), lambda i: (2 * jnp.minimum(i + 1, n_steps - 1),),
                               memory_space=pltpu.SMEM),
                  tok(D_MODEL), tok(PLE_DIM),
                  pl.BlockSpec(memory_space=pl.ANY),
                  full(nple), full(wpg), full(wpp), full(nfin)],
        out_specs=tok(D_MODEL),
        out_shape=jax.ShapeDtypeStruct((n, D_MODEL), F32),
        scratch_shapes=[pltpu.VMEM((tm, D_MODEL), F32), pltpu.VMEM((tm, D_MODEL), F32),
                        pltpu.SemaphoreType.DMA((2,))],
        compiler_params=pltpu.CompilerParams(
            dimension_semantics=("arbitrary",), vmem_limit_bytes=VMEM_LIMIT),
        name="moe_tail",
    )(pos, pos, x1, p2d, ys, nple, wpg, wpp, nfin)


def _moe_tail(groups, cnt, wg, wu, wd, nple, wpg, wpp, nfin, *, tm):
    n_all = sum(g[0].shape[0] for g in groups)
    n_tiles_max = (n_all + GROUP_TILE - 1) // GROUP_TILE + N_GROUPS
    tile_group, n_used, row0 = _route_tables(cnt, n_tiles_max)
    xs = jnp.zeros((n_tiles_max * GROUP_TILE, MOE_ROW), F32)
    poss = []
    for hx, rt, _, _, td in groups:
        poss.append(_sorted_rows(rt, row0))
        xs = _dispatch(hx, poss[-1], xs, td=td)
    ys = _experts(xs, tile_group, n_used, wg, wu, wd)
    return [_tail(pos, x1, p2d, ys, nple, wpg, wpp, nfin, tm=tm)
            for pos, (_, _, x1, p2d, _) in zip(poss, groups)]


def kernel(x_prompt, x_sample, state_pool, state_shift, state_wkv, p_prompt, p_sample, norm_mix, w_in, pool_mix, pool_scale, w_branch_a, shift_mu, decay_w0, decay_w2, iclr_a0, iclr_a2, gate_g2, k_k, k_a, r_k, ln_x_w, ln_x_b, w_branch_b, w_out, norm_ffn, w_route_group, b_route_group, w_route_expert, b_route_expert, expert_gate, expert_up, expert_down, norm_ple, w_ple_gate, w_ple_proj, norm_final):
    l = 0
    bsz, seq, _ = x_prompt.shape
    dbsz, dseq, _ = x_sample.shape
    row = lambda vec: vec.reshape(1, -1).astype(F32)
    col = lambda vec: vec.reshape(-1, 1).astype(F32)
    scan = lambda arr, axis=0: _swap_channel_order(arr, axis, True)

    o1 = POOL_WIDTH
    o2 = o1 + SHIFT_WIDTH
    w_z = _swap_shift_order(w_in[l][:, o1:o2], 1, True).astype(BF16)
    zeros_lora = jnp.zeros((RWKV_WIDTH, LORA_PAIR // 2), F32)
    w2t_pad = jnp.concatenate([scan(decay_w2[l], 1).T, zeros_lora], axis=1).astype(BF16)
    a2t_pad = jnp.concatenate([zeros_lora, scan(iclr_a2[l], 1).T], axis=1).astype(BF16)
    w_router = jnp.concatenate(
        [w_route_expert[l], w_route_group[l],
         jnp.zeros((D_MODEL, ROUTER_LANES - N_EXPERTS - N_GROUPS), F32)], axis=1)
    wr_hi = w_router.astype(BF16)
    wr_lo = (w_router - wr_hi.astype(F32)).astype(BF16)
    b_router = jnp.concatenate(
        [b_route_expert[l], b_route_group[l],
         jnp.zeros((ROUTER_LANES - N_EXPERTS - N_GROUPS,), F32)]).reshape(1, -1)
    prep_w = [row(norm_mix[l]), w_in[l][:, :o1].astype(BF16), w_z, w_in[l][:, o2:].astype(BF16),
              col(_swap_shift_order(shift_mu[l], 0, True)), col(scan(decay_w0[l])), w2t_pad,
              col(scan(iclr_a0[l])), a2t_pad, scan(gate_g2[l], 1).T.astype(BF16), col(scan(k_k[l])),
              col(scan(k_a[l])), col(scan(r_k[l].reshape(-1))), pool_mix[l].astype(BF16),
              row(pool_scale[l]), w_branch_a[l].astype(BF16)]
    post_w = [col(scan(ln_x_w[l])), col(scan(ln_x_b[l])), scan(w_branch_b[l]).astype(BF16),
              w_out[l].astype(BF16), row(norm_ffn[l]), wr_hi, wr_lo, b_router]
    moe_w = [expert_gate[l], expert_up[l], expert_down[l],
             row(norm_ple[l]), w_ple_gate[l].astype(BF16), w_ple_proj[l].astype(BF16), row(norm_final)]

    tm_p = PROMPT_TILE
    tiles_p = seq // tm_p
    x_p = x_prompt.reshape(bsz * seq, D_MODEL)
    outs = _mix_prep(x_p, jnp.zeros((bsz * SHIFT_WIDTH, LANES), F32),
                     jnp.zeros((bsz * 16, POOL_WIDTH), F32), prep_w,
                     nb=bsz, tiles=tiles_p, tm=tm_p, s=1, pos0=0, cm_index=lambda b, t: (b, 0, t))
    u_p, zl_p, ops, g, bon, pa, sgb = outs
    o_p, s_p = _wkv_scan_prompt(ops, tt=LANES)
    tm_post = POST_TILE
    n_p = bsz * seq
    x1_p, hx_p, rt_p, cnt_p = _post(
        o_p, bon, g, pa, sgb, x_p, jnp.zeros((SUBLANES, ROUTER_LANES), F32), post_w,
        tm=tm_post, sub=POST_SUB_TILE, cm_index=lambda i: (i // (seq // tm_post), 0, i % (seq // tm_post)))
    wkv_p = s_p.reshape(KEY_HALF, 2, KEY_HALF, 2, bsz, HEADS).transpose(4, 5, 3, 2, 1, 0)
    wkv_p = wkv_p.reshape(bsz, HEADS, HEAD_DIM, HEAD_DIM)

    n_s = dbsz * dseq
    x_s = x_sample.transpose(1, 0, 2).reshape(n_s, D_MODEL)
    p_s = p_sample[l].transpose(1, 0, 2).reshape(n_s, PLE_DIM)
    uc0_s = jnp.concatenate(
        [jnp.zeros((dbsz, POOL_WIDTH), F32),
         state_pool[l].transpose(1, 0, 2).reshape(POOL_STATE * dbsz, POOL_WIDTH)], axis=0)
    zc0_s = _swap_shift_order(state_shift[l], 1, True).T
    outs = _mix_prep(x_s, zc0_s, uc0_s, prep_w, nb=1, tiles=dseq, tm=dbsz, s=dbsz, pos0=PAST_LEN,
                     cm_index=lambda b, t: (t, 0, 0))
    u_s, zl_s, ops, g, bon, pa, sgb = outs
    s0_s = state_wkv[l].reshape(dbsz, HEADS, 2, KEY_HALF, 2, KEY_HALF).transpose(1, 5, 4, 3, 2, 0)
    s0_s = s0_s.reshape(HEADS, HEAD_DIM, HEAD_DIM, dbsz)
    o_s, s_s = _wkv_scan_sample(ops, s0_s)
    x1_s, hx_s, rt_s, cnt_all = _post(o_s, bon, g, pa, sgb, x_s, cnt_p, post_w,
                                      tm=dbsz, sub=dbsz, cm_index=lambda i: (i, 0, 0))

    y_p, y_s = _moe_tail(
        [(hx_p, rt_p, x1_p, p_prompt[l].reshape(n_p, PLE_DIM), DISPATCH_TILE),
         (hx_s, rt_s, x1_s, p_s, n_s)],
        cnt_all, *moe_w, tm=TAIL_TILE)
    wkv_s = s_s.reshape(HEADS, KEY_HALF, 2, KEY_HALF, 2, dbsz).transpose(5, 0, 4, 3, 2, 1)
    wkv_s = wkv_s.reshape(dbsz, HEADS, HEAD_DIM, HEAD_DIM)

    y_prompt = y_p.reshape(bsz, seq, D_MODEL)
    y_sample = y_s.reshape(dseq, dbsz, D_MODEL).transpose(1, 0, 2)
    pool_prompt = u_p.reshape(bsz, seq, POOL_WIDTH)[:, seq - POOL_STATE:]
    u_s_bt = u_s.reshape(dseq, dbsz, POOL_WIDTH).transpose(1, 0, 2)
    pool_sample = jnp.concatenate([state_pool[l][:, dseq:], u_s_bt], axis=1)
    shift_prompt = _swap_shift_order(zl_p[:, SUBLANES - 1, :], 1, False)
    shift_sample = _swap_shift_order(zl_s[0], 1, False)
    return (y_prompt, y_sample, pool_prompt[None], shift_prompt[None], wkv_p[None],
            pool_sample[None], shift_sample[None], wkv_s[None])
```

```python
import functools
import math

import jax
import jax.numpy as jnp
from jax import lax
from jax.experimental import pallas as pl
from jax.experimental.pallas import tpu as pltpu

F32 = jnp.float32
BF16 = jnp.bfloat16

D_MODEL = 1024
PLE_DIM = 256
POOL_WIDTH = 512
POOL_WINDOWS = (2, 4, 8, 16)
POOL_GROUP_DIM = 128
POOL_STATE = 15
RWKV_WIDTH = 512
HEAD_DIM = 64
HEADS = 8
LORA_PAIR = 128
GATE_LORA = 128
SHIFT_WIDTH = 3 * RWKV_WIDTH + LORA_PAIR + GATE_LORA
IN_WIDTH = POOL_WIDTH + SHIFT_WIDTH + 2 * D_MODEL
N_GROUPS = 4
EXPERTS_PER_GROUP = 8
N_EXPERTS = 32
D_EXPERT = 256
RMS_EPS = 1e-6
GN_EPS = 64e-5
PAST_LEN = 16384

LANES = 128
SUBLANES = 8
KEY_HALF = HEAD_DIM // 2
SCAN_OPERANDS = 6
KEY_GROUP = 32
ROUTER_LANES = 128
PROMPT_TILE = 256
POST_TILE = 512
POST_SUB_TILE = 256
DISPATCH_TILE = 1024
TAIL_TILE = 256
GROUP_TILE = 1024
EXPERTS_PER_STEP = 2
DECAY_SCALE = math.exp(-0.5)
COMB_LANE0 = 8
MOE_ROW = D_MODEL + ROUTER_LANES
NEG_BIG = -1e30
VMEM_LIMIT = 56 * 1024 * 1024


def _swap_channel_order(x, axis, to_scan):
    x = jnp.moveaxis(x, axis, -1)
    lead = x.shape[:-1]
    split = (HEADS, 2, KEY_HALF) if to_scan else (KEY_HALF, 2, HEADS)
    x = jnp.swapaxes(x.reshape(lead + split), -1, -3).reshape(lead + (RWKV_WIDTH,))
    return jnp.moveaxis(x, -1, axis)


def _swap_shift_order(x, axis, to_scan):
    x = jnp.moveaxis(x, axis, -1)
    lead = x.shape[:-1]
    rkv = x[..., :3 * RWKV_WIDTH].reshape(lead + (3, RWKV_WIDTH))
    rkv = _swap_channel_order(rkv, -1, to_scan).reshape(lead + (3 * RWKV_WIDTH,))
    return jnp.moveaxis(jnp.concatenate([rkv, x[..., 3 * RWKV_WIDTH:]], axis=-1), -1, axis)


def _dot(a, b):
    return jnp.dot(a, b, preferred_element_type=F32)


def _split_bf16(x):
    hi = x.astype(BF16)
    return hi, (x - hi.astype(F32)).astype(BF16)


def _head_sum(x):
    x3 = x.reshape(HEAD_DIM, HEADS, x.shape[1])
    s = jnp.sum(x3, axis=0, keepdims=True)
    return jnp.broadcast_to(s, x3.shape).reshape(x.shape)


def _rmsnorm(x, g):
    return x * lax.rsqrt(jnp.mean(x * x, axis=-1, keepdims=True) + RMS_EPS) * g


def _sigmoid(x):
    return 0.5 * jnp.tanh(0.5 * x) + 0.5


def _mix_prep_kernel(x_ref, zc0_ref, uc0_ref, nmix_ref, win_ref, wz_ref, mu_ref, w0_ref,
                     w2t_ref, a0_ref, a2t_ref, g2t_ref, kk_ref, ka_ref, rk_ref, mix_ref,
                     pscale_ref, wa_ref,
                     u_ref, zl_ref, q_ref, g_ref, bon_ref,
                     pa_ref, sgb_ref, zc, uext, *, tm, s, pos0):
    t = pl.program_id(1)
    up = 16 * s

    @pl.when(t == 0)
    def _():
        zc[...] = zc0_ref[...]
        uext[0:up] = uc0_ref[...]

    x = x_ref[...]
    h = _rmsnorm(x, nmix_ref[...]).astype(BF16)
    u = _dot(h, win_ref[:, 0:POOL_WIDTH])
    gab = _dot(h, win_ref[:, POOL_WIDTH + SHIFT_WIDTH:IN_WIDTH])
    u_ref[...] = u
    sgb_ref[...] = _sigmoid(gab[:, D_MODEL:])
    uext[up:up + tm] = u

    z = _dot(h, wz_ref[...])
    zl_ref[0] = z[tm - max(SUBLANES, s):tm]
    z_t = z.T
    if s == 1:
        rolled = pltpu.roll(z_t, 1, axis=1)
        lane = lax.broadcasted_iota(jnp.int32, (SHIFT_WIDTH, LANES), 1)
        first = jnp.where(lane == 0, zc[:, LANES - 1:LANES], rolled[:, 0:LANES])
        zprev = jnp.concatenate([first, rolled[:, LANES:]], axis=1)
    else:
        zprev = zc[...]
    zc[...] = z_t[:, tm - LANES:tm]
    zm = z_t + (zprev - z_t) * mu_ref[...]
    r = zm[0:RWKV_WIDTH]
    k = zm[RWKV_WIDTH:2 * RWKV_WIDTH]
    v = zm[2 * RWKV_WIDTH:3 * RWKV_WIDTH]
    lora_in = zm[3 * RWKV_WIDTH:3 * RWKV_WIDTH + LORA_PAIR]
    gd = zm[3 * RWKV_WIDTH + LORA_PAIR:SHIFT_WIDTH]
    dw = _dot(w2t_ref[...], jnp.tanh(lora_in).astype(BF16))
    da = _dot(a2t_ref[...], lora_in.astype(BF16))
    decay = jnp.exp(-DECAY_SCALE * _sigmoid(w0_ref[...] + dw))
    a = _sigmoid(a0_ref[...] + da)
    kk = k * kk_ref[...]
    kk = kk / jnp.maximum(jnp.sqrt(_head_sum(kk * kk)), 1e-12)
    k2 = k * (1.0 + (a - 1.0) * ka_ref[...])
    q_ref[0, 0] = -kk
    q_ref[1, 0] = decay
    q_ref[2, 0] = kk * a
    q_ref[3, 0] = k2
    q_ref[4, 0] = r
    q_ref[5, 0] = v
    g_ref[0] = _dot(g2t_ref[...], _sigmoid(gd).astype(BF16))
    bon_ref[0] = _head_sum(r * k2 * rk_ref[...]) * v

    rows = lax.broadcasted_iota(jnp.int32, (tm, POOL_GROUP_DIM), 0)
    if s > 1:
        rows = rows // s
    pos = pos0 + t * (tm // s) + rows
    ys = []
    for gi, wnd in enumerate(POOL_WINDOWS):
        lanes = slice(gi * POOL_GROUP_DIM, (gi + 1) * POOL_GROUP_DIM)
        wsum = uext[:, lanes]
        span = 1
        while span < wnd:
            wsum = wsum + pltpu.roll(wsum, span * s, axis=0)
            span *= 2
        cur = uext[pl.ds(up, tm), lanes]
        cnt = jnp.minimum(pos + 1, wnd).astype(F32)
        pooled = wsum[up:up + tm] / cnt - cur
        ys.append(_dot(pooled.astype(BF16), mix_ref[gi]))
    y = jnp.concatenate(ys, axis=-1) * pscale_ref[...]
    pa_ref[...] = _sigmoid(gab[:, :D_MODEL]) * _dot(y.astype(BF16), wa_ref[...])

    uext[0:up] = uext[tm:tm + up]


def _mix_prep(x2d, zc0, uc0, wts, *, nb, tiles, tm, s, pos0, cm_index):
    n = x2d.shape[0]
    up = 16 * s
    row = lambda b, t: (b * tiles + t, 0)
    full = lambda arr: pl.BlockSpec(arr.shape, lambda b, t: (0,) * arr.ndim)
    in_specs = [
        pl.BlockSpec((tm, D_MODEL), row),
        pl.BlockSpec((SHIFT_WIDTH, LANES), lambda b, t: (b % (zc0.shape[0] // SHIFT_WIDTH), 0)),
        pl.BlockSpec((up, POOL_WIDTH), lambda b, t: (b % (uc0.shape[0] // up), 0)),
    ] + [full(w) for w in wts]
    tok = lambda width: pl.BlockSpec((tm, width), row)
    cm_shape = (nb, RWKV_WIDTH, tiles * tm) if s == 1 else (tiles, RWKV_WIDTH, tm)
    cm = pl.BlockSpec((1, RWKV_WIDTH, tm), cm_index)
    cm_stack = pl.BlockSpec((SCAN_OPERANDS, 1, RWKV_WIDTH, tm), lambda b, t: (0,) + cm_index(b, t))
    zrows = max(SUBLANES, s)
    out_specs = [tok(POOL_WIDTH), pl.BlockSpec((1, zrows, SHIFT_WIDTH), lambda b, t: (b, 0, 0)),
                 cm_stack, cm, cm] + [tok(D_MODEL)] * 2
    out_shape = [jax.ShapeDtypeStruct((n, POOL_WIDTH), F32),
                 jax.ShapeDtypeStruct((nb, zrows, SHIFT_WIDTH), F32),
                 jax.ShapeDtypeStruct((SCAN_OPERANDS,) + cm_shape, F32)] \
        + [jax.ShapeDtypeStruct(cm_shape, F32)] * 2 \
        + [jax.ShapeDtypeStruct((n, D_MODEL), F32)] * 2
    return pl.pallas_call(
        functools.partial(_mix_prep_kernel, tm=tm, s=s, pos0=pos0),
        grid=(nb, tiles),
        in_specs=in_specs,
        out_specs=out_specs,
        out_shape=out_shape,
        scratch_shapes=[pltpu.VMEM((SHIFT_WIDTH, LANES), F32),
                        pltpu.VMEM((up + tm, POOL_WIDTH), F32)],
        compiler_params=pltpu.CompilerParams(
            dimension_semantics=("arbitrary", "arbitrary"), vmem_limit_bytes=VMEM_LIMIT),
        name="mix_prep",
    )(x2d, zc0, uc0, *wts)


def _wkv_step(s_ref, row, vv):
    groups = HEAD_DIM // KEY_GROUP

    def sa_pass(g, acc):
        for kk in range(KEY_GROUP):
            kp = g * KEY_GROUP + kk
            acc = acc + s_ref[kp] * row(0, kp)
        return acc
    sa = lax.fori_loop(0, groups, sa_pass, jnp.zeros(vv.shape, F32))

    def update_pass(g, acc):
        for kk in range(KEY_GROUP):
            kp = g * KEY_GROUP + kk
            sn = s_ref[kp] * row(1, kp) + sa * row(2, kp) + vv * row(3, kp)
            s_ref[kp] = sn
            acc = acc + sn * row(4, kp)
        return acc
    return lax.fori_loop(0, groups, update_pass, jnp.zeros(vv.shape, F32))


def _swap_major_sublane(x):
    return jnp.swapaxes(x, 0, 1)


def _wkv_scan_prompt_kernel(q_ref, o_ref, sout_ref, s_ref, kv_ref, vv_ref, ov_ref, *, tt):
    t = pl.program_id(0)
    q = pl.program_id(1)
    nb = q_ref.shape[1]

    @pl.when((t == 0) & (q == 0))
    def _():
        s_ref[...] = jnp.zeros_like(s_ref)

    def gather_t(base_lo, base_hi):
        pieces = [q_ref[0, bb, pl.ds(base, SUBLANES), :]
                  for base in (base_lo, base_hi) for bb in range(nb)]
        return jnp.concatenate(pieces, axis=0).T

    @pl.when(q < SCAN_OPERANDS - 1)
    def _():
        def kgroup(gi, c):
            slabs = []
            for kk in range(SUBLANES):
                base = pl.multiple_of((gi * SUBLANES + kk) * SUBLANES, SUBLANES)
                slabs.append(gather_t(base, base))
            kv_ref[q, :, pl.ds(pl.multiple_of(gi * SUBLANES, SUBLANES), SUBLANES), :] = \
                _swap_major_sublane(jnp.stack(slabs))
            return c
        lax.fori_loop(0, HEAD_DIM // SUBLANES, kgroup, 0, unroll=True)

    @pl.when(q == SCAN_OPERANDS - 1)
    def _():
        def vgroup(gi, c):
            slabs = []
            for vi in range(SUBLANES):
                base = pl.multiple_of((gi * SUBLANES + vi) * 2 * SUBLANES, 2 * SUBLANES)
                slabs.append(gather_t(base, base + SUBLANES))
            vv_ref[:, pl.ds(pl.multiple_of(gi * SUBLANES, SUBLANES), SUBLANES), :] = \
                _swap_major_sublane(jnp.stack(slabs))
            return c
        lax.fori_loop(0, KEY_HALF // SUBLANES, vgroup, 0, unroll=True)

        def step(i, c):
            row = lambda qi, kp: kv_ref[qi, i, pl.ds(kp, 1), :]
            ov_ref[i] = _wkv_step(s_ref, row, vv_ref[i])
            return c
        lax.fori_loop(0, tt, step, 0)

        def ogroup(gi, c):
            g0 = pl.multiple_of(gi * SUBLANES, SUBLANES)
            x = _swap_major_sublane(ov_ref[:, pl.ds(g0, SUBLANES), :])
            for vi in range(SUBLANES):
                xt = x[vi].T
                base = pl.multiple_of((gi * SUBLANES + vi) * 2 * SUBLANES, 2 * SUBLANES)
                for vh in range(2):
                    for bb in range(nb):
                        r0 = (vh * nb + bb) * SUBLANES
                        o_ref[bb, pl.ds(base + vh * SUBLANES, SUBLANES), :] = xt[r0:r0 + SUBLANES]
            return c
        lax.fori_loop(0, KEY_HALF // SUBLANES, ogroup, 0, unroll=True)

        @pl.when(t == pl.num_programs(0) - 1)
        def _():
            sout_ref[...] = s_ref[...]


def _wkv_scan_prompt(ops, *, tt):
    _, nb, _, t_len = ops.shape
    assert 2 * nb * HEADS == LANES
    sspec = pl.BlockSpec((HEAD_DIM, KEY_HALF, LANES), lambda ti, qi: (0, 0, 0))
    return pl.pallas_call(
        functools.partial(_wkv_scan_prompt_kernel, tt=tt),
        grid=(t_len // tt, SCAN_OPERANDS),
        in_specs=[pl.BlockSpec((1, nb, RWKV_WIDTH, tt), lambda ti, qi: (qi, 0, 0, ti))],
        out_specs=[pl.BlockSpec((nb, RWKV_WIDTH, tt), lambda ti, qi: (0, 0, ti)), sspec],
        out_shape=[jax.ShapeDtypeStruct((nb, RWKV_WIDTH, t_len), F32),
                   jax.ShapeDtypeStruct((HEAD_DIM, KEY_HALF, LANES), F32)],
        scratch_shapes=[pltpu.VMEM((HEAD_DIM, KEY_HALF, LANES), F32),
                        pltpu.VMEM((SCAN_OPERANDS - 1, tt, HEAD_DIM, LANES), F32),
                        pltpu.VMEM((tt, KEY_HALF, LANES), F32),
                        pltpu.VMEM((tt, KEY_HALF, LANES), F32)],
        compiler_params=pltpu.CompilerParams(
            dimension_semantics=("arbitrary", "arbitrary"), vmem_limit_bytes=VMEM_LIMIT),
        name="wkv_scan_prompt",
    )(ops)


def _wkv_scan_sample_kernel(q_ref, s0_ref, o_ref, sout_ref, s_ref, *, t_len):
    h = pl.program_id(0)
    s_ref[...] = s0_ref[0]
    for i in range(t_len):
        row = lambda qi, kp, i=i: q_ref[qi, i, pl.ds(kp * HEADS + h, 1), :]
        vv = q_ref[SCAN_OPERANDS - 1, i, pl.ds(h, HEAD_DIM, stride=HEADS), :]
        o_ref[i, pl.ds(h, HEAD_DIM, stride=HEADS), :] = _wkv_step(s_ref, row, vv)
    sout_ref[0] = s_ref[...]


def _wkv_scan_sample(ops, s0):
    t_len = ops.shape[1]
    spec = pl.BlockSpec(ops.shape[1:], lambda h: (0, 0, 0))
    sspec = pl.BlockSpec((1, HEAD_DIM, HEAD_DIM, LANES), lambda h: (h, 0, 0, 0))
    return pl.pallas_call(
        functools.partial(_wkv_scan_sample_kernel, t_len=t_len),
        grid=(HEADS,),
        in_specs=[pl.BlockSpec(ops.shape, lambda h: (0, 0, 0, 0)), sspec],
        out_specs=[spec, sspec],
        out_shape=[jax.ShapeDtypeStruct(ops.shape[1:], F32), jax.ShapeDtypeStruct(s0.shape, F32)],
        scratch_shapes=[pltpu.VMEM((HEAD_DIM, HEAD_DIM, LANES), F32)],
        compiler_params=pltpu.CompilerParams(
            dimension_semantics=("arbitrary",), vmem_limit_bytes=VMEM_LIMIT),
        name="wkv_scan_sample",
    )(ops, s0)


def _post_kernel(o_ref, bon_ref, g_ref, pa_ref, sgb_ref, x_ref, cnt0_ref, lnw_ref, lnb_ref,
                 wb_ref, wout_ref, nffn_ref, wrh_ref, wrl_ref, br_ref, x1_ref, h2_ref, rt_ref, cnt_ref,
                 carry, *, sub):
    @pl.when(pl.program_id(0) == 0)
    def _():
        carry[...] = cnt0_ref[...]

    parts = [pl.ds(part * sub, sub) for part in range(o_ref.shape[2] // sub)]
    logits = [_post_project(rows, o_ref, bon_ref, g_ref, pa_ref, sgb_ref, x_ref, lnw_ref, lnb_ref,
                            wb_ref, wout_ref, nffn_ref, wrh_ref, wrl_ref, br_ref, x1_ref, h2_ref)
              for rows in parts]
    prev = carry[0:1, :]
    for rows, lg in zip(parts, logits):
        prev = _post_route(rows, prev, lg, h2_ref, rt_ref)
    carry[...] = jnp.broadcast_to(prev, carry.shape)
    cnt_ref[...] = jnp.broadcast_to(prev, cnt_ref.shape)


def _post_project(rows, o_ref, bon_ref, g_ref, pa_ref, sgb_ref, x_ref, lnw_ref, lnb_ref,
                  wb_ref, wout_ref, nffn_ref, wrh_ref, wrl_ref, br_ref, x1_ref, h2_ref):
    o = o_ref[0, :, rows]
    mean = _head_sum(o) * (1.0 / HEAD_DIM)
    d = o - mean
    var = _head_sum(d * d) * (1.0 / HEAD_DIM)
    on = d * lax.rsqrt(var + GN_EPS) * lnw_ref[...] + lnb_ref[...]
    yb = ((on + bon_ref[0, :, rows]) * g_ref[0, :, rows]).astype(BF16)
    mb = lax.dot_general(yb, wb_ref[...], (((0,), (0,)), ((), ())), preferred_element_type=F32)
    merged = pa_ref[rows, :] + sgb_ref[rows, :] * mb
    x1 = x_ref[rows, :] + _dot(merged.astype(BF16), wout_ref[...])
    x1_ref[rows, :] = x1
    h2 = _rmsnorm(x1, nffn_ref[...])
    h2_ref[rows, 0:D_MODEL] = h2

    h_hi, h_lo = _split_bf16(h2)
    return (_dot(h_hi, wrh_ref[...]) + _dot(h_lo, wrh_ref[...]) + _dot(h_hi, wrl_ref[...])
            + br_ref[...])


def _post_route(rows, prev_count, logits, h2_ref, rt_ref):
    ln = lax.broadcasted_iota(jnp.int32, logits.shape, 1)
    is_group = (ln >= N_EXPERTS) & (ln < N_EXPERTS + N_GROUPS)
    gl = jnp.where(is_group, logits, NEG_BIG)
    gmax = jnp.max(gl, axis=-1, keepdims=True)
    gsel = jnp.min(jnp.where(gl == gmax, ln, ROUTER_LANES), axis=-1, keepdims=True) - N_EXPERTS
    den = jnp.sum(jnp.where(is_group, jnp.exp(gl - gmax), 0.0), axis=-1, keepdims=True)
    pg = 1.0 / den
    in_group = (ln < N_EXPERTS) & ((ln // EXPERTS_PER_GROUP) == gsel)
    el = jnp.where(in_group, logits, NEG_BIG)
    m1 = jnp.max(el, axis=-1, keepdims=True)
    i1 = jnp.min(jnp.where(el == m1, ln, ROUTER_LANES), axis=-1, keepdims=True)
    el2 = jnp.where(ln == i1, NEG_BIG, el)
    m2 = jnp.max(el2, axis=-1, keepdims=True)
    i2 = jnp.min(jnp.where(el2 == m2, ln, ROUTER_LANES), axis=-1, keepdims=True)
    e2 = jnp.exp(m2 - m1)
    p1 = 1.0 / (1.0 + e2)
    p2 = e2 / (1.0 + e2)

    tm = logits.shape[0]
    sel = ln == gsel
    tri = (lax.broadcasted_iota(jnp.int32, (tm, tm), 1)
           < lax.broadcasted_iota(jnp.int32, (tm, tm), 0)).astype(BF16)
    before = prev_count + _dot(tri, sel.astype(BF16))
    rank = jnp.sum(jnp.where(sel, before, 0.0), axis=-1, keepdims=True)
    first = gsel * EXPERTS_PER_GROUP - COMB_LANE0
    rt = (jnp.where(ln == 0, gsel.astype(F32), 0.0) + jnp.where(ln == 1, rank, 0.0)
          + jnp.where(ln == i1 - first, p1 * pg, 0.0) + jnp.where(ln == i2 - first, p2 * pg, 0.0))
    rt_ref[rows, :] = rt
    h2_ref[rows, D_MODEL:MOE_ROW] = rt
    return prev_count + jnp.sum(sel.astype(F32), axis=0, keepdims=True)


def _post(o, bon, g, pa, sgb, x2d, cnt0, wts, *, tm, sub, cm_index):
    n = x2d.shape[0]
    row = lambda i: (i, 0)
    tok = lambda width: pl.BlockSpec((tm, width), row)
    cm = pl.BlockSpec((1, RWKV_WIDTH, tm), cm_index)
    full = lambda arr: pl.BlockSpec(arr.shape, lambda i: (0, 0))
    return pl.pallas_call(
        functools.partial(_post_kernel, sub=sub),
        grid=(n // tm,),
        in_specs=[cm] * 3 + [tok(D_MODEL)] * 3 + [full(cnt0)] + [full(w) for w in wts],
        out_specs=[tok(D_MODEL), tok(MOE_ROW), tok(ROUTER_LANES),
                   pl.BlockSpec((SUBLANES, ROUTER_LANES), lambda i: (0, 0))],
        out_shape=[jax.ShapeDtypeStruct((n, D_MODEL), F32),
                   jax.ShapeDtypeStruct((n, MOE_ROW), F32),
                   jax.ShapeDtypeStruct((n, ROUTER_LANES), F32),
                   jax.ShapeDtypeStruct((SUBLANES, ROUTER_LANES), F32)],
        scratch_shapes=[pltpu.VMEM((SUBLANES, ROUTER_LANES), F32)],
        compiler_params=pltpu.CompilerParams(
            dimension_semantics=("arbitrary",), vmem_limit_bytes=VMEM_LIMIT),
        name="post",
    )(o, bon, g, pa, sgb, x2d, cnt0, *wts)


def _route_tables(cnt, n_tiles_max):
    counts = cnt[0, :N_GROUPS].astype(jnp.int32)
    tiles_g = (counts + GROUP_TILE - 1) // GROUP_TILE
    tile_end = jnp.cumsum(tiles_g)
    row0 = (tile_end - tiles_g) * GROUP_TILE
    tile_ids = jnp.arange(n_tiles_max, dtype=jnp.int32)
    tile_group = jnp.minimum(jnp.sum(tile_ids[:, None] >= tile_end[None, :], axis=-1),
                             N_GROUPS - 1).astype(jnp.int32)
    return tile_group, tile_end[N_GROUPS - 1:].astype(jnp.int32), row0


def _sorted_rows(rt, row0):
    g = rt[:, 0].astype(jnp.int32)
    r = rt[:, 1].astype(jnp.int32)
    return r + jnp.sum(jnp.where(g[:, None] == jnp.arange(N_GROUPS), row0, 0), axis=-1)


def _dispatch_kernel(pos_ref, h_ref, xs0_ref, xs_ref, sem, *, td):
    del xs0_ref
    for r in range(td):
        pltpu.make_async_copy(h_ref.at[pl.ds(r, 1)], xs_ref.at[pl.ds(pos_ref[r], 1)],
                              sem).start(priority=r % 2)
    pltpu.make_async_copy(h_ref, xs_ref.at[pl.ds(0, td)], sem).wait()


def _dispatch(h, pos, xs_in, *, td):
    n = h.shape[0]
    return pl.pallas_call(
        functools.partial(_dispatch_kernel, td=td),
        grid=(n // td,),
        in_specs=[pl.BlockSpec((td,), lambda i: (i,), memory_space=pltpu.SMEM),
                  pl.BlockSpec((td, MOE_ROW), lambda i: (i, 0)),
                  pl.BlockSpec(memory_space=pl.ANY)],
        out_specs=pl.BlockSpec(memory_space=pl.ANY),
        out_shape=jax.ShapeDtypeStruct(xs_in.shape, F32),
        scratch_shapes=[pltpu.SemaphoreType.DMA(())],
        input_output_aliases={2: 0},
        compiler_params=pltpu.CompilerParams(
            dimension_semantics=("arbitrary",), vmem_limit_bytes=VMEM_LIMIT),
        name="moe_dispatch",
    )(pos, h, xs_in)


def _experts_kernel(tg_ref, nu_ref, xs_ref, wg_ref, wu_ref, wd_ref, ys_ref, acc, xb):
    t = pl.program_id(0)
    e = pl.program_id(1)
    last_e = pl.num_programs(1) - 1

    @pl.when(t < nu_ref[0])
    def _():
        @pl.when(e == 0)
        def _():
            acc[...] = jnp.zeros_like(acc)
            xb[...] = xs_ref[:, 0:D_MODEL].astype(BF16)

        x = xb[...]
        rt = xs_ref[:, D_MODEL:MOE_ROW]
        ln = lax.broadcasted_iota(jnp.int32, rt.shape, 1)
        y = acc[...]
        for j in range(EXPERTS_PER_STEP):
            gate = _dot(x, wg_ref[j].astype(BF16))
            hid = gate * _sigmoid(gate) * _dot(x, wu_ref[j].astype(BF16))
            ye = _dot(hid.astype(BF16), wd_ref[j].astype(BF16))
            lane = COMB_LANE0 + e * EXPERTS_PER_STEP + j
            ce = jnp.sum(jnp.where(ln == lane, rt, 0.0), axis=-1, keepdims=True)
            y = y + ce * ye
        acc[...] = y

        @pl.when(e == last_e)
        def _():
            ys_ref[...] = acc[...]

    @pl.when((t >= nu_ref[0]) & (e == last_e))
    def _():
        ys_ref[...] = jnp.zeros_like(ys_ref)


def _experts(xs, tile_group, n_used, wg, wu, wd):
    n_rows = xs.shape[0]
    tile = lambda t, e, tg, nu: (jnp.minimum(t, nu[0] - 1), 0)
    steps = EXPERTS_PER_GROUP // EXPERTS_PER_STEP
    wsel = lambda t, e, tg, nu: (tg[jnp.minimum(t, nu[0] - 1)] * steps + e, 0, 0)
    return pl.pallas_call(
        _experts_kernel,
        grid_spec=pltpu.PrefetchScalarGridSpec(
            num_scalar_prefetch=2,
            grid=(n_rows // GROUP_TILE, steps),
            in_specs=[pl.BlockSpec((GROUP_TILE, MOE_ROW), tile),
                      pl.BlockSpec((EXPERTS_PER_STEP, D_MODEL, D_EXPERT), wsel),
                      pl.BlockSpec((EXPERTS_PER_STEP, D_MODEL, D_EXPERT), wsel),
                      pl.BlockSpec((EXPERTS_PER_STEP, D_EXPERT, D_MODEL), wsel)],
            out_specs=pl.BlockSpec((GROUP_TILE, D_MODEL), lambda t, e, tg, nu: (t, 0)),
            scratch_shapes=[pltpu.VMEM((GROUP_TILE, D_MODEL), F32),
                            pltpu.VMEM((GROUP_TILE, D_MODEL), BF16)]),
        out_shape=jax.ShapeDtypeStruct((n_rows, D_MODEL), F32),
        compiler_params=pltpu.CompilerParams(
            dimension_semantics=("arbitrary", "arbitrary"), vmem_limit_bytes=VMEM_LIMIT),
        name="moe_experts",
    )(tile_group, n_used, xs, wg, wu, wd)


def _tail_kernel(pos_ref, posn_ref, x1_ref, p_ref, ys_ref, nple_ref, wpg_ref, wpp_ref,
                 nfin_ref, y_ref, ybuf0, ybuf1, sem, *, tm):
    i = pl.program_id(0)
    last = pl.num_programs(0) - 1
    ybuf = (ybuf0, ybuf1)

    def issue(p_ref, p_off, slot):
        for r in range(tm):
            pltpu.make_async_copy(ys_ref.at[pl.ds(p_ref[p_off + r], 1)],
                                  ybuf[slot].at[pl.ds(r, 1)], sem.at[slot]).start(priority=r % 2)

    def wait(slot):
        pltpu.make_async_copy(ys_ref.at[pl.ds(0, tm)], ybuf[slot], sem.at[slot]).wait()

    def compute(slot):
        rows = pl.ds(slot * tm, tm)
        x2 = x1_ref[rows, :] + ybuf[slot][...]
        h3 = _rmsnorm(x2, nple_ref[...]).astype(BF16)
        ple = (_sigmoid(_dot(h3, wpg_ref[...]))
               * _dot(p_ref[rows, :].astype(BF16), wpp_ref[...]))
        y_ref[rows, :] = _rmsnorm(x2 + ple, nfin_ref[...])

    @pl.when(i == 0)
    def _():
        issue(pos_ref, 0, 0)

    wait(0)
    issue(pos_ref, tm, 1)
    compute(0)
    wait(1)
    issue(posn_ref, 0, 0)
    compute(1)

    @pl.when(i == last)
    def _():
        wait(0)


def _tail(pos, x1, p2d, ys, nple, wpg, wpp, nfin, *, tm):
    n = x1.shape[0]
    n_steps = n // (2 * tm)
    tok = lambda width: pl.BlockSpec((2 * tm, width), lambda i: (i, 0))
    full = lambda arr: pl.BlockSpec(arr.shape, lambda i: (0, 0))
    return pl.pallas_call(
        functools.partial(_tail_kernel, tm=tm),
        grid=(n_steps,),
        in_specs=[pl.BlockSpec((2 * tm,), lambda i: (i,), memory_space=pltpu.SMEM),
                  pl.BlockSpec((tm,), lambda i: (2 * jnp.minimum(i + 1, n_steps - 1),),
                               memory_space=pltpu.SMEM),
                  tok(D_MODEL), tok(PLE_DIM),
                  pl.BlockSpec(memory_space=pl.ANY),
                  full(nple), full(wpg), full(wpp), full(nfin)],
        out_specs=tok(D_MODEL),
        out_shape=jax.ShapeDtypeStruct((n, D_MODEL), F32),
        scratch_shapes=[pltpu.VMEM((tm, D_MODEL), F32), pltpu.VMEM((tm, D_MODEL), F32),
                        pltpu.SemaphoreType.DMA((2,))],
        compiler_params=pltpu.CompilerParams(
            dimension_semantics=("arbitrary",), vmem_limit_bytes=VMEM_LIMIT),
        name="moe_tail",
    )(pos, pos, x1, p2d, ys, nple, wpg, wpp, nfin)


def _moe_tail(groups, cnt, wg, wu, wd, nple, wpg, wpp, nfin, *, tm):
    n_all = sum(g[0].shape[0] for g in groups)
    n_tiles_max = (n_all + GROUP_TILE - 1) // GROUP_TILE + N_GROUPS
    tile_group, n_used, row0 = _route_tables(cnt, n_tiles_max)
    xs = jnp.zeros((n_tiles_max * GROUP_TILE, MOE_ROW), F32)
    poss = []
    for hx, rt, _, _, td in groups:
        poss.append(_sorted_rows(rt, row0))
        xs = _dispatch(hx, poss[-1], xs, td=td)
    ys = _experts(xs, tile_group, n_used, wg, wu, wd)
    return [_tail(pos, x1, p2d, ys, nple, wpg, wpp, nfin, tm=tm)
            for pos, (_, _, x1, p2d, _) in zip(poss, groups)]


def kernel(x_prompt, x_sample, state_pool, state_shift, state_wkv, p_prompt, p_sample, norm_mix, w_in, pool_mix, pool_scale, w_branch_a, shift_mu, decay_w0, decay_w2, iclr_a0, iclr_a2, gate_g2, k_k, k_a, r_k, ln_x_w, ln_x_b, w_branch_b, w_out, norm_ffn, w_route_group, b_route_group, w_route_expert, b_route_expert, expert_gate, expert_up, expert_down, norm_ple, w_ple_gate, w_ple_proj, norm_final):
    l = 0
    bsz, seq, _ = x_prompt.shape
    dbsz, dseq, _ = x_sample.shape
    row = lambda vec: vec.reshape(1, -1).astype(F32)
    col = lambda vec: vec.reshape(-1, 1).astype(F32)
    scan = lambda arr, axis=0: _swap_channel_order(arr, axis, True)

    o1 = POOL_WIDTH
    o2 = o1 + SHIFT_WIDTH
    w_z = _swap_shift_order(w_in[l][:, o1:o2], 1, True).astype(BF16)
    zeros_lora = jnp.zeros((RWKV_WIDTH, LORA_PAIR // 2), F32)
    w2t_pad = jnp.concatenate([scan(decay_w2[l], 1).T, zeros_lora], axis=1).astype(BF16)
    a2t_pad = jnp.concatenate([zeros_lora, scan(iclr_a2[l], 1).T], axis=1).astype(BF16)
    w_router = jnp.concatenate(
        [w_route_expert[l], w_route_group[l],
         jnp.zeros((D_MODEL, ROUTER_LANES - N_EXPERTS - N_GROUPS), F32)], axis=1)
    wr_hi = w_router.astype(BF16)
    wr_lo = (w_router - wr_hi.astype(F32)).astype(BF16)
    b_router = jnp.concatenate(
        [b_route_expert[l], b_route_group[l],
         jnp.zeros((ROUTER_LANES - N_EXPERTS - N_GROUPS,), F32)]).reshape(1, -1)
    prep_w = [row(norm_mix[l]), w_in[l].astype(BF16), w_z,
              col(_swap_shift_order(shift_mu[l], 0, True)), col(scan(decay_w0[l])), w2t_pad,
              col(scan(iclr_a0[l])), a2t_pad, scan(gate_g2[l], 1).T.astype(BF16), col(scan(k_k[l])),
              col(scan(k_a[l])), col(scan(r_k[l].reshape(-1))), pool_mix[l].astype(BF16),
              row(pool_scale[l]), w_branch_a[l].astype(BF16)]
    post_w = [col(scan(ln_x_w[l])), col(scan(ln_x_b[l])), scan(w_branch_b[l]).astype(BF16),
              w_out[l].astype(BF16), row(norm_ffn[l]), wr_hi, wr_lo, b_router]
    moe_w = [expert_gate[l], expert_up[l], expert_down[l],
             row(norm_ple[l]), w_ple_gate[l].astype(BF16), w_ple_proj[l].astype(BF16), row(norm_final)]

    tm_p = PROMPT_TILE
    tiles_p = seq // tm_p
    x_p = x_prompt.reshape(bsz * seq, D_MODEL)
    outs = _mix_prep(x_p, jnp.zeros((SHIFT_WIDTH, LANES), F32),
                     jnp.zeros((16, POOL_WIDTH), F32), prep_w,
                     nb=bsz, tiles=tiles_p, tm=tm_p, s=1, pos0=0, cm_index=lambda b, t: (b, 0, t))
    u_p, zl_p, ops, g, bon, pa, sgb = outs
    o_p, s_p = _wkv_scan_prompt(ops, tt=LANES)
    tm_post = POST_TILE
    n_p = bsz * seq
    x1_p, hx_p, rt_p, cnt_p = _post(
        o_p, bon, g, pa, sgb, x_p, jnp.zeros((SUBLANES, ROUTER_LANES), F32), post_w,
        tm=tm_post, sub=POST_SUB_TILE, cm_index=lambda i: (i // (seq // tm_post), 0, i % (seq // tm_post)))
    wkv_p = s_p.reshape(KEY_HALF, 2, KEY_HALF, 2, bsz, HEADS).transpose(4, 5, 3, 2, 1, 0)
    wkv_p = wkv_p.reshape(bsz, HEADS, HEAD_DIM, HEAD_DIM)

    n_s = dbsz * dseq
    x_s = x_sample.transpose(1, 0, 2).reshape(n_s, D_MODEL)
    p_s = p_sample[l].transpose(1, 0, 2).reshape(n_s, PLE_DIM)
    uc0_s = jnp.concatenate(
        [jnp.zeros((dbsz, POOL_WIDTH), F32),
         state_pool[l].transpose(1, 0, 2).reshape(POOL_STATE * dbsz, POOL_WIDTH)], axis=0)
    zc0_s = _swap_shift_order(state_shift[l], 1, True).T
    outs = _mix_prep(x_s, zc0_s, uc0_s, prep_w, nb=1, tiles=dseq, tm=dbsz, s=dbsz, pos0=PAST_LEN,
                     cm_index=lambda b, t: (t, 0, 0))
    u_s, zl_s, ops, g, bon, pa, sgb = outs
    s0_s = state_wkv[l].reshape(dbsz, HEADS, 2, KEY_HALF, 2, KEY_HALF).transpose(1, 5, 4, 3, 2, 0)
    s0_s = s0_s.reshape(HEADS, HEAD_DIM, HEAD_DIM, dbsz)
    o_s, s_s = _wkv_scan_sample(ops, s0_s)
    x1_s, hx_s, rt_s, cnt_all = _post(o_s, bon, g, pa, sgb, x_s, cnt_p, post_w,
                                      tm=dbsz, sub=dbsz, cm_index=lambda i: (i, 0, 0))

    y_p, y_s = _moe_tail(
        [(hx_p, rt_p, x1_p, p_prompt[l].reshape(n_p, PLE_DIM), DISPATCH_TILE),
         (hx_s, rt_s, x1_s, p_s, n_s)],
        cnt_all, *moe_w, tm=TAIL_TILE)
    wkv_s = s_s.reshape(HEADS, KEY_HALF, 2, KEY_HALF, 2, dbsz).transpose(5, 0, 4, 3, 2, 1)
    wkv_s = wkv_s.reshape(dbsz, HEADS, HEAD_DIM, HEAD_DIM)

    y_prompt = y_p.reshape(bsz, seq, D_MODEL)
    y_sample = y_s.reshape(dseq, dbsz, D_MODEL).transpose(1, 0, 2)
    pool_prompt = u_p.reshape(bsz, seq, POOL_WIDTH)[:, seq - POOL_STATE:]
    u_s_bt = u_s.reshape(dseq, dbsz, POOL_WIDTH).transpose(1, 0, 2)
    pool_sample = jnp.concatenate([state_pool[l][:, dseq:], u_s_bt], axis=1)
    shift_prompt = _swap_shift_order(zl_p[:, SUBLANES - 1, :], 1, False)
    shift_sample = _swap_shift_order(zl_s[0], 1, False)
    return (y_prompt, y_sample, pool_prompt[None], shift_prompt[None], wkv_p[None],
            pool_sample[None], shift_sample[None], wkv_s[None])
```

```python
import functools
import math

import jax
import jax.numpy as jnp
from jax import lax
from jax.experimental import pallas as pl
from jax.experimental.pallas import tpu as pltpu

F32 = jnp.float32
BF16 = jnp.bfloat16

D_MODEL = 1024
PLE_DIM = 256
POOL_WIDTH = 512
POOL_WINDOWS = (2, 4, 8, 16)
POOL_GROUP_DIM = 128
POOL_STATE = 15
RWKV_WIDTH = 512
HEAD_DIM = 64
HEADS = 8
LORA_PAIR = 128
GATE_LORA = 128
SHIFT_WIDTH = 3 * RWKV_WIDTH + LORA_PAIR + GATE_LORA
IN_WIDTH = POOL_WIDTH + SHIFT_WIDTH + 2 * D_MODEL
N_GROUPS = 4
EXPERTS_PER_GROUP = 8
N_EXPERTS = 32
D_EXPERT = 256
RMS_EPS = 1e-6
GN_EPS = 64e-5
PAST_LEN = 16384

LANES = 128
SUBLANES = 8
KEY_HALF = HEAD_DIM // 2
SCAN_OPERANDS = 6
KEY_GROUP = 32
ROUTER_LANES = 128
PROMPT_TILE = 256
POST_TILE = 512
POST_SUB_TILE = 256
DISPATCH_TILE = 1024
TAIL_TILE = 256
GROUP_TILE = 1024
EXPERTS_PER_STEP = 2
DECAY_SCALE = math.exp(-0.5)
COMB_LANE0 = 8
MOE_ROW = D_MODEL + ROUTER_LANES
NEG_BIG = -1e30
VMEM_LIMIT = 56 * 1024 * 1024


def _swap_channel_order(x, axis, to_scan):
    x = jnp.moveaxis(x, axis, -1)
    lead = x.shape[:-1]
    split = (HEADS, 2, KEY_HALF) if to_scan else (KEY_HALF, 2, HEADS)
    x = jnp.swapaxes(x.reshape(lead + split), -1, -3).reshape(lead + (RWKV_WIDTH,))
    return jnp.moveaxis(x, -1, axis)


def _swap_shift_order(x, axis, to_scan):
    x = jnp.moveaxis(x, axis, -1)
    lead = x.shape[:-1]
    rkv = x[..., :3 * RWKV_WIDTH].reshape(lead + (3, RWKV_WIDTH))
    rkv = _swap_channel_order(rkv, -1, to_scan).reshape(lead + (3 * RWKV_WIDTH,))
    return jnp.moveaxis(jnp.concatenate([rkv, x[..., 3 * RWKV_WIDTH:]], axis=-1), -1, axis)


def _dot(a, b):
    return jnp.dot(a, b, preferred_element_type=F32)


def _split_bf16(x):
    hi = x.astype(BF16)
    return hi, (x - hi.astype(F32)).astype(BF16)


def _head_sum(x):
    x3 = x.reshape(HEAD_DIM, HEADS, x.shape[1])
    s = jnp.sum(x3, axis=0, keepdims=True)
    return jnp.broadcast_to(s, x3.shape).reshape(x.shape)


def _rmsnorm(x, g):
    return x * lax.rsqrt(jnp.mean(x * x, axis=-1, keepdims=True) + RMS_EPS) * g


def _sigmoid(x):
    return 0.5 * jnp.tanh(0.5 * x) + 0.5


def _mix_prep_kernel(x_ref, zc0_ref, uc0_ref, nmix_ref, win_ref, wz_ref, mu_ref, w0_ref,
                     w2t_ref, a0_ref, a2t_ref, g2t_ref, kk_ref, ka_ref, rk_ref, mix_ref,
                     pscale_ref, wa_ref,
                     u_ref, zl_ref, q_ref, g_ref, bon_ref,
                     pa_ref, sgb_ref, zc, uext, *, tm, s, pos0):
    t = pl.program_id(1)
    up = 16 * s

    @pl.when(t == 0)
    def _():
        zc[...] = zc0_ref[...]
        uext[0:up] = uc0_ref[...]

    x = x_ref[...]
    h = _rmsnorm(x, nmix_ref[...]).astype(BF16)
    u = _dot(h, win_ref[:, 0:POOL_WIDTH])
    gab = _dot(h, win_ref[:, POOL_WIDTH + SHIFT_WIDTH:IN_WIDTH])
    u_ref[...] = u
    sgb_ref[...] = _sigmoid(gab[:, D_MODEL:])
    uext[up:up + tm] = u

    z = _dot(h, wz_ref[...])
    zl_ref[0] = z[tm - max(SUBLANES, s):tm]
    z_t = z.T
    if s == 1:
        rolled = pltpu.roll(z_t, 1, axis=1)
        lane = lax.broadcasted_iota(jnp.int32, (SHIFT_WIDTH, LANES), 1)
        first = jnp.where(lane == 0, zc[:, LANES - 1:LANES], rolled[:, 0:LANES])
        zprev = jnp.concatenate([first, rolled[:, LANES:]], axis=1)
    else:
        zprev = zc[...]
    zc[...] = z_t[:, tm - LANES:tm]
    zm = z_t + (zprev - z_t) * mu_ref[...]
    r = zm[0:RWKV_WIDTH]
    k = zm[RWKV_WIDTH:2 * RWKV_WIDTH]
    v = zm[2 * RWKV_WIDTH:3 * RWKV_WIDTH]
    lora_in = zm[3 * RWKV_WIDTH:3 * RWKV_WIDTH + LORA_PAIR]
    gd = zm[3 * RWKV_WIDTH + LORA_PAIR:SHIFT_WIDTH]
    dw = _dot(w2t_ref[...], jnp.tanh(lora_in).astype(BF16))
    da = _dot(a2t_ref[...], lora_in.astype(BF16))
    decay = jnp.exp(-DECAY_SCALE * _sigmoid(w0_ref[...] + dw))
    a = _sigmoid(a0_ref[...] + da)
    kk = k * kk_ref[...]
    kk = kk / jnp.maximum(jnp.sqrt(_head_sum(kk * kk)), 1e-12)
    k2 = k * (1.0 + (a - 1.0) * ka_ref[...])
    q_ref[0, 0] = -kk
    q_ref[1, 0] = decay
    q_ref[2, 0] = kk * a
    q_ref[3, 0] = k2
    q_ref[4, 0] = r
    q_ref[5, 0] = v
    g_ref[0] = _dot(g2t_ref[...], _sigmoid(gd).astype(BF16))
    bon_ref[0] = _head_sum(r * k2 * rk_ref[...]) * v

    rows = lax.broadcasted_iota(jnp.int32, (tm, POOL_GROUP_DIM), 0)
    if s > 1:
        rows = rows // s
    pos = pos0 + t * (tm // s) + rows
    ys = []
    for gi, wnd in enumerate(POOL_WINDOWS):
        lanes = slice(gi * POOL_GROUP_DIM, (gi + 1) * POOL_GROUP_DIM)
        wsum = uext[:, lanes]
        span = 1
        while span < wnd:
            wsum = wsum + pltpu.roll(wsum, span * s, axis=0)
            span *= 2
        cur = uext[pl.ds(up, tm), lanes]
        cnt = jnp.minimum(pos + 1, wnd).astype(F32)
        pooled = wsum[up:up + tm] / cnt - cur
        ys.append(_dot(pooled.astype(BF16), mix_ref[gi]))
    y = jnp.concatenate(ys, axis=-1) * pscale_ref[...]
    pa_ref[...] = _sigmoid(gab[:, :D_MODEL]) * _dot(y.astype(BF16), wa_ref[...])

    uext[0:up] = uext[tm:tm + up]


def _mix_prep(x2d, zc0, uc0, wts, *, nb, tiles, tm, s, pos0, cm_index):
    n = x2d.shape[0]
    up = 16 * s
    row = lambda b, t: (b * tiles + t, 0)
    full = lambda arr: pl.BlockSpec(arr.shape, lambda b, t: (0,) * arr.ndim)
    in_specs = [
        pl.BlockSpec((tm, D_MODEL), row),
        pl.BlockSpec((SHIFT_WIDTH, LANES), lambda b, t: (b % (zc0.shape[0] // SHIFT_WIDTH), 0)),
        pl.BlockSpec((up, POOL_WIDTH), lambda b, t: (b % (uc0.shape[0] // up), 0)),
    ] + [full(w) for w in wts]
    tok = lambda width: pl.BlockSpec((tm, width), row)
    cm_shape = (nb, RWKV_WIDTH, tiles * tm) if s == 1 else (tiles, RWKV_WIDTH, tm)
    cm = pl.BlockSpec((1, RWKV_WIDTH, tm), cm_index)
    cm_stack = pl.BlockSpec((SCAN_OPERANDS, 1, RWKV_WIDTH, tm), lambda b, t: (0,) + cm_index(b, t))
    zrows = max(SUBLANES, s)
    out_specs = [tok(POOL_WIDTH), pl.BlockSpec((1, zrows, SHIFT_WIDTH), lambda b, t: (b, 0, 0)),
                 cm_stack, cm, cm] + [tok(D_MODEL)] * 2
    out_shape = [jax.ShapeDtypeStruct((n, POOL_WIDTH), F32),
                 jax.ShapeDtypeStruct((nb, zrows, SHIFT_WIDTH), F32),
                 jax.ShapeDtypeStruct((SCAN_OPERANDS,) + cm_shape, F32)] \
        + [jax.ShapeDtypeStruct(cm_shape, F32)] * 2 \
        + [jax.ShapeDtypeStruct((n, D_MODEL), F32)] * 2
    return pl.pallas_call(
        functools.partial(_mix_prep_kernel, tm=tm, s=s, pos0=pos0),
        grid=(nb, tiles),
        in_specs=in_specs,
        out_specs=out_specs,
        out_shape=out_shape,
        scratch_shapes=[pltpu.VMEM((SHIFT_WIDTH, LANES), F32),
                        pltpu.VMEM((up + tm, POOL_WIDTH), F32)],
        compiler_params=pltpu.CompilerParams(
            dimension_semantics=("arbitrary", "arbitrary"), vmem_limit_bytes=VMEM_LIMIT),
        name="mix_prep",
    )(x2d, zc0, uc0, *wts)


def _wkv_step(s_ref, row, vv):
    groups = HEAD_DIM // KEY_GROUP

    def sa_pass(g, acc):
        for kk in range(KEY_GROUP):
            kp = g * KEY_GROUP + kk
            acc = acc + s_ref[kp] * row(0, kp)
        return acc
    sa = lax.fori_loop(0, groups, sa_pass, jnp.zeros(vv.shape, F32))

    def update_pass(g, acc):
        for kk in range(KEY_GROUP):
            kp = g * KEY_GROUP + kk
            sn = s_ref[kp] * row(1, kp) + sa * row(2, kp) + vv * row(3, kp)
            s_ref[kp] = sn
            acc = acc + sn * row(4, kp)
        return acc
    return lax.fori_loop(0, groups, update_pass, jnp.zeros(vv.shape, F32))


def _swap_major_sublane(x):
    return jnp.swapaxes(x, 0, 1)


def _wkv_scan_prompt_kernel(q_ref, o_ref, sout_ref, s_ref, kv_ref, vv_ref, ov_ref, *, tt):
    t = pl.program_id(0)
    q = pl.program_id(1)
    nb = q_ref.shape[1]

    @pl.when((t == 0) & (q == 0))
    def _():
        s_ref[...] = jnp.zeros_like(s_ref)

    def gather_t(base_lo, base_hi):
        pieces = [q_ref[0, bb, pl.ds(base, SUBLANES), :]
                  for base in (base_lo, base_hi) for bb in range(nb)]
        return jnp.concatenate(pieces, axis=0).T

    @pl.when(q < SCAN_OPERANDS - 1)
    def _():
        def kgroup(gi, c):
            slabs = []
            for kk in range(SUBLANES):
                base = pl.multiple_of((gi * SUBLANES + kk) * SUBLANES, SUBLANES)
                slabs.append(gather_t(base, base))
            kv_ref[q, :, pl.ds(pl.multiple_of(gi * SUBLANES, SUBLANES), SUBLANES), :] = \
                _swap_major_sublane(jnp.stack(slabs))
            return c
        lax.fori_loop(0, HEAD_DIM // SUBLANES, kgroup, 0, unroll=True)

    @pl.when(q == SCAN_OPERANDS - 1)
    def _():
        def vgroup(gi, c):
            slabs = []
            for vi in range(SUBLANES):
                base = pl.multiple_of((gi * SUBLANES + vi) * 2 * SUBLANES, 2 * SUBLANES)
                slabs.append(gather_t(base, base + SUBLANES))
            vv_ref[:, pl.ds(pl.multiple_of(gi * SUBLANES, SUBLANES), SUBLANES), :] = \
                _swap_major_sublane(jnp.stack(slabs))
            return c
        lax.fori_loop(0, KEY_HALF // SUBLANES, vgroup, 0, unroll=True)

        def step(i, c):
            row = lambda qi, kp: kv_ref[qi, i, pl.ds(kp, 1), :]
            ov_ref[i] = _wkv_step(s_ref, row, vv_ref[i])
            return c
        lax.fori_loop(0, tt, step, 0)

        def ogroup(gi, c):
            g0 = pl.multiple_of(gi * SUBLANES, SUBLANES)
            x = _swap_major_sublane(ov_ref[:, pl.ds(g0, SUBLANES), :])
            for vi in range(SUBLANES):
                xt = x[vi].T
                base = pl.multiple_of((gi * SUBLANES + vi) * 2 * SUBLANES, 2 * SUBLANES)
                for vh in range(2):
                    for bb in range(nb):
                        r0 = (vh * nb + bb) * SUBLANES
                        o_ref[bb, pl.ds(base + vh * SUBLANES, SUBLANES), :] = xt[r0:r0 + SUBLANES]
            return c
        lax.fori_loop(0, KEY_HALF // SUBLANES, ogroup, 0, unroll=True)

        @pl.when(t == pl.num_programs(0) - 1)
        def _():
            sout_ref[...] = s_ref[...]


def _wkv_scan_prompt(ops, *, tt):
    _, nb, _, t_len = ops.shape
    assert 2 * nb * HEADS == LANES
    sspec = pl.BlockSpec((HEAD_DIM, KEY_HALF, LANES), lambda ti, qi: (0, 0, 0))
    return pl.pallas_call(
        functools.partial(_wkv_scan_prompt_kernel, tt=tt),
        grid=(t_len // tt, SCAN_OPERANDS),
        in_specs=[pl.BlockSpec((1, nb, RWKV_WIDTH, tt), lambda ti, qi: (qi, 0, 0, ti))],
        out_specs=[pl.BlockSpec((nb, RWKV_WIDTH, tt), lambda ti, qi: (0, 0, ti)), sspec],
        out_shape=[jax.ShapeDtypeStruct((nb, RWKV_WIDTH, t_len), F32),
                   jax.ShapeDtypeStruct((HEAD_DIM, KEY_HALF, LANES), F32)],
        scratch_shapes=[pltpu.VMEM((HEAD_DIM, KEY_HALF, LANES), F32),
                        pltpu.VMEM((SCAN_OPERANDS - 1, tt, HEAD_DIM, LANES), F32),
                        pltpu.VMEM((tt, KEY_HALF, LANES), F32),
                        pltpu.VMEM((tt, KEY_HALF, LANES), F32)],
        compiler_params=pltpu.CompilerParams(
            dimension_semantics=("arbitrary", "arbitrary"), vmem_limit_bytes=VMEM_LIMIT),
        name="wkv_scan_prompt",
    )(ops)


def _wkv_scan_sample_kernel(q_ref, s0_ref, o_ref, sout_ref, s_ref, *, t_len):
    h = pl.program_id(0)
    s_ref[...] = s0_ref[0]
    for i in range(t_len):
        row = lambda qi, kp, i=i: q_ref[qi, i, pl.ds(kp * HEADS + h, 1), :]
        vv = q_ref[SCAN_OPERANDS - 1, i, pl.ds(h, HEAD_DIM, stride=HEADS), :]
        o_ref[i, pl.ds(h, HEAD_DIM, stride=HEADS), :] = _wkv_step(s_ref, row, vv)
    sout_ref[0] = s_ref[...]


def _wkv_scan_sample(ops, s0):
    t_len = ops.shape[1]
    spec = pl.BlockSpec(ops.shape[1:], lambda h: (0, 0, 0))
    sspec = pl.BlockSpec((1, HEAD_DIM, HEAD_DIM, LANES), lambda h: (h, 0, 0, 0))
    return pl.pallas_call(
        functools.partial(_wkv_scan_sample_kernel, t_len=t_len),
        grid=(HEADS,),
        in_specs=[pl.BlockSpec(ops.shape, lambda h: (0, 0, 0, 0)), sspec],
        out_specs=[spec, sspec],
        out_shape=[jax.ShapeDtypeStruct(ops.shape[1:], F32), jax.ShapeDtypeStruct(s0.shape, F32)],
        scratch_shapes=[pltpu.VMEM((HEAD_DIM, HEAD_DIM, LANES), F32)],
        compiler_params=pltpu.CompilerParams(
            dimension_semantics=("arbitrary",), vmem_limit_bytes=VMEM_LIMIT),
        name="wkv_scan_sample",
    )(ops, s0)


def _post_kernel(o_ref, bon_ref, g_ref, pa_ref, sgb_ref, x_ref, cnt0_ref, lnw_ref, lnb_ref,
                 wb_ref, wout_ref, nffn_ref, wrh_ref, wrl_ref, br_ref, x1_ref, h2_ref, rt_ref, cnt_ref,
                 carry, *, sub):
    @pl.when(pl.program_id(0) == 0)
    def _():
        carry[...] = cnt0_ref[...]

    parts = [pl.ds(part * sub, sub) for part in range(o_ref.shape[2] // sub)]
    logits = [_post_project(rows, o_ref, bon_ref, g_ref, pa_ref, sgb_ref, x_ref, lnw_ref, lnb_ref,
                            wb_ref, wout_ref, nffn_ref, wrh_ref, wrl_ref, br_ref, x1_ref, h2_ref)
              for rows in parts]
    prev = carry[:, 0:1]
    for rows, lg in zip(parts, logits):
        prev = _post_route(rows, prev, lg, h2_ref, rt_ref)
    carry[...] = jnp.broadcast_to(prev, carry.shape)
    cnt_ref[...] = jnp.broadcast_to(prev, cnt_ref.shape)


def _post_project(rows, o_ref, bon_ref, g_ref, pa_ref, sgb_ref, x_ref, lnw_ref, lnb_ref,
                  wb_ref, wout_ref, nffn_ref, wrh_ref, wrl_ref, br_ref, x1_ref, h2_ref):
    o = o_ref[0, :, rows]
    mean = _head_sum(o) * (1.0 / HEAD_DIM)
    d = o - mean
    var = _head_sum(d * d) * (1.0 / HEAD_DIM)
    on = d * lax.rsqrt(var + GN_EPS) * lnw_ref[...] + lnb_ref[...]
    yb = ((on + bon_ref[0, :, rows]) * g_ref[0, :, rows]).astype(BF16)
    mb = lax.dot_general(yb, wb_ref[...], (((0,), (0,)), ((), ())), preferred_element_type=F32)
    merged = pa_ref[rows, :] + sgb_ref[rows, :] * mb
    x1 = x_ref[rows, :] + _dot(merged.astype(BF16), wout_ref[...])
    x1_ref[rows, :] = x1
    h2 = _rmsnorm(x1, nffn_ref[...])
    h2_ref[rows, 0:D_MODEL] = h2

    h_hi, h_lo = _split_bf16(h2)
    return (_dot(h_hi, wrh_ref[...]) + _dot(h_lo, wrh_ref[...]) + _dot(h_hi, wrl_ref[...])
            + br_ref[...])


def _post_route(rows, prev_count, logits, h2_ref, rt_ref):
    tm = logits.shape[0]
    lt = logits.T
    grow = lax.broadcasted_iota(jnp.int32, (SUBLANES, tm), 0)
    is_group = grow < N_GROUPS
    gl = jnp.where(is_group, lt[N_EXPERTS:N_EXPERTS + SUBLANES], NEG_BIG)
    gmax = jnp.max(gl, axis=0, keepdims=True)
    gsel = jnp.min(jnp.where(gl == gmax, grow, SUBLANES), axis=0, keepdims=True)
    den = jnp.sum(jnp.where(is_group, jnp.exp(gl - gmax), 0.0), axis=0, keepdims=True)
    pg = 1.0 / den
    erow = lax.broadcasted_iota(jnp.int32, (N_EXPERTS, tm), 0)
    el = jnp.where((erow // EXPERTS_PER_GROUP) == gsel, lt[0:N_EXPERTS], NEG_BIG)
    m1 = jnp.max(el, axis=0, keepdims=True)
    i1 = jnp.min(jnp.where(el == m1, erow, N_EXPERTS), axis=0, keepdims=True)
    el2 = jnp.where(erow == i1, NEG_BIG, el)
    m2 = jnp.max(el2, axis=0, keepdims=True)
    i2 = jnp.min(jnp.where(el2 == m2, erow, N_EXPERTS), axis=0, keepdims=True)
    e2 = jnp.exp(m2 - m1)
    p1 = 1.0 / (1.0 + e2)
    p2 = e2 / (1.0 + e2)

    sel = grow == gsel
    earlier = (lax.broadcasted_iota(jnp.int32, (tm, tm), 0)
               < lax.broadcasted_iota(jnp.int32, (tm, tm), 1)).astype(BF16)
    before = prev_count + _dot(sel.astype(BF16), earlier)
    rank = jnp.sum(jnp.where(sel, before, 0.0), axis=0, keepdims=True)
    rrow = lax.broadcasted_iota(jnp.int32, (2 * SUBLANES, tm), 0)
    first = gsel * EXPERTS_PER_GROUP - COMB_LANE0
    rt_t = (jnp.where(rrow == 0, gsel.astype(F32), 0.0) + jnp.where(rrow == 1, rank, 0.0)
            + jnp.where(rrow == i1 - first, p1 * pg, 0.0) + jnp.where(rrow == i2 - first, p2 * pg, 0.0))
    rt = jnp.concatenate([rt_t, jnp.zeros((ROUTER_LANES - 2 * SUBLANES, tm), F32)], axis=0).T
    rt_ref[rows, :] = rt
    h2_ref[rows, D_MODEL:MOE_ROW] = rt
    return prev_count + jnp.sum(sel.astype(F32), axis=1, keepdims=True)


def _post(o, bon, g, pa, sgb, x2d, cnt0, wts, *, tm, sub, cm_index):
    n = x2d.shape[0]
    row = lambda i: (i, 0)
    tok = lambda width: pl.BlockSpec((tm, width), row)
    cm = pl.BlockSpec((1, RWKV_WIDTH, tm), cm_index)
    full = lambda arr: pl.BlockSpec(arr.shape, lambda i: (0, 0))
    return pl.pallas_call(
        functools.partial(_post_kernel, sub=sub),
        grid=(n // tm,),
        in_specs=[cm] * 3 + [tok(D_MODEL)] * 3 + [full(cnt0)] + [full(w) for w in wts],
        out_specs=[tok(D_MODEL), tok(MOE_ROW), tok(ROUTER_LANES),
                   pl.BlockSpec((SUBLANES, ROUTER_LANES), lambda i: (0, 0))],
        out_shape=[jax.ShapeDtypeStruct((n, D_MODEL), F32),
                   jax.ShapeDtypeStruct((n, MOE_ROW), F32),
                   jax.ShapeDtypeStruct((n, ROUTER_LANES), F32),
                   jax.ShapeDtypeStruct((SUBLANES, ROUTER_LANES), F32)],
        scratch_shapes=[pltpu.VMEM((SUBLANES, ROUTER_LANES), F32)],
        compiler_params=pltpu.CompilerParams(
            dimension_semantics=("arbitrary",), vmem_limit_bytes=VMEM_LIMIT),
        name="post",
    )(o, bon, g, pa, sgb, x2d, cnt0, *wts)


def _route_tables(cnt, n_tiles_max):
    counts = cnt[:N_GROUPS, 0].astype(jnp.int32)
    tiles_g = (counts + GROUP_TILE - 1) // GROUP_TILE
    tile_end = jnp.cumsum(tiles_g)
    row0 = (tile_end - tiles_g) * GROUP_TILE
    tile_ids = jnp.arange(n_tiles_max, dtype=jnp.int32)
    tile_group = jnp.minimum(jnp.sum(tile_ids[:, None] >= tile_end[None, :], axis=-1),
                             N_GROUPS - 1).astype(jnp.int32)
    return tile_group, tile_end[N_GROUPS - 1:].astype(jnp.int32), row0


def _sorted_rows(rt, row0):
    g = rt[:, 0].astype(jnp.int32)
    r = rt[:, 1].astype(jnp.int32)
    return r + jnp.sum(jnp.where(g[:, None] == jnp.arange(N_GROUPS), row0, 0), axis=-1)


def _dispatch_kernel(pos_ref, h_ref, xs0_ref, xs_ref, sem, *, td):
    del xs0_ref
    for r in range(td):
        pltpu.make_async_copy(h_ref.at[pl.ds(r, 1)], xs_ref.at[pl.ds(pos_ref[r], 1)],
                              sem).start(priority=r % 2)
    pltpu.make_async_copy(h_ref, xs_ref.at[pl.ds(0, td)], sem).wait()


def _dispatch(h, pos, xs_in, *, td):
    n = h.shape[0]
    return pl.pallas_call(
        functools.partial(_dispatch_kernel, td=td),
        grid=(n // td,),
        in_specs=[pl.BlockSpec((td,), lambda i: (i,), memory_space=pltpu.SMEM),
                  pl.BlockSpec((td, MOE_ROW), lambda i: (i, 0)),
                  pl.BlockSpec(memory_space=pl.ANY)],
        out_specs=pl.BlockSpec(memory_space=pl.ANY),
        out_shape=jax.ShapeDtypeStruct(xs_in.shape, F32),
        scratch_shapes=[pltpu.SemaphoreType.DMA(())],
        input_output_aliases={2: 0},
        compiler_params=pltpu.CompilerParams(
            dimension_semantics=("arbitrary",), vmem_limit_bytes=VMEM_LIMIT),
        name="moe_dispatch",
    )(pos, h, xs_in)


def _experts_kernel(tg_ref, nu_ref, xs_ref, wg_ref, wu_ref, wd_ref, ys_ref, acc, xb):
    t = pl.program_id(0)
    e = pl.program_id(1)
    last_e = pl.num_programs(1) - 1

    @pl.when(t < nu_ref[0])
    def _():
        @pl.when(e == 0)
        def _():
            acc[...] = jnp.zeros_like(acc)
            xb[...] = xs_ref[:, 0:D_MODEL].astype(BF16)

        x = xb[...]
        rt = xs_ref[:, D_MODEL:MOE_ROW]
        ln = lax.broadcasted_iota(jnp.int32, rt.shape, 1)
        y = acc[...]
        for j in range(EXPERTS_PER_STEP):
            gate = _dot(x, wg_ref[j].astype(BF16))
            hid = gate * _sigmoid(gate) * _dot(x, wu_ref[j].astype(BF16))
            ye = _dot(hid.astype(BF16), wd_ref[j].astype(BF16))
            lane = COMB_LANE0 + e * EXPERTS_PER_STEP + j
            ce = jnp.sum(jnp.where(ln == lane, rt, 0.0), axis=-1, keepdims=True)
            y = y + ce * ye
        acc[...] = y

        @pl.when(e == last_e)
        def _():
            ys_ref[...] = acc[...]

    @pl.when((t >= nu_ref[0]) & (e == last_e))
    def _():
        ys_ref[...] = jnp.zeros_like(ys_ref)


def _experts(xs, tile_group, n_used, wg, wu, wd):
    n_rows = xs.shape[0]
    tile = lambda t, e, tg, nu: (jnp.minimum(t, nu[0] - 1), 0)
    steps = EXPERTS_PER_GROUP // EXPERTS_PER_STEP
    wsel = lambda t, e, tg, nu: (tg[jnp.minimum(t, nu[0] - 1)] * steps + e, 0, 0)
    return pl.pallas_call(
        _experts_kernel,
        grid_spec=pltpu.PrefetchScalarGridSpec(
            num_scalar_prefetch=2,
            grid=(n_rows // GROUP_TILE, steps),
            in_specs=[pl.BlockSpec((GROUP_TILE, MOE_ROW), tile),
                      pl.BlockSpec((EXPERTS_PER_STEP, D_MODEL, D_EXPERT), wsel),
                      pl.BlockSpec((EXPERTS_PER_STEP, D_MODEL, D_EXPERT), wsel),
                      pl.BlockSpec((EXPERTS_PER_STEP, D_EXPERT, D_MODEL), wsel)],
            out_specs=pl.BlockSpec((GROUP_TILE, D_MODEL), lambda t, e, tg, nu: (t, 0)),
            scratch_shapes=[pltpu.VMEM((GROUP_TILE, D_MODEL), F32),
                            pltpu.VMEM((GROUP_TILE, D_MODEL), BF16)]),
        out_shape=jax.ShapeDtypeStruct((n_rows, D_MODEL), F32),
        compiler_params=pltpu.CompilerParams(
            dimension_semantics=("arbitrary", "arbitrary"), vmem_limit_bytes=VMEM_LIMIT),
        name="moe_experts",
    )(tile_group, n_used, xs, wg, wu, wd)


def _tail_kernel(pos_ref, posn_ref, x1_ref, p_ref, ys_ref, nple_ref, wpg_ref, wpp_ref,
                 nfin_ref, y_ref, ybuf0, ybuf1, sem, *, tm):
    i = pl.program_id(0)
    last = pl.num_programs(0) - 1
    ybuf = (ybuf0, ybuf1)

    def issue(p_ref, p_off, slot):
        for r in range(tm):
            pltpu.make_async_copy(ys_ref.at[pl.ds(p_ref[p_off + r], 1)],
                                  ybuf[slot].at[pl.ds(r, 1)], sem.at[slot]).start(priority=r % 2)

    def wait(slot):
        pltpu.make_async_copy(ys_ref.at[pl.ds(0, tm)], ybuf[slot], sem.at[slot]).wait()

    def compute(slot):
        rows = pl.ds(slot * tm, tm)
        x2 = x1_ref[rows, :] + ybuf[slot][...]
        h3 = _rmsnorm(x2, nple_ref[...]).astype(BF16)
        ple = (_sigmoid(_dot(h3, wpg_ref[...]))
               * _dot(p_ref[rows, :].astype(BF16), wpp_ref[...]))
        y_ref[rows, :] = _rmsnorm(x2 + ple, nfin_ref[...])

    @pl.when(i == 0)
    def _():
        issue(pos_ref, 0, 0)

    wait(0)
    issue(pos_ref, tm, 1)
    compute(0)
    wait(1)
    issue(posn_ref, 0, 0)
    compute(1)

    @pl.when(i == last)
    def _():
        wait(0)


def _tail(pos, x1, p2d, ys, nple, wpg, wpp, nfin, *, tm):
    n = x1.shape[0]
    n_steps = n // (2 * tm)
    tok = lambda width: pl.BlockSpec((2 * tm, width), lambda i: (i, 0))
    full = lambda arr: pl.BlockSpec(arr.shape, lambda i: (0, 0))
    return pl.pallas_call(
        functools.partial(_tail_kernel, tm=tm),
        grid=(n_steps,),
        in_specs=[pl.BlockSpec((2 * tm,), lambda i: (i,), memory_space=pltpu.SMEM),
                  pl.BlockSpec((tm,), lambda i: (2 * jnp.minimum(i + 1, n_steps - 1),),
                               memory_space=pltpu.SMEM),
                  tok(D_MODEL), tok(PLE_DIM),
                  pl.BlockSpec(memory_space=pl.ANY),
                  full(nple), full(wpg), full(wpp), full(nfin)],
        out_specs=tok(D_MODEL),
        out_shape=jax.ShapeDtypeStruct((n, D_MODEL), F32),
        scratch_shapes=[pltpu.VMEM((tm, D_MODEL), F32), pltpu.VMEM((tm, D_MODEL), F32),
                        pltpu.SemaphoreType.DMA((2,))],
        compiler_params=pltpu.CompilerParams(
            dimension_semantics=("arbitrary",), vmem_limit_bytes=VMEM_LIMIT),
        name="moe_tail",
    )(pos, pos, x1, p2d, ys, nple, wpg, wpp, nfin)


def _moe_tail(groups, cnt, wg, wu, wd, nple, wpg, wpp, nfin, *, tm):
    n_all = sum(g[0].shape[0] for g in groups)
    n_tiles_max = (n_all + GROUP_TILE - 1) // GROUP_TILE + N_GROUPS
    tile_group, n_used, row0 = _route_tables(cnt, n_tiles_max)
    xs = jnp.zeros((n_tiles_max * GROUP_TILE, MOE_ROW), F32)
    poss = []
    for hx, rt, _, _, td in groups:
        poss.append(_sorted_rows(rt, row0))
        xs = _dispatch(hx, poss[-1], xs, td=td)
    ys = _experts(xs, tile_group, n_used, wg, wu, wd)
    return [_tail(pos, x1, p2d, ys, nple, wpg, wpp, nfin, tm=tm)
            for pos, (_, _, x1, p2d, _) in zip(poss, groups)]


def kernel(x_prompt, x_sample, state_pool, state_shift, state_wkv, p_prompt, p_sample, norm_mix, w_in, pool_mix, pool_scale, w_branch_a, shift_mu, decay_w0, decay_w2, iclr_a0, iclr_a2, gate_g2, k_k, k_a, r_k, ln_x_w, ln_x_b, w_branch_b, w_out, norm_ffn, w_route_group, b_route_group, w_route_expert, b_route_expert, expert_gate, expert_up, expert_down, norm_ple, w_ple_gate, w_ple_proj, norm_final):
    l = 0
    bsz, seq, _ = x_prompt.shape
    dbsz, dseq, _ = x_sample.shape
    row = lambda vec: vec.reshape(1, -1).astype(F32)
    col = lambda vec: vec.reshape(-1, 1).astype(F32)
    scan = lambda arr, axis=0: _swap_channel_order(arr, axis, True)

    o1 = POOL_WIDTH
    o2 = o1 + SHIFT_WIDTH
    w_z = _swap_shift_order(w_in[l][:, o1:o2], 1, True).astype(BF16)
    zeros_lora = jnp.zeros((RWKV_WIDTH, LORA_PAIR // 2), F32)
    w2t_pad = jnp.concatenate([scan(decay_w2[l], 1).T, zeros_lora], axis=1).astype(BF16)
    a2t_pad = jnp.concatenate([zeros_lora, scan(iclr_a2[l], 1).T], axis=1).astype(BF16)
    w_router = jnp.concatenate(
        [w_route_expert[l], w_route_group[l],
         jnp.zeros((D_MODEL, ROUTER_LANES - N_EXPERTS - N_GROUPS), F32)], axis=1)
    wr_hi = w_router.astype(BF16)
    wr_lo = (w_router - wr_hi.astype(F32)).astype(BF16)
    b_router = jnp.concatenate(
        [b_route_expert[l], b_route_group[l],
         jnp.zeros((ROUTER_LANES - N_EXPERTS - N_GROUPS,), F32)]).reshape(1, -1)
    prep_w = [row(norm_mix[l]), w_in[l].astype(BF16), w_z,
              col(_swap_shift_order(shift_mu[l], 0, True)), col(scan(decay_w0[l])), w2t_pad,
              col(scan(iclr_a0[l])), a2t_pad, scan(gate_g2[l], 1).T.astype(BF16), col(scan(k_k[l])),
              col(scan(k_a[l])), col(scan(r_k[l].reshape(-1))), pool_mix[l].astype(BF16),
              row(pool_scale[l]), w_branch_a[l].astype(BF16)]
    post_w = [col(scan(ln_x_w[l])), col(scan(ln_x_b[l])), scan(w_branch_b[l]).astype(BF16),
              w_out[l].astype(BF16), row(norm_ffn[l]), wr_hi, wr_lo, b_router]
    moe_w = [expert_gate[l], expert_up[l], expert_down[l],
             row(norm_ple[l]), w_ple_gate[l].astype(BF16), w_ple_proj[l].astype(BF16), row(norm_final)]

    tm_p = PROMPT_TILE
    tiles_p = seq // tm_p
    x_p = x_prompt.reshape(bsz * seq, D_MODEL)
    outs = _mix_prep(x_p, jnp.zeros((SHIFT_WIDTH, LANES), F32),
                     jnp.zeros((16, POOL_WIDTH), F32), prep_w,
                     nb=bsz, tiles=tiles_p, tm=tm_p, s=1, pos0=0, cm_index=lambda b, t: (b, 0, t))
    u_p, zl_p, ops, g, bon, pa, sgb = outs
    o_p, s_p = _wkv_scan_prompt(ops, tt=LANES)
    tm_post = POST_TILE
    n_p = bsz * seq
    x1_p, hx_p, rt_p, cnt_p = _post(
        o_p, bon, g, pa, sgb, x_p, jnp.zeros((SUBLANES, ROUTER_LANES), F32), post_w,
        tm=tm_post, sub=POST_SUB_TILE, cm_index=lambda i: (i // (seq // tm_post), 0, i % (seq // tm_post)))
    wkv_p = s_p.reshape(KEY_HALF, 2, KEY_HALF, 2, bsz, HEADS).transpose(4, 5, 3, 2, 1, 0)
    wkv_p = wkv_p.reshape(bsz, HEADS, HEAD_DIM, HEAD_DIM)

    n_s = dbsz * dseq
    x_s = x_sample.transpose(1, 0, 2).reshape(n_s, D_MODEL)
    p_s = p_sample[l].transpose(1, 0, 2).reshape(n_s, PLE_DIM)
    uc0_s = jnp.concatenate(
        [jnp.zeros((dbsz, POOL_WIDTH), F32),
         state_pool[l].transpose(1, 0, 2).reshape(POOL_STATE * dbsz, POOL_WIDTH)], axis=0)
    zc0_s = _swap_shift_order(state_shift[l], 1, True).T
    outs = _mix_prep(x_s, zc0_s, uc0_s, prep_w, nb=1, tiles=dseq, tm=dbsz, s=dbsz, pos0=PAST_LEN,
                     cm_index=lambda b, t: (t, 0, 0))
    u_s, zl_s, ops, g, bon, pa, sgb = outs
    s0_s = state_wkv[l].reshape(dbsz, HEADS, 2, KEY_HALF, 2, KEY_HALF).transpose(1, 5, 4, 3, 2, 0)
    s0_s = s0_s.reshape(HEADS, HEAD_DIM, HEAD_DIM, dbsz)
    o_s, s_s = _wkv_scan_sample(ops, s0_s)
    x1_s, hx_s, rt_s, cnt_all = _post(o_s, bon, g, pa, sgb, x_s, cnt_p, post_w,
                                      tm=dbsz, sub=dbsz, cm_index=lambda i: (i, 0, 0))

    y_p, y_s = _moe_tail(
        [(hx_p, rt_p, x1_p, p_prompt[l].reshape(n_p, PLE_DIM), DISPATCH_TILE),
         (hx_s, rt_s, x1_s, p_s, n_s)],
        cnt_all, *moe_w, tm=TAIL_TILE)
    wkv_s = s_s.reshape(HEADS, KEY_HALF, 2, KEY_HALF, 2, dbsz).transpose(5, 0, 4, 3, 2, 1)
    wkv_s = wkv_s.reshape(dbsz, HEADS, HEAD_DIM, HEAD_DIM)

    y_prompt = y_p.reshape(bsz, seq, D_MODEL)
    y_sample = y_s.reshape(dseq, dbsz, D_MODEL).transpose(1, 0, 2)
    pool_prompt = u_p.reshape(bsz, seq, POOL_WIDTH)[:, seq - POOL_STATE:]
    u_s_bt = u_s.reshape(dseq, dbsz, POOL_WIDTH).transpose(1, 0, 2)
    pool_sample = jnp.concatenate([state_pool[l][:, dseq:], u_s_bt], axis=1)
    shift_prompt = _swap_shift_order(zl_p[:, SUBLANES - 1, :], 1, False)
    shift_sample = _swap_shift_order(zl_s[0], 1, False)
    return (y_prompt, y_sample, pool_prompt[None], shift_prompt[None], wkv_p[None],
            pool_sample[None], shift_sample[None], wkv_s[None])
```

```python
import functools
import math

import jax
import jax.numpy as jnp
from jax import lax
from jax.experimental import pallas as pl
from jax.experimental.pallas import tpu as pltpu

F32 = jnp.float32
BF16 = jnp.bfloat16

D_MODEL = 1024
PLE_DIM = 256
POOL_WIDTH = 512
POOL_WINDOWS = (2, 4, 8, 16)
POOL_GROUP_DIM = 128
POOL_STATE = 15
RWKV_WIDTH = 512
HEAD_DIM = 64
HEADS = 8
LORA_PAIR = 128
GATE_LORA = 128
SHIFT_WIDTH = 3 * RWKV_WIDTH + LORA_PAIR + GATE_LORA
IN_WIDTH = POOL_WIDTH + SHIFT_WIDTH + 2 * D_MODEL
N_GROUPS = 4
EXPERTS_PER_GROUP = 8
N_EXPERTS = 32
D_EXPERT = 256
RMS_EPS = 1e-6
GN_EPS = 64e-5
PAST_LEN = 16384

LANES = 128
SUBLANES = 8
KEY_HALF = HEAD_DIM // 2
SCAN_OPERANDS = 6
KEY_GROUP = 32
ROUTER_LANES = 128
PROMPT_TILE = 256
POST_TILE = 512
POST_SUB_TILE = 256
DISPATCH_TILE = 1024
TAIL_TILE = 256
GROUP_TILE = 1024
EXPERTS_PER_STEP = 4
DECAY_SCALE = math.exp(-0.5)
COMB_LANE0 = 8
MOE_ROW = D_MODEL + ROUTER_LANES
NEG_BIG = -1e30
VMEM_LIMIT = 56 * 1024 * 1024


def _swap_channel_order(x, axis, to_scan):
    x = jnp.moveaxis(x, axis, -1)
    lead = x.shape[:-1]
    split = (HEADS, 2, KEY_HALF) if to_scan else (KEY_HALF, 2, HEADS)
    x = jnp.swapaxes(x.reshape(lead + split), -1, -3).reshape(lead + (RWKV_WIDTH,))
    return jnp.moveaxis(x, -1, axis)


def _swap_shift_order(x, axis, to_scan):
    x = jnp.moveaxis(x, axis, -1)
    lead = x.shape[:-1]
    rkv = x[..., :3 * RWKV_WIDTH].reshape(lead + (3, RWKV_WIDTH))
    rkv = _swap_channel_order(rkv, -1, to_scan).reshape(lead + (3 * RWKV_WIDTH,))
    return jnp.moveaxis(jnp.concatenate([rkv, x[..., 3 * RWKV_WIDTH:]], axis=-1), -1, axis)


def _dot(a, b):
    return jnp.dot(a, b, preferred_element_type=F32)


def _split_bf16(x):
    hi = x.astype(BF16)
    return hi, (x - hi.astype(F32)).astype(BF16)


def _head_sum(x):
    x3 = x.reshape(HEAD_DIM, HEADS, x.shape[1])
    s = jnp.sum(x3, axis=0, keepdims=True)
    return jnp.broadcast_to(s, x3.shape).reshape(x.shape)


def _rmsnorm(x, g):
    return x * lax.rsqrt(jnp.mean(x * x, axis=-1, keepdims=True) + RMS_EPS) * g


def _sigmoid(x):
    return 0.5 * jnp.tanh(0.5 * x) + 0.5


def _mix_prep_kernel(x_ref, zc0_ref, uc0_ref, nmix_ref, win_ref, wz_ref, mu_ref, w0_ref,
                     w2t_ref, a0_ref, a2t_ref, g2t_ref, kk_ref, ka_ref, rk_ref, mix_ref,
                     pscale_ref, wa_ref,
                     u_ref, zl_ref, q_ref, g_ref, bon_ref,
                     pa_ref, sgb_ref, zc, uext, *, tm, s, pos0):
    t = pl.program_id(1)
    up = 16 * s

    @pl.when(t == 0)
    def _():
        zc[...] = zc0_ref[...]
        uext[0:up] = uc0_ref[...]

    x = x_ref[...]
    h = _rmsnorm(x, nmix_ref[...]).astype(BF16)
    u = _dot(h, win_ref[:, 0:POOL_WIDTH])
    gab = _dot(h, win_ref[:, POOL_WIDTH + SHIFT_WIDTH:IN_WIDTH])
    u_ref[...] = u
    sgb_ref[...] = _sigmoid(gab[:, D_MODEL:])
    uext[up:up + tm] = u

    z = _dot(h, wz_ref[...])
    zl_ref[0] = z[tm - max(SUBLANES, s):tm]
    z_t = z.T
    if s == 1:
        rolled = pltpu.roll(z_t, 1, axis=1)
        lane = lax.broadcasted_iota(jnp.int32, (SHIFT_WIDTH, LANES), 1)
        first = jnp.where(lane == 0, zc[:, LANES - 1:LANES], rolled[:, 0:LANES])
        zprev = jnp.concatenate([first, rolled[:, LANES:]], axis=1)
    else:
        zprev = zc[...]
    zc[...] = z_t[:, tm - LANES:tm]
    zm = z_t + (zprev - z_t) * mu_ref[...]
    r = zm[0:RWKV_WIDTH]
    k = zm[RWKV_WIDTH:2 * RWKV_WIDTH]
    v = zm[2 * RWKV_WIDTH:3 * RWKV_WIDTH]
    lora_in = zm[3 * RWKV_WIDTH:3 * RWKV_WIDTH + LORA_PAIR]
    gd = zm[3 * RWKV_WIDTH + LORA_PAIR:SHIFT_WIDTH]
    dw = _dot(w2t_ref[...], jnp.tanh(lora_in).astype(BF16))
    da = _dot(a2t_ref[...], lora_in.astype(BF16))
    decay = jnp.exp(-DECAY_SCALE * _sigmoid(w0_ref[...] + dw))
    a = _sigmoid(a0_ref[...] + da)
    kk = k * kk_ref[...]
    kk = kk / jnp.maximum(jnp.sqrt(_head_sum(kk * kk)), 1e-12)
    k2 = k * (1.0 + (a - 1.0) * ka_ref[...])
    q_ref[0, 0] = -kk
    q_ref[1, 0] = decay
    q_ref[2, 0] = kk * a
    q_ref[3, 0] = k2
    q_ref[4, 0] = r
    q_ref[5, 0] = v
    g_ref[0] = _dot(g2t_ref[...], _sigmoid(gd).astype(BF16))
    bon_ref[0] = _head_sum(r * k2 * rk_ref[...]) * v

    rows = lax.broadcasted_iota(jnp.int32, (tm, POOL_GROUP_DIM), 0)
    if s > 1:
        rows = rows // s
    pos = pos0 + t * (tm // s) + rows
    ys = []
    for gi, wnd in enumerate(POOL_WINDOWS):
        lanes = slice(gi * POOL_GROUP_DIM, (gi + 1) * POOL_GROUP_DIM)
        wsum = uext[:, lanes]
        span = 1
        while span < wnd:
            wsum = wsum + pltpu.roll(wsum, span * s, axis=0)
            span *= 2
        cur = uext[pl.ds(up, tm), lanes]
        cnt = jnp.minimum(pos + 1, wnd).astype(F32)
        pooled = wsum[up:up + tm] / cnt - cur
        ys.append(_dot(pooled.astype(BF16), mix_ref[gi]))
    y = jnp.concatenate(ys, axis=-1) * pscale_ref[...]
    pa_ref[...] = _sigmoid(gab[:, :D_MODEL]) * _dot(y.astype(BF16), wa_ref[...])

    uext[0:up] = uext[tm:tm + up]


def _mix_prep(x2d, zc0, uc0, wts, *, nb, tiles, tm, s, pos0, cm_index):
    n = x2d.shape[0]
    up = 16 * s
    row = lambda b, t: (b * tiles + t, 0)
    full = lambda arr: pl.BlockSpec(arr.shape, lambda b, t: (0,) * arr.ndim)
    in_specs = [
        pl.BlockSpec((tm, D_MODEL), row),
        pl.BlockSpec((SHIFT_WIDTH, LANES), lambda b, t: (b % (zc0.shape[0] // SHIFT_WIDTH), 0)),
        pl.BlockSpec((up, POOL_WIDTH), lambda b, t: (b % (uc0.shape[0] // up), 0)),
    ] + [full(w) for w in wts]
    tok = lambda width: pl.BlockSpec((tm, width), row)
    cm_shape = (nb, RWKV_WIDTH, tiles * tm) if s == 1 else (tiles, RWKV_WIDTH, tm)
    cm = pl.BlockSpec((1, RWKV_WIDTH, tm), cm_index)
    cm_stack = pl.BlockSpec((SCAN_OPERANDS, 1, RWKV_WIDTH, tm), lambda b, t: (0,) + cm_index(b, t))
    zrows = max(SUBLANES, s)
    out_specs = [tok(POOL_WIDTH), pl.BlockSpec((1, zrows, SHIFT_WIDTH), lambda b, t: (b, 0, 0)),
                 cm_stack, cm, cm] + [tok(D_MODEL)] * 2
    out_shape = [jax.ShapeDtypeStruct((n, POOL_WIDTH), F32),
                 jax.ShapeDtypeStruct((nb, zrows, SHIFT_WIDTH), F32),
                 jax.ShapeDtypeStruct((SCAN_OPERANDS,) + cm_shape, F32)] \
        + [jax.ShapeDtypeStruct(cm_shape, F32)] * 2 \
        + [jax.ShapeDtypeStruct((n, D_MODEL), F32)] * 2
    return pl.pallas_call(
        functools.partial(_mix_prep_kernel, tm=tm, s=s, pos0=pos0),
        grid=(nb, tiles),
        in_specs=in_specs,
        out_specs=out_specs,
        out_shape=out_shape,
        scratch_shapes=[pltpu.VMEM((SHIFT_WIDTH, LANES), F32),
                        pltpu.VMEM((up + tm, POOL_WIDTH), F32)],
        compiler_params=pltpu.CompilerParams(
            dimension_semantics=("arbitrary", "arbitrary"), vmem_limit_bytes=VMEM_LIMIT),
        name="mix_prep",
    )(x2d, zc0, uc0, *wts)


def _wkv_step(s_ref, row, vv):
    groups = HEAD_DIM // KEY_GROUP

    def sa_pass(g, acc):
        for kk in range(KEY_GROUP):
            kp = g * KEY_GROUP + kk
            acc = acc + s_ref[kp] * row(0, kp)
        return acc
    sa = lax.fori_loop(0, groups, sa_pass, jnp.zeros(vv.shape, F32))

    def update_pass(g, acc):
        for kk in range(KEY_GROUP):
            kp = g * KEY_GROUP + kk
            sn = s_ref[kp] * row(1, kp) + sa * row(2, kp) + vv * row(3, kp)
            s_ref[kp] = sn
            acc = acc + sn * row(4, kp)
        return acc
    return lax.fori_loop(0, groups, update_pass, jnp.zeros(vv.shape, F32))


def _swap_major_sublane(x):
    return jnp.swapaxes(x, 0, 1)


def _wkv_scan_prompt_kernel(q_ref, o_ref, sout_ref, s_ref, kv_ref, vv_ref, ov_ref, *, tt):
    t = pl.program_id(0)
    q = pl.program_id(1)
    nb = q_ref.shape[1]

    @pl.when((t == 0) & (q == 0))
    def _():
        s_ref[...] = jnp.zeros_like(s_ref)

    def gather_t(base_lo, base_hi):
        pieces = [q_ref[0, bb, pl.ds(base, SUBLANES), :]
                  for base in (base_lo, base_hi) for bb in range(nb)]
        return jnp.concatenate(pieces, axis=0).T

    @pl.when(q < SCAN_OPERANDS - 1)
    def _():
        def kgroup(gi, c):
            slabs = []
            for kk in range(SUBLANES):
                base = pl.multiple_of((gi * SUBLANES + kk) * SUBLANES, SUBLANES)
                slabs.append(gather_t(base, base))
            kv_ref[q, :, pl.ds(pl.multiple_of(gi * SUBLANES, SUBLANES), SUBLANES), :] = \
                _swap_major_sublane(jnp.stack(slabs))
            return c
        lax.fori_loop(0, HEAD_DIM // SUBLANES, kgroup, 0, unroll=True)

    @pl.when(q == SCAN_OPERANDS - 1)
    def _():
        def vgroup(gi, c):
            slabs = []
            for vi in range(SUBLANES):
                base = pl.multiple_of((gi * SUBLANES + vi) * 2 * SUBLANES, 2 * SUBLANES)
                slabs.append(gather_t(base, base + SUBLANES))
            vv_ref[:, pl.ds(pl.multiple_of(gi * SUBLANES, SUBLANES), SUBLANES), :] = \
                _swap_major_sublane(jnp.stack(slabs))
            return c
        lax.fori_loop(0, KEY_HALF // SUBLANES, vgroup, 0, unroll=True)

        def step(i, c):
            row = lambda qi, kp: kv_ref[qi, i, pl.ds(kp, 1), :]
            ov_ref[i] = _wkv_step(s_ref, row, vv_ref[i])
            return c
        lax.fori_loop(0, tt, step, 0)

        def ogroup(gi, c):
            g0 = pl.multiple_of(gi * SUBLANES, SUBLANES)
            x = _swap_major_sublane(ov_ref[:, pl.ds(g0, SUBLANES), :])
            for vi in range(SUBLANES):
                xt = x[vi].T
                base = pl.multiple_of((gi * SUBLANES + vi) * 2 * SUBLANES, 2 * SUBLANES)
                for vh in range(2):
                    for bb in range(nb):
                        r0 = (vh * nb + bb) * SUBLANES
                        o_ref[bb, pl.ds(base + vh * SUBLANES, SUBLANES), :] = xt[r0:r0 + SUBLANES]
            return c
        lax.fori_loop(0, KEY_HALF // SUBLANES, ogroup, 0, unroll=True)

        @pl.when(t == pl.num_programs(0) - 1)
        def _():
            sout_ref[...] = s_ref[...]


def _wkv_scan_prompt(ops, *, tt):
    _, nb, _, t_len = ops.shape
    assert 2 * nb * HEADS == LANES
    sspec = pl.BlockSpec((HEAD_DIM, KEY_HALF, LANES), lambda ti, qi: (0, 0, 0))
    return pl.pallas_call(
        functools.partial(_wkv_scan_prompt_kernel, tt=tt),
        grid=(t_len // tt, SCAN_OPERANDS),
        in_specs=[pl.BlockSpec((1, nb, RWKV_WIDTH, tt), lambda ti, qi: (qi, 0, 0, ti))],
        out_specs=[pl.BlockSpec((nb, RWKV_WIDTH, tt), lambda ti, qi: (0, 0, ti)), sspec],
        out_shape=[jax.ShapeDtypeStruct((nb, RWKV_WIDTH, t_len), F32),
                   jax.ShapeDtypeStruct((HEAD_DIM, KEY_HALF, LANES), F32)],
        scratch_shapes=[pltpu.VMEM((HEAD_DIM, KEY_HALF, LANES), F32),
                        pltpu.VMEM((SCAN_OPERANDS - 1, tt, HEAD_DIM, LANES), F32),
                        pltpu.VMEM((tt, KEY_HALF, LANES), F32),
                        pltpu.VMEM((tt, KEY_HALF, LANES), F32)],
        compiler_params=pltpu.CompilerParams(
            dimension_semantics=("arbitrary", "arbitrary"), vmem_limit_bytes=VMEM_LIMIT),
        name="wkv_scan_prompt",
    )(ops)


def _wkv_scan_sample_kernel(q_ref, s0_ref, o_ref, sout_ref, s_ref, *, t_len):
    h = pl.program_id(0)
    s_ref[...] = s0_ref[0]
    for i in range(t_len):
        row = lambda qi, kp, i=i: q_ref[qi, i, pl.ds(kp * HEADS + h, 1), :]
        vv = q_ref[SCAN_OPERANDS - 1, i, pl.ds(h, HEAD_DIM, stride=HEADS), :]
        o_ref[i, pl.ds(h, HEAD_DIM, stride=HEADS), :] = _wkv_step(s_ref, row, vv)
    sout_ref[0] = s_ref[...]


def _wkv_scan_sample(ops, s0):
    t_len = ops.shape[1]
    spec = pl.BlockSpec(ops.shape[1:], lambda h: (0, 0, 0))
    sspec = pl.BlockSpec((1, HEAD_DIM, HEAD_DIM, LANES), lambda h: (h, 0, 0, 0))
    return pl.pallas_call(
        functools.partial(_wkv_scan_sample_kernel, t_len=t_len),
        grid=(HEADS,),
        in_specs=[pl.BlockSpec(ops.shape, lambda h: (0, 0, 0, 0)), sspec],
        out_specs=[spec, sspec],
        out_shape=[jax.ShapeDtypeStruct(ops.shape[1:], F32), jax.ShapeDtypeStruct(s0.shape, F32)],
        scratch_shapes=[pltpu.VMEM((HEAD_DIM, HEAD_DIM, LANES), F32)],
        compiler_params=pltpu.CompilerParams(
            dimension_semantics=("arbitrary",), vmem_limit_bytes=VMEM_LIMIT),
        name="wkv_scan_sample",
    )(ops, s0)


def _post_kernel(o_ref, bon_ref, g_ref, pa_ref, sgb_ref, x_ref, cnt0_ref, lnw_ref, lnb_ref,
                 wb_ref, wout_ref, nffn_ref, wrh_ref, wrl_ref, br_ref, x1_ref, h2_ref, rt_ref, cnt_ref,
                 carry, *, sub):
    @pl.when(pl.program_id(0) == 0)
    def _():
        carry[...] = cnt0_ref[...]

    parts = [pl.ds(part * sub, sub) for part in range(o_ref.shape[2] // sub)]
    logits = [_post_project(rows, o_ref, bon_ref, g_ref, pa_ref, sgb_ref, x_ref, lnw_ref, lnb_ref,
                            wb_ref, wout_ref, nffn_ref, wrh_ref, wrl_ref, br_ref, x1_ref, h2_ref)
              for rows in parts]
    prev = carry[:, 0:1]
    for rows, lg in zip(parts, logits):
        prev = _post_route(rows, prev, lg, h2_ref, rt_ref)
    carry[...] = jnp.broadcast_to(prev, carry.shape)
    cnt_ref[...] = jnp.broadcast_to(prev, cnt_ref.shape)


def _post_project(rows, o_ref, bon_ref, g_ref, pa_ref, sgb_ref, x_ref, lnw_ref, lnb_ref,
                  wb_ref, wout_ref, nffn_ref, wrh_ref, wrl_ref, br_ref, x1_ref, h2_ref):
    o = o_ref[0, :, rows]
    mean = _head_sum(o) * (1.0 / HEAD_DIM)
    d = o - mean
    var = _head_sum(d * d) * (1.0 / HEAD_DIM)
    on = d * lax.rsqrt(var + GN_EPS) * lnw_ref[...] + lnb_ref[...]
    yb = ((on + bon_ref[0, :, rows]) * g_ref[0, :, rows]).astype(BF16)
    mb = lax.dot_general(yb, wb_ref[...], (((0,), (0,)), ((), ())), preferred_element_type=F32)
    merged = pa_ref[rows, :] + sgb_ref[rows, :] * mb
    x1 = x_ref[rows, :] + _dot(merged.astype(BF16), wout_ref[...])
    x1_ref[rows, :] = x1
    h2 = _rmsnorm(x1, nffn_ref[...])
    h2_ref[rows, 0:D_MODEL] = h2

    h_hi, h_lo = _split_bf16(h2)
    return (_dot(h_hi, wrh_ref[...]) + _dot(h_lo, wrh_ref[...]) + _dot(h_hi, wrl_ref[...])
            + br_ref[...])


def _post_route(rows, prev_count, logits, h2_ref, rt_ref):
    tm = logits.shape[0]
    lt = logits.T
    grow = lax.broadcasted_iota(jnp.int32, (SUBLANES, tm), 0)
    is_group = grow < N_GROUPS
    gl = jnp.where(is_group, lt[N_EXPERTS:N_EXPERTS + SUBLANES], NEG_BIG)
    gmax = jnp.max(gl, axis=0, keepdims=True)
    gsel = jnp.min(jnp.where(gl == gmax, grow, SUBLANES), axis=0, keepdims=True)
    den = jnp.sum(jnp.where(is_group, jnp.exp(gl - gmax), 0.0), axis=0, keepdims=True)
    pg = 1.0 / den
    erow = lax.broadcasted_iota(jnp.int32, (N_EXPERTS, tm), 0)
    el = jnp.where((erow // EXPERTS_PER_GROUP) == gsel, lt[0:N_EXPERTS], NEG_BIG)
    m1 = jnp.max(el, axis=0, keepdims=True)
    i1 = jnp.min(jnp.where(el == m1, erow, N_EXPERTS), axis=0, keepdims=True)
    el2 = jnp.where(erow == i1, NEG_BIG, el)
    m2 = jnp.max(el2, axis=0, keepdims=True)
    i2 = jnp.min(jnp.where(el2 == m2, erow, N_EXPERTS), axis=0, keepdims=True)
    e2 = jnp.exp(m2 - m1)
    p1 = 1.0 / (1.0 + e2)
    p2 = e2 / (1.0 + e2)

    sel = grow == gsel
    earlier = (lax.broadcasted_iota(jnp.int32, (tm, tm), 0)
               < lax.broadcasted_iota(jnp.int32, (tm, tm), 1)).astype(BF16)
    before = prev_count + _dot(sel.astype(BF16), earlier)
    rank = jnp.sum(jnp.where(sel, before, 0.0), axis=0, keepdims=True)
    rrow = lax.broadcasted_iota(jnp.int32, (2 * SUBLANES, tm), 0)
    first = gsel * EXPERTS_PER_GROUP - COMB_LANE0
    rt_t = (jnp.where(rrow == 0, gsel.astype(F32), 0.0) + jnp.where(rrow == 1, rank, 0.0)
            + jnp.where(rrow == i1 - first, p1 * pg, 0.0) + jnp.where(rrow == i2 - first, p2 * pg, 0.0))
    rt = jnp.concatenate([rt_t, jnp.zeros((ROUTER_LANES - 2 * SUBLANES, tm), F32)], axis=0).T
    rt_ref[rows, :] = rt
    h2_ref[rows, D_MODEL:MOE_ROW] = rt
    return prev_count + jnp.sum(sel.astype(F32), axis=1, keepdims=True)


def _post(o, bon, g, pa, sgb, x2d, cnt0, wts, *, tm, sub, cm_index):
    n = x2d.shape[0]
    row = lambda i: (i, 0)
    tok = lambda width: pl.BlockSpec((tm, width), row)
    cm = pl.BlockSpec((1, RWKV_WIDTH, tm), cm_index)
    full = lambda arr: pl.BlockSpec(arr.shape, lambda i: (0, 0))
    return pl.pallas_call(
        functools.partial(_post_kernel, sub=sub),
        grid=(n // tm,),
        in_specs=[cm] * 3 + [tok(D_MODEL)] * 3 + [full(cnt0)] + [full(w) for w in wts],
        out_specs=[tok(D_MODEL), tok(MOE_ROW), tok(ROUTER_LANES),
                   pl.BlockSpec((SUBLANES, ROUTER_LANES), lambda i: (0, 0))],
        out_shape=[jax.ShapeDtypeStruct((n, D_MODEL), F32),
                   jax.ShapeDtypeStruct((n, MOE_ROW), F32),
                   jax.ShapeDtypeStruct((n, ROUTER_LANES), F32),
                   jax.ShapeDtypeStruct((SUBLANES, ROUTER_LANES), F32)],
        scratch_shapes=[pltpu.VMEM((SUBLANES, ROUTER_LANES), F32)],
        compiler_params=pltpu.CompilerParams(
            dimension_semantics=("arbitrary",), vmem_limit_bytes=VMEM_LIMIT),
        name="post",
    )(o, bon, g, pa, sgb, x2d, cnt0, *wts)


def _route_tables(cnt, n_tiles_max):
    counts = cnt[:N_GROUPS, 0].astype(jnp.int32)
    tiles_g = (counts + GROUP_TILE - 1) // GROUP_TILE
    tile_end = jnp.cumsum(tiles_g)
    row0 = (tile_end - tiles_g) * GROUP_TILE
    tile_ids = jnp.arange(n_tiles_max, dtype=jnp.int32)
    tile_group = jnp.minimum(jnp.sum(tile_ids[:, None] >= tile_end[None, :], axis=-1),
                             N_GROUPS - 1).astype(jnp.int32)
    return tile_group, tile_end[N_GROUPS - 1:].astype(jnp.int32), row0


def _sorted_rows(rt, row0):
    g = rt[:, 0].astype(jnp.int32)
    r = rt[:, 1].astype(jnp.int32)
    return r + jnp.sum(jnp.where(g[:, None] == jnp.arange(N_GROUPS), row0, 0), axis=-1)


def _dispatch_kernel(pos_ref, h_ref, xs0_ref, xs_ref, sem, *, td):
    del xs0_ref
    for r in range(td):
        pltpu.make_async_copy(h_ref.at[pl.ds(r, 1)], xs_ref.at[pl.ds(pos_ref[r], 1)],
                              sem).start(priority=r % 2)
    pltpu.make_async_copy(h_ref, xs_ref.at[pl.ds(0, td)], sem).wait()


def _dispatch(h, pos, xs_in, *, td):
    n = h.shape[0]
    return pl.pallas_call(
        functools.partial(_dispatch_kernel, td=td),
        grid=(n // td,),
        in_specs=[pl.BlockSpec((td,), lambda i: (i,), memory_space=pltpu.SMEM),
                  pl.BlockSpec((td, MOE_ROW), lambda i: (i, 0)),
                  pl.BlockSpec(memory_space=pl.ANY)],
        out_specs=pl.BlockSpec(memory_space=pl.ANY),
        out_shape=jax.ShapeDtypeStruct(xs_in.shape, F32),
        scratch_shapes=[pltpu.SemaphoreType.DMA(())],
        input_output_aliases={2: 0},
        compiler_params=pltpu.CompilerParams(
            dimension_semantics=("arbitrary",), vmem_limit_bytes=VMEM_LIMIT),
        name="moe_dispatch",
    )(pos, h, xs_in)


def _experts_kernel(tg_ref, nu_ref, xs_ref, wg_ref, wu_ref, wd_ref, ys_ref, acc, xb):
    t = pl.program_id(0)
    e = pl.program_id(1)
    last_e = pl.num_programs(1) - 1

    @pl.when(t < nu_ref[0])
    def _():
        @pl.when(e == 0)
        def _():
            acc[...] = jnp.zeros_like(acc)
            xb[...] = xs_ref[:, 0:D_MODEL].astype(BF16)

        x = xb[...]
        rt = xs_ref[:, D_MODEL:MOE_ROW]
        ln = lax.broadcasted_iota(jnp.int32, rt.shape, 1)
        y = acc[...]
        for j in range(EXPERTS_PER_STEP):
            gate = _dot(x, wg_ref[j].astype(BF16))
            hid = gate * _sigmoid(gate) * _dot(x, wu_ref[j].astype(BF16))
            ye = _dot(hid.astype(BF16), wd_ref[j].astype(BF16))
            lane = COMB_LANE0 + e * EXPERTS_PER_STEP + j
            ce = jnp.sum(jnp.where(ln == lane, rt, 0.0), axis=-1, keepdims=True)
            y = y + ce * ye
        acc[...] = y

        @pl.when(e == last_e)
        def _():
            ys_ref[...] = acc[...]

    @pl.when((t >= nu_ref[0]) & (e == last_e))
    def _():
        ys_ref[...] = jnp.zeros_like(ys_ref)


def _experts(xs, tile_group, n_used, wg, wu, wd):
    n_rows = xs.shape[0]
    tile = lambda t, e, tg, nu: (jnp.minimum(t, nu[0] - 1), 0)
    steps = EXPERTS_PER_GROUP // EXPERTS_PER_STEP
    wsel = lambda t, e, tg, nu: (tg[jnp.minimum(t, nu[0] - 1)] * steps + e, 0, 0)
    return pl.pallas_call(
        _experts_kernel,
        grid_spec=pltpu.PrefetchScalarGridSpec(
            num_scalar_prefetch=2,
            grid=(n_rows // GROUP_TILE, steps),
            in_specs=[pl.BlockSpec((GROUP_TILE, MOE_ROW), tile),
                      pl.BlockSpec((EXPERTS_PER_STEP, D_MODEL, D_EXPERT), wsel),
                      pl.BlockSpec((EXPERTS_PER_STEP, D_MODEL, D_EXPERT), wsel),
                      pl.BlockSpec((EXPERTS_PER_STEP, D_EXPERT, D_MODEL), wsel)],
            out_specs=pl.BlockSpec((GROUP_TILE, D_MODEL), lambda t, e, tg, nu: (t, 0)),
            scratch_shapes=[pltpu.VMEM((GROUP_TILE, D_MODEL), F32),
                            pltpu.VMEM((GROUP_TILE, D_MODEL), BF16)]),
        out_shape=jax.ShapeDtypeStruct((n_rows, D_MODEL), F32),
        compiler_params=pltpu.CompilerParams(
            dimension_semantics=("arbitrary", "arbitrary"), vmem_limit_bytes=VMEM_LIMIT),
        name="moe_experts",
    )(tile_group, n_used, xs, wg, wu, wd)


def _tail_kernel(pos_ref, posn_ref, x1_ref, p_ref, ys_ref, nple_ref, wpg_ref, wpp_ref,
                 nfin_ref, y_ref, ybuf0, ybuf1, sem, *, tm):
    i = pl.program_id(0)
    last = pl.num_programs(0) - 1
    ybuf = (ybuf0, ybuf1)

    def issue(p_ref, p_off, slot):
        for r in range(tm):
            pltpu.make_async_copy(ys_ref.at[pl.ds(p_ref[p_off + r], 1)],
                                  ybuf[slot].at[pl.ds(r, 1)], sem.at[slot]).start(priority=r % 2)

    def wait(slot):
        pltpu.make_async_copy(ys_ref.at[pl.ds(0, tm)], ybuf[slot], sem.at[slot]).wait()

    def compute(slot):
        rows = pl.ds(slot * tm, tm)
        x2 = x1_ref[rows, :] + ybuf[slot][...]
        h3 = _rmsnorm(x2, nple_ref[...]).astype(BF16)
        ple = (_sigmoid(_dot(h3, wpg_ref[...]))
               * _dot(p_ref[rows, :].astype(BF16), wpp_ref[...]))
        y_ref[rows, :] = _rmsnorm(x2 + ple, nfin_ref[...])

    @pl.when(i == 0)
    def _():
        issue(pos_ref, 0, 0)

    wait(0)
    issue(pos_ref, tm, 1)
    compute(0)
    wait(1)
    issue(posn_ref, 0, 0)
    compute(1)

    @pl.when(i == last)
    def _():
        wait(0)


def _tail(pos, x1, p2d, ys, nple, wpg, wpp, nfin, *, tm):
    n = x1.shape[0]
    n_steps = n // (2 * tm)
    tok = lambda width: pl.BlockSpec((2 * tm, width), lambda i: (i, 0))
    full = lambda arr: pl.BlockSpec(arr.shape, lambda i: (0, 0))
    return pl.pallas_call(
        functools.partial(_tail_kernel, tm=tm),
        grid=(n_steps,),
        in_specs=[pl.BlockSpec((2 * tm,), lambda i: (i,), memory_space=pltpu.SMEM),
                  pl.BlockSpec((tm,), lambda i: (2 * jnp.minimum(i + 1, n_steps - 1),),
                               memory_space=pltpu.SMEM),
                  tok(D_MODEL), tok(PLE_DIM),
                  pl.BlockSpec(memory_space=pl.ANY),
                  full(nple), full(wpg), full(wpp), full(nfin)],
        out_specs=tok(D_MODEL),
        out_shape=jax.ShapeDtypeStruct((n, D_MODEL), F32),
        scratch_shapes=[pltpu.VMEM((tm, D_MODEL), F32), pltpu.VMEM((tm, D_MODEL), F32),
                        pltpu.SemaphoreType.DMA((2,))],
        compiler_params=pltpu.CompilerParams(
            dimension_semantics=("arbitrary",), vmem_limit_bytes=VMEM_LIMIT),
        name="moe_tail",
    )(pos, pos, x1, p2d, ys, nple, wpg, wpp, nfin)


def _moe_tail(groups, cnt, wg, wu, wd, nple, wpg, wpp, nfin, *, tm):
    n_all = sum(g[0].shape[0] for g in groups)
    n_tiles_max = (n_all + GROUP_TILE - 1) // GROUP_TILE + N_GROUPS
    tile_group, n_used, row0 = _route_tables(cnt, n_tiles_max)
    xs = jnp.zeros((n_tiles_max * GROUP_TILE, MOE_ROW), F32)
    poss = []
    for hx, rt, _, _, td in groups:
        poss.append(_sorted_rows(rt, row0))
        xs = _dispatch(hx, poss[-1], xs, td=td)
    ys = _experts(xs, tile_group, n_used, wg, wu, wd)
    return [_tail(pos, x1, p2d, ys, nple, wpg, wpp, nfin, tm=tm)
            for pos, (_, _, x1, p2d, _) in zip(poss, groups)]


def kernel(x_prompt, x_sample, state_pool, state_shift, state_wkv, p_prompt, p_sample, norm_mix, w_in, pool_mix, pool_scale, w_branch_a, shift_mu, decay_w0, decay_w2, iclr_a0, iclr_a2, gate_g2, k_k, k_a, r_k, ln_x_w, ln_x_b, w_branch_b, w_out, norm_ffn, w_route_group, b_route_group, w_route_expert, b_route_expert, expert_gate, expert_up, expert_down, norm_ple, w_ple_gate, w_ple_proj, norm_final):
    l = 0
    bsz, seq, _ = x_prompt.shape
    dbsz, dseq, _ = x_sample.shape
    row = lambda vec: vec.reshape(1, -1).astype(F32)
    col = lambda vec: vec.reshape(-1, 1).astype(F32)
    scan = lambda arr, axis=0: _swap_channel_order(arr, axis, True)

    o1 = POOL_WIDTH
    o2 = o1 + SHIFT_WIDTH
    w_z = _swap_shift_order(w_in[l][:, o1:o2], 1, True).astype(BF16)
    zeros_lora = jnp.zeros((RWKV_WIDTH, LORA_PAIR // 2), F32)
    w2t_pad = jnp.concatenate([scan(decay_w2[l], 1).T, zeros_lora], axis=1).astype(BF16)
    a2t_pad = jnp.concatenate([zeros_lora, scan(iclr_a2[l], 1).T], axis=1).astype(BF16)
    w_router = jnp.concatenate(
        [w_route_expert[l], w_route_group[l],
         jnp.zeros((D_MODEL, ROUTER_LANES - N_EXPERTS - N_GROUPS), F32)], axis=1)
    wr_hi = w_router.astype(BF16)
    wr_lo = (w_router - wr_hi.astype(F32)).astype(BF16)
    b_router = jnp.concatenate(
        [b_route_expert[l], b_route_group[l],
         jnp.zeros((ROUTER_LANES - N_EXPERTS - N_GROUPS,), F32)]).reshape(1, -1)
    prep_w = [row(norm_mix[l]), w_in[l].astype(BF16), w_z,
              col(_swap_shift_order(shift_mu[l], 0, True)), col(scan(decay_w0[l])), w2t_pad,
              col(scan(iclr_a0[l])), a2t_pad, scan(gate_g2[l], 1).T.astype(BF16), col(scan(k_k[l])),
              col(scan(k_a[l])), col(scan(r_k[l].reshape(-1))), pool_mix[l].astype(BF16),
              row(pool_scale[l]), w_branch_a[l].astype(BF16)]
    post_w = [col(scan(ln_x_w[l])), col(scan(ln_x_b[l])), scan(w_branch_b[l]).astype(BF16),
              w_out[l].astype(BF16), row(norm_ffn[l]), wr_hi, wr_lo, b_router]
    moe_w = [expert_gate[l], expert_up[l], expert_down[l],
             row(norm_ple[l]), w_ple_gate[l].astype(BF16), w_ple_proj[l].astype(BF16), row(norm_final)]

    tm_p = PROMPT_TILE
    tiles_p = seq // tm_p
    x_p = x_prompt.reshape(bsz * seq, D_MODEL)
    outs = _mix_prep(x_p, jnp.zeros((SHIFT_WIDTH, LANES), F32),
                     jnp.zeros((16, POOL_WIDTH), F32), prep_w,
                     nb=bsz, tiles=tiles_p, tm=tm_p, s=1, pos0=0, cm_index=lambda b, t: (b, 0, t))
    u_p, zl_p, ops, g, bon, pa, sgb = outs
    o_p, s_p = _wkv_scan_prompt(ops, tt=LANES)
    tm_post = POST_TILE
    n_p = bsz * seq
    x1_p, hx_p, rt_p, cnt_p = _post(
        o_p, bon, g, pa, sgb, x_p, jnp.zeros((SUBLANES, ROUTER_LANES), F32), post_w,
        tm=tm_post, sub=POST_SUB_TILE, cm_index=lambda i: (i // (seq // tm_post), 0, i % (seq // tm_post)))
    wkv_p = s_p.reshape(KEY_HALF, 2, KEY_HALF, 2, bsz, HEADS).transpose(4, 5, 3, 2, 1, 0)
    wkv_p = wkv_p.reshape(bsz, HEADS, HEAD_DIM, HEAD_DIM)

    n_s = dbsz * dseq
    x_s = x_sample.transpose(1, 0, 2).reshape(n_s, D_MODEL)
    p_s = p_sample[l].transpose(1, 0, 2).reshape(n_s, PLE_DIM)
    uc0_s = jnp.concatenate(
        [jnp.zeros((dbsz, POOL_WIDTH), F32),
         state_pool[l].transpose(1, 0, 2).reshape(POOL_STATE * dbsz, POOL_WIDTH)], axis=0)
    zc0_s = _swap_shift_order(state_shift[l], 1, True).T
    outs = _mix_prep(x_s, zc0_s, uc0_s, prep_w, nb=1, tiles=dseq, tm=dbsz, s=dbsz, pos0=PAST_LEN,
                     cm_index=lambda b, t: (t, 0, 0))
    u_s, zl_s, ops, g, bon, pa, sgb = outs
    s0_s = state_wkv[l].reshape(dbsz, HEADS, 2, KEY_HALF, 2, KEY_HALF).transpose(1, 5, 4, 3, 2, 0)
    s0_s = s0_s.reshape(HEADS, HEAD_DIM, HEAD_DIM, dbsz)
    o_s, s_s = _wkv_scan_sample(ops, s0_s)
    x1_s, hx_s, rt_s, cnt_all = _post(o_s, bon, g, pa, sgb, x_s, cnt_p, post_w,
                                      tm=dbsz, sub=dbsz, cm_index=lambda i: (i, 0, 0))

    y_p, y_s = _moe_tail(
        [(hx_p, rt_p, x1_p, p_prompt[l].reshape(n_p, PLE_DIM), DISPATCH_TILE),
         (hx_s, rt_s, x1_s, p_s, n_s)],
        cnt_all, *moe_w, tm=TAIL_TILE)
    wkv_s = s_s.reshape(HEADS, KEY_HALF, 2, KEY_HALF, 2, dbsz).transpose(5, 0, 4, 3, 2, 1)
    wkv_s = wkv_s.reshape(dbsz, HEADS, HEAD_DIM, HEAD_DIM)

    y_prompt = y_p.reshape(bsz, seq, D_MODEL)
    y_sample = y_s.reshape(dseq, dbsz, D_MODEL).transpose(1, 0, 2)
    pool_prompt = u_p.reshape(bsz, seq, POOL_WIDTH)[:, seq - POOL_STATE:]
    u_s_bt = u_s.reshape(dseq, dbsz, POOL_WIDTH).transpose(1, 0, 2)
    pool_sample = jnp.concatenate([state_pool[l][:, dseq:], u_s_bt], axis=1)
    shift_prompt = _swap_shift_order(zl_p[:, SUBLANES - 1, :], 1, False)
    shift_sample = _swap_shift_order(zl_s[0], 1, False)
    return (y_prompt, y_sample, pool_prompt[None], shift_prompt[None], wkv_p[None],
            pool_sample[None], shift_sample[None], wkv_s[None])
```

```python
import functools
import math

import jax
import jax.numpy as jnp
from jax import lax
from jax.experimental import pallas as pl
from jax.experimental.pallas import tpu as pltpu

F32 = jnp.float32
BF16 = jnp.bfloat16

D_MODEL = 1024
PLE_DIM = 256
POOL_WIDTH = 512
POOL_WINDOWS = (2, 4, 8, 16)
POOL_GROUP_DIM = 128
POOL_STATE = 15
RWKV_WIDTH = 512
HEAD_DIM = 64
HEADS = 8
LORA_PAIR = 128
GATE_LORA = 128
SHIFT_WIDTH = 3 * RWKV_WIDTH + LORA_PAIR + GATE_LORA
IN_WIDTH = POOL_WIDTH + SHIFT_WIDTH + 2 * D_MODEL
N_GROUPS = 4
EXPERTS_PER_GROUP = 8
N_EXPERTS = 32
D_EXPERT = 256
RMS_EPS = 1e-6
GN_EPS = 64e-5
PAST_LEN = 16384

LANES = 128
SUBLANES = 8
KEY_HALF = HEAD_DIM // 2
SCAN_OPERANDS = 6
KEY_GROUP = 32
ROUTER_LANES = 128
PROMPT_TILE = 256
POST_TILE = 512
POST_SUB_TILE = 256
DISPATCH_TILE = 1024
TAIL_TILE = 256
GROUP_TILE = 1024
EXPERTS_PER_STEP = 4
DECAY_SCALE = math.exp(-0.5)
COMB_LANE0 = 8
MOE_ROW = D_MODEL + ROUTER_LANES
NEG_BIG = -1e30
VMEM_LIMIT = 56 * 1024 * 1024


def _swap_channel_order(x, axis, to_scan):
    x = jnp.moveaxis(x, axis, -1)
    lead = x.shape[:-1]
    split = (HEADS, 2, KEY_HALF) if to_scan else (KEY_HALF, 2, HEADS)
    x = jnp.swapaxes(x.reshape(lead + split), -1, -3).reshape(lead + (RWKV_WIDTH,))
    return jnp.moveaxis(x, -1, axis)


def _swap_shift_order(x, axis, to_scan):
    x = jnp.moveaxis(x, axis, -1)
    lead = x.shape[:-1]
    rkv = x[..., :3 * RWKV_WIDTH].reshape(lead + (3, RWKV_WIDTH))
    rkv = _swap_channel_order(rkv, -1, to_scan).reshape(lead + (3 * RWKV_WIDTH,))
    return jnp.moveaxis(jnp.concatenate([rkv, x[..., 3 * RWKV_WIDTH:]], axis=-1), -1, axis)


def _dot(a, b):
    return jnp.dot(a, b, preferred_element_type=F32)


def _split_bf16(x):
    hi = x.astype(BF16)
    return hi, (x - hi.astype(F32)).astype(BF16)


def _head_sum(x):
    x3 = x.reshape(HEAD_DIM, HEADS, x.shape[1])
    s = jnp.sum(x3, axis=0, keepdims=True)
    return jnp.broadcast_to(s, x3.shape).reshape(x.shape)


def _rmsnorm(x, g):
    return x * lax.rsqrt(jnp.mean(x * x, axis=-1, keepdims=True) + RMS_EPS) * g


def _sigmoid(x):
    return 0.5 * jnp.tanh(0.5 * x) + 0.5


def _mix_prep_kernel(x_ref, zc0_ref, uc0_ref, nmix_ref, win_ref, wz_ref, mu_ref, w0_ref,
                     w2t_ref, a0_ref, a2t_ref, g2t_ref, kk_ref, ka_ref, rk_ref, mix_ref,
                     pscale_ref, wa_ref,
                     u_ref, zl_ref, q_ref, g_ref, bon_ref,
                     pa_ref, sgb_ref, zc, uext, *, tm, s, pos0):
    t = pl.program_id(1)
    up = 16 * s

    @pl.when(t == 0)
    def _():
        zc[...] = zc0_ref[...]
        uext[0:up] = uc0_ref[...]

    x = x_ref[...]
    h = _rmsnorm(x, nmix_ref[...]).astype(BF16)
    u = _dot(h, win_ref[:, 0:POOL_WIDTH])
    gab = _dot(h, win_ref[:, POOL_WIDTH + SHIFT_WIDTH:IN_WIDTH])
    u_ref[...] = u
    sgb_ref[...] = _sigmoid(gab[:, D_MODEL:])
    uext[up:up + tm] = u

    z = _dot(h, wz_ref[...])
    zl_ref[0] = z[tm - max(SUBLANES, s):tm]
    z_t = z.T
    if s == 1:
        rolled = pltpu.roll(z_t, 1, axis=1)
        lane = lax.broadcasted_iota(jnp.int32, (SHIFT_WIDTH, LANES), 1)
        first = jnp.where(lane == 0, zc[:, LANES - 1:LANES], rolled[:, 0:LANES])
        zprev = jnp.concatenate([first, rolled[:, LANES:]], axis=1)
    else:
        zprev = zc[...]
    zc[...] = z_t[:, tm - LANES:tm]
    zm = z_t + (zprev - z_t) * mu_ref[...]
    r = zm[0:RWKV_WIDTH]
    k = zm[RWKV_WIDTH:2 * RWKV_WIDTH]
    v = zm[2 * RWKV_WIDTH:3 * RWKV_WIDTH]
    lora_in = zm[3 * RWKV_WIDTH:3 * RWKV_WIDTH + LORA_PAIR]
    gd = zm[3 * RWKV_WIDTH + LORA_PAIR:SHIFT_WIDTH]
    dw = _dot(w2t_ref[...], jnp.tanh(lora_in).astype(BF16))
    da = _dot(a2t_ref[...], lora_in.astype(BF16))
    decay = jnp.exp(-DECAY_SCALE * _sigmoid(w0_ref[...] + dw))
    a = _sigmoid(a0_ref[...] + da)
    kk = k * kk_ref[...]
    kk = kk / jnp.maximum(jnp.sqrt(_head_sum(kk * kk)), 1e-12)
    k2 = k * (1.0 + (a - 1.0) * ka_ref[...])
    q_ref[0, 0] = -kk
    q_ref[1, 0] = decay
    q_ref[2, 0] = kk * a
    q_ref[3, 0] = k2
    q_ref[4, 0] = r
    q_ref[5, 0] = v
    g_ref[0] = _dot(g2t_ref[...], _sigmoid(gd).astype(BF16))
    bon_ref[0] = _head_sum(r * k2 * rk_ref[...]) * v

    rows = lax.broadcasted_iota(jnp.int32, (tm, POOL_GROUP_DIM), 0)
    if s > 1:
        rows = rows // s
    pos = pos0 + t * (tm // s) + rows
    ys = []
    for gi, wnd in enumerate(POOL_WINDOWS):
        lanes = slice(gi * POOL_GROUP_DIM, (gi + 1) * POOL_GROUP_DIM)
        wsum = uext[:, lanes]
        span = 1
        while span < wnd:
            wsum = wsum + pltpu.roll(wsum, span * s, axis=0)
            span *= 2
        cur = uext[pl.ds(up, tm), lanes]
        cnt = jnp.minimum(pos + 1, wnd).astype(F32)
        pooled = wsum[up:up + tm] / cnt - cur
        ys.append(_dot(pooled.astype(BF16), mix_ref[gi]))
    y = jnp.concatenate(ys, axis=-1) * pscale_ref[...]
    pa_ref[...] = _sigmoid(gab[:, :D_MODEL]) * _dot(y.astype(BF16), wa_ref[...])

    uext[0:up] = uext[tm:tm + up]


def _mix_prep(x2d, zc0, uc0, wts, *, nb, tiles, tm, s, pos0, cm_index):
    n = x2d.shape[0]
    up = 16 * s
    row = lambda b, t: (b * tiles + t, 0)
    full = lambda arr: pl.BlockSpec(arr.shape, lambda b, t: (0,) * arr.ndim)
    in_specs = [
        pl.BlockSpec((tm, D_MODEL), row),
        pl.BlockSpec((SHIFT_WIDTH, LANES), lambda b, t: (b % (zc0.shape[0] // SHIFT_WIDTH), 0)),
        pl.BlockSpec((up, POOL_WIDTH), lambda b, t: (b % (uc0.shape[0] // up), 0)),
    ] + [full(w) for w in wts]
    tok = lambda width: pl.BlockSpec((tm, width), row)
    cm_shape = (nb, RWKV_WIDTH, tiles * tm) if s == 1 else (tiles, RWKV_WIDTH, tm)
    cm = pl.BlockSpec((1, RWKV_WIDTH, tm), cm_index)
    cm_stack = pl.BlockSpec((SCAN_OPERANDS, 1, RWKV_WIDTH, tm), lambda b, t: (0,) + cm_index(b, t))
    zrows = max(SUBLANES, s)
    out_specs = [tok(POOL_WIDTH), pl.BlockSpec((1, zrows, SHIFT_WIDTH), lambda b, t: (b, 0, 0)),
                 cm_stack, cm, cm] + [tok(D_MODEL)] * 2
    out_shape = [jax.ShapeDtypeStruct((n, POOL_WIDTH), F32),
                 jax.ShapeDtypeStruct((nb, zrows, SHIFT_WIDTH), F32),
                 jax.ShapeDtypeStruct((SCAN_OPERANDS,) + cm_shape, F32)] \
        + [jax.ShapeDtypeStruct(cm_shape, F32)] * 2 \
        + [jax.ShapeDtypeStruct((n, D_MODEL), F32)] * 2
    return pl.pallas_call(
        functools.partial(_mix_prep_kernel, tm=tm, s=s, pos0=pos0),
        grid=(nb, tiles),
        in_specs=in_specs,
        out_specs=out_specs,
        out_shape=out_shape,
        scratch_shapes=[pltpu.VMEM((SHIFT_WIDTH, LANES), F32),
                        pltpu.VMEM((up + tm, POOL_WIDTH), F32)],
        compiler_params=pltpu.CompilerParams(
            dimension_semantics=("arbitrary", "arbitrary"), vmem_limit_bytes=VMEM_LIMIT),
        name="mix_prep",
    )(x2d, zc0, uc0, *wts)


def _wkv_step(s_ref, row, vv):
    groups = HEAD_DIM // KEY_GROUP

    def sa_pass(g, acc):
        for kk in range(KEY_GROUP):
            kp = g * KEY_GROUP + kk
            acc = acc + s_ref[kp] * row(0, kp)
        return acc
    sa = lax.fori_loop(0, groups, sa_pass, jnp.zeros(vv.shape, F32))

    def update_pass(g, acc):
        for kk in range(KEY_GROUP):
            kp = g * KEY_GROUP + kk
            sn = s_ref[kp] * row(1, kp) + sa * row(2, kp) + vv * row(3, kp)
            s_ref[kp] = sn
            acc = acc + sn * row(4, kp)
        return acc
    return lax.fori_loop(0, groups, update_pass, jnp.zeros(vv.shape, F32))


def _swap_major_sublane(x):
    return jnp.swapaxes(x, 0, 1)


def _wkv_scan_prompt_kernel(q_ref, o_ref, sout_ref, s_ref, kv_ref, vv_ref, ov_ref, *, tt):
    t = pl.program_id(0)
    q = pl.program_id(1)
    nb = q_ref.shape[1]

    @pl.when((t == 0) & (q == 0))
    def _():
        s_ref[...] = jnp.zeros_like(s_ref)

    def gather_t(base_lo, base_hi):
        pieces = [q_ref[0, bb, pl.ds(base, SUBLANES), :]
                  for base in (base_lo, base_hi) for bb in range(nb)]
        return jnp.concatenate(pieces, axis=0).T

    @pl.when(q < SCAN_OPERANDS - 1)
    def _():
        def kgroup(gi, c):
            slabs = []
            for kk in range(SUBLANES):
                base = pl.multiple_of((gi * SUBLANES + kk) * SUBLANES, SUBLANES)
                slabs.append(gather_t(base, base))
            kv_ref[q, :, pl.ds(pl.multiple_of(gi * SUBLANES, SUBLANES), SUBLANES), :] = \
                _swap_major_sublane(jnp.stack(slabs))
            return c
        lax.fori_loop(0, HEAD_DIM // SUBLANES, kgroup, 0, unroll=True)

    @pl.when(q == SCAN_OPERANDS - 1)
    def _():
        def vgroup(gi, c):
            slabs = []
            for vi in range(SUBLANES):
                base = pl.multiple_of((gi * SUBLANES + vi) * 2 * SUBLANES, 2 * SUBLANES)
                slabs.append(gather_t(base, base + SUBLANES))
            vv_ref[:, pl.ds(pl.multiple_of(gi * SUBLANES, SUBLANES), SUBLANES), :] = \
                _swap_major_sublane(jnp.stack(slabs))
            return c
        lax.fori_loop(0, KEY_HALF // SUBLANES, vgroup, 0, unroll=True)

        groups = HEAD_DIM // KEY_GROUP
        zero = jnp.zeros((KEY_HALF, LANES), F32)

        def first_sa(g, acc):
            for kk in range(KEY_GROUP):
                kp = g * KEY_GROUP + kk
                acc = acc + s_ref[kp] * kv_ref[0, 0, pl.ds(kp, 1), :]
            return acc

        def step(i, sa):
            nxt = jnp.minimum(i + 1, tt - 1)
            vv = vv_ref[i]

            def fused(g, accs):
                acc_o, acc_sa = accs
                for kk in range(KEY_GROUP):
                    kp = g * KEY_GROUP + kk
                    row = lambda qi, t=i: kv_ref[qi, t, pl.ds(kp, 1), :]
                    sn = s_ref[kp] * row(1) + sa * row(2) + vv * row(3)
                    s_ref[kp] = sn
                    acc_o = acc_o + sn * row(4)
                    acc_sa = acc_sa + sn * row(0, nxt)
                return acc_o, acc_sa
            o, sa_next = lax.fori_loop(0, groups, fused, (zero, zero))
            ov_ref[i] = o
            return sa_next
        lax.fori_loop(0, tt, step, lax.fori_loop(0, groups, first_sa, zero))

        def ogroup(gi, c):
            g0 = pl.multiple_of(gi * SUBLANES, SUBLANES)
            x = _swap_major_sublane(ov_ref[:, pl.ds(g0, SUBLANES), :])
            for vi in range(SUBLANES):
                xt = x[vi].T
                base = pl.multiple_of((gi * SUBLANES + vi) * 2 * SUBLANES, 2 * SUBLANES)
                for vh in range(2):
                    for bb in range(nb):
                        r0 = (vh * nb + bb) * SUBLANES
                        o_ref[bb, pl.ds(base + vh * SUBLANES, SUBLANES), :] = xt[r0:r0 + SUBLANES]
            return c
        lax.fori_loop(0, KEY_HALF // SUBLANES, ogroup, 0, unroll=True)

        @pl.when(t == pl.num_programs(0) - 1)
        def _():
            sout_ref[...] = s_ref[...]


def _wkv_scan_prompt(ops, *, tt):
    _, nb, _, t_len = ops.shape
    assert 2 * nb * HEADS == LANES
    sspec = pl.BlockSpec((HEAD_DIM, KEY_HALF, LANES), lambda ti, qi: (0, 0, 0))
    return pl.pallas_call(
        functools.partial(_wkv_scan_prompt_kernel, tt=tt),
        grid=(t_len // tt, SCAN_OPERANDS),
        in_specs=[pl.BlockSpec((1, nb, RWKV_WIDTH, tt), lambda ti, qi: (qi, 0, 0, ti))],
        out_specs=[pl.BlockSpec((nb, RWKV_WIDTH, tt), lambda ti, qi: (0, 0, ti)), sspec],
        out_shape=[jax.ShapeDtypeStruct((nb, RWKV_WIDTH, t_len), F32),
                   jax.ShapeDtypeStruct((HEAD_DIM, KEY_HALF, LANES), F32)],
        scratch_shapes=[pltpu.VMEM((HEAD_DIM, KEY_HALF, LANES), F32),
                        pltpu.VMEM((SCAN_OPERANDS - 1, tt, HEAD_DIM, LANES), F32),
                        pltpu.VMEM((tt, KEY_HALF, LANES), F32),
                        pltpu.VMEM((tt, KEY_HALF, LANES), F32)],
        compiler_params=pltpu.CompilerParams(
            dimension_semantics=("arbitrary", "arbitrary"), vmem_limit_bytes=VMEM_LIMIT),
        name="wkv_scan_prompt",
    )(ops)


def _wkv_scan_sample_kernel(q_ref, s0_ref, o_ref, sout_ref, s_ref, *, t_len):
    h = pl.program_id(0)
    s_ref[...] = s0_ref[0]
    for i in range(t_len):
        row = lambda qi, kp, i=i: q_ref[qi, i, pl.ds(kp * HEADS + h, 1), :]
        vv = q_ref[SCAN_OPERANDS - 1, i, pl.ds(h, HEAD_DIM, stride=HEADS), :]
        o_ref[i, pl.ds(h, HEAD_DIM, stride=HEADS), :] = _wkv_step(s_ref, row, vv)
    sout_ref[0] = s_ref[...]


def _wkv_scan_sample(ops, s0):
    t_len = ops.shape[1]
    spec = pl.BlockSpec(ops.shape[1:], lambda h: (0, 0, 0))
    sspec = pl.BlockSpec((1, HEAD_DIM, HEAD_DIM, LANES), lambda h: (h, 0, 0, 0))
    return pl.pallas_call(
        functools.partial(_wkv_scan_sample_kernel, t_len=t_len),
        grid=(HEADS,),
        in_specs=[pl.BlockSpec(ops.shape, lambda h: (0, 0, 0, 0)), sspec],
        out_specs=[spec, sspec],
        out_shape=[jax.ShapeDtypeStruct(ops.shape[1:], F32), jax.ShapeDtypeStruct(s0.shape, F32)],
        scratch_shapes=[pltpu.VMEM((HEAD_DIM, HEAD_DIM, LANES), F32)],
        compiler_params=pltpu.CompilerParams(
            dimension_semantics=("arbitrary",), vmem_limit_bytes=VMEM_LIMIT),
        name="wkv_scan_sample",
    )(ops, s0)


def _post_kernel(o_ref, bon_ref, g_ref, pa_ref, sgb_ref, x_ref, cnt0_ref, lnw_ref, lnb_ref,
                 wb_ref, wout_ref, nffn_ref, wrh_ref, wrl_ref, br_ref, x1_ref, h2_ref, rt_ref, cnt_ref,
                 carry, *, sub):
    @pl.when(pl.program_id(0) == 0)
    def _():
        carry[...] = cnt0_ref[...]

    parts = [pl.ds(part * sub, sub) for part in range(o_ref.shape[2] // sub)]
    logits = [_post_project(rows, o_ref, bon_ref, g_ref, pa_ref, sgb_ref, x_ref, lnw_ref, lnb_ref,
                            wb_ref, wout_ref, nffn_ref, wrh_ref, wrl_ref, br_ref, x1_ref, h2_ref)
              for rows in parts]
    prev = carry[:, 0:1]
    for rows, lg in zip(parts, logits):
        prev = _post_route(rows, prev, lg, h2_ref, rt_ref)
    carry[...] = jnp.broadcast_to(prev, carry.shape)
    cnt_ref[...] = jnp.broadcast_to(prev, cnt_ref.shape)


def _post_project(rows, o_ref, bon_ref, g_ref, pa_ref, sgb_ref, x_ref, lnw_ref, lnb_ref,
                  wb_ref, wout_ref, nffn_ref, wrh_ref, wrl_ref, br_ref, x1_ref, h2_ref):
    o = o_ref[0, :, rows]
    mean = _head_sum(o) * (1.0 / HEAD_DIM)
    d = o - mean
    var = _head_sum(d * d) * (1.0 / HEAD_DIM)
    on = d * lax.rsqrt(var + GN_EPS) * lnw_ref[...] + lnb_ref[...]
    yb = ((on + bon_ref[0, :, rows]) * g_ref[0, :, rows]).astype(BF16)
    mb = lax.dot_general(yb, wb_ref[...], (((0,), (0,)), ((), ())), preferred_element_type=F32)
    merged = pa_ref[rows, :] + sgb_ref[rows, :] * mb
    x1 = x_ref[rows, :] + _dot(merged.astype(BF16), wout_ref[...])
    x1_ref[rows, :] = x1
    h2 = _rmsnorm(x1, nffn_ref[...])
    h2_ref[rows, 0:D_MODEL] = h2

    h_hi, h_lo = _split_bf16(h2)
    return (_dot(h_hi, wrh_ref[...]) + _dot(h_lo, wrh_ref[...]) + _dot(h_hi, wrl_ref[...])
            + br_ref[...])


def _post_route(rows, prev_count, logits, h2_ref, rt_ref):
    tm = logits.shape[0]
    lt = logits.T
    grow = lax.broadcasted_iota(jnp.int32, (SUBLANES, tm), 0)
    is_group = grow < N_GROUPS
    gl = jnp.where(is_group, lt[N_EXPERTS:N_EXPERTS + SUBLANES], NEG_BIG)
    gmax = jnp.max(gl, axis=0, keepdims=True)
    gsel = jnp.min(jnp.where(gl == gmax, grow, SUBLANES), axis=0, keepdims=True)
    den = jnp.sum(jnp.where(is_group, jnp.exp(gl - gmax), 0.0), axis=0, keepdims=True)
    pg = 1.0 / den
    erow = lax.broadcasted_iota(jnp.int32, (N_EXPERTS, tm), 0)
    el = jnp.where((erow // EXPERTS_PER_GROUP) == gsel, lt[0:N_EXPERTS], NEG_BIG)
    m1 = jnp.max(el, axis=0, keepdims=True)
    i1 = jnp.min(jnp.where(el == m1, erow, N_EXPERTS), axis=0, keepdims=True)
    el2 = jnp.where(erow == i1, NEG_BIG, el)
    m2 = jnp.max(el2, axis=0, keepdims=True)
    i2 = jnp.min(jnp.where(el2 == m2, erow, N_EXPERTS), axis=0, keepdims=True)
    e2 = jnp.exp(m2 - m1)
    p1 = 1.0 / (1.0 + e2)
    p2 = e2 / (1.0 + e2)

    sel = grow == gsel
    earlier = (lax.broadcasted_iota(jnp.int32, (tm, tm), 0)
               < lax.broadcasted_iota(jnp.int32, (tm, tm), 1)).astype(BF16)
    before = prev_count + _dot(sel.astype(BF16), earlier)
    rank = jnp.sum(jnp.where(sel, before, 0.0), axis=0, keepdims=True)
    rrow = lax.broadcasted_iota(jnp.int32, (2 * SUBLANES, tm), 0)
    first = gsel * EXPERTS_PER_GROUP - COMB_LANE0
    rt_t = (jnp.where(rrow == 0, gsel.astype(F32), 0.0) + jnp.where(rrow == 1, rank, 0.0)
            + jnp.where(rrow == i1 - first, p1 * pg, 0.0) + jnp.where(rrow == i2 - first, p2 * pg, 0.0))
    rt = jnp.concatenate([rt_t, jnp.zeros((ROUTER_LANES - 2 * SUBLANES, tm), F32)], axis=0).T
    rt_ref[rows, :] = rt
    h2_ref[rows, D_MODEL:MOE_ROW] = rt
    return prev_count + jnp.sum(sel.astype(F32), axis=1, keepdims=True)


def _post(o, bon, g, pa, sgb, x2d, cnt0, wts, *, tm, sub, cm_index):
    n = x2d.shape[0]
    row = lambda i: (i, 0)
    tok = lambda width: pl.BlockSpec((tm, width), row)
    cm = pl.BlockSpec((1, RWKV_WIDTH, tm), cm_index)
    full = lambda arr: pl.BlockSpec(arr.shape, lambda i: (0, 0))
    return pl.pallas_call(
        functools.partial(_post_kernel, sub=sub),
        grid=(n // tm,),
        in_specs=[cm] * 3 + [tok(D_MODEL)] * 3 + [full(cnt0)] + [full(w) for w in wts],
        out_specs=[tok(D_MODEL), tok(MOE_ROW), tok(ROUTER_LANES),
                   pl.BlockSpec((SUBLANES, ROUTER_LANES), lambda i: (0, 0))],
        out_shape=[jax.ShapeDtypeStruct((n, D_MODEL), F32),
                   jax.ShapeDtypeStruct((n, MOE_ROW), F32),
                   jax.ShapeDtypeStruct((n, ROUTER_LANES), F32),
                   jax.ShapeDtypeStruct((SUBLANES, ROUTER_LANES), F32)],
        scratch_shapes=[pltpu.VMEM((SUBLANES, ROUTER_LANES), F32)],
        compiler_params=pltpu.CompilerParams(
            dimension_semantics=("arbitrary",), vmem_limit_bytes=VMEM_LIMIT),
        name="post",
    )(o, bon, g, pa, sgb, x2d, cnt0, *wts)


def _route_tables(cnt, n_tiles_max):
    counts = cnt[:N_GROUPS, 0].astype(jnp.int32)
    tiles_g = (counts + GROUP_TILE - 1) // GROUP_TILE
    tile_end = jnp.cumsum(tiles_g)
    row0 = (tile_end - tiles_g) * GROUP_TILE
    tile_ids = jnp.arange(n_tiles_max, dtype=jnp.int32)
    tile_group = jnp.minimum(jnp.sum(tile_ids[:, None] >= tile_end[None, :], axis=-1),
                             N_GROUPS - 1).astype(jnp.int32)
    return tile_group, tile_end[N_GROUPS - 1:].astype(jnp.int32), row0


def _sorted_rows(rt, row0):
    g = rt[:, 0].astype(jnp.int32)
    r = rt[:, 1].astype(jnp.int32)
    return r + jnp.sum(jnp.where(g[:, None] == jnp.arange(N_GROUPS), row0, 0), axis=-1)


def _dispatch_kernel(pos_ref, h_ref, xs0_ref, xs_ref, sem, *, td):
    del xs0_ref
    for r in range(td):
        pltpu.make_async_copy(h_ref.at[pl.ds(r, 1)], xs_ref.at[pl.ds(pos_ref[r], 1)],
                              sem).start(priority=r % 2)
    pltpu.make_async_copy(h_ref, xs_ref.at[pl.ds(0, td)], sem).wait()


def _dispatch(h, pos, xs_in, *, td):
    n = h.shape[0]
    return pl.pallas_call(
        functools.partial(_dispatch_kernel, td=td),
        grid=(n // td,),
        in_specs=[pl.BlockSpec((td,), lambda i: (i,), memory_space=pltpu.SMEM),
                  pl.BlockSpec((td, MOE_ROW), lambda i: (i, 0)),
                  pl.BlockSpec(memory_space=pl.ANY)],
        out_specs=pl.BlockSpec(memory_space=pl.ANY),
        out_shape=jax.ShapeDtypeStruct(xs_in.shape, F32),
        scratch_shapes=[pltpu.SemaphoreType.DMA(())],
        input_output_aliases={2: 0},
        compiler_params=pltpu.CompilerParams(
            dimension_semantics=("arbitrary",), vmem_limit_bytes=VMEM_LIMIT),
        name="moe_dispatch",
    )(pos, h, xs_in)


def _experts_kernel(tg_ref, nu_ref, xs_ref, wg_ref, wu_ref, wd_ref, ys_ref, acc, xb):
    t = pl.program_id(0)
    e = pl.program_id(1)
    last_e = pl.num_programs(1) - 1

    @pl.when(t < nu_ref[0])
    def _():
        @pl.when(e == 0)
        def _():
            acc[...] = jnp.zeros_like(acc)
            xb[...] = xs_ref[:, 0:D_MODEL].astype(BF16)

        x = xb[...]
        rt = xs_ref[:, D_MODEL:MOE_ROW]
        ln = lax.broadcasted_iota(jnp.int32, rt.shape, 1)
        y = acc[...]
        for j in range(EXPERTS_PER_STEP):
            gate = _dot(x, wg_ref[j].astype(BF16))
            hid = gate * _sigmoid(gate) * _dot(x, wu_ref[j].astype(BF16))
            ye = _dot(hid.astype(BF16), wd_ref[j].astype(BF16))
            lane = COMB_LANE0 + e * EXPERTS_PER_STEP + j
            ce = jnp.sum(jnp.where(ln == lane, rt, 0.0), axis=-1, keepdims=True)
            y = y + ce * ye
        acc[...] = y

        @pl.when(e == last_e)
        def _():
            ys_ref[...] = acc[...]

    @pl.when((t >= nu_ref[0]) & (e == last_e))
    def _():
        ys_ref[...] = jnp.zeros_like(ys_ref)


def _experts(xs, tile_group, n_used, wg, wu, wd):
    n_rows = xs.shape[0]
    tile = lambda t, e, tg, nu: (jnp.minimum(t, nu[0] - 1), 0)
    steps = EXPERTS_PER_GROUP // EXPERTS_PER_STEP
    wsel = lambda t, e, tg, nu: (tg[jnp.minimum(t, nu[0] - 1)] * steps + e, 0, 0)
    return pl.pallas_call(
        _experts_kernel,
        grid_spec=pltpu.PrefetchScalarGridSpec(
            num_scalar_prefetch=2,
            grid=(n_rows // GROUP_TILE, steps),
            in_specs=[pl.BlockSpec((GROUP_TILE, MOE_ROW), tile),
                      pl.BlockSpec((EXPERTS_PER_STEP, D_MODEL, D_EXPERT), wsel),
                      pl.BlockSpec((EXPERTS_PER_STEP, D_MODEL, D_EXPERT), wsel),
                      pl.BlockSpec((EXPERTS_PER_STEP, D_EXPERT, D_MODEL), wsel)],
            out_specs=pl.BlockSpec((GROUP_TILE, D_MODEL), lambda t, e, tg, nu: (t, 0)),
            scratch_shapes=[pltpu.VMEM((GROUP_TILE, D_MODEL), F32),
                            pltpu.VMEM((GROUP_TILE, D_MODEL), BF16)]),
        out_shape=jax.ShapeDtypeStruct((n_rows, D_MODEL), F32),
        compiler_params=pltpu.CompilerParams(
            dimension_semantics=("arbitrary", "arbitrary"), vmem_limit_bytes=VMEM_LIMIT),
        name="moe_experts",
    )(tile_group, n_used, xs, wg, wu, wd)


def _tail_kernel(pos_ref, posn_ref, x1_ref, p_ref, ys_ref, nple_ref, wpg_ref, wpp_ref,
                 nfin_ref, y_ref, ybuf0, ybuf1, sem, *, tm):
    i = pl.program_id(0)
    last = pl.num_programs(0) - 1
    ybuf = (ybuf0, ybuf1)

    def issue(p_ref, p_off, slot):
        for r in range(tm):
            pltpu.make_async_copy(ys_ref.at[pl.ds(p_ref[p_off + r], 1)],
                                  ybuf[slot].at[pl.ds(r, 1)], sem.at[slot]).start(priority=r % 2)

    def wait(slot):
        pltpu.make_async_copy(ys_ref.at[pl.ds(0, tm)], ybuf[slot], sem.at[slot]).wait()

    def compute(slot):
        rows = pl.ds(slot * tm, tm)
        x2 = x1_ref[rows, :] + ybuf[slot][...]
        h3 = _rmsnorm(x2, nple_ref[...]).astype(BF16)
        ple = (_sigmoid(_dot(h3, wpg_ref[...]))
               * _dot(p_ref[rows, :].astype(BF16), wpp_ref[...]))
        y_ref[rows, :] = _rmsnorm(x2 + ple, nfin_ref[...])

    @pl.when(i == 0)
    def _():
        issue(pos_ref, 0, 0)

    wait(0)
    issue(pos_ref, tm, 1)
    compute(0)
    wait(1)
    issue(posn_ref, 0, 0)
    compute(1)

    @pl.when(i == last)
    def _():
        wait(0)


def _tail(pos, x1, p2d, ys, nple, wpg, wpp, nfin, *, tm):
    n = x1.shape[0]
    n_steps = n // (2 * tm)
    tok = lambda width: pl.BlockSpec((2 * tm, width), lambda i: (i, 0))
    full = lambda arr: pl.BlockSpec(arr.shape, lambda i: (0, 0))
    return pl.pallas_call(
        functools.partial(_tail_kernel, tm=tm),
        grid=(n_steps,),
        in_specs=[pl.BlockSpec((2 * tm,), lambda i: (i,), memory_space=pltpu.SMEM),
                  pl.BlockSpec((tm,), lambda i: (2 * jnp.minimum(i + 1, n_steps - 1),),
                               memory_space=pltpu.SMEM),
                  tok(D_MODEL), tok(PLE_DIM),
                  pl.BlockSpec(memory_space=pl.ANY),
                  full(nple), full(wpg), full(wpp), full(nfin)],
        out_specs=tok(D_MODEL),
        out_shape=jax.ShapeDtypeStruct((n, D_MODEL), F32),
        scratch_shapes=[pltpu.VMEM((tm, D_MODEL), F32), pltpu.VMEM((tm, D_MODEL), F32),
                        pltpu.SemaphoreType.DMA((2,))],
        compiler_params=pltpu.CompilerParams(
            dimension_semantics=("arbitrary",), vmem_limit_bytes=VMEM_LIMIT),
        name="moe_tail",
    )(pos, pos, x1, p2d, ys, nple, wpg, wpp, nfin)


def _moe_tail(groups, cnt, wg, wu, wd, nple, wpg, wpp, nfin, *, tm):
    n_all = sum(g[0].shape[0] for g in groups)
    n_tiles_max = (n_all + GROUP_TILE - 1) // GROUP_TILE + N_GROUPS
    tile_group, n_used, row0 = _route_tables(cnt, n_tiles_max)
    xs = jnp.zeros((n_tiles_max * GROUP_TILE, MOE_ROW), F32)
    poss = []
    for hx, rt, _, _, td in groups:
        poss.append(_sorted_rows(rt, row0))
        xs = _dispatch(hx, poss[-1], xs, td=td)
    ys = _experts(xs, tile_group, n_used, wg, wu, wd)
    return [_tail(pos, x1, p2d, ys, nple, wpg, wpp, nfin, tm=tm)
            for pos, (_, _, x1, p2d, _) in zip(poss, groups)]


def kernel(x_prompt, x_sample, state_pool, state_shift, state_wkv, p_prompt, p_sample, norm_mix, w_in, pool_mix, pool_scale, w_branch_a, shift_mu, decay_w0, decay_w2, iclr_a0, iclr_a2, gate_g2, k_k, k_a, r_k, ln_x_w, ln_x_b, w_branch_b, w_out, norm_ffn, w_route_group, b_route_group, w_route_expert, b_route_expert, expert_gate, expert_up, expert_down, norm_ple, w_ple_gate, w_ple_proj, norm_final):
    l = 0
    bsz, seq, _ = x_prompt.shape
    dbsz, dseq, _ = x_sample.shape
    row = lambda vec: vec.reshape(1, -1).astype(F32)
    col = lambda vec: vec.reshape(-1, 1).astype(F32)
    scan = lambda arr, axis=0: _swap_channel_order(arr, axis, True)

    o1 = POOL_WIDTH
    o2 = o1 + SHIFT_WIDTH
    w_z = _swap_shift_order(w_in[l][:, o1:o2], 1, True).astype(BF16)
    zeros_lora = jnp.zeros((RWKV_WIDTH, LORA_PAIR // 2), F32)
    w2t_pad = jnp.concatenate([scan(decay_w2[l], 1).T, zeros_lora], axis=1).astype(BF16)
    a2t_pad = jnp.concatenate([zeros_lora, scan(iclr_a2[l], 1).T], axis=1).astype(BF16)
    w_router = jnp.concatenate(
        [w_route_expert[l], w_route_group[l],
         jnp.zeros((D_MODEL, ROUTER_LANES - N_EXPERTS - N_GROUPS), F32)], axis=1)
    wr_hi = w_router.astype(BF16)
    wr_lo = (w_router - wr_hi.astype(F32)).astype(BF16)
    b_router = jnp.concatenate(
        [b_route_expert[l], b_route_group[l],
         jnp.zeros((ROUTER_LANES - N_EXPERTS - N_GROUPS,), F32)]).reshape(1, -1)
    prep_w = [row(norm_mix[l]), w_in[l].astype(BF16), w_z,
              col(_swap_shift_order(shift_mu[l], 0, True)), col(scan(decay_w0[l])), w2t_pad,
              col(scan(iclr_a0[l])), a2t_pad, scan(gate_g2[l], 1).T.astype(BF16), col(scan(k_k[l])),
              col(scan(k_a[l])), col(scan(r_k[l].reshape(-1))), pool_mix[l].astype(BF16),
              row(pool_scale[l]), w_branch_a[l].astype(BF16)]
    post_w = [col(scan(ln_x_w[l])), col(scan(ln_x_b[l])), scan(w_branch_b[l]).astype(BF16),
              w_out[l].astype(BF16), row(norm_ffn[l]), wr_hi, wr_lo, b_router]
    moe_w = [expert_gate[l], expert_up[l], expert_down[l],
             row(norm_ple[l]), w_ple_gate[l].astype(BF16), w_ple_proj[l].astype(BF16), row(norm_final)]

    tm_p = PROMPT_TILE
    tiles_p = seq // tm_p
    x_p = x_prompt.reshape(bsz * seq, D_MODEL)
    outs = _mix_prep(x_p, jnp.zeros((SHIFT_WIDTH, LANES), F32),
                     jnp.zeros((16, POOL_WIDTH), F32), prep_w,
                     nb=bsz, tiles=tiles_p, tm=tm_p, s=1, pos0=0, cm_index=lambda b, t: (b, 0, t))
    u_p, zl_p, ops, g, bon, pa, sgb = outs
    o_p, s_p = _wkv_scan_prompt(ops, tt=LANES)
    tm_post = POST_TILE
    n_p = bsz * seq
    x1_p, hx_p, rt_p, cnt_p = _post(
        o_p, bon, g, pa, sgb, x_p, jnp.zeros((SUBLANES, ROUTER_LANES), F32), post_w,
        tm=tm_post, sub=POST_SUB_TILE, cm_index=lambda i: (i // (seq // tm_post), 0, i % (seq // tm_post)))
    wkv_p = s_p.reshape(KEY_HALF, 2, KEY_HALF, 2, bsz, HEADS).transpose(4, 5, 3, 2, 1, 0)
    wkv_p = wkv_p.reshape(bsz, HEADS, HEAD_DIM, HEAD_DIM)

    n_s = dbsz * dseq
    x_s = x_sample.transpose(1, 0, 2).reshape(n_s, D_MODEL)
    p_s = p_sample[l].transpose(1, 0, 2).reshape(n_s, PLE_DIM)
    uc0_s = jnp.concatenate(
        [jnp.zeros((dbsz, POOL_WIDTH), F32),
         state_pool[l].transpose(1, 0, 2).reshape(POOL_STATE * dbsz, POOL_WIDTH)], axis=0)
    zc0_s = _swap_shift_order(state_shift[l], 1, True).T
    outs = _mix_prep(x_s, zc0_s, uc0_s, prep_w, nb=1, tiles=dseq, tm=dbsz, s=dbsz, pos0=PAST_LEN,
                     cm_index=lambda b, t: (t, 0, 0))
    u_s, zl_s, ops, g, bon, pa, sgb = outs
    s0_s = state_wkv[l].reshape(dbsz, HEADS, 2, KEY_HALF, 2, KEY_HALF).transpose(1, 5, 4, 3, 2, 0)
    s0_s = s0_s.reshape(HEADS, HEAD_DIM, HEAD_DIM, dbsz)
    o_s, s_s = _wkv_scan_sample(ops, s0_s)
    x1_s, hx_s, rt_s, cnt_all = _post(o_s, bon, g, pa, sgb, x_s, cnt_p, post_w,
                                      tm=dbsz, sub=dbsz, cm_index=lambda i: (i, 0, 0))

    y_p, y_s = _moe_tail(
        [(hx_p, rt_p, x1_p, p_prompt[l].reshape(n_p, PLE_DIM), DISPATCH_TILE),
         (hx_s, rt_s, x1_s, p_s, n_s)],
        cnt_all, *moe_w, tm=TAIL_TILE)
    wkv_s = s_s.reshape(HEADS, KEY_HALF, 2, KEY_HALF, 2, dbsz).transpose(5, 0, 4, 3, 2, 1)
    wkv_s = wkv_s.reshape(dbsz, HEADS, HEAD_DIM, HEAD_DIM)

    y_prompt = y_p.reshape(bsz, seq, D_MODEL)
    y_sample = y_s.reshape(dseq, dbsz, D_MODEL).transpose(1, 0, 2)
    pool_prompt = u_p.reshape(bsz, seq, POOL_WIDTH)[:, seq - POOL_STATE:]
    u_s_bt = u_s.reshape(dseq, dbsz, POOL_WIDTH).transpose(1, 0, 2)
    pool_sample = jnp.concatenate([state_pool[l][:, dseq:], u_s_bt], axis=1)
    shift_prompt = _swap_shift_order(zl_p[:, SUBLANES - 1, :], 1, False)
    shift_sample = _swap_shift_order(zl_s[0], 1, False)
    return (y_prompt, y_sample, pool_prompt[None], shift_prompt[None], wkv_p[None],
            pool_sample[None], shift_sample[None], wkv_s[None])
```
